```python
import math
import jax, jax.numpy as jnp
from jax import lax
import numpy as np

D_MODEL = 1024
BATCH = 8
SEQ = 4096
DEPTH = 1

DN_HEADS = 4
DN_HEAD_DIM = 128
DN_WIDTH = DN_HEADS * DN_HEAD_DIM
CONV_WIDTH = 4
DN_CHUNK = 64
SWA_Q_HEADS = 8
SWA_KV_HEADS = 2
SWA_HEAD_DIM = 64
SWA_WIDTH = SWA_Q_HEADS * SWA_HEAD_DIM
SWA_KV_WIDTH = SWA_KV_HEADS * SWA_HEAD_DIM
SWA_WINDOW = 128
SWA_BLOCK = 128
MIX_WIDTH = DN_WIDTH + SWA_WIDTH
IN_SIZES = (DN_WIDTH, DN_WIDTH, DN_WIDTH, SWA_WIDTH, SWA_KV_WIDTH, SWA_KV_WIDTH, DN_WIDTH, DN_HEADS, DN_HEADS)
IN_COLS = sum(IN_SIZES)
N_EXPERTS = 256
N_GROUPS = 8
TOPK_GROUPS = 4
TOP_K = 8
EXPERT_FF = 256
SHARED_FF = 256
ROUTED_SCALE = 2.5
MOE_BLOCK = 256
DEEPNORM_ALPHA = (2.0 * DEPTH) ** 0.25
DEEPNORM_BETA = (8.0 * DEPTH) ** -0.25
LN_EPS = 1e-5
RMS_EPS = 1e-6
L2_EPS = 1e-6

kernel_name = "hymba_deltanet_swa_moe_deepnorm"


def layer_norm(x, g, b):
    xf = x.astype(jnp.float32)
    mu = jnp.mean(xf, axis=-1, keepdims=True)
    var = jnp.mean(jnp.square(xf - mu), axis=-1, keepdims=True)
    return ((xf - mu) * lax.rsqrt(var + LN_EPS) * g.astype(jnp.float32) + b.astype(jnp.float32)).astype(x.dtype)


def l2_normalize(x):
    return x * lax.rsqrt(jnp.sum(jnp.square(x), axis=-1, keepdims=True) + L2_EPS)


def causal_depthwise_conv(x, w):
    width = w.shape[0]
    return lax.conv_general_dilated(
        x, w[:, None, :].astype(x.dtype), window_strides=(1,), padding=[(width - 1, 0)],
        dimension_numbers=("NWC", "WIO", "NWC"), feature_group_count=x.shape[-1])


def gated_delta_rule(q, k, v, g, beta):
    B, T, H, DK = q.shape
    DV = v.shape[-1]
    C = DN_CHUNK
    NC = T // C

    def chunks(t):
        return t.reshape(B, NC, C, H, t.shape[-1]).transpose(0, 3, 1, 2, 4)

    q = chunks(q) * (DK ** -0.5)
    k = chunks(k)
    v = chunks(v)
    g = g.reshape(B, NC, C, H).transpose(0, 3, 1, 2)
    beta = beta.reshape(B, NC, C, H).transpose(0, 3, 1, 2)
    gc = jnp.cumsum(g, axis=-1)

    tri_incl = jnp.tril(jnp.ones((C, C), dtype=bool))
    tri_strict = jnp.tril(jnp.ones((C, C), dtype=bool), k=-1)
    decay = jnp.exp(jnp.where(tri_incl, gc[..., :, None] - gc[..., None, :], -jnp.inf))

    kb = k * beta[..., None]
    vb = v * beta[..., None]
    a_strict = jnp.where(tri_strict, jnp.einsum("bhnik,bhnjk->bhnij", kb, k) * decay, 0.0)
    rhs = jnp.concatenate([vb, kb * jnp.exp(gc)[..., None]], axis=-1)
    eye = jnp.eye(C, dtype=q.dtype)
    sol = lax.linalg.triangular_solve(eye + a_strict, rhs, left_side=True, lower=True, unit_diagonal=True)
    u, w = sol[..., :DV], sol[..., DV:]

    attn_in = jnp.einsum("bhnik,bhnjk->bhnij", q, k) * decay
    q_dec = q * jnp.exp(gc)[..., None]
    k_tail = k * jnp.exp(gc[..., -1:] - gc)[..., None]
    g_last = jnp.exp(gc[..., -1])

    def to_front(t):
        return jnp.moveaxis(t, 2, 0)

    xs = (to_front(w), to_front(u), to_front(q_dec), to_front(attn_in), to_front(k_tail), to_front(g_last))

    def step(S, inp):
        w_c, u_c, qd_c, a_c, kt_c, gl_c = inp
        v_new = u_c - jnp.einsum("bhck,bhkv->bhcv", w_c, S)
        o_c = jnp.einsum("bhck,bhkv->bhcv", qd_c, S) + jnp.einsum("bhij,bhjv->bhiv", a_c, v_new)
        S = S * gl_c[..., None, None] + jnp.einsum("bhck,bhcv->bhkv", kt_c, v_new)
        return S, o_c

    S0 = jnp.zeros((B, H, DK, DV), dtype=q.dtype)
    _, o = lax.scan(step, S0, xs)
    return o.transpose(1, 0, 3, 2, 4).reshape(B, T, H, DV)


def sliding_window_attention(q, k, v, sinks):
    B, T, Hq, D = q.shape
    Hkv = k.shape[2]
    G = Hq // Hkv
    BLK = SWA_BLOCK
    NB = T // BLK
    qb = q.reshape(B, NB, BLK, Hkv, G, D)

    def band(t):
        cur = t.reshape(B, NB, BLK, Hkv, D)
        prev = jnp.pad(t, ((0, 0), (BLK, 0), (0, 0), (0, 0)))[:, :T].reshape(B, NB, BLK, Hkv, D)
        return jnp.concatenate([prev, cur], axis=2)

    kb, vb = band(k), band(v)
    s = jnp.einsum("bnqhgd,bnkhd->bnhgqk", qb, kb).astype(jnp.float32) * (D ** -0.5)
    dist = (jnp.arange(BLK)[:, None] + BLK) - jnp.arange(2 * BLK)[None, :]
    key_global = (jnp.arange(NB) * BLK - BLK)[:, None] + jnp.arange(2 * BLK)[None, :]
    valid = ((dist >= 0) & (dist < SWA_WINDOW))[None] & (key_global >= 0)[:, None, :]
    slopes = (2.0 ** (-8.0 * (jnp.arange(Hq, dtype=jnp.float32) + 1.0) / Hq)).reshape(Hkv, G)
    s = s - slopes[:, :, None, None] * dist.astype(jnp.float32)
    s = jnp.where(valid[None, :, None, None], s, -jnp.inf)
    sink = sinks.astype(jnp.float32).reshape(Hkv, G)[:, :, None, None]
    m = jnp.maximum(jnp.max(s, axis=-1, keepdims=True), sink)
    p = jnp.exp(s - m)
    denom = jnp.sum(p, axis=-1, keepdims=True) + jnp.exp(sink - m)
    p = (p / denom).astype(v.dtype)
    o = jnp.einsum("bnhgqk,bnkhd->bnqhgd", p, vb)
    return o.reshape(B, T, Hq * D)


def hybrid_token_mixer(x, w_in, conv_w, a_log, dt_bias, dn_norm_g, sinks, w_out):
    B, T, _ = x.shape
    proj = x @ w_in
    split_idx = list(np.cumsum(np.array(IN_SIZES))[:-1])
    dn_q, dn_k, dn_v, sq, sk, sv, z, b_logit, a_logit = jnp.split(proj, split_idx, axis=-1)

    qkv = jax.nn.silu(causal_depthwise_conv(jnp.concatenate([dn_q, dn_k, dn_v], axis=-1), conv_w))
    qkv = qkv.astype(jnp.float32)
    q, k, v = jnp.split(qkv, 3, axis=-1)
    q = l2_normalize(q.reshape(B, T, DN_HEADS, DN_HEAD_DIM))
    k = l2_normalize(k.reshape(B, T, DN_HEADS, DN_HEAD_DIM))
    v = v.reshape(B, T, DN_HEADS, DN_HEAD_DIM)
    beta = jax.nn.sigmoid(b_logit.astype(jnp.float32))
    g = -jnp.exp(a_log.astype(jnp.float32)) * jax.nn.softplus(a_logit.astype(jnp.float32) + dt_bias.astype(jnp.float32))
    o = gated_delta_rule(q, k, v, g, beta)
    o = o * lax.rsqrt(jnp.mean(jnp.square(o), axis=-1, keepdims=True) + RMS_EPS) * dn_norm_g.astype(jnp.float32)
    o = o * jax.nn.silu(z.astype(jnp.float32).reshape(B, T, DN_HEADS, DN_HEAD_DIM))
    dn_out = o.reshape(B, T, DN_WIDTH).astype(x.dtype)

    swa_out = sliding_window_attention(
        sq.reshape(B, T, SWA_Q_HEADS, SWA_HEAD_DIM),
        sk.reshape(B, T, SWA_KV_HEADS, SWA_HEAD_DIM),
        sv.reshape(B, T, SWA_KV_HEADS, SWA_HEAD_DIM), sinks)

    mixed = jnp.concatenate([dn_out, swa_out.astype(x.dtype)], axis=-1)
    return mixed @ w_out


def moe_ffn(x, router_w, router_bias, w_gate, w_up, w_down, shared_w_gate, shared_w_up, shared_w_down):
    B, T, D = x.shape
    N = B * T
    xf = x.reshape(N, D)
    scores = jax.nn.sigmoid((xf @ router_w).astype(jnp.float32))
    sel = scores + router_bias.astype(jnp.float32)
    grp_score = lax.top_k(sel.reshape(N, N_GROUPS, N_EXPERTS // N_GROUPS), 2)[0].sum(-1)
    _, gidx = lax.top_k(grp_score, TOPK_GROUPS)
    gmask = jnp.any(gidx[..., None] == jnp.arange(N_GROUPS)[None, None, :], axis=1)
    emask = jnp.repeat(gmask, N_EXPERTS // N_GROUPS, axis=-1)
    _, eidx = lax.top_k(jnp.where(emask, sel, -jnp.inf), TOP_K)
    gate = jnp.take_along_axis(scores, eidx, axis=-1)
    gate = gate / jnp.sum(gate, axis=-1, keepdims=True) * ROUTED_SCALE

    NA = N * TOP_K
    flat_e = eidx.reshape(-1)
    order = jnp.argsort(flat_e)
    sorted_e = flat_e[order]
    tok = (order // TOP_K).astype(jnp.int32)
    sizes = jnp.bincount(flat_e, length=N_EXPERTS).astype(jnp.int32)
    start = jnp.cumsum(sizes) - sizes
    padded = (sizes + MOE_BLOCK - 1) // MOE_BLOCK * MOE_BLOCK
    pend = jnp.cumsum(padded)
    pstart = pend - padded
    dest = pstart[sorted_e] + (jnp.arange(NA, dtype=jnp.int32) - start[sorted_e])
    n_blocks = -(-NA // MOE_BLOCK) + N_EXPERTS
    rows = n_blocks * MOE_BLOCK
    row_tok = jnp.full((rows,), N, jnp.int32).at[dest].set(tok)
    row_gate = jnp.zeros((rows,), jnp.float32).at[dest].set(gate.reshape(-1)[order])
    blk_e = jnp.clip(jnp.searchsorted(pend, jnp.arange(n_blocks, dtype=jnp.int32) * MOE_BLOCK, side="right"),
                     0, N_EXPERTS - 1)
    xpad = jnp.concatenate([xf, jnp.zeros((1, D), xf.dtype)], axis=0)
    xb = xpad[row_tok].reshape(n_blocks, MOE_BLOCK, D)

    def expert_block(args):
        xblk, e = args
        h = jax.nn.silu(xblk @ w_gate[e]) * (xblk @ w_up[e])
        return h @ w_down[e]

    y = lax.map(expert_block, (xb, blk_e)).reshape(rows, D) * row_gate[:, None].astype(xf.dtype)
    routed = jax.ops.segment_sum(y, row_tok, num_segments=N + 1)[:N]
    shared = (jax.nn.silu(xf @ shared_w_gate) * (xf @ shared_w_up)) @ shared_w_down
    return (routed + shared).reshape(B, T, D)


def setup_inputs(seed: int = 0) -> dict:
    key = jax.random.key(seed)
    ks = jax.random.split(key, 24)

    def nrm(k, shape, scale):
        return jax.random.normal(k, shape, jnp.float32) * scale

    x = nrm(ks[0], (BATCH, SEQ, D_MODEL), 1.0)
    col_scale = np.ones((IN_COLS,), np.float32)
    offs = np.concatenate([[0], np.cumsum(np.array(IN_SIZES))])
    for idx in (2, 5):
        col_scale[offs[idx]:offs[idx + 1]] = DEEPNORM_BETA
    w_in = nrm(ks[1], (DEPTH, D_MODEL, IN_COLS), D_MODEL ** -0.5) * jnp.asarray(col_scale)
    conv_w = nrm(ks[2], (DEPTH, CONV_WIDTH, 3 * DN_WIDTH), CONV_WIDTH ** -0.5)
    a_log = jnp.log(jax.random.uniform(ks[3], (DEPTH, DN_HEADS), jnp.float32, 1.0, 16.0))
    dt = jnp.exp(jax.random.uniform(ks[4], (DEPTH, DN_HEADS), jnp.float32, math.log(1e-3), math.log(1e-1)))
    dt_bias = dt + jnp.log(-jnp.expm1(-dt))
    dn_norm_g = 1.0 + nrm(ks[5], (DEPTH, DN_HEAD_DIM), 0.02)
    sinks = nrm(ks[6], (DEPTH, SWA_Q_HEADS), 0.5)
    w_out = nrm(ks[7], (DEPTH, MIX_WIDTH, D_MODEL), MIX_WIDTH ** -0.5 * DEEPNORM_BETA)
    ln1_g = 1.0 + nrm(ks[8], (DEPTH, D_MODEL), 0.02)
    ln1_b = nrm(ks[9], (DEPTH, D_MODEL), 0.02)
    router_w = nrm(ks[10], (DEPTH, D_MODEL, N_EXPERTS), D_MODEL ** -0.5)
    router_bias = nrm(ks[11], (DEPTH, N_EXPERTS), 0.01)
    w_gate = nrm(ks[12], (DEPTH, N_EXPERTS, D_MODEL, EXPERT_FF), D_MODEL ** -0.5)
    w_up = nrm(ks[13], (DEPTH, N_EXPERTS, D_MODEL, EXPERT_FF), D_MODEL ** -0.5 * DEEPNORM_BETA)
    w_down = nrm(ks[14], (DEPTH, N_EXPERTS, EXPERT_FF, D_MODEL), EXPERT_FF ** -0.5 * DEEPNORM_BETA)
    shared_w_gate = nrm(ks[15], (DEPTH, D_MODEL, SHARED_FF), D_MODEL ** -0.5)
    shared_w_up = nrm(ks[16], (DEPTH, D_MODEL, SHARED_FF), D_MODEL ** -0.5 * DEEPNORM_BETA)
    shared_w_down = nrm(ks[17], (DEPTH, SHARED_FF, D_MODEL), SHARED_FF ** -0.5 * DEEPNORM_BETA)
    ln2_g = 1.0 + nrm(ks[18], (DEPTH, D_MODEL), 0.02)
    ln2_b = nrm(ks[19], (DEPTH, D_MODEL), 0.02)
    return {"x": x, "w_in": w_in, "conv_w": conv_w, "a_log": a_log, "dt_bias": dt_bias,
            "dn_norm_g": dn_norm_g, "sinks": sinks, "w_out": w_out, "ln1_g": ln1_g, "ln1_b": ln1_b,
            "router_w": router_w, "router_bias": router_bias, "w_gate": w_gate, "w_up": w_up,
            "w_down": w_down, "shared_w_gate": shared_w_gate, "shared_w_up": shared_w_up,
            "shared_w_down": shared_w_down, "ln2_g": ln2_g, "ln2_b": ln2_b}


def reference(x, w_in, conv_w, a_log, dt_bias, dn_norm_g, sinks, w_out, ln1_g, ln1_b,
              router_w, router_bias, w_gate, w_up, w_down, shared_w_gate, shared_w_up,
              shared_w_down, ln2_g, ln2_b):
    for l in range(DEPTH):
        mix = hybrid_token_mixer(x, w_in[l], conv_w[l], a_log[l], dt_bias[l], dn_norm_g[l], sinks[l], w_out[l])
        x = layer_norm(DEEPNORM_ALPHA * x + mix, ln1_g[l], ln1_b[l])
        ffn = moe_ffn(x, router_w[l], router_bias[l], w_gate[l], w_up[l], w_down[l],
                      shared_w_gate[l], shared_w_up[l], shared_w_down[l])
        x = layer_norm(DEEPNORM_ALPHA * x + ffn, ln2_g[l], ln2_b[l])
    return x
```

```python
import functools

import jax
import jax.numpy as jnp
from jax import lax
from jax.experimental import pallas as pl
from jax.experimental.pallas import tpu as pltpu

F32 = jnp.float32
BF16 = jnp.bfloat16
I32 = jnp.int32
U32 = jnp.uint32

D_MODEL = 1024
DN_HEADS = 4
DN_HEAD_DIM = 128
DN_WIDTH = DN_HEADS * DN_HEAD_DIM
CONV_WIDTH = 4
DN_CHUNK = 64
SWA_Q_HEADS = 8
SWA_KV_HEADS = 2
SWA_HEAD_DIM = 64
SWA_WIDTH = SWA_Q_HEADS * SWA_HEAD_DIM
SWA_KV_WIDTH = SWA_KV_HEADS * SWA_HEAD_DIM
SWA_WINDOW = 128
SWA_BLOCK = 128
N_EXPERTS = 256
N_GROUPS = 8
GROUP_SIZE = N_EXPERTS // N_GROUPS
TOPK_GROUPS = 4
TOP_K = 8
EXPERT_FF = 256
SHARED_FF = 256
ROUTED_SCALE = 2.5
DEEPNORM_ALPHA = 2.0 ** 0.25
LN_EPS = 1e-5
RMS_EPS = 1e-6
L2_EPS = 1e-6

COL_DNQ = 0
COL_DNK = DN_WIDTH
COL_DNV = 2 * DN_WIDTH
COL_Z = 3 * DN_WIDTH
COL_SQ = 4 * DN_WIDTH
COL_SK = COL_SQ + SWA_WIDTH
COL_SV = COL_SK + SWA_KV_WIDTH
MAIN_COLS = COL_SV + SWA_KV_WIDTH
GATE_COLS = 128

TM_PROJ = 512
TS_DN = 512
TM_POST = 512
TT_ROUTE = 512
BM_EXP = 256
TT_COMB = 256
VMEM_LIMIT = 56 * 1024 * 1024
NEG_INF = float("-inf")


def _dot(a, b):
    return jnp.dot(a, b, preferred_element_type=F32)


def _mm(a, b):
    return _dot(a.astype(BF16), b.astype(BF16))


def _mm_nt(a, b):
    return lax.dot_general(a.astype(BF16), b.astype(BF16), (((1,), (1,)), ((), ())),
                           preferred_element_type=F32)


def _mm_tn(a, b):
    return lax.dot_general(a.astype(BF16), b.astype(BF16), (((0,), (0,)), ((), ())),
                           preferred_element_type=F32)


def _split2(a):
    hi = a.astype(BF16)
    lo = (a - hi.astype(F32)).astype(BF16)
    return hi, lo


def _mm3(a, b):
    ah, al = _split2(a)
    bh, bl = _split2(b)
    return _dot(ah, bh) + _dot(ah, bl) + _dot(al, bh)


def _mm_exact_lhs(l_bf16, g):
    g1 = g.astype(BF16)
    r1 = g - g1.astype(F32)
    g2 = r1.astype(BF16)
    g3 = (r1 - g2.astype(F32)).astype(BF16)
    return _dot(l_bf16, g1) + _dot(l_bf16, g2) + _dot(l_bf16, g3)


def _sigmoid(x):
    return 1.0 / (1.0 + jnp.exp(-x))


def _silu(x):
    return x * _sigmoid(x)


def _in_proj_kernel(x_ref, w_ref, wg_ref, main_ref, gates_ref):
    xb = x_ref[...].astype(BF16)
    main_ref[...] = _dot(xb, w_ref[...]).astype(BF16)
    gates_ref[...] = _dot(xb, wg_ref[...])


def _in_proj(x2d, w_main, w_gates):
    n = x2d.shape[0]
    return pl.pallas_call(
        _in_proj_kernel,
        grid=(n // TM_PROJ,),
        in_specs=[
            pl.BlockSpec((TM_PROJ, D_MODEL), lambda i: (i, 0)),
            pl.BlockSpec((D_MODEL, MAIN_COLS), lambda i: (0, 0)),
            pl.BlockSpec((D_MODEL, GATE_COLS), lambda i: (0, 0)),
        ],
        out_specs=[
            pl.BlockSpec((TM_PROJ, MAIN_COLS), lambda i: (i, 0)),
            pl.BlockSpec((TM_PROJ, GATE_COLS), lambda i: (i, 0)),
        ],
        out_shape=[
            jax.ShapeDtypeStruct((n, MAIN_COLS), BF16),
            jax.ShapeDtypeStruct((n, GATE_COLS), F32),
        ],
        compiler_params=pltpu.CompilerParams(
            dimension_semantics=("parallel",), vmem_limit_bytes=VMEM_LIMIT),
        name="in_proj",
    )(x2d, w_main, w_gates)


def _dn_kernel(x_ref, gates_ref, convw_ref, gpar_ref, normg_ref, out_ref,
               xc_ref, qkv_ref, gl_ref, s_ref, hist_ref):
    ts = x_ref.shape[0]
    c = DN_CHUNK
    hd = DN_HEAD_DIM
    qkv_w = 3 * DN_WIDTH

    @pl.when(pl.program_id(1) == 0)
    def _():
        s_ref[...] = jnp.zeros_like(s_ref)
        hist_ref[...] = jnp.zeros_like(hist_ref)

    xin = x_ref[:, 0:qkv_w].astype(F32)
    xc_ref[0:8, :] = hist_ref[...]
    xc_ref[8:ts + 8, :] = xin
    hist_ref[...] = xin[ts - 8:ts, :]
    w = convw_ref[...]
    y = w[CONV_WIDTH - 1:CONV_WIDTH, :] * xin
    for j in range(CONV_WIDTH - 1):
        off = 8 - (CONV_WIDTH - 1) + j
        y = y + w[j:j + 1, :] * xc_ref[off:off + ts, :]
    y = _silu(y)

    for h in range(DN_HEADS):
        for base, scale in ((COL_DNQ, hd ** -0.5), (COL_DNK, 1.0)):
            lo = base + h * hd
            t = y[:, lo:lo + hd]
            inv = lax.rsqrt(jnp.sum(t * t, axis=-1, keepdims=True) + L2_EPS)
            qkv_ref[:, lo:lo + hd] = t * (inv * scale)
    qkv_ref[:, COL_DNV:COL_DNV + DN_WIDTH] = y[:, COL_DNV:COL_DNV + DN_WIDTH]

    gsl = gates_ref[...]
    a_log = gpar_ref[0:1, :]
    dt_b = gpar_ref[1:2, :]
    sp_in = gsl + dt_b
    softplus = jnp.maximum(sp_in, 0.0) + jnp.log(1.0 + jnp.exp(-jnp.abs(sp_in)))
    lane = lax.broadcasted_iota(I32, gsl.shape, 1)
    gl_ref[...] = jnp.where(lane < DN_HEADS, _sigmoid(gsl), -jnp.exp(a_log) * softplus)

    ii = lax.broadcasted_iota(I32, (c, c), 0)
    jj = lax.broadcasted_iota(I32, (c, c), 1)
    tri_incl = ii >= jj
    tri_strict = ii > jj
    l_incl = jnp.where(tri_incl, 1.0, 0.0).astype(BF16)
    eye = jnp.where(ii == jj, 1.0, 0.0).astype(F32)
    normg = normg_ref[...]

    def chunk_body(ci, carry):
        r0 = pl.multiple_of(ci * c, c)
        rows = pl.ds(r0, c)
        gl = gl_ref[rows, :]
        for h in range(DN_HEADS):
            q = qkv_ref[rows, COL_DNQ + h * hd:COL_DNQ + (h + 1) * hd]
            k = qkv_ref[rows, COL_DNK + h * hd:COL_DNK + (h + 1) * hd]
            v = qkv_ref[rows, COL_DNV + h * hd:COL_DNV + (h + 1) * hd]
            beta = gl[:, h:h + 1]
            g = gl[:, DN_HEADS + h:DN_HEADS + h + 1]
            g_masked = jnp.where(tri_strict, jnp.broadcast_to(g, (c, c)), 0.0)
            rdiff = _mm_exact_lhs(l_incl, g_masked)
            gcb = _mm_exact_lhs(l_incl, jnp.broadcast_to(g, (c, hd)))
            decay = jnp.where(tri_incl, jnp.exp(jnp.where(tri_incl, rdiff, 0.0)), 0.0)
            egc = jnp.exp(gcb)
            g_last = gcb[c - 1:c, :]
            e_tail = jnp.exp(g_last - gcb)
            kb = k * beta
            vb = v * beta
            a_mat = jnp.where(tri_strict, _mm_nt(kb, k) * decay, 0.0)
            attn = _mm_nt(q, k) * decay
            t_inv = eye - a_mat
            p = a_mat
            for _ in range(5):
                p = _mm3(p, p)
                t_inv = t_inv + _mm3(t_inv, p)
            u = _mm3(t_inv, vb)
            wmat = _mm3(t_inv, kb * egc)
            s_old = s_ref[h]
            v_new = u - _mm(wmat, s_old)
            o = _mm(q * egc, s_old) + _mm(attn, v_new)
            s_ref[h] = s_old * jnp.exp(g_last) + _mm_tn(k * e_tail, v_new)
            o = o * lax.rsqrt(jnp.mean(o * o, axis=-1, keepdims=True) + RMS_EPS) * normg
            z = x_ref[rows, COL_Z + h * hd:COL_Z + (h + 1) * hd].astype(F32)
            out_ref[rows, h * hd:(h + 1) * hd] = (o * _silu(z)).astype(out_ref.dtype)
        return carry

    lax.fori_loop(0, ts // c, chunk_body, 0)


def _deltanet(main3d, gates3d, conv_w, gpar, normg):
    b, t, _ = main3d.shape
    ts = TS_DN
    dn_in = COL_Z + DN_WIDTH
    return pl.pallas_call(
        _dn_kernel,
        grid=(b, t // ts),
        in_specs=[
            pl.BlockSpec((None, ts, dn_in), lambda bi, si: (bi, si, 0)),
            pl.BlockSpec((None, ts, GATE_COLS), lambda bi, si: (bi, si, 0)),
            pl.BlockSpec((CONV_WIDTH, 3 * DN_WIDTH), lambda bi, si: (0, 0)),
            pl.BlockSpec((2, GATE_COLS), lambda bi, si: (0, 0)),
            pl.BlockSpec((1, DN_HEAD_DIM), lambda bi, si: (0, 0)),
        ],
        out_specs=pl.BlockSpec((None, ts, DN_WIDTH), lambda bi, si: (bi, si, 0)),
        out_shape=jax.ShapeDtypeStruct((b, t, DN_WIDTH), BF16),
        scratch_shapes=[
            pltpu.VMEM((ts + 8, 3 * DN_WIDTH), F32),
            pltpu.VMEM((ts, 3 * DN_WIDTH), F32),
            pltpu.VMEM((ts, GATE_COLS), F32),
            pltpu.VMEM((DN_HEADS, DN_HEAD_DIM, DN_HEAD_DIM), F32),
            pltpu.VMEM((8, 3 * DN_WIDTH), F32),
        ],
        compiler_params=pltpu.CompilerParams(
            dimension_semantics=("parallel", "arbitrary"), vmem_limit_bytes=VMEM_LIMIT),
        name="deltanet",
    )(main3d, gates3d, conv_w, gpar, normg)


def _swa_kernel(sinks_ref, q_ref, kp_ref, kc_ref, vp_ref, vc_ref, out_ref):
    n = pl.program_id(1)
    blk = SWA_BLOCK
    d = SWA_HEAD_DIM
    grp = SWA_Q_HEADS // SWA_KV_HEADS
    qi = lax.broadcasted_iota(I32, (blk, 2 * blk), 0)
    kj = lax.broadcasted_iota(I32, (blk, 2 * blk), 1)
    dist = qi + blk - kj
    valid = (dist >= 0) & (dist < SWA_WINDOW) & ((kj >= blk) | (n > 0))
    dist_f = dist.astype(F32)
    outs = []
    for hq in range(SWA_Q_HEADS):
        hk = hq // grp
        q = q_ref[:, hq * d:(hq + 1) * d]
        kband = jnp.concatenate([kp_ref[:, hk * d:(hk + 1) * d], kc_ref[:, hk * d:(hk + 1) * d]], axis=0)
        vband = jnp.concatenate([vp_ref[:, hk * d:(hk + 1) * d], vc_ref[:, hk * d:(hk + 1) * d]], axis=0)
        slope = 2.0 ** (-8.0 * (hq + 1.0) / SWA_Q_HEADS)
        s = _mm_nt(q, kband) * (d ** -0.5) - slope * dist_f
        s = jnp.where(valid, s, NEG_INF)
        sink = sinks_ref[hq]
        m = jnp.maximum(jnp.max(s, axis=-1, keepdims=True), sink)
        p = jnp.exp(s - m)
        denom = jnp.sum(p, axis=-1, keepdims=True) + jnp.exp(sink - m)
        outs.append(_mm(p, vband) / denom)
    out_ref[...] = jnp.concatenate(outs, axis=-1).astype(out_ref.dtype)


def _swa(main3d, sinks):
    b, t, _ = main3d.shape
    blk = SWA_BLOCK
    qb = COL_SQ // SWA_WIDTH
    kb = COL_SK // SWA_KV_WIDTH
    vb = COL_SV // SWA_KV_WIDTH
    grid_spec = pltpu.PrefetchScalarGridSpec(
        num_scalar_prefetch=1,
        grid=(b, t // blk),
        in_specs=[
            pl.BlockSpec((None, blk, SWA_WIDTH), lambda bi, ni, s: (bi, ni, qb)),
            pl.BlockSpec((None, blk, SWA_KV_WIDTH), lambda bi, ni, s: (bi, jnp.maximum(ni - 1, 0), kb)),
            pl.BlockSpec((None, blk, SWA_KV_WIDTH), lambda bi, ni, s: (bi, ni, kb)),
            pl.BlockSpec((None, blk, SWA_KV_WIDTH), lambda bi, ni, s: (bi, jnp.maximum(ni - 1, 0), vb)),
            pl.BlockSpec((None, blk, SWA_KV_WIDTH), lambda bi, ni, s: (bi, ni, vb)),
        ],
        out_specs=pl.BlockSpec((None, blk, SWA_WIDTH), lambda bi, ni, s: (bi, ni, 0)),
    )
    return pl.pallas_call(
        _swa_kernel,
        grid_spec=grid_spec,
        out_shape=jax.ShapeDtypeStruct((b, t, SWA_WIDTH), BF16),
        compiler_params=pltpu.CompilerParams(
            dimension_semantics=("parallel", "parallel"), vmem_limit_bytes=VMEM_LIMIT),
        name="swa",
    )(sinks, main3d, main3d, main3d, main3d, main3d)


def _layer_norm(y, g, b):
    mu = jnp.mean(y, axis=-1, keepdims=True)
    yc = y - mu
    var = jnp.mean(yc * yc, axis=-1, keepdims=True)
    return yc * lax.rsqrt(var + LN_EPS) * g + b


def _pack_bf16_pair(lo, hi):
    lo_bits = lax.bitcast_convert_type(lo.astype(BF16).astype(F32), U32)
    hi_bits = lax.bitcast_convert_type(hi.astype(BF16).astype(F32), U32)
    return (hi_bits & jnp.uint32(0xFFFF0000)) | (lo_bits >> 16)


def _unpack_bf16_pair(packed):
    lo = lax.bitcast_convert_type(packed << 16, F32)
    hi = lax.bitcast_convert_type(packed & jnp.uint32(0xFFFF0000), F32)
    return lo, hi


def _post_mix_kernel(x_ref, dn_ref, swa_ref, wo_dn_ref, wo_swa_ref, g_ref, b_ref,
                     rwh_ref, rwl_ref, sg_ref, su_ref, sd_ref,
                     base_ref, xpk_ref, logit_ref):
    mix = _dot(dn_ref[...], wo_dn_ref[...]) + _dot(swa_ref[...], wo_swa_ref[...])
    x1 = _layer_norm(DEEPNORM_ALPHA * x_ref[...] + mix, g_ref[...], b_ref[...])
    half = D_MODEL // 2
    xpk_ref[...] = _pack_bf16_pair(x1[:, :half], x1[:, half:])
    xh, xl = _split2(x1)
    nt = (((1,), (1,)), ((), ()))
    logit_ref[...] = (lax.dot_general(rwh_ref[...], xh, nt, preferred_element_type=F32)
                      + lax.dot_general(rwh_ref[...], xl, nt, preferred_element_type=F32)
                      + lax.dot_general(rwl_ref[...], xh, nt, preferred_element_type=F32))
    hmid = _silu(_dot(xh, sg_ref[...])) * _dot(xh, su_ref[...])
    base_ref[...] = DEEPNORM_ALPHA * x1 + _dot(hmid.astype(BF16), sd_ref[...])


def _post_mix(x2d, dn2d, swa2d, wo_dn, wo_swa, ln_g, ln_b, rwh, rwl, sg, su, sd):
    n = x2d.shape[0]
    tm = TM_POST
    full = lambda shape: pl.BlockSpec(shape, lambda i: (0, 0))
    return pl.pallas_call(
        _post_mix_kernel,
        grid=(n // tm,),
        in_specs=[
            pl.BlockSpec((tm, D_MODEL), lambda i: (i, 0)),
            pl.BlockSpec((tm, DN_WIDTH), lambda i: (i, 0)),
            pl.BlockSpec((tm, SWA_WIDTH), lambda i: (i, 0)),
            full((DN_WIDTH, D_MODEL)), full((SWA_WIDTH, D_MODEL)),
            full((1, D_MODEL)), full((1, D_MODEL)),
            full((N_EXPERTS, D_MODEL)), full((N_EXPERTS, D_MODEL)),
            full((D_MODEL, SHARED_FF)), full((D_MODEL, SHARED_FF)), full((SHARED_FF, D_MODEL)),
        ],
        out_specs=[
            pl.BlockSpec((tm, D_MODEL), lambda i: (i, 0)),
            pl.BlockSpec((tm, D_MODEL // 2), lambda i: (i, 0)),
            pl.BlockSpec((N_EXPERTS, tm), lambda i: (0, i)),
        ],
        out_shape=[
            jax.ShapeDtypeStruct((n, D_MODEL), F32),
            jax.ShapeDtypeStruct((n, D_MODEL // 2), U32),
            jax.ShapeDtypeStruct((N_EXPERTS, n), F32),
        ],
        compiler_params=pltpu.CompilerParams(
            dimension_semantics=("parallel",), vmem_limit_bytes=VMEM_LIMIT),
        name="post_mix",
    )(x2d, dn2d, swa2d, wo_dn, wo_swa, ln_g, ln_b, rwh, rwl, sg, su, sd)


def _route_kernel(lg_ref, bias_ref, eidx_ref, gate_ref, rank_ref, cnt_ref, carry_ref):
    @pl.when(pl.program_id(0) == 0)
    def _():
        carry_ref[...] = jnp.zeros_like(carry_ref)

    tt = lg_ref.shape[1]
    scores = _sigmoid(lg_ref[...])
    sel = scores + bias_ref[...]

    iog = lax.broadcasted_iota(I32, (GROUP_SIZE, tt), 0)
    grp_rows = []
    for g in range(N_GROUPS):
        blk = sel[g * GROUP_SIZE:(g + 1) * GROUP_SIZE, :]
        m1 = jnp.max(blk, axis=0, keepdims=True)
        i1 = jnp.min(jnp.where(blk == m1, iog, GROUP_SIZE), axis=0, keepdims=True)
        m2 = jnp.max(jnp.where(iog == i1, NEG_INF, blk), axis=0, keepdims=True)
        grp_rows.append(m1 + m2)
    gs = jnp.concatenate(grp_rows, axis=0)

    io8 = lax.broadcasted_iota(I32, (N_GROUPS, tt), 0)
    gsel = jnp.zeros((N_GROUPS, tt), F32)
    for _ in range(TOPK_GROUPS):
        mg = jnp.max(gs, axis=0, keepdims=True)
        ig = jnp.min(jnp.where(gs == mg, io8, N_GROUPS), axis=0, keepdims=True)
        hit = io8 == ig
        gsel = jnp.where(hit, 1.0, gsel)
        gs = jnp.where(hit, NEG_INF, gs)

    val = jnp.concatenate(
        [jnp.where(gsel[g:g + 1, :] > 0.0, sel[g * GROUP_SIZE:(g + 1) * GROUP_SIZE, :], NEG_INF)
         for g in range(N_GROUPS)], axis=0)

    ioe = lax.broadcasted_iota(I32, (N_EXPERTS, tt), 0)
    onehot = jnp.zeros((N_EXPERTS, tt), F32)
    idx_rows, gate_rows = [], []
    for _ in range(TOP_K):
        m = jnp.max(val, axis=0, keepdims=True)
        ik = jnp.min(jnp.where(val == m, ioe, N_EXPERTS), axis=0, keepdims=True)
        hit = ioe == ik
        gate_rows.append(jnp.sum(jnp.where(hit, scores, 0.0), axis=0, keepdims=True))
        idx_rows.append(ik)
        val = jnp.where(hit, NEG_INF, val)
        onehot = jnp.where(hit, 1.0, onehot)
    gsum = gate_rows[0]
    for r in gate_rows[1:]:
        gsum = gsum + r
    gate_ref[...] = jnp.concatenate(gate_rows, axis=0) / gsum * ROUTED_SCALE
    eidx_ref[...] = jnp.concatenate(idx_rows, axis=0)

    ti = lax.broadcasted_iota(I32, (tt, tt), 0)
    tj = lax.broadcasted_iota(I32, (tt, tt), 1)
    upper = jnp.where(ti < tj, 1.0, 0.0).astype(BF16)
    cum = _dot(onehot.astype(BF16), upper) + jnp.broadcast_to(carry_ref[:, 0:1], (N_EXPERTS, tt))
    rank_rows = [jnp.sum(jnp.where(ioe == ik, cum, 0.0), axis=0, keepdims=True) for ik in idx_rows]
    rank_ref[...] = jnp.concatenate(rank_rows, axis=0).astype(I32)
    carry_ref[...] = carry_ref[...] + jnp.broadcast_to(
        jnp.sum(onehot, axis=1, keepdims=True), carry_ref.shape)
    cnt_ref[...] = carry_ref[...].astype(I32)


def _route(logits_t, bias_col):
    n = logits_t.shape[1]
    tt = TT_ROUTE
    row_spec = pl.BlockSpec((TOP_K, tt), lambda i: (0, i))
    return pl.pallas_call(
        _route_kernel,
        grid=(n // tt,),
        in_specs=[
            pl.BlockSpec((N_EXPERTS, tt), lambda i: (0, i)),
            pl.BlockSpec((N_EXPERTS, 1), lambda i: (0, 0)),
        ],
        out_specs=[row_spec, row_spec, row_spec,
                   pl.BlockSpec((N_EXPERTS, 128), lambda i: (0, 0))],
        out_shape=[
            jax.ShapeDtypeStruct((TOP_K, n), I32),
            jax.ShapeDtypeStruct((TOP_K, n), F32),
            jax.ShapeDtypeStruct((TOP_K, n), I32),
            jax.ShapeDtypeStruct((N_EXPERTS, 128), I32),
        ],
        scratch_shapes=[pltpu.VMEM((N_EXPERTS, 128), F32)],
        compiler_params=pltpu.CompilerParams(
            dimension_semantics=("arbitrary",), vmem_limit_bytes=VMEM_LIMIT),
        name="route",
    )(logits_t, bias_col)


def _expert_kernel(blk_e_ref, rt_cur_ref, rt_nxt_ref, xpk_hbm, wg_ref, wu_ref, wd_ref, y_ref,
                   buf_ref, sem_ref):
    i = pl.program_id(0)
    nblk = pl.num_programs(0)
    bm = buf_ref.shape[1]
    slot = i % 2

    def row_copy(tok, dst_slot, r):
        return pltpu.make_async_copy(
            xpk_hbm.at[pl.ds(tok, 1), :], buf_ref.at[dst_slot, pl.ds(r, 1), :], sem_ref.at[dst_slot])

    def issue(rt_ref, dst_slot):
        def body(r, carry):
            row_copy(rt_ref[0, r], dst_slot, r).start()
            return carry
        lax.fori_loop(0, bm, body, 0, unroll=8)

    @pl.when(i == 0)
    def _():
        issue(rt_cur_ref, 0)

    @pl.when(i + 1 < nblk)
    def _():
        issue(rt_nxt_ref, 1 - slot)

    pltpu.make_async_copy(xpk_hbm.at[pl.ds(0, bm), :], buf_ref.at[slot], sem_ref.at[slot]).wait()

    half = D_MODEL // 2
    x_lo, x_hi = _unpack_bf16_pair(buf_ref[slot])
    x_lo = x_lo.astype(BF16)
    x_hi = x_hi.astype(BF16)
    wg = wg_ref[...].astype(BF16)
    wu = wu_ref[...].astype(BF16)
    gate = _dot(x_lo, wg[:half]) + _dot(x_hi, wg[half:])
    up = _dot(x_lo, wu[:half]) + _dot(x_hi, wu[half:])
    hmid = (_silu(gate) * up).astype(BF16)
    y = _dot(hmid, wd_ref[...].astype(BF16))
    y_ref[...] = _pack_bf16_pair(y[:, :half], y[:, half:])


def _experts(blk_e, row_tok3d, xpk, w_gate, w_up, w_down):
    nblk = row_tok3d.shape[0]
    bm = BM_EXP
    half = D_MODEL // 2
    grid_spec = pltpu.PrefetchScalarGridSpec(
        num_scalar_prefetch=1,
        grid=(nblk,),
        in_specs=[
            pl.BlockSpec((None, 1, bm), lambda i, be: (i, 0, 0), memory_space=pltpu.SMEM),
            pl.BlockSpec((None, 1, bm), lambda i, be: (jnp.minimum(i + 1, nblk - 1), 0, 0),
                         memory_space=pltpu.SMEM),
            pl.BlockSpec(memory_space=pl.ANY),
            pl.BlockSpec((None, D_MODEL, EXPERT_FF), lambda i, be: (be[i], 0, 0)),
            pl.BlockSpec((None, D_MODEL, EXPERT_FF), lambda i, be: (be[i], 0, 0)),
            pl.BlockSpec((None, EXPERT_FF, D_MODEL), lambda i, be: (be[i], 0, 0)),
        ],
        out_specs=pl.BlockSpec((bm, half), lambda i, be: (i, 0)),
        scratch_shapes=[
            pltpu.VMEM((2, bm, half), U32),
            pltpu.SemaphoreType.DMA((2,)),
        ],
    )
    return pl.pallas_call(
        _expert_kernel,
        grid_spec=grid_spec,
        out_shape=jax.ShapeDtypeStruct((nblk * bm, half), U32),
        compiler_params=pltpu.CompilerParams(
            dimension_semantics=("arbitrary",), vmem_limit_bytes=VMEM_LIMIT),
        name="experts",
    )(blk_e, row_tok3d, row_tok3d, xpk, w_gate, w_up, w_down)


def _combine_kernel(dst_cur_ref, dst_nxt_ref, ypk_hbm, base_ref, gate_ref, g_ref, b_ref, out_ref,
                    buf_ref, sem_ref):
    i = pl.program_id(0)
    nsteps = pl.num_programs(0)
    tt = buf_ref.shape[2]
    slot = i % 2

    def issue(dst_ref, dst_slot):
        for k in range(TOP_K):
            def body(t, carry, k=k):
                pltpu.make_async_copy(
                    ypk_hbm.at[pl.ds(dst_ref[k, t], 1), :],
                    buf_ref.at[dst_slot, k, pl.ds(t, 1), :],
                    sem_ref.at[dst_slot]).start()
                return carry
            lax.fori_loop(0, tt, body, 0, unroll=8)

    @pl.when(i == 0)
    def _():
        issue(dst_cur_ref, 0)

    @pl.when(i + 1 < nsteps)
    def _():
        issue(dst_nxt_ref, 1 - slot)

    for k in range(TOP_K):
        pltpu.make_async_copy(ypk_hbm.at[pl.ds(0, tt), :], buf_ref.at[slot, k], sem_ref.at[slot]).wait()

    half = D_MODEL // 2
    gates = gate_ref[...]
    acc_lo = base_ref[:, :half]
    acc_hi = base_ref[:, half:]
    for k in range(TOP_K):
        y_lo, y_hi = _unpack_bf16_pair(buf_ref[slot, k])
        gk = gates[:, k:k + 1]
        acc_lo = acc_lo + gk * y_lo
        acc_hi = acc_hi + gk * y_hi
    mu = (jnp.sum(acc_lo, axis=-1, keepdims=True) + jnp.sum(acc_hi, axis=-1, keepdims=True)) / D_MODEL
    c_lo = acc_lo - mu
    c_hi = acc_hi - mu
    var = (jnp.sum(c_lo * c_lo, axis=-1, keepdims=True)
           + jnp.sum(c_hi * c_hi, axis=-1, keepdims=True)) / D_MODEL
    inv = lax.rsqrt(var + LN_EPS)
    out_ref[:, :half] = c_lo * inv * g_ref[:, :half] + b_ref[:, :half]
    out_ref[:, half:] = c_hi * inv * g_ref[:, half:] + b_ref[:, half:]


def _combine(dest, ypk, base, gate_tok, ln_g, ln_b):
    n = base.shape[0]
    tt = TT_COMB
    half = D_MODEL // 2
    nsteps = n // tt
    return pl.pallas_call(
        _combine_kernel,
        grid=(nsteps,),
        in_specs=[
            pl.BlockSpec((TOP_K, tt), lambda i: (0, i), memory_space=pltpu.SMEM),
            pl.BlockSpec((TOP_K, tt), lambda i: (0, jnp.minimum(i + 1, nsteps - 1)),
                         memory_space=pltpu.SMEM),
            pl.BlockSpec(memory_space=pl.ANY),
            pl.BlockSpec((tt, D_MODEL), lambda i: (i, 0)),
            pl.BlockSpec((tt, TOP_K), lambda i: (i, 0)),
            pl.BlockSpec((1, D_MODEL), lambda i: (0, 0)),
            pl.BlockSpec((1, D_MODEL), lambda i: (0, 0)),
        ],
        out_specs=pl.BlockSpec((tt, D_MODEL), lambda i: (i, 0)),
        out_shape=jax.ShapeDtypeStruct((n, D_MODEL), F32),
        scratch_shapes=[
            pltpu.VMEM((2, TOP_K, tt, half), U32),
            pltpu.SemaphoreType.DMA((2,)),
        ],
        compiler_params=pltpu.CompilerParams(
            dimension_semantics=("arbitrary",), vmem_limit_bytes=VMEM_LIMIT),
        name="combine",
    )(dest, dest, ypk, base, gate_tok, ln_g, ln_b)


def _regroup_w_in(w_in):
    o = 0
    cols = {}
    for name, width in (("dnq", DN_WIDTH), ("dnk", DN_WIDTH), ("dnv", DN_WIDTH), ("sq", SWA_WIDTH),
                        ("sk", SWA_KV_WIDTH), ("sv", SWA_KV_WIDTH), ("z", DN_WIDTH),
                        ("b", DN_HEADS), ("a", DN_HEADS)):
        cols[name] = w_in[:, o:o + width]
        o += width
    w_main = jnp.concatenate([cols[k] for k in ("dnq", "dnk", "dnv", "z", "sq", "sk", "sv")], axis=1)
    w_gates = jnp.concatenate(
        [cols["b"], cols["a"], jnp.zeros((D_MODEL, GATE_COLS - 2 * DN_HEADS), w_in.dtype)], axis=1)
    return w_main.astype(BF16), w_gates.astype(BF16)


def _layer(x, w_in, conv_w, a_log, dt_bias, dn_norm_g, sinks, w_out, ln1_g, ln1_b,
           router_w, router_bias, w_gate, w_up, w_down, sh_gate, sh_up, sh_down, ln2_g, ln2_b):
    b, t, d = x.shape
    n = b * t
    x2d = x.reshape(n, d)

    w_main, w_gates = _regroup_w_in(w_in)
    main, gates = _in_proj(x2d, w_main, w_gates)
    main3d = main.reshape(b, t, MAIN_COLS)

    pad = jnp.zeros((GATE_COLS - 2 * DN_HEADS,), F32)
    gpar = jnp.stack([jnp.concatenate([jnp.zeros((DN_HEADS,), F32), a_log.astype(F32), pad]),
                      jnp.concatenate([jnp.zeros((DN_HEADS,), F32), dt_bias.astype(F32), pad])])
    dn_out = _deltanet(main3d, gates.reshape(b, t, GATE_COLS), conv_w.astype(F32), gpar,
                       dn_norm_g.astype(F32).reshape(1, DN_HEAD_DIM))
    swa_out = _swa(main3d, sinks.astype(F32))

    rw_t = router_w.T.astype(F32)
    rwh = rw_t.astype(BF16)
    rwl = (rw_t - rwh.astype(F32)).astype(BF16)
    base, xpk, logits_t = _post_mix(
        x2d, dn_out.reshape(n, DN_WIDTH), swa_out.reshape(n, SWA_WIDTH),
        w_out[:DN_WIDTH].astype(BF16), w_out[DN_WIDTH:].astype(BF16),
        ln1_g.reshape(1, d).astype(F32), ln1_b.reshape(1, d).astype(F32), rwh, rwl,
        sh_gate.astype(BF16), sh_up.astype(BF16), sh_down.astype(BF16))

    eidx, gate, rank, cnt = _route(logits_t, router_bias.astype(F32).reshape(N_EXPERTS, 1))

    bm = BM_EXP
    counts = cnt[:, 0]
    padded = (counts + bm - 1) // bm * bm
    pend = jnp.cumsum(padded)
    pstart = pend - padded
    dest = pstart[eidx] + rank
    nblk = -(-(n * TOP_K) // bm) + N_EXPERTS
    tok = jnp.broadcast_to(jnp.arange(n, dtype=I32)[None, :], (TOP_K, n))
    row_tok = jnp.zeros((nblk * bm,), I32).at[dest.reshape(-1)].set(tok.reshape(-1))
    blk_e = jnp.clip(jnp.searchsorted(pend, jnp.arange(nblk, dtype=I32) * bm, side="right"),
                     0, N_EXPERTS - 1).astype(I32)

    ypk = _experts(blk_e, row_tok.reshape(nblk, 1, bm), xpk, w_gate, w_up, w_down)
    out = _combine(dest, ypk, base, gate.T, ln2_g.reshape(1, d).astype(F32), ln2_b.reshape(1, d).astype(F32))
    return out.reshape(b, t, d)


def kernel(x, w_in, conv_w, a_log, dt_bias, dn_norm_g, sinks, w_out, ln1_g, ln1_b, router_w, router_bias,
           w_gate, w_up, w_down, shared_w_gate, shared_w_up, shared_w_down, ln2_g, ln2_b):
    depth = w_in.shape[0]
    for l in range(depth):
        x = _layer(x, w_in[l], conv_w[l], a_log[l], dt_bias[l], dn_norm_g[l], sinks[l], w_out[l],
                   ln1_g[l], ln1_b[l], router_w[l], router_bias[l], w_gate[l], w_up[l], w_down[l],
                   shared_w_gate[l], shared_w_up[l], shared_w_down[l], ln2_g[l], ln2_b[l])
    return x
```

```python
import functools

import jax
import jax.numpy as jnp
from jax import lax
from jax.experimental import pallas as pl
from jax.experimental.pallas import tpu as pltpu

F32 = jnp.float32
BF16 = jnp.bfloat16
I32 = jnp.int32
U32 = jnp.uint32

D_MODEL = 1024
DN_HEADS = 4
DN_HEAD_DIM = 128
DN_WIDTH = DN_HEADS * DN_HEAD_DIM
CONV_WIDTH = 4
DN_CHUNK = 64
SWA_Q_HEADS = 8
SWA_KV_HEADS = 2
SWA_HEAD_DIM = 64
SWA_WIDTH = SWA_Q_HEADS * SWA_HEAD_DIM
SWA_KV_WIDTH = SWA_KV_HEADS * SWA_HEAD_DIM
SWA_WINDOW = 128
SWA_BLOCK = 128
N_EXPERTS = 256
N_GROUPS = 8
GROUP_SIZE = N_EXPERTS // N_GROUPS
TOPK_GROUPS = 4
TOP_K = 8
EXPERT_FF = 256
SHARED_FF = 256
ROUTED_SCALE = 2.5
DEEPNORM_ALPHA = 2.0 ** 0.25
LN_EPS = 1e-5
RMS_EPS = 1e-6
L2_EPS = 1e-6

COL_DNQ = 0
COL_DNK = DN_WIDTH
COL_DNV = 2 * DN_WIDTH
COL_Z = 3 * DN_WIDTH
COL_SQ = 4 * DN_WIDTH
COL_SK = COL_SQ + SWA_WIDTH
COL_SV = COL_SK + SWA_KV_WIDTH
MAIN_COLS = COL_SV + SWA_KV_WIDTH
GATE_COLS = 128

TM_PROJ = 512
TS_DN = 512
DN_A_UNROLL = 2
TM_POST = 512
TT_ROUTE = 512
BM_EXP = 256
TT_DISP = 256
TT_COMB = 256
VMEM_LIMIT = 56 * 1024 * 1024
NEG_INF = float("-inf")


def _dot(a, b):
    return jnp.dot(a, b, preferred_element_type=F32)


def _mm(a, b):
    return _dot(a.astype(BF16), b.astype(BF16))


def _mm_nt(a, b):
    return lax.dot_general(a.astype(BF16), b.astype(BF16), (((1,), (1,)), ((), ())),
                           preferred_element_type=F32)


def _mm_tn(a, b):
    return lax.dot_general(a.astype(BF16), b.astype(BF16), (((0,), (0,)), ((), ())),
                           preferred_element_type=F32)


def _split2(a):
    hi = a.astype(BF16)
    lo = (a - hi.astype(F32)).astype(BF16)
    return hi, lo


def _mm3(a, b):
    ah, al = _split2(a)
    bh, bl = _split2(b)
    return _dot(ah, bh) + _dot(ah, bl) + _dot(al, bh)


def _mm_exact_lhs(l_bf16, g):
    g1 = g.astype(BF16)
    r1 = g - g1.astype(F32)
    g2 = r1.astype(BF16)
    g3 = (r1 - g2.astype(F32)).astype(BF16)
    return _dot(l_bf16, g1) + _dot(l_bf16, g2) + _dot(l_bf16, g3)


def _sigmoid(x):
    return 1.0 / (1.0 + jnp.exp(-x))


def _silu(x):
    return x * _sigmoid(x)


def _in_proj_kernel(x_ref, w_ref, wg_ref, main_ref, gates_ref):
    xb = x_ref[...].astype(BF16)
    main_ref[...] = _dot(xb, w_ref[...]).astype(BF16)
    gates_ref[...] = _dot(xb, wg_ref[...])


def _in_proj(x2d, w_main, w_gates):
    n = x2d.shape[0]
    return pl.pallas_call(
        _in_proj_kernel,
        grid=(n // TM_PROJ,),
        in_specs=[
            pl.BlockSpec((TM_PROJ, D_MODEL), lambda i: (i, 0)),
            pl.BlockSpec((D_MODEL, MAIN_COLS), lambda i: (0, 0)),
            pl.BlockSpec((D_MODEL, GATE_COLS), lambda i: (0, 0)),
        ],
        out_specs=[
            pl.BlockSpec((TM_PROJ, MAIN_COLS), lambda i: (i, 0)),
            pl.BlockSpec((TM_PROJ, GATE_COLS), lambda i: (i, 0)),
        ],
        out_shape=[
            jax.ShapeDtypeStruct((n, MAIN_COLS), BF16),
            jax.ShapeDtypeStruct((n, GATE_COLS), F32),
        ],
        compiler_params=pltpu.CompilerParams(
            dimension_semantics=("parallel",), vmem_limit_bytes=VMEM_LIMIT),
        name="in_proj",
    )(x2d, w_main, w_gates)


def _dn_kernel(x_ref, gates_ref, convw_ref, gpar_ref, normg_ref, out_ref,
               xc_ref, gl_ref, gc_ref, wq_ref, u_ref, kt_ref, attn_ref, egl_ref, s_ref, hist_ref):
    ts = x_ref.shape[0]
    c = DN_CHUNK
    hd = DN_HEAD_DIM
    qkv_w = 3 * DN_WIDTH
    nch = ts // c

    @pl.when(pl.program_id(1) == 0)
    def _():
        s_ref[...] = jnp.zeros_like(s_ref)
        hist_ref[...] = jnp.zeros_like(hist_ref)

    xc_ref[0:8, :] = hist_ref[...]
    xc_ref[8:ts + 8, :] = x_ref[:, 0:qkv_w].astype(F32)
    hist_ref[...] = xc_ref[ts:ts + 8, :]

    gsl = gates_ref[...]
    sp_in = gsl + gpar_ref[1:2, :]
    softplus = jnp.maximum(sp_in, 0.0) + jnp.log(1.0 + jnp.exp(-jnp.abs(sp_in)))
    lane = lax.broadcasted_iota(I32, gsl.shape, 1)
    gl = jnp.where(lane < DN_HEADS, _sigmoid(gsl), -jnp.exp(gpar_ref[0:1, :]) * softplus)
    gl_ref[...] = gl
    row_in_chunk = lax.broadcasted_iota(I32, gsl.shape, 0) % c
    gc = gl
    shift = 1
    while shift < c:
        gc = gc + jnp.where(row_in_chunk >= shift, pltpu.roll(gc, shift, 0), 0.0)
        shift *= 2
    gc_ref[...] = gc

    ii = lax.broadcasted_iota(I32, (c, c), 0)
    jj = lax.broadcasted_iota(I32, (c, c), 1)
    tri_incl = ii >= jj
    tri_strict = ii > jj
    eye = jnp.where(ii == jj, 1.0, 0.0).astype(F32)
    heads = range(DN_HEADS)

    def conv_silu(r0, col):
        xt = xc_ref[pl.ds(r0, c + 8), col:col + hd]
        w = convw_ref[:, col:col + hd]
        y = w[CONV_WIDTH - 1:CONV_WIDTH, :] * xt[8:8 + c, :]
        for j in range(CONV_WIDTH - 1):
            off = 8 - (CONV_WIDTH - 1) + j
            y = y + w[j:j + 1, :] * xt[off:off + c, :]
        return _silu(y)

    def l2n(t, scale):
        return t * (lax.rsqrt(jnp.sum(t * t, axis=-1, keepdims=True) + L2_EPS) * scale)

    def phase_a(i, carry):
        chains = []
        for sub in range(DN_A_UNROLL):
            ci = i * DN_A_UNROLL + sub
            r0 = pl.multiple_of(ci * c, c)
            glc = gl_ref[pl.ds(r0, c), :]
            gcc = gc_ref[pl.ds(r0, c), :]
            gct = jnp.concatenate([gcc, gcc], axis=0).T
            egl_ref[ci] = jnp.exp(gcc[c - 8:c, :])
            for h in heads:
                chains.append((ci, r0, h, glc, gcc, gct))
        nchain = len(chains)
        q = [l2n(conv_silu(r0, COL_DNQ + h * hd), hd ** -0.5) for (ci, r0, h, _, _, _) in chains]
        k = [l2n(conv_silu(r0, COL_DNK + h * hd), 1.0) for (ci, r0, h, _, _, _) in chains]
        v = [conv_silu(r0, COL_DNV + h * hd) for (ci, r0, h, _, _, _) in chains]
        kb, vb, decay, egc = [], [], [], []
        for n_, (ci, r0, h, glc, gcc, gct) in enumerate(chains):
            beta = glc[:, h:h + 1]
            gc_col = gcc[:, DN_HEADS + h:DN_HEADS + h + 1]
            gc_row = gct[DN_HEADS + h:DN_HEADS + h + 1, 0:c]
            decay.append(jnp.where(tri_incl, jnp.exp(jnp.minimum(gc_col - gc_row, 0.0)), 0.0))
            egc.append(jnp.exp(gc_col))
            e_tail = jnp.exp(gcc[c - 1:c, DN_HEADS + h:DN_HEADS + h + 1] - gc_col)
            kb.append(k[n_] * beta)
            vb.append(v[n_] * beta)
            kt_ref[ci, h] = (k[n_] * e_tail).astype(BF16)
        kq = [_mm_nt(jnp.concatenate([kb[n_], q[n_]], axis=0), k[n_]) for n_ in range(nchain)]
        a_mat = [jnp.where(tri_strict, kq[n_][0:c] * decay[n_], 0.0) for n_ in range(nchain)]
        for n_, (ci, r0, h, _, _, _) in enumerate(chains):
            attn_ref[ci, h] = (kq[n_][c:2 * c] * decay[n_]).astype(BF16)
        t_inv = [eye - a for a in a_mat]
        p = a_mat
        for _ in range(5):
            p = [_mm(x, x) for x in p]
            t_inv = [t + _mm(t, x) for t, x in zip(t_inv, p)]
        for n_, (ci, r0, h, _, _, _) in enumerate(chains):
            uw = _mm3(t_inv[n_], jnp.concatenate([vb[n_], kb[n_] * egc[n_]], axis=1))
            u_ref[ci, h] = uw[:, 0:hd]
            wq_ref[ci, h, 0:c, :] = uw[:, hd:2 * hd].astype(BF16)
            wq_ref[ci, h, c:2 * c, :] = (q[n_] * egc[n_]).astype(BF16)
        return carry

    lax.fori_loop(0, nch // DN_A_UNROLL, phase_a, 0)

    normg = normg_ref[...]

    def phase_b(ci, carry):
        r0 = pl.multiple_of(ci * c, c)
        rows = pl.ds(r0, c)
        egl = egl_ref[ci]
        s_old = [s_ref[h] for h in heads]
        ws = [_dot(wq_ref[ci, h], s_old[h].astype(BF16)) for h in heads]
        v_new = [(u_ref[ci, h] - ws[h][0:c]).astype(BF16) for h in heads]
        for h in heads:
            s_ref[h] = (s_old[h] * egl[7:8, DN_HEADS + h:DN_HEADS + h + 1]
                        + lax.dot_general(kt_ref[ci, h], v_new[h], (((0,), (0,)), ((), ())),
                                          preferred_element_type=F32))
        for h in heads:
            o = ws[h][c:2 * c] + _dot(attn_ref[ci, h], v_new[h])
            o = o * lax.rsqrt(jnp.mean(o * o, axis=-1, keepdims=True) + RMS_EPS) * normg
            z = x_ref[rows, COL_Z + h * hd:COL_Z + (h + 1) * hd].astype(F32)
            out_ref[rows, h * hd:(h + 1) * hd] = (o * _silu(z)).astype(out_ref.dtype)
        return carry

    lax.fori_loop(0, nch, phase_b, 0)


def _deltanet(main3d, gates3d, conv_w, gpar, normg):
    b, t, _ = main3d.shape
    ts = TS_DN
    nch = ts // DN_CHUNK
    dn_in = COL_Z + DN_WIDTH
    return pl.pallas_call(
        _dn_kernel,
        grid=(b, t // ts),
        in_specs=[
            pl.BlockSpec((None, ts, dn_in), lambda bi, si: (bi, si, 0)),
            pl.BlockSpec((None, ts, GATE_COLS), lambda bi, si: (bi, si, 0)),
            pl.BlockSpec((CONV_WIDTH, 3 * DN_WIDTH), lambda bi, si: (0, 0)),
            pl.BlockSpec((2, GATE_COLS), lambda bi, si: (0, 0)),
            pl.BlockSpec((1, DN_HEAD_DIM), lambda bi, si: (0, 0)),
        ],
        out_specs=pl.BlockSpec((None, ts, DN_WIDTH), lambda bi, si: (bi, si, 0)),
        out_shape=jax.ShapeDtypeStruct((b, t, DN_WIDTH), BF16),
        scratch_shapes=[
            pltpu.VMEM((ts + 8, 3 * DN_WIDTH), F32),
            pltpu.VMEM((ts, GATE_COLS), F32),
            pltpu.VMEM((ts, GATE_COLS), F32),
            pltpu.VMEM((nch, DN_HEADS, 2 * DN_CHUNK, DN_HEAD_DIM), BF16),
            pltpu.VMEM((nch, DN_HEADS, DN_CHUNK, DN_HEAD_DIM), F32),
            pltpu.VMEM((nch, DN_HEADS, DN_CHUNK, DN_HEAD_DIM), BF16),
            pltpu.VMEM((nch, DN_HEADS, DN_CHUNK, DN_CHUNK), BF16),
            pltpu.VMEM((nch, 8, GATE_COLS), F32),
            pltpu.VMEM((DN_HEADS, DN_HEAD_DIM, DN_HEAD_DIM), F32),
            pltpu.VMEM((8, 3 * DN_WIDTH), F32),
        ],
        compiler_params=pltpu.CompilerParams(
            dimension_semantics=("parallel", "arbitrary"), vmem_limit_bytes=VMEM_LIMIT),
        name="deltanet",
    )(main3d, gates3d, conv_w, gpar, normg)


def _swa_kernel(sinks_ref, q_ref, kp_ref, kc_ref, vp_ref, vc_ref, out_ref):
    n = pl.program_id(1)
    blk = SWA_BLOCK
    d = SWA_HEAD_DIM
    grp = SWA_Q_HEADS // SWA_KV_HEADS
    qi = lax.broadcasted_iota(I32, (blk, 2 * blk), 0)
    kj = lax.broadcasted_iota(I32, (blk, 2 * blk), 1)
    dist = qi + blk - kj
    valid = (dist >= 0) & (dist < SWA_WINDOW) & ((kj >= blk) | (n > 0))
    dist_f = dist.astype(F32)
    outs = []
    for hq in range(SWA_Q_HEADS):
        hk = hq // grp
        q = q_ref[:, hq * d:(hq + 1) * d]
        kband = jnp.concatenate([kp_ref[:, hk * d:(hk + 1) * d], kc_ref[:, hk * d:(hk + 1) * d]], axis=0)
        vband = jnp.concatenate([vp_ref[:, hk * d:(hk + 1) * d], vc_ref[:, hk * d:(hk + 1) * d]], axis=0)
        slope = 2.0 ** (-8.0 * (hq + 1.0) / SWA_Q_HEADS)
        s = _mm_nt(q, kband) * (d ** -0.5) - slope * dist_f
        s = jnp.where(valid, s, NEG_INF)
        sink = sinks_ref[hq]
        m = jnp.maximum(jnp.max(s, axis=-1, keepdims=True), sink)
        p = jnp.exp(s - m)
        denom = jnp.sum(p, axis=-1, keepdims=True) + jnp.exp(sink - m)
        outs.append(_mm(p, vband) / denom)
    out_ref[...] = jnp.concatenate(outs, axis=-1).astype(out_ref.dtype)


def _swa(main3d, sinks):
    b, t, _ = main3d.shape
    blk = SWA_BLOCK
    qb = COL_SQ // SWA_WIDTH
    kb = COL_SK // SWA_KV_WIDTH
    vb = COL_SV // SWA_KV_WIDTH
    grid_spec = pltpu.PrefetchScalarGridSpec(
        num_scalar_prefetch=1,
        grid=(b, t // blk),
        in_specs=[
            pl.BlockSpec((None, blk, SWA_WIDTH), lambda bi, ni, s: (bi, ni, qb)),
            pl.BlockSpec((None, blk, SWA_KV_WIDTH), lambda bi, ni, s: (bi, jnp.maximum(ni - 1, 0), kb)),
            pl.BlockSpec((None, blk, SWA_KV_WIDTH), lambda bi, ni, s: (bi, ni, kb)),
            pl.BlockSpec((None, blk, SWA_KV_WIDTH), lambda bi, ni, s: (bi, jnp.maximum(ni - 1, 0), vb)),
            pl.BlockSpec((None, blk, SWA_KV_WIDTH), lambda bi, ni, s: (bi, ni, vb)),
        ],
        out_specs=pl.BlockSpec((None, blk, SWA_WIDTH), lambda bi, ni, s: (bi, ni, 0)),
    )
    return pl.pallas_call(
        _swa_kernel,
        grid_spec=grid_spec,
        out_shape=jax.ShapeDtypeStruct((b, t, SWA_WIDTH), BF16),
        compiler_params=pltpu.CompilerParams(
            dimension_semantics=("parallel", "parallel"), vmem_limit_bytes=VMEM_LIMIT),
        name="swa",
    )(sinks, main3d, main3d, main3d, main3d, main3d)


def _layer_norm(y, g, b):
    mu = jnp.mean(y, axis=-1, keepdims=True)
    yc = y - mu
    var = jnp.mean(yc * yc, axis=-1, keepdims=True)
    return yc * lax.rsqrt(var + LN_EPS) * g + b


def _pack_bf16_pair(lo, hi):
    lo_bits = lax.bitcast_convert_type(lo.astype(BF16).astype(F32), U32)
    hi_bits = lax.bitcast_convert_type(hi.astype(BF16).astype(F32), U32)
    return (hi_bits & jnp.uint32(0xFFFF0000)) | (lo_bits >> 16)


def _unpack_bf16_pair(packed):
    lo = lax.bitcast_convert_type(packed << 16, F32)
    hi = lax.bitcast_convert_type(packed & jnp.uint32(0xFFFF0000), F32)
    return lo, hi


def _post_mix_kernel(x_ref, dn_ref, swa_ref, wo_dn_ref, wo_swa_ref, g_ref, b_ref,
                     rwh_ref, rwl_ref, sg_ref, su_ref, sd_ref,
                     base_ref, xpk_ref, logit_ref):
    mix = _dot(dn_ref[...], wo_dn_ref[...]) + _dot(swa_ref[...], wo_swa_ref[...])
    x1 = _layer_norm(DEEPNORM_ALPHA * x_ref[...] + mix, g_ref[...], b_ref[...])
    half = D_MODEL // 2
    xpk_ref[...] = _pack_bf16_pair(x1[:, :half], x1[:, half:])
    xh, xl = _split2(x1)
    nt = (((1,), (1,)), ((), ()))
    logit_ref[...] = (lax.dot_general(rwh_ref[...], xh, nt, preferred_element_type=F32)
                      + lax.dot_general(rwh_ref[...], xl, nt, preferred_element_type=F32)
                      + lax.dot_general(rwl_ref[...], xh, nt, preferred_element_type=F32))
    hmid = _silu(_dot(xh, sg_ref[...])) * _dot(xh, su_ref[...])
    base_ref[...] = DEEPNORM_ALPHA * x1 + _dot(hmid.astype(BF16), sd_ref[...])


def _post_mix(x2d, dn2d, swa2d, wo_dn, wo_swa, ln_g, ln_b, rwh, rwl, sg, su, sd):
    n = x2d.shape[0]
    tm = TM_POST
    full = lambda shape: pl.BlockSpec(shape, lambda i: (0, 0))
    return pl.pallas_call(
        _post_mix_kernel,
        grid=(n // tm,),
        in_specs=[
            pl.BlockSpec((tm, D_MODEL), lambda i: (i, 0)),
            pl.BlockSpec((tm, DN_WIDTH), lambda i: (i, 0)),
            pl.BlockSpec((tm, SWA_WIDTH), lambda i: (i, 0)),
            full((DN_WIDTH, D_MODEL)), full((SWA_WIDTH, D_MODEL)),
            full((1, D_MODEL)), full((1, D_MODEL)),
            full((N_EXPERTS, D_MODEL)), full((N_EXPERTS, D_MODEL)),
            full((D_MODEL, SHARED_FF)), full((D_MODEL, SHARED_FF)), full((SHARED_FF, D_MODEL)),
        ],
        out_specs=[
            pl.BlockSpec((tm, D_MODEL), lambda i: (i, 0)),
            pl.BlockSpec((tm, D_MODEL // 2), lambda i: (i, 0)),
            pl.BlockSpec((N_EXPERTS, tm), lambda i: (0, i)),
        ],
        out_shape=[
            jax.ShapeDtypeStruct((n, D_MODEL), F32),
            jax.ShapeDtypeStruct((n, D_MODEL // 2), U32),
            jax.ShapeDtypeStruct((N_EXPERTS, n), F32),
        ],
        compiler_params=pltpu.CompilerParams(
            dimension_semantics=("parallel",), vmem_limit_bytes=VMEM_LIMIT),
        name="post_mix",
    )(x2d, dn2d, swa2d, wo_dn, wo_swa, ln_g, ln_b, rwh, rwl, sg, su, sd)


def _route_kernel(lg_ref, bias_ref, eidx_ref, gate_ref, rank_ref, cnt_ref, carry_ref):
    @pl.when(pl.program_id(0) == 0)
    def _():
        carry_ref[...] = jnp.zeros_like(carry_ref)

    tt = lg_ref.shape[1]
    scores = _sigmoid(lg_ref[...])
    sel = scores + bias_ref[...]

    iog = lax.broadcasted_iota(I32, (GROUP_SIZE, tt), 0)
    grp_rows = []
    for g in range(N_GROUPS):
        blk = sel[g * GROUP_SIZE:(g + 1) * GROUP_SIZE, :]
        m1 = jnp.max(blk, axis=0, keepdims=True)
        i1 = jnp.min(jnp.where(blk == m1, iog, GROUP_SIZE), axis=0, keepdims=True)
        m2 = jnp.max(jnp.where(iog == i1, NEG_INF, blk), axis=0, keepdims=True)
        grp_rows.append(m1 + m2)
    gs = jnp.concatenate(grp_rows, axis=0)

    io8 = lax.broadcasted_iota(I32, (N_GROUPS, tt), 0)
    gsel = jnp.zeros((N_GROUPS, tt), F32)
    for _ in range(TOPK_GROUPS):
        mg = jnp.max(gs, axis=0, keepdims=True)
        ig = jnp.min(jnp.where(gs == mg, io8, N_GROUPS), axis=0, keepdims=True)
        hit = io8 == ig
        gsel = jnp.where(hit, 1.0, gsel)
        gs = jnp.where(hit, NEG_INF, gs)

    val = jnp.concatenate(
        [jnp.where(gsel[g:g + 1, :] > 0.0, sel[g * GROUP_SIZE:(g + 1) * GROUP_SIZE, :], NEG_INF)
         for g in range(N_GROUPS)], axis=0)

    ioe = lax.broadcasted_iota(I32, (N_EXPERTS, tt), 0)
    onehot = jnp.zeros((N_EXPERTS, tt), F32)
    idx_rows, gate_rows = [], []
    for _ in range(TOP_K):
        m = jnp.max(val, axis=0, keepdims=True)
        ik = jnp.min(jnp.where(val == m, ioe, N_EXPERTS), axis=0, keepdims=True)
        hit = ioe == ik
        gate_rows.append(jnp.sum(jnp.where(hit, scores, 0.0), axis=0, keepdims=True))
        idx_rows.append(ik)
        val = jnp.where(hit, NEG_INF, val)
        onehot = jnp.where(hit, 1.0, onehot)
    gsum = gate_rows[0]
    for r in gate_rows[1:]:
        gsum = gsum + r
    gate_ref[...] = jnp.concatenate(gate_rows, axis=0) / gsum * ROUTED_SCALE
    eidx_ref[...] = jnp.concatenate(idx_rows, axis=0)

    ti = lax.broadcasted_iota(I32, (tt, tt), 0)
    tj = lax.broadcasted_iota(I32, (tt, tt), 1)
    upper = jnp.where(ti < tj, 1.0, 0.0).astype(BF16)
    cum = _dot(onehot.astype(BF16), upper) + jnp.broadcast_to(carry_ref[:, 0:1], (N_EXPERTS, tt))
    rank_rows = [jnp.sum(jnp.where(ioe == ik, cum, 0.0), axis=0, keepdims=True) for ik in idx_rows]
    rank_ref[...] = jnp.concatenate(rank_rows, axis=0).astype(I32)
    carry_ref[...] = carry_ref[...] + jnp.broadcast_to(
        jnp.sum(onehot, axis=1, keepdims=True), carry_ref.shape)
    cnt_ref[...] = carry_ref[...].astype(I32)


def _route(logits_t, bias_col):
    n = logits_t.shape[1]
    tt = TT_ROUTE
    row_spec = pl.BlockSpec((TOP_K, tt), lambda i: (0, i))
    return pl.pallas_call(
        _route_kernel,
        grid=(n // tt,),
        in_specs=[
            pl.BlockSpec((N_EXPERTS, tt), lambda i: (0, i)),
            pl.BlockSpec((N_EXPERTS, 1), lambda i: (0, 0)),
        ],
        out_specs=[row_spec, row_spec, row_spec,
                   pl.BlockSpec((N_EXPERTS, 128), lambda i: (0, 0))],
        out_shape=[
            jax.ShapeDtypeStruct((TOP_K, n), I32),
            jax.ShapeDtypeStruct((TOP_K, n), F32),
            jax.ShapeDtypeStruct((TOP_K, n), I32),
            jax.ShapeDtypeStruct((N_EXPERTS, 128), I32),
        ],
        scratch_shapes=[pltpu.VMEM((N_EXPERTS, 128), F32)],
        compiler_params=pltpu.CompilerParams(
            dimension_semantics=("arbitrary",), vmem_limit_bytes=VMEM_LIMIT),
        name="route",
    )(logits_t, bias_col)


def _place_kernel(eidx_ref, rank_ref, pstart_ref, dest_ref):
    tt = eidx_ref.shape[1]
    ioe = lax.broadcasted_iota(I32, (N_EXPERTS, tt), 0)
    pstart = pstart_ref[...]
    rows = [jnp.sum(jnp.where(ioe == eidx_ref[k:k + 1, :], pstart, 0.0), axis=0, keepdims=True)
            for k in range(TOP_K)]
    dest_ref[...] = jnp.concatenate(rows, axis=0).astype(I32) + rank_ref[...]


def _place(eidx, rank, pstart_col):
    n = eidx.shape[1]
    tt = TT_ROUTE
    row_spec = pl.BlockSpec((TOP_K, tt), lambda i: (0, i))
    return pl.pallas_call(
        _place_kernel,
        grid=(n // tt,),
        in_specs=[row_spec, row_spec, pl.BlockSpec((N_EXPERTS, 1), lambda i: (0, 0))],
        out_specs=row_spec,
        out_shape=jax.ShapeDtypeStruct((TOP_K, n), I32),
        compiler_params=pltpu.CompilerParams(
            dimension_semantics=("parallel",), vmem_limit_bytes=VMEM_LIMIT),
        name="place",
    )(eidx, rank, pstart_col)


def _dispatch_kernel(dst_ref, x_ref, zeros_hbm, xs_hbm, xbuf_ref, sem_ref):
    del zeros_hbm
    i = pl.program_id(0)
    nsteps = pl.num_programs(0)
    tt = x_ref.shape[0]
    slot = i % 2
    xbuf_ref[slot] = x_ref[...]
    for k in range(TOP_K):
        def body(t, carry, k=k):
            pltpu.make_async_copy(
                xbuf_ref.at[slot, pl.ds(t, 1), :], xs_hbm.at[pl.ds(dst_ref[k, t], 1), :],
                sem_ref.at[slot]).start()
            return carry
        lax.fori_loop(0, tt, body, 0, unroll=8)

    def wait_slot(s):
        for _ in range(TOP_K):
            pltpu.make_async_copy(xbuf_ref.at[s], xs_hbm.at[pl.ds(0, tt), :], sem_ref.at[s]).wait()

    @pl.when(i > 0)
    def _():
        wait_slot(1 - slot)

    @pl.when(i == nsteps - 1)
    def _():
        wait_slot(slot)


def _dispatch(dest, xpk, nrows):
    n, half = xpk.shape
    tt = TT_DISP
    return pl.pallas_call(
        _dispatch_kernel,
        grid=(n // tt,),
        in_specs=[
            pl.BlockSpec((TOP_K, tt), lambda i: (0, i), memory_space=pltpu.SMEM),
            pl.BlockSpec((tt, half), lambda i: (i, 0)),
            pl.BlockSpec(memory_space=pl.ANY),
        ],
        out_specs=pl.BlockSpec(memory_space=pl.ANY),
        out_shape=jax.ShapeDtypeStruct((nrows, half), U32),
        scratch_shapes=[
            pltpu.VMEM((2, tt, half), U32),
            pltpu.SemaphoreType.DMA((2,)),
        ],
        input_output_aliases={2: 0},
        compiler_params=pltpu.CompilerParams(
            dimension_semantics=("arbitrary",), vmem_limit_bytes=VMEM_LIMIT),
        name="dispatch",
    )(dest, xpk, jnp.zeros((nrows, half), U32))


def _expert_kernel(blk_e_ref, nused_ref, xs_ref, wg_ref, wu_ref, wd_ref, y_ref):
    i = pl.program_id(0)

    @pl.when(i >= nused_ref[0])
    def _():
        y_ref[...] = jnp.zeros_like(y_ref)

    @pl.when(i < nused_ref[0])
    def _():
        half = D_MODEL // 2
        x_lo, x_hi = _unpack_bf16_pair(xs_ref[...])
        x_lo = x_lo.astype(BF16)
        x_hi = x_hi.astype(BF16)
        wg = wg_ref[...].astype(BF16)
        wu = wu_ref[...].astype(BF16)
        gate = _dot(x_lo, wg[:half]) + _dot(x_hi, wg[half:])
        up = _dot(x_lo, wu[:half]) + _dot(x_hi, wu[half:])
        hmid = (_silu(gate) * up).astype(BF16)
        y = _dot(hmid, wd_ref[...].astype(BF16))
        y_ref[...] = _pack_bf16_pair(y[:, :half], y[:, half:])


def _experts(blk_e, nused, xs, w_gate, w_up, w_down):
    bm = BM_EXP
    nblk = xs.shape[0] // bm
    half = D_MODEL // 2

    def last_used(i, nu):
        return jnp.minimum(i, nu[0] - 1)

    grid_spec = pltpu.PrefetchScalarGridSpec(
        num_scalar_prefetch=2,
        grid=(nblk,),
        in_specs=[
            pl.BlockSpec((bm, half), lambda i, be, nu: (last_used(i, nu), 0)),
            pl.BlockSpec((None, D_MODEL, EXPERT_FF), lambda i, be, nu: (be[last_used(i, nu)], 0, 0)),
            pl.BlockSpec((None, D_MODEL, EXPERT_FF), lambda i, be, nu: (be[last_used(i, nu)], 0, 0)),
            pl.BlockSpec((None, EXPERT_FF, D_MODEL), lambda i, be, nu: (be[last_used(i, nu)], 0, 0)),
        ],
        out_specs=pl.BlockSpec((bm, half), lambda i, be, nu: (i, 0)),
    )
    return pl.pallas_call(
        _expert_kernel,
        grid_spec=grid_spec,
        out_shape=jax.ShapeDtypeStruct((nblk * bm, half), U32),
        compiler_params=pltpu.CompilerParams(
            dimension_semantics=("arbitrary",), vmem_limit_bytes=VMEM_LIMIT),
        name="experts",
    )(blk_e, nused, xs, w_gate, w_up, w_down)


def _combine_kernel(dst_cur_ref, dst_nxt_ref, ypk_hbm, base_ref, gate_ref, g_ref, b_ref, out_ref,
                    buf_ref, sem_ref):
    i = pl.program_id(0)
    nsteps = pl.num_programs(0)
    tt = buf_ref.shape[2]
    slot = i % 2

    def issue(dst_ref, dst_slot):
        for k in range(TOP_K):
            def body(t, carry, k=k):
                pltpu.make_async_copy(
                    ypk_hbm.at[pl.ds(dst_ref[k, t], 1), :],
                    buf_ref.at[dst_slot, k, pl.ds(t, 1), :],
                    sem_ref.at[dst_slot]).start()
                return carry
            lax.fori_loop(0, tt, body, 0, unroll=8)

    @pl.when(i == 0)
    def _():
        issue(dst_cur_ref, 0)

    @pl.when(i + 1 < nsteps)
    def _():
        issue(dst_nxt_ref, 1 - slot)

    for k in range(TOP_K):
        pltpu.make_async_copy(ypk_hbm.at[pl.ds(0, tt), :], buf_ref.at[slot, k], sem_ref.at[slot]).wait()

    half = D_MODEL // 2
    gates = gate_ref[...]
    acc_lo = base_ref[:, :half]
    acc_hi = base_ref[:, half:]
    for k in range(TOP_K):
        y_lo, y_hi = _unpack_bf16_pair(buf_ref[slot, k])
        gk = gates[:, k:k + 1]
        acc_lo = acc_lo + gk * y_lo
        acc_hi = acc_hi + gk * y_hi
    mu = (jnp.sum(acc_lo, axis=-1, keepdims=True) + jnp.sum(acc_hi, axis=-1, keepdims=True)) / D_MODEL
    c_lo = acc_lo - mu
    c_hi = acc_hi - mu
    var = (jnp.sum(c_lo * c_lo, axis=-1, keepdims=True)
           + jnp.sum(c_hi * c_hi, axis=-1, keepdims=True)) / D_MODEL
    inv = lax.rsqrt(var + LN_EPS)
    out_ref[:, :half] = c_lo * inv * g_ref[:, :half] + b_ref[:, :half]
    out_ref[:, half:] = c_hi * inv * g_ref[:, half:] + b_ref[:, half:]


def _combine(dest, ypk, base, gate_tok, ln_g, ln_b):
    n = base.shape[0]
    tt = TT_COMB
    half = D_MODEL // 2
    nsteps = n // tt
    return pl.pallas_call(
        _combine_kernel,
        grid=(nsteps,),
        in_specs=[
            pl.BlockSpec((TOP_K, tt), lambda i: (0, i), memory_space=pltpu.SMEM),
            pl.BlockSpec((TOP_K, tt), lambda i: (0, jnp.minimum(i + 1, nsteps - 1)),
                         memory_space=pltpu.SMEM),
            pl.BlockSpec(memory_space=pl.ANY),
            pl.BlockSpec((tt, D_MODEL), lambda i: (i, 0)),
            pl.BlockSpec((tt, TOP_K), lambda i: (i, 0)),
            pl.BlockSpec((1, D_MODEL), lambda i: (0, 0)),
            pl.BlockSpec((1, D_MODEL), lambda i: (0, 0)),
        ],
        out_specs=pl.BlockSpec((tt, D_MODEL), lambda i: (i, 0)),
        out_shape=jax.ShapeDtypeStruct((n, D_MODEL), F32),
        scratch_shapes=[
            pltpu.VMEM((2, TOP_K, tt, half), U32),
            pltpu.SemaphoreType.DMA((2,)),
        ],
        compiler_params=pltpu.CompilerParams(
            dimension_semantics=("arbitrary",), vmem_limit_bytes=VMEM_LIMIT),
        name="combine",
    )(dest, dest, ypk, base, gate_tok, ln_g, ln_b)


def _regroup_w_in(w_in):
    o = 0
    cols = {}
    for name, width in (("dnq", DN_WIDTH), ("dnk", DN_WIDTH), ("dnv", DN_WIDTH), ("sq", SWA_WIDTH),
                        ("sk", SWA_KV_WIDTH), ("sv", SWA_KV_WIDTH), ("z", DN_WIDTH),
                        ("b", DN_HEADS), ("a", DN_HEADS)):
        cols[name] = w_in[:, o:o + width]
        o += width
    w_main = jnp.concatenate([cols[k] for k in ("dnq", "dnk", "dnv", "z", "sq", "sk", "sv")], axis=1)
    w_gates = jnp.concatenate(
        [cols["b"], cols["a"], jnp.zeros((D_MODEL, GATE_COLS - 2 * DN_HEADS), w_in.dtype)], axis=1)
    return w_main.astype(BF16), w_gates.astype(BF16)


def _layer(x, w_in, conv_w, a_log, dt_bias, dn_norm_g, sinks, w_out, ln1_g, ln1_b,
           router_w, router_bias, w_gate, w_up, w_down, sh_gate, sh_up, sh_down, ln2_g, ln2_b):
    b, t, d = x.shape
    n = b * t
    x2d = x.reshape(n, d)

    w_main, w_gates = _regroup_w_in(w_in)
    main, gates = _in_proj(x2d, w_main, w_gates)
    main3d = main.reshape(b, t, MAIN_COLS)

    pad = jnp.zeros((GATE_COLS - 2 * DN_HEADS,), F32)
    gpar = jnp.stack([jnp.concatenate([jnp.zeros((DN_HEADS,), F32), a_log.astype(F32), pad]),
                      jnp.concatenate([jnp.zeros((DN_HEADS,), F32), dt_bias.astype(F32), pad])])
    dn_out = _deltanet(main3d, gates.reshape(b, t, GATE_COLS), conv_w.astype(F32), gpar,
                       dn_norm_g.astype(F32).reshape(1, DN_HEAD_DIM))
    swa_out = _swa(main3d, sinks.astype(F32))

    rw_t = router_w.T.astype(F32)
    rwh = rw_t.astype(BF16)
    rwl = (rw_t - rwh.astype(F32)).astype(BF16)
    base, xpk, logits_t = _post_mix(
        x2d, dn_out.reshape(n, DN_WIDTH), swa_out.reshape(n, SWA_WIDTH),
        w_out[:DN_WIDTH].astype(BF16), w_out[DN_WIDTH:].astype(BF16),
        ln1_g.reshape(1, d).astype(F32), ln1_b.reshape(1, d).astype(F32), rwh, rwl,
        sh_gate.astype(BF16), sh_up.astype(BF16), sh_down.astype(BF16))

    eidx, gate, rank, cnt = _route(logits_t, router_bias.astype(F32).reshape(N_EXPERTS, 1))

    bm = BM_EXP
    counts = cnt[:, 0]
    padded = (counts + bm - 1) // bm * bm
    pend = jnp.cumsum(padded)
    pstart = pend - padded
    nblk = -(-(n * TOP_K) // bm) + N_EXPERTS
    blk_row0 = jnp.arange(nblk, dtype=I32) * bm
    blk_e = jnp.clip(jnp.searchsorted(pend, blk_row0, side="right"), 0, N_EXPERTS - 1).astype(I32)
    nused = (pend[-1:] // bm).astype(I32)

    dest = _place(eidx, rank, pstart.astype(F32).reshape(N_EXPERTS, 1))
    xs = _dispatch(dest, xpk, nblk * bm)
    ypk = _experts(blk_e, nused, xs, w_gate, w_up, w_down)
    out = _combine(dest, ypk, base, gate.T, ln2_g.reshape(1, d).astype(F32), ln2_b.reshape(1, d).astype(F32))
    return out.reshape(b, t, d)


def kernel(x, w_in, conv_w, a_log, dt_bias, dn_norm_g, sinks, w_out, ln1_g, ln1_b, router_w, router_bias,
           w_gate, w_up, w_down, shared_w_gate, shared_w_up, shared_w_down, ln2_g, ln2_b):
    depth = w_in.shape[0]
    for l in range(depth):
        x = _layer(x, w_in[l], conv_w[l], a_log[l], dt_bias[l], dn_norm_g[l], sinks[l], w_out[l],
                   ln1_g[l], ln1_b[l], router_w[l], router_bias[l], w_gate[l], w_up[l], w_down[l],
                   shared_w_gate[l], shared_w_up[l], shared_w_down[l], ln2_g[l], ln2_b[l])
    return x
```

```python
import functools

import jax
import jax.numpy as jnp
from jax import lax
from jax.experimental import pallas as pl
from jax.experimental.pallas import tpu as pltpu
from jax.experimental.pallas import tpu_sc as plsc

F32 = jnp.float32
BF16 = jnp.bfloat16
I32 = jnp.int32
U32 = jnp.uint32

D_MODEL = 1024
DN_HEADS = 4
DN_HEAD_DIM = 128
DN_WIDTH = DN_HEADS * DN_HEAD_DIM
CONV_WIDTH = 4
DN_CHUNK = 64
SWA_Q_HEADS = 8
SWA_KV_HEADS = 2
SWA_HEAD_DIM = 64
SWA_WIDTH = SWA_Q_HEADS * SWA_HEAD_DIM
SWA_KV_WIDTH = SWA_KV_HEADS * SWA_HEAD_DIM
SWA_WINDOW = 128
SWA_BLOCK = 128
N_EXPERTS = 256
N_GROUPS = 8
GROUP_SIZE = N_EXPERTS // N_GROUPS
TOPK_GROUPS = 4
TOP_K = 8
EXPERT_FF = 256
SHARED_FF = 256
ROUTED_SCALE = 2.5
DEEPNORM_ALPHA = 2.0 ** 0.25
LN_EPS = 1e-5
RMS_EPS = 1e-6
L2_EPS = 1e-6

COL_DNQ = 0
COL_DNK = DN_WIDTH
COL_DNV = 2 * DN_WIDTH
COL_Z = 3 * DN_WIDTH
COL_SQ = 4 * DN_WIDTH
COL_SK = COL_SQ + SWA_WIDTH
COL_SV = COL_SK + SWA_KV_WIDTH
MAIN_COLS = COL_SV + SWA_KV_WIDTH
GATE_COLS = 128

TM_PROJ = 512
TS_DN = 512
DN_A_UNROLL = 2
TM_POST = 512
TT_ROUTE = 512
BM_EXP = 256
TT_DISP = 256
TT_COMB = 256
SC_NC = 2
SC_NS = 16
SC_NW = SC_NC * SC_NS
SC_WIN = 64
VMEM_LIMIT = 56 * 1024 * 1024
NEG_INF = float("-inf")


def _dot(a, b):
    return jnp.dot(a, b, preferred_element_type=F32)


def _mm(a, b):
    return _dot(a.astype(BF16), b.astype(BF16))


def _mm_nt(a, b):
    return lax.dot_general(a.astype(BF16), b.astype(BF16), (((1,), (1,)), ((), ())),
                           preferred_element_type=F32)


def _mm_tn(a, b):
    return lax.dot_general(a.astype(BF16), b.astype(BF16), (((0,), (0,)), ((), ())),
                           preferred_element_type=F32)


def _split2(a):
    hi = a.astype(BF16)
    lo = (a - hi.astype(F32)).astype(BF16)
    return hi, lo


def _mm3(a, b):
    ah, al = _split2(a)
    bh, bl = _split2(b)
    return _dot(ah, bh) + _dot(ah, bl) + _dot(al, bh)


def _mm_exact_lhs(l_bf16, g):
    g1 = g.astype(BF16)
    r1 = g - g1.astype(F32)
    g2 = r1.astype(BF16)
    g3 = (r1 - g2.astype(F32)).astype(BF16)
    return _dot(l_bf16, g1) + _dot(l_bf16, g2) + _dot(l_bf16, g3)


def _sigmoid(x):
    return 1.0 / (1.0 + jnp.exp(-x))


def _silu(x):
    return x * _sigmoid(x)


def _in_proj_kernel(x_ref, w_ref, wg_ref, main_ref, gates_ref):
    xb = x_ref[...].astype(BF16)
    main_ref[...] = _dot(xb, w_ref[...]).astype(BF16)
    gates_ref[...] = _dot(xb, wg_ref[...])


def _in_proj(x2d, w_main, w_gates):
    n = x2d.shape[0]
    return pl.pallas_call(
        _in_proj_kernel,
        grid=(n // TM_PROJ,),
        in_specs=[
            pl.BlockSpec((TM_PROJ, D_MODEL), lambda i: (i, 0)),
            pl.BlockSpec((D_MODEL, MAIN_COLS), lambda i: (0, 0)),
            pl.BlockSpec((D_MODEL, GATE_COLS), lambda i: (0, 0)),
        ],
        out_specs=[
            pl.BlockSpec((TM_PROJ, MAIN_COLS), lambda i: (i, 0)),
            pl.BlockSpec((TM_PROJ, GATE_COLS), lambda i: (i, 0)),
        ],
        out_shape=[
            jax.ShapeDtypeStruct((n, MAIN_COLS), BF16),
            jax.ShapeDtypeStruct((n, GATE_COLS), F32),
        ],
        compiler_params=pltpu.CompilerParams(
            dimension_semantics=("parallel",), vmem_limit_bytes=VMEM_LIMIT),
        name="in_proj",
    )(x2d, w_main, w_gates)


def _dn_kernel(x_ref, gates_ref, convw_ref, gpar_ref, normg_ref, out_ref,
               xc_ref, gl_ref, gc_ref, wq_ref, u_ref, kt_ref, attn_ref, egl_ref, s_ref, hist_ref):
    ts = x_ref.shape[0]
    c = DN_CHUNK
    hd = DN_HEAD_DIM
    qkv_w = 3 * DN_WIDTH
    nch = ts // c

    @pl.when(pl.program_id(1) == 0)
    def _():
        s_ref[...] = jnp.zeros_like(s_ref)
        hist_ref[...] = jnp.zeros_like(hist_ref)

    xc_ref[0:8, :] = hist_ref[...]
    xc_ref[8:ts + 8, :] = x_ref[:, 0:qkv_w].astype(F32)
    hist_ref[...] = xc_ref[ts:ts + 8, :]

    gsl = gates_ref[...]
    sp_in = gsl + gpar_ref[1:2, :]
    softplus = jnp.maximum(sp_in, 0.0) + jnp.log(1.0 + jnp.exp(-jnp.abs(sp_in)))
    lane = lax.broadcasted_iota(I32, gsl.shape, 1)
    gl = jnp.where(lane < DN_HEADS, _sigmoid(gsl), -jnp.exp(gpar_ref[0:1, :]) * softplus)
    gl_ref[...] = gl
    row_in_chunk = lax.broadcasted_iota(I32, gsl.shape, 0) % c
    gc = gl
    shift = 1
    while shift < c:
        gc = gc + jnp.where(row_in_chunk >= shift, pltpu.roll(gc, shift, 0), 0.0)
        shift *= 2
    gc_ref[...] = gc

    ii = lax.broadcasted_iota(I32, (c, c), 0)
    jj = lax.broadcasted_iota(I32, (c, c), 1)
    tri_incl = ii >= jj
    tri_strict = ii > jj
    eye = jnp.where(ii == jj, 1.0, 0.0).astype(F32)
    heads = range(DN_HEADS)

    def conv_silu(r0, col):
        xt = xc_ref[pl.ds(r0, c + 8), col:col + hd]
        w = convw_ref[:, col:col + hd]
        y = w[CONV_WIDTH - 1:CONV_WIDTH, :] * xt[8:8 + c, :]
        for j in range(CONV_WIDTH - 1):
            off = 8 - (CONV_WIDTH - 1) + j
            y = y + w[j:j + 1, :] * xt[off:off + c, :]
        return _silu(y)

    def l2n(t, scale):
        return t * (lax.rsqrt(jnp.sum(t * t, axis=-1, keepdims=True) + L2_EPS) * scale)

    def phase_a(i, carry):
        chains = []
        for sub in range(DN_A_UNROLL):
            ci = i * DN_A_UNROLL + sub
            r0 = pl.multiple_of(ci * c, c)
            glc = gl_ref[pl.ds(r0, c), :]
            gcc = gc_ref[pl.ds(r0, c), :]
            gct = jnp.concatenate([gcc, gcc], axis=0).T
            egl_ref[ci] = jnp.exp(gcc[c - 8:c, :])
            for h in heads:
                chains.append((ci, r0, h, glc, gcc, gct))
        nchain = len(chains)
        q = [l2n(conv_silu(r0, COL_DNQ + h * hd), hd ** -0.5) for (ci, r0, h, _, _, _) in chains]
        k = [l2n(conv_silu(r0, COL_DNK + h * hd), 1.0) for (ci, r0, h, _, _, _) in chains]
        v = [conv_silu(r0, COL_DNV + h * hd) for (ci, r0, h, _, _, _) in chains]
        kb, vb, decay, egc = [], [], [], []
        for n_, (ci, r0, h, glc, gcc, gct) in enumerate(chains):
            beta = glc[:, h:h + 1]
            gc_col = gcc[:, DN_HEADS + h:DN_HEADS + h + 1]
            gc_row = gct[DN_HEADS + h:DN_HEADS + h + 1, 0:c]
            decay.append(jnp.where(tri_incl, jnp.exp(jnp.minimum(gc_col - gc_row, 0.0)), 0.0))
            egc.append(jnp.exp(gc_col))
            e_tail = jnp.exp(gcc[c - 1:c, DN_HEADS + h:DN_HEADS + h + 1] - gc_col)
            kb.append(k[n_] * beta)
            vb.append(v[n_] * beta)
            kt_ref[ci, h] = (k[n_] * e_tail).astype(BF16)
        kq = [_mm_nt(jnp.concatenate([kb[n_], q[n_]], axis=0), k[n_]) for n_ in range(nchain)]
        a_mat = [jnp.where(tri_strict, kq[n_][0:c] * decay[n_], 0.0) for n_ in range(nchain)]
        for n_, (ci, r0, h, _, _, _) in enumerate(chains):
            attn_ref[ci, h] = (kq[n_][c:2 * c] * decay[n_]).astype(BF16)
        t_inv = [eye - a for a in a_mat]
        p = a_mat
        for _ in range(5):
            p = [_mm(x, x) for x in p]
            t_inv = [t + _mm(t, x) for t, x in zip(t_inv, p)]
        for n_, (ci, r0, h, _, _, _) in enumerate(chains):
            uw = _mm3(t_inv[n_], jnp.concatenate([vb[n_], kb[n_] * egc[n_]], axis=1))
            u_ref[ci, h] = uw[:, 0:hd]
            wq_ref[ci, h, 0:c, :] = uw[:, hd:2 * hd].astype(BF16)
            wq_ref[ci, h, c:2 * c, :] = (q[n_] * egc[n_]).astype(BF16)
        return carry

    lax.fori_loop(0, nch // DN_A_UNROLL, phase_a, 0)

    normg = normg_ref[...]

    def phase_b(ci, carry):
        r0 = pl.multiple_of(ci * c, c)
        rows = pl.ds(r0, c)
        egl = egl_ref[ci]
        s_old = [s_ref[h] for h in heads]
        ws = [_dot(wq_ref[ci, h], s_old[h].astype(BF16)) for h in heads]
        v_new = [(u_ref[ci, h] - ws[h][0:c]).astype(BF16) for h in heads]
        for h in heads:
            s_ref[h] = (s_old[h] * egl[7:8, DN_HEADS + h:DN_HEADS + h + 1]
                        + lax.dot_general(kt_ref[ci, h], v_new[h], (((0,), (0,)), ((), ())),
                                          preferred_element_type=F32))
        for h in heads:
            o = ws[h][c:2 * c] + _dot(attn_ref[ci, h], v_new[h])
            o = o * lax.rsqrt(jnp.mean(o * o, axis=-1, keepdims=True) + RMS_EPS) * normg
            z = x_ref[rows, COL_Z + h * hd:COL_Z + (h + 1) * hd].astype(F32)
            out_ref[rows, h * hd:(h + 1) * hd] = (o * _silu(z)).astype(out_ref.dtype)
        return carry

    lax.fori_loop(0, nch, phase_b, 0)


def _deltanet(main3d, gates3d, conv_w, gpar, normg):
    b, t, _ = main3d.shape
    ts = TS_DN
    nch = ts // DN_CHUNK
    dn_in = COL_Z + DN_WIDTH
    return pl.pallas_call(
        _dn_kernel,
        grid=(b, t // ts),
        in_specs=[
            pl.BlockSpec((None, ts, dn_in), lambda bi, si: (bi, si, 0)),
            pl.BlockSpec((None, ts, GATE_COLS), lambda bi, si: (bi, si, 0)),
            pl.BlockSpec((CONV_WIDTH, 3 * DN_WIDTH), lambda bi, si: (0, 0)),
            pl.BlockSpec((2, GATE_COLS), lambda bi, si: (0, 0)),
            pl.BlockSpec((1, DN_HEAD_DIM), lambda bi, si: (0, 0)),
        ],
        out_specs=pl.BlockSpec((None, ts, DN_WIDTH), lambda bi, si: (bi, si, 0)),
        out_shape=jax.ShapeDtypeStruct((b, t, DN_WIDTH), BF16),
        scratch_shapes=[
            pltpu.VMEM((ts + 8, 3 * DN_WIDTH), F32),
            pltpu.VMEM((ts, GATE_COLS), F32),
            pltpu.VMEM((ts, GATE_COLS), F32),
            pltpu.VMEM((nch, DN_HEADS, 2 * DN_CHUNK, DN_HEAD_DIM), BF16),
            pltpu.VMEM((nch, DN_HEADS, DN_CHUNK, DN_HEAD_DIM), F32),
            pltpu.VMEM((nch, DN_HEADS, DN_CHUNK, DN_HEAD_DIM), BF16),
            pltpu.VMEM((nch, DN_HEADS, DN_CHUNK, DN_CHUNK), BF16),
            pltpu.VMEM((nch, 8, GATE_COLS), F32),
            pltpu.VMEM((DN_HEADS, DN_HEAD_DIM, DN_HEAD_DIM), F32),
            pltpu.VMEM((8, 3 * DN_WIDTH), F32),
        ],
        compiler_params=pltpu.CompilerParams(
            dimension_semantics=("parallel", "arbitrary"), vmem_limit_bytes=VMEM_LIMIT),
        name="deltanet",
    )(main3d, gates3d, conv_w, gpar, normg)


def _swa_kernel(sinks_ref, q_ref, kp_ref, kc_ref, vp_ref, vc_ref, out_ref):
    n = pl.program_id(1)
    blk = SWA_BLOCK
    d = SWA_HEAD_DIM
    grp = SWA_Q_HEADS // SWA_KV_HEADS
    qi = lax.broadcasted_iota(I32, (blk, 2 * blk), 0)
    kj = lax.broadcasted_iota(I32, (blk, 2 * blk), 1)
    dist = qi + blk - kj
    valid = (dist >= 0) & (dist < SWA_WINDOW) & ((kj >= blk) | (n > 0))
    dist_f = dist.astype(F32)
    outs = []
    for hq in range(SWA_Q_HEADS):
        hk = hq // grp
        q = q_ref[:, hq * d:(hq + 1) * d]
        kband = jnp.concatenate([kp_ref[:, hk * d:(hk + 1) * d], kc_ref[:, hk * d:(hk + 1) * d]], axis=0)
        vband = jnp.concatenate([vp_ref[:, hk * d:(hk + 1) * d], vc_ref[:, hk * d:(hk + 1) * d]], axis=0)
        slope = 2.0 ** (-8.0 * (hq + 1.0) / SWA_Q_HEADS)
        s = _mm_nt(q, kband) * (d ** -0.5) - slope * dist_f
        s = jnp.where(valid, s, NEG_INF)
        sink = sinks_ref[hq]
        m = jnp.maximum(jnp.max(s, axis=-1, keepdims=True), sink)
        p = jnp.exp(s - m)
        denom = jnp.sum(p, axis=-1, keepdims=True) + jnp.exp(sink - m)
        outs.append(_mm(p, vband) / denom)
    out_ref[...] = jnp.concatenate(outs, axis=-1).astype(out_ref.dtype)


def _swa(main3d, sinks):
    b, t, _ = main3d.shape
    blk = SWA_BLOCK
    qb = COL_SQ // SWA_WIDTH
    kb = COL_SK // SWA_KV_WIDTH
    vb = COL_SV // SWA_KV_WIDTH
    grid_spec = pltpu.PrefetchScalarGridSpec(
        num_scalar_prefetch=1,
        grid=(b, t // blk),
        in_specs=[
            pl.BlockSpec((None, blk, SWA_WIDTH), lambda bi, ni, s: (bi, ni, qb)),
            pl.BlockSpec((None, blk, SWA_KV_WIDTH), lambda bi, ni, s: (bi, jnp.maximum(ni - 1, 0), kb)),
            pl.BlockSpec((None, blk, SWA_KV_WIDTH), lambda bi, ni, s: (bi, ni, kb)),
            pl.BlockSpec((None, blk, SWA_KV_WIDTH), lambda bi, ni, s: (bi, jnp.maximum(ni - 1, 0), vb)),
            pl.BlockSpec((None, blk, SWA_KV_WIDTH), lambda bi, ni, s: (bi, ni, vb)),
        ],
        out_specs=pl.BlockSpec((None, blk, SWA_WIDTH), lambda bi, ni, s: (bi, ni, 0)),
    )
    return pl.pallas_call(
        _swa_kernel,
        grid_spec=grid_spec,
        out_shape=jax.ShapeDtypeStruct((b, t, SWA_WIDTH), BF16),
        compiler_params=pltpu.CompilerParams(
            dimension_semantics=("parallel", "parallel"), vmem_limit_bytes=VMEM_LIMIT),
        name="swa",
    )(sinks, main3d, main3d, main3d, main3d, main3d)


def _layer_norm(y, g, b):
    mu = jnp.mean(y, axis=-1, keepdims=True)
    yc = y - mu
    var = jnp.mean(yc * yc, axis=-1, keepdims=True)
    return yc * lax.rsqrt(var + LN_EPS) * g + b


def _pack_bf16_pair(lo, hi):
    lo_bits = lax.bitcast_convert_type(lo.astype(BF16).astype(F32), U32)
    hi_bits = lax.bitcast_convert_type(hi.astype(BF16).astype(F32), U32)
    return (hi_bits & jnp.uint32(0xFFFF0000)) | (lo_bits >> 16)


def _unpack_bf16_pair(packed):
    lo = lax.bitcast_convert_type(packed << 16, F32)
    hi = lax.bitcast_convert_type(packed & jnp.uint32(0xFFFF0000), F32)
    return lo, hi


def _post_mix_kernel(x_ref, dn_ref, swa_ref, wo_dn_ref, wo_swa_ref, g_ref, b_ref,
                     rwh_ref, rwl_ref, sg_ref, su_ref, sd_ref,
                     base_ref, xpk_ref, logit_ref):
    mix = _dot(dn_ref[...], wo_dn_ref[...]) + _dot(swa_ref[...], wo_swa_ref[...])
    x1 = _layer_norm(DEEPNORM_ALPHA * x_ref[...] + mix, g_ref[...], b_ref[...])
    half = D_MODEL // 2
    xpk_ref[...] = _pack_bf16_pair(x1[:, :half], x1[:, half:])
    xh, xl = _split2(x1)
    nt = (((1,), (1,)), ((), ()))
    logit_ref[...] = (lax.dot_general(rwh_ref[...], xh, nt, preferred_element_type=F32)
                      + lax.dot_general(rwh_ref[...], xl, nt, preferred_element_type=F32)
                      + lax.dot_general(rwl_ref[...], xh, nt, preferred_element_type=F32))
    hmid = _silu(_dot(xh, sg_ref[...])) * _dot(xh, su_ref[...])
    base_ref[...] = DEEPNORM_ALPHA * x1 + _dot(hmid.astype(BF16), sd_ref[...])


def _post_mix(x2d, dn2d, swa2d, wo_dn, wo_swa, ln_g, ln_b, rwh, rwl, sg, su, sd):
    n = x2d.shape[0]
    tm = TM_POST
    full = lambda shape: pl.BlockSpec(shape, lambda i: (0, 0))
    return pl.pallas_call(
        _post_mix_kernel,
        grid=(n // tm,),
        in_specs=[
            pl.BlockSpec((tm, D_MODEL), lambda i: (i, 0)),
            pl.BlockSpec((tm, DN_WIDTH), lambda i: (i, 0)),
            pl.BlockSpec((tm, SWA_WIDTH), lambda i: (i, 0)),
            full((DN_WIDTH, D_MODEL)), full((SWA_WIDTH, D_MODEL)),
            full((1, D_MODEL)), full((1, D_MODEL)),
            full((N_EXPERTS, D_MODEL)), full((N_EXPERTS, D_MODEL)),
            full((D_MODEL, SHARED_FF)), full((D_MODEL, SHARED_FF)), full((SHARED_FF, D_MODEL)),
        ],
        out_specs=[
            pl.BlockSpec((tm, D_MODEL), lambda i: (i, 0)),
            pl.BlockSpec((tm, D_MODEL // 2), lambda i: (i, 0)),
            pl.BlockSpec((N_EXPERTS, tm), lambda i: (0, i)),
        ],
        out_shape=[
            jax.ShapeDtypeStruct((n, D_MODEL), F32),
            jax.ShapeDtypeStruct((n, D_MODEL // 2), U32),
            jax.ShapeDtypeStruct((N_EXPERTS, n), F32),
        ],
        compiler_params=pltpu.CompilerParams(
            dimension_semantics=("parallel",), vmem_limit_bytes=VMEM_LIMIT),
        name="post_mix",
    )(x2d, dn2d, swa2d, wo_dn, wo_swa, ln_g, ln_b, rwh, rwl, sg, su, sd)


def _route_kernel(lg_ref, bias_ref, eidx_ref, gate_ref, rank_ref, cnt_ref, carry_ref):
    @pl.when(pl.program_id(0) == 0)
    def _():
        carry_ref[...] = jnp.zeros_like(carry_ref)

    tt = lg_ref.shape[1]
    scores = _sigmoid(lg_ref[...])
    sel = scores + bias_ref[...]

    iog = lax.broadcasted_iota(I32, (GROUP_SIZE, tt), 0)
    grp_rows = []
    for g in range(N_GROUPS):
        blk = sel[g * GROUP_SIZE:(g + 1) * GROUP_SIZE, :]
        m1 = jnp.max(blk, axis=0, keepdims=True)
        i1 = jnp.min(jnp.where(blk == m1, iog, GROUP_SIZE), axis=0, keepdims=True)
        m2 = jnp.max(jnp.where(iog == i1, NEG_INF, blk), axis=0, keepdims=True)
        grp_rows.append(m1 + m2)
    gs = jnp.concatenate(grp_rows, axis=0)

    io8 = lax.broadcasted_iota(I32, (N_GROUPS, tt), 0)
    gsel = jnp.zeros((N_GROUPS, tt), F32)
    for _ in range(TOPK_GROUPS):
        mg = jnp.max(gs, axis=0, keepdims=True)
        ig = jnp.min(jnp.where(gs == mg, io8, N_GROUPS), axis=0, keepdims=True)
        hit = io8 == ig
        gsel = jnp.where(hit, 1.0, gsel)
        gs = jnp.where(hit, NEG_INF, gs)

    val = jnp.concatenate(
        [jnp.where(gsel[g:g + 1, :] > 0.0, sel[g * GROUP_SIZE:(g + 1) * GROUP_SIZE, :], NEG_INF)
         for g in range(N_GROUPS)], axis=0)

    ioe = lax.broadcasted_iota(I32, (N_EXPERTS, tt), 0)
    onehot = jnp.zeros((N_EXPERTS, tt), F32)
    idx_rows, gate_rows = [], []
    for _ in range(TOP_K):
        m = jnp.max(val, axis=0, keepdims=True)
        ik = jnp.min(jnp.where(val == m, ioe, N_EXPERTS), axis=0, keepdims=True)
        hit = ioe == ik
        gate_rows.append(jnp.sum(jnp.where(hit, scores, 0.0), axis=0, keepdims=True))
        idx_rows.append(ik)
        val = jnp.where(hit, NEG_INF, val)
        onehot = jnp.where(hit, 1.0, onehot)
    gsum = gate_rows[0]
    for r in gate_rows[1:]:
        gsum = gsum + r
    gate_ref[...] = jnp.concatenate(gate_rows, axis=0) / gsum * ROUTED_SCALE
    eidx_ref[...] = jnp.concatenate(idx_rows, axis=0)

    ti = lax.broadcasted_iota(I32, (tt, tt), 0)
    tj = lax.broadcasted_iota(I32, (tt, tt), 1)
    upper = jnp.where(ti < tj, 1.0, 0.0).astype(BF16)
    cum = _dot(onehot.astype(BF16), upper) + jnp.broadcast_to(carry_ref[:, 0:1], (N_EXPERTS, tt))
    rank_rows = [jnp.sum(jnp.where(ioe == ik, cum, 0.0), axis=0, keepdims=True) for ik in idx_rows]
    rank_ref[...] = jnp.concatenate(rank_rows, axis=0).astype(I32)
    carry_ref[...] = carry_ref[...] + jnp.broadcast_to(
        jnp.sum(onehot, axis=1, keepdims=True), carry_ref.shape)
    cnt_ref[...] = carry_ref[...].astype(I32)


def _route(logits_t, bias_col):
    n = logits_t.shape[1]
    tt = TT_ROUTE
    row_spec = pl.BlockSpec((TOP_K, tt), lambda i: (0, i))
    return pl.pallas_call(
        _route_kernel,
        grid=(n // tt,),
        in_specs=[
            pl.BlockSpec((N_EXPERTS, tt), lambda i: (0, i)),
            pl.BlockSpec((N_EXPERTS, 1), lambda i: (0, 0)),
        ],
        out_specs=[row_spec, row_spec, row_spec,
                   pl.BlockSpec((N_EXPERTS, 128), lambda i: (0, 0))],
        out_shape=[
            jax.ShapeDtypeStruct((TOP_K, n), I32),
            jax.ShapeDtypeStruct((TOP_K, n), F32),
            jax.ShapeDtypeStruct((TOP_K, n), I32),
            jax.ShapeDtypeStruct((N_EXPERTS, 128), I32),
        ],
        scratch_shapes=[pltpu.VMEM((N_EXPERTS, 128), F32)],
        compiler_params=pltpu.CompilerParams(
            dimension_semantics=("arbitrary",), vmem_limit_bytes=VMEM_LIMIT),
        name="route",
    )(logits_t, bias_col)


def _place_kernel(eidx_ref, rank_ref, pstart_ref, dest_ref):
    tt = eidx_ref.shape[1]
    ioe = lax.broadcasted_iota(I32, (N_EXPERTS, tt), 0)
    pstart = pstart_ref[...]
    rows = [jnp.sum(jnp.where(ioe == eidx_ref[k:k + 1, :], pstart, 0.0), axis=0, keepdims=True)
            for k in range(TOP_K)]
    dest_ref[...] = jnp.concatenate(rows, axis=0).astype(I32) + rank_ref[...]


def _place(eidx, rank, pstart_col):
    n = eidx.shape[1]
    tt = TT_ROUTE
    row_spec = pl.BlockSpec((TOP_K, tt), lambda i: (0, i))
    return pl.pallas_call(
        _place_kernel,
        grid=(n // tt,),
        in_specs=[row_spec, row_spec, pl.BlockSpec((N_EXPERTS, 1), lambda i: (0, 0))],
        out_specs=row_spec,
        out_shape=jax.ShapeDtypeStruct((TOP_K, n), I32),
        compiler_params=pltpu.CompilerParams(
            dimension_semantics=("parallel",), vmem_limit_bytes=VMEM_LIMIT),
        name="place",
    )(eidx, rank, pstart_col)


def _dispatch_kernel(dst_ref, x_ref, zeros_hbm, xs_hbm, xbuf_ref, sem_ref):
    del zeros_hbm
    i = pl.program_id(0)
    nsteps = pl.num_programs(0)
    tt = x_ref.shape[0]
    slot = i % 2
    xbuf_ref[slot] = x_ref[...]
    for k in range(TOP_K):
        def body(t, carry, k=k):
            pltpu.make_async_copy(
                xbuf_ref.at[slot, pl.ds(t, 1), :], xs_hbm.at[pl.ds(dst_ref[k, t], 1), :],
                sem_ref.at[slot]).start()
            return carry
        lax.fori_loop(0, tt, body, 0, unroll=8)

    def wait_slot(s):
        for _ in range(TOP_K):
            pltpu.make_async_copy(xbuf_ref.at[s], xs_hbm.at[pl.ds(0, tt), :], sem_ref.at[s]).wait()

    @pl.when(i > 0)
    def _():
        wait_slot(1 - slot)

    @pl.when(i == nsteps - 1)
    def _():
        wait_slot(slot)


def _dispatch(dest, xpk, nrows):
    n, half = xpk.shape
    tt = TT_DISP
    return pl.pallas_call(
        _dispatch_kernel,
        grid=(n // tt,),
        in_specs=[
            pl.BlockSpec((TOP_K, tt), lambda i: (0, i), memory_space=pltpu.SMEM),
            pl.BlockSpec((tt, half), lambda i: (i, 0)),
            pl.BlockSpec(memory_space=pl.ANY),
        ],
        out_specs=pl.BlockSpec(memory_space=pl.ANY),
        out_shape=jax.ShapeDtypeStruct((nrows, half), U32),
        scratch_shapes=[
            pltpu.VMEM((2, tt, half), U32),
            pltpu.SemaphoreType.DMA((2,)),
        ],
        input_output_aliases={2: 0},
        compiler_params=pltpu.CompilerParams(
            dimension_semantics=("arbitrary",), vmem_limit_bytes=VMEM_LIMIT),
        name="dispatch",
    )(dest, xpk, jnp.zeros((nrows, half), U32))


def _expert_kernel(blk_e_ref, nused_ref, xs_ref, wg_ref, wu_ref, wd_ref, y_ref):
    i = pl.program_id(0)

    @pl.when(i >= nused_ref[0])
    def _():
        y_ref[...] = jnp.zeros_like(y_ref)

    @pl.when(i < nused_ref[0])
    def _():
        half = D_MODEL // 2
        x_lo, x_hi = _unpack_bf16_pair(xs_ref[...])
        x_lo = x_lo.astype(BF16)
        x_hi = x_hi.astype(BF16)
        wg = wg_ref[...].astype(BF16)
        wu = wu_ref[...].astype(BF16)
        gate = _dot(x_lo, wg[:half]) + _dot(x_hi, wg[half:])
        up = _dot(x_lo, wu[:half]) + _dot(x_hi, wu[half:])
        hmid = (_silu(gate) * up).astype(BF16)
        y = _dot(hmid, wd_ref[...].astype(BF16))
        y_ref[...] = _pack_bf16_pair(y[:, :half], y[:, half:])


def _experts(blk_e, nused, xs, w_gate, w_up, w_down):
    bm = BM_EXP
    nblk = xs.shape[0] // bm
    half = D_MODEL // 2

    def last_used(i, nu):
        return jnp.minimum(i, nu[0] - 1)

    grid_spec = pltpu.PrefetchScalarGridSpec(
        num_scalar_prefetch=2,
        grid=(nblk,),
        in_specs=[
            pl.BlockSpec((bm, half), lambda i, be, nu: (last_used(i, nu), 0)),
            pl.BlockSpec((None, D_MODEL, EXPERT_FF), lambda i, be, nu: (be[last_used(i, nu)], 0, 0)),
            pl.BlockSpec((None, D_MODEL, EXPERT_FF), lambda i, be, nu: (be[last_used(i, nu)], 0, 0)),
            pl.BlockSpec((None, EXPERT_FF, D_MODEL), lambda i, be, nu: (be[last_used(i, nu)], 0, 0)),
        ],
        out_specs=pl.BlockSpec((bm, half), lambda i, be, nu: (i, 0)),
    )
    return pl.pallas_call(
        _expert_kernel,
        grid_spec=grid_spec,
        out_shape=jax.ShapeDtypeStruct((nblk * bm, half), U32),
        compiler_params=pltpu.CompilerParams(
            dimension_semantics=("arbitrary",), vmem_limit_bytes=VMEM_LIMIT),
        name="experts",
    )(blk_e, nused, xs, w_gate, w_up, w_down)


def _sc_gather_rows(table, idx):
    nrows = idx.shape[0]
    d = table.shape[1]
    per_w = nrows // SC_NW
    nwin = per_w // SC_WIN
    assert per_w * SC_NW == nrows and nwin * SC_WIN == per_w and nwin % 2 == 0
    mesh = plsc.VectorSubcoreMesh(core_axis_name="c", subcore_axis_name="s",
                                  num_cores=SC_NC, num_subcores=SC_NS)

    @functools.partial(
        pl.kernel, mesh=mesh,
        out_type=jax.ShapeDtypeStruct((nrows, d), table.dtype),
        scratch_types=[
            pltpu.VMEM((nwin, SC_WIN), I32),
            pltpu.VMEM((2, SC_WIN, d), table.dtype),
            pltpu.SemaphoreType.DMA((2,)),
            pltpu.SemaphoreType.DMA((2,)),
        ],
        compiler_params=pltpu.CompilerParams(use_tc_tiling_on_sc=True),
        name="sc_gather_rows",
    )
    def gather_kernel(table_hbm, idx_hbm, out_hbm, idx_v, rows_v, gsem, wsem):
        wid = lax.axis_index("s") * SC_NC + lax.axis_index("c")
        base = wid * per_w
        pltpu.sync_copy(idx_hbm.at[wid], idx_v)

        def gather(w, slot):
            return pltpu.make_async_copy(table_hbm.at[idx_v.at[w]], rows_v.at[slot], gsem.at[slot])

        def put(w, slot):
            return pltpu.make_async_copy(
                rows_v.at[slot], out_hbm.at[pl.ds(base + w * SC_WIN, SC_WIN)], wsem.at[slot])

        gather(0, 0).start()

        @pl.loop(0, nwin, step=2)
        def _(w0):
            for slot in range(2):
                w = w0 + slot
                gather(w, slot).wait()

                @pl.when(w + 1 < nwin)
                def _():
                    @pl.when(w >= 1)
                    def _():
                        put(w - 1, 1 - slot).wait()
                    gather(w + 1, 1 - slot).start()

                put(w, slot).start()

        put(nwin - 2, 0).wait()
        put(nwin - 1, 1).wait()

    return gather_kernel(table, idx.reshape(SC_NW, nwin, SC_WIN))


def _combine_kernel(y_ref, base_ref, gate_ref, g_ref, b_ref, out_ref):
    half = D_MODEL // 2
    gates = gate_ref[...]
    acc_lo = base_ref[:, :half]
    acc_hi = base_ref[:, half:]
    for k in range(TOP_K):
        y_lo, y_hi = _unpack_bf16_pair(y_ref[k])
        gk = gates[:, k:k + 1]
        acc_lo = acc_lo + gk * y_lo
        acc_hi = acc_hi + gk * y_hi
    mu = (jnp.sum(acc_lo, axis=-1, keepdims=True) + jnp.sum(acc_hi, axis=-1, keepdims=True)) / D_MODEL
    c_lo = acc_lo - mu
    c_hi = acc_hi - mu
    var = (jnp.sum(c_lo * c_lo, axis=-1, keepdims=True)
           + jnp.sum(c_hi * c_hi, axis=-1, keepdims=True)) / D_MODEL
    inv = lax.rsqrt(var + LN_EPS)
    out_ref[:, :half] = c_lo * inv * g_ref[:, :half] + b_ref[:, :half]
    out_ref[:, half:] = c_hi * inv * g_ref[:, half:] + b_ref[:, half:]


def _combine(ybuf, base, gate_tok, ln_g, ln_b):
    n = base.shape[0]
    tt = TT_COMB
    half = D_MODEL // 2
    return pl.pallas_call(
        _combine_kernel,
        grid=(n // tt,),
        in_specs=[
            pl.BlockSpec((TOP_K, tt, half), lambda i: (0, i, 0)),
            pl.BlockSpec((tt, D_MODEL), lambda i: (i, 0)),
            pl.BlockSpec((tt, TOP_K), lambda i: (i, 0)),
            pl.BlockSpec((1, D_MODEL), lambda i: (0, 0)),
            pl.BlockSpec((1, D_MODEL), lambda i: (0, 0)),
        ],
        out_specs=pl.BlockSpec((tt, D_MODEL), lambda i: (i, 0)),
        out_shape=jax.ShapeDtypeStruct((n, D_MODEL), F32),
        compiler_params=pltpu.CompilerParams(
            dimension_semantics=("parallel",), vmem_limit_bytes=VMEM_LIMIT),
        name="combine",
    )(ybuf, base, gate_tok, ln_g, ln_b)


def _regroup_w_in(w_in):
    o = 0
    cols = {}
    for name, width in (("dnq", DN_WIDTH), ("dnk", DN_WIDTH), ("dnv", DN_WIDTH), ("sq", SWA_WIDTH),
                        ("sk", SWA_KV_WIDTH), ("sv", SWA_KV_WIDTH), ("z", DN_WIDTH),
                        ("b", DN_HEADS), ("a", DN_HEADS)):
        cols[name] = w_in[:, o:o + width]
        o += width
    w_main = jnp.concatenate([cols[k] for k in ("dnq", "dnk", "dnv", "z", "sq", "sk", "sv")], axis=1)
    w_gates = jnp.concatenate(
        [cols["b"], cols["a"], jnp.zeros((D_MODEL, GATE_COLS - 2 * DN_HEADS), w_in.dtype)], axis=1)
    return w_main.astype(BF16), w_gates.astype(BF16)


def _layer(x, w_in, conv_w, a_log, dt_bias, dn_norm_g, sinks, w_out, ln1_g, ln1_b,
           router_w, router_bias, w_gate, w_up, w_down, sh_gate, sh_up, sh_down, ln2_g, ln2_b):
    b, t, d = x.shape
    n = b * t
    x2d = x.reshape(n, d)

    w_main, w_gates = _regroup_w_in(w_in)
    main, gates = _in_proj(x2d, w_main, w_gates)
    main3d = main.reshape(b, t, MAIN_COLS)

    pad = jnp.zeros((GATE_COLS - 2 * DN_HEADS,), F32)
    gpar = jnp.stack([jnp.concatenate([jnp.zeros((DN_HEADS,), F32), a_log.astype(F32), pad]),
                      jnp.concatenate([jnp.zeros((DN_HEADS,), F32), dt_bias.astype(F32), pad])])
    dn_out = _deltanet(main3d, gates.reshape(b, t, GATE_COLS), conv_w.astype(F32), gpar,
                       dn_norm_g.astype(F32).reshape(1, DN_HEAD_DIM))
    swa_out = _swa(main3d, sinks.astype(F32))

    rw_t = router_w.T.astype(F32)
    rwh = rw_t.astype(BF16)
    rwl = (rw_t - rwh.astype(F32)).astype(BF16)
    base, xpk, logits_t = _post_mix(
        x2d, dn_out.reshape(n, DN_WIDTH), swa_out.reshape(n, SWA_WIDTH),
        w_out[:DN_WIDTH].astype(BF16), w_out[DN_WIDTH:].astype(BF16),
        ln1_g.reshape(1, d).astype(F32), ln1_b.reshape(1, d).astype(F32), rwh, rwl,
        sh_gate.astype(BF16), sh_up.astype(BF16), sh_down.astype(BF16))

    eidx, gate, rank, cnt = _route(logits_t, router_bias.astype(F32).reshape(N_EXPERTS, 1))

    bm = BM_EXP
    counts = cnt[:, 0]
    padded = (counts + bm - 1) // bm * bm
    pend = jnp.cumsum(padded)
    pstart = pend - padded
    nblk = -(-(n * TOP_K) // bm) + N_EXPERTS
    blk_row0 = jnp.arange(nblk, dtype=I32) * bm
    blk_e = jnp.clip(jnp.searchsorted(pend, blk_row0, side="right"), 0, N_EXPERTS - 1).astype(I32)
    nused = (pend[-1:] // bm).astype(I32)

    dest = _place(eidx, rank, pstart.astype(F32).reshape(N_EXPERTS, 1))
    xs = _dispatch(dest, xpk, nblk * bm)
    ypk = _experts(blk_e, nused, xs, w_gate, w_up, w_down)
    ybuf = _sc_gather_rows(ypk, dest.reshape(-1)).reshape(TOP_K, n, d // 2)
    out = _combine(ybuf, base, gate.T, ln2_g.reshape(1, d).astype(F32), ln2_b.reshape(1, d).astype(F32))
    return out.reshape(b, t, d)


def kernel(x, w_in, conv_w, a_log, dt_bias, dn_norm_g, sinks, w_out, ln1_g, ln1_b, router_w, router_bias,
           w_gate, w_up, w_down, shared_w_gate, shared_w_up, shared_w_down, ln2_g, ln2_b):
    depth = w_in.shape[0]
    for l in range(depth):
        x = _layer(x, w_in[l], conv_w[l], a_log[l], dt_bias[l], dn_norm_g[l], sinks[l], w_out[l],
                   ln1_g[l], ln1_b[l], router_w[l], router_bias[l], w_gate[l], w_up[l], w_down[l],
                   shared_w_gate[l], shared_w_up[l], shared_w_down[l], ln2_g[l], ln2_b[l])
    return x
```

```python
import functools

import jax
import jax.numpy as jnp
from jax import lax
from jax.experimental import pallas as pl
from jax.experimental.pallas import tpu as pltpu
from jax.experimental.pallas import tpu_sc as plsc

F32 = jnp.float32
BF16 = jnp.bfloat16
I32 = jnp.int32
U32 = jnp.uint32

D_MODEL = 1024
DN_HEADS = 4
DN_HEAD_DIM = 128
DN_WIDTH = DN_HEADS * DN_HEAD_DIM
CONV_WIDTH = 4
DN_CHUNK = 64
SWA_Q_HEADS = 8
SWA_KV_HEADS = 2
SWA_HEAD_DIM = 64
SWA_WIDTH = SWA_Q_HEADS * SWA_HEAD_DIM
SWA_KV_WIDTH = SWA_KV_HEADS * SWA_HEAD_DIM
SWA_WINDOW = 128
SWA_BLOCK = 128
N_EXPERTS = 256
N_GROUPS = 8
GROUP_SIZE = N_EXPERTS // N_GROUPS
TOPK_GROUPS = 4
TOP_K = 8
EXPERT_FF = 256
SHARED_FF = 256
ROUTED_SCALE = 2.5
DEEPNORM_ALPHA = 2.0 ** 0.25
LN_EPS = 1e-5
RMS_EPS = 1e-6
L2_EPS = 1e-6

COL_DNQ = 0
COL_DNK = DN_WIDTH
COL_DNV = 2 * DN_WIDTH
COL_Z = 3 * DN_WIDTH
COL_SQ = 4 * DN_WIDTH
COL_SK = COL_SQ + SWA_WIDTH
COL_SV = COL_SK + SWA_KV_WIDTH
MAIN_COLS = COL_SV + SWA_KV_WIDTH
GATE_COLS = 128

TM_PROJ = 512
TS_DN = 512
DN_A_UNROLL = 2
TM_POST = 512
TT_ROUTE = 512
BM_EXP = 256
TT_COMB = 256
SC_NC = 2
SC_NS = 16
SC_NW = SC_NC * SC_NS
SC_WIN = 64
VMEM_LIMIT = 56 * 1024 * 1024
NEG_INF = float("-inf")


def _dot(a, b):
    return jnp.dot(a, b, preferred_element_type=F32)


def _mm(a, b):
    return _dot(a.astype(BF16), b.astype(BF16))


def _mm_nt(a, b):
    return lax.dot_general(a.astype(BF16), b.astype(BF16), (((1,), (1,)), ((), ())),
                           preferred_element_type=F32)


def _mm_tn(a, b):
    return lax.dot_general(a.astype(BF16), b.astype(BF16), (((0,), (0,)), ((), ())),
                           preferred_element_type=F32)


def _split2(a):
    hi = a.astype(BF16)
    lo = (a - hi.astype(F32)).astype(BF16)
    return hi, lo


def _mm3(a, b):
    ah, al = _split2(a)
    bh, bl = _split2(b)
    return _dot(ah, bh) + _dot(ah, bl) + _dot(al, bh)


def _mm_exact_lhs(l_bf16, g):
    g1 = g.astype(BF16)
    r1 = g - g1.astype(F32)
    g2 = r1.astype(BF16)
    g3 = (r1 - g2.astype(F32)).astype(BF16)
    return _dot(l_bf16, g1) + _dot(l_bf16, g2) + _dot(l_bf16, g3)


def _sigmoid(x):
    return 1.0 / (1.0 + jnp.exp(-x))


def _silu(x):
    return x * _sigmoid(x)


def _in_proj_kernel(x_ref, w_ref, wg_ref, main_ref, gates_ref):
    xb = x_ref[...].astype(BF16)
    main_ref[...] = _dot(xb, w_ref[...]).astype(BF16)
    gates_ref[...] = _dot(xb, wg_ref[...])


def _in_proj(x2d, w_main, w_gates):
    n = x2d.shape[0]
    return pl.pallas_call(
        _in_proj_kernel,
        grid=(n // TM_PROJ,),
        in_specs=[
            pl.BlockSpec((TM_PROJ, D_MODEL), lambda i: (i, 0)),
            pl.BlockSpec((D_MODEL, MAIN_COLS), lambda i: (0, 0)),
            pl.BlockSpec((D_MODEL, GATE_COLS), lambda i: (0, 0)),
        ],
        out_specs=[
            pl.BlockSpec((TM_PROJ, MAIN_COLS), lambda i: (i, 0)),
            pl.BlockSpec((TM_PROJ, GATE_COLS), lambda i: (i, 0)),
        ],
        out_shape=[
            jax.ShapeDtypeStruct((n, MAIN_COLS), BF16),
            jax.ShapeDtypeStruct((n, GATE_COLS), F32),
        ],
        compiler_params=pltpu.CompilerParams(
            dimension_semantics=("parallel",), vmem_limit_bytes=VMEM_LIMIT),
        name="in_proj",
    )(x2d, w_main, w_gates)


def _dn_kernel(x_ref, gates_ref, convw_ref, gpar_ref, normg_ref, out_ref,
               xc_ref, gl_ref, gc_ref, wq_ref, u_ref, kt_ref, attn_ref, egl_ref, s_ref, hist_ref):
    ts = x_ref.shape[0]
    c = DN_CHUNK
    hd = DN_HEAD_DIM
    qkv_w = 3 * DN_WIDTH
    nch = ts // c

    @pl.when(pl.program_id(1) == 0)
    def _():
        s_ref[...] = jnp.zeros_like(s_ref)
        hist_ref[...] = jnp.zeros_like(hist_ref)

    xc_ref[0:8, :] = hist_ref[...]
    xc_ref[8:ts + 8, :] = x_ref[:, 0:qkv_w].astype(F32)
    hist_ref[...] = xc_ref[ts:ts + 8, :]

    gsl = gates_ref[...]
    sp_in = gsl + gpar_ref[1:2, :]
    softplus = jnp.maximum(sp_in, 0.0) + jnp.log(1.0 + jnp.exp(-jnp.abs(sp_in)))
    lane = lax.broadcasted_iota(I32, gsl.shape, 1)
    gl = jnp.where(lane < DN_HEADS, _sigmoid(gsl), -jnp.exp(gpar_ref[0:1, :]) * softplus)
    gl_ref[...] = gl
    row_in_chunk = lax.broadcasted_iota(I32, gsl.shape, 0) % c
    gc = gl
    shift = 1
    while shift < c:
        gc = gc + jnp.where(row_in_chunk >= shift, pltpu.roll(gc, shift, 0), 0.0)
        shift *= 2
    gc_ref[...] = gc

    ii = lax.broadcasted_iota(I32, (c, c), 0)
    jj = lax.broadcasted_iota(I32, (c, c), 1)
    tri_incl = ii >= jj
    tri_strict = ii > jj
    eye = jnp.where(ii == jj, 1.0, 0.0).astype(F32)
    heads = range(DN_HEADS)

    def conv_silu(r0, col):
        xt = xc_ref[pl.ds(r0, c + 8), col:col + hd]
        w = convw_ref[:, col:col + hd]
        y = w[CONV_WIDTH - 1:CONV_WIDTH, :] * xt[8:8 + c, :]
        for j in range(CONV_WIDTH - 1):
            off = 8 - (CONV_WIDTH - 1) + j
            y = y + w[j:j + 1, :] * xt[off:off + c, :]
        return _silu(y)

    def l2n(t, scale):
        return t * (lax.rsqrt(jnp.sum(t * t, axis=-1, keepdims=True) + L2_EPS) * scale)

    def phase_a(i, carry):
        chains = []
        for sub in range(DN_A_UNROLL):
            ci = i * DN_A_UNROLL + sub
            r0 = pl.multiple_of(ci * c, c)
            glc = gl_ref[pl.ds(r0, c), :]
            gcc = gc_ref[pl.ds(r0, c), :]
            gct = jnp.concatenate([gcc, gcc], axis=0).T
            egl_ref[ci] = jnp.exp(gcc[c - 8:c, :])
            for h in heads:
                chains.append((ci, r0, h, glc, gcc, gct))
        nchain = len(chains)
        q = [l2n(conv_silu(r0, COL_DNQ + h * hd), hd ** -0.5) for (ci, r0, h, _, _, _) in chains]
        k = [l2n(conv_silu(r0, COL_DNK + h * hd), 1.0) for (ci, r0, h, _, _, _) in chains]
        v = [conv_silu(r0, COL_DNV + h * hd) for (ci, r0, h, _, _, _) in chains]
        kb, vb, decay, egc = [], [], [], []
        for n_, (ci, r0, h, glc, gcc, gct) in enumerate(chains):
            beta = glc[:, h:h + 1]
            gc_col = gcc[:, DN_HEADS + h:DN_HEADS + h + 1]
            gc_row = gct[DN_HEADS + h:DN_HEADS + h + 1, 0:c]
            decay.append(jnp.where(tri_incl, jnp.exp(jnp.minimum(gc_col - gc_row, 0.0)), 0.0))
            egc.append(jnp.exp(gc_col))
            e_tail = jnp.exp(gcc[c - 1:c, DN_HEADS + h:DN_HEADS + h + 1] - gc_col)
            kb.append(k[n_] * beta)
            vb.append(v[n_] * beta)
            kt_ref[ci, h] = (k[n_] * e_tail).astype(BF16)
        kq = [_mm_nt(jnp.concatenate([kb[n_], q[n_]], axis=0), k[n_]) for n_ in range(nchain)]
        a_mat = [jnp.where(tri_strict, kq[n_][0:c] * decay[n_], 0.0) for n_ in range(nchain)]
        for n_, (ci, r0, h, _, _, _) in enumerate(chains):
            attn_ref[ci, h] = (kq[n_][c:2 * c] * decay[n_]).astype(BF16)
        t_inv = [eye - a for a in a_mat]
        p = a_mat
        for _ in range(5):
            p = [_mm(x, x) for x in p]
            t_inv = [t + _mm(t, x) for t, x in zip(t_inv, p)]
        for n_, (ci, r0, h, _, _, _) in enumerate(chains):
            uw = _mm3(t_inv[n_], jnp.concatenate([vb[n_], kb[n_] * egc[n_]], axis=1))
            u_ref[ci, h] = uw[:, 0:hd]
            wq_ref[ci, h, 0:c, :] = uw[:, hd:2 * hd].astype(BF16)
            wq_ref[ci, h, c:2 * c, :] = (q[n_] * egc[n_]).astype(BF16)
        return carry

    lax.fori_loop(0, nch // DN_A_UNROLL, phase_a, 0)

    normg = normg_ref[...]

    def phase_b(ci, carry):
        r0 = pl.multiple_of(ci * c, c)
        rows = pl.ds(r0, c)
        egl = egl_ref[ci]
        s_old = [s_ref[h] for h in heads]
        ws = [_dot(wq_ref[ci, h], s_old[h].astype(BF16)) for h in heads]
        v_new = [(u_ref[ci, h] - ws[h][0:c]).astype(BF16) for h in heads]
        for h in heads:
            s_ref[h] = (s_old[h] * egl[7:8, DN_HEADS + h:DN_HEADS + h + 1]
                        + lax.dot_general(kt_ref[ci, h], v_new[h], (((0,), (0,)), ((), ())),
                                          preferred_element_type=F32))
        for h in heads:
            o = ws[h][c:2 * c] + _dot(attn_ref[ci, h], v_new[h])
            o = o * lax.rsqrt(jnp.mean(o * o, axis=-1, keepdims=True) + RMS_EPS) * normg
            z = x_ref[rows, COL_Z + h * hd:COL_Z + (h + 1) * hd].astype(F32)
            out_ref[rows, h * hd:(h + 1) * hd] = (o * _silu(z)).astype(out_ref.dtype)
        return carry

    lax.fori_loop(0, nch, phase_b, 0)


def _deltanet(main3d, gates3d, conv_w, gpar, normg):
    b, t, _ = main3d.shape
    ts = TS_DN
    nch = ts // DN_CHUNK
    dn_in = COL_Z + DN_WIDTH
    return pl.pallas_call(
        _dn_kernel,
        grid=(b, t // ts),
        in_specs=[
            pl.BlockSpec((None, ts, dn_in), lambda bi, si: (bi, si, 0)),
            pl.BlockSpec((None, ts, GATE_COLS), lambda bi, si: (bi, si, 0)),
            pl.BlockSpec((CONV_WIDTH, 3 * DN_WIDTH), lambda bi, si: (0, 0)),
            pl.BlockSpec((2, GATE_COLS), lambda bi, si: (0, 0)),
            pl.BlockSpec((1, DN_HEAD_DIM), lambda bi, si: (0, 0)),
        ],
        out_specs=pl.BlockSpec((None, ts, DN_WIDTH), lambda bi, si: (bi, si, 0)),
        out_shape=jax.ShapeDtypeStruct((b, t, DN_WIDTH), BF16),
        scratch_shapes=[
            pltpu.VMEM((ts + 8, 3 * DN_WIDTH), F32),
            pltpu.VMEM((ts, GATE_COLS), F32),
            pltpu.VMEM((ts, GATE_COLS), F32),
            pltpu.VMEM((nch, DN_HEADS, 2 * DN_CHUNK, DN_HEAD_DIM), BF16),
            pltpu.VMEM((nch, DN_HEADS, DN_CHUNK, DN_HEAD_DIM), F32),
            pltpu.VMEM((nch, DN_HEADS, DN_CHUNK, DN_HEAD_DIM), BF16),
            pltpu.VMEM((nch, DN_HEADS, DN_CHUNK, DN_CHUNK), BF16),
            pltpu.VMEM((nch, 8, GATE_COLS), F32),
            pltpu.VMEM((DN_HEADS, DN_HEAD_DIM, DN_HEAD_DIM), F32),
            pltpu.VMEM((8, 3 * DN_WIDTH), F32),
        ],
        compiler_params=pltpu.CompilerParams(
            dimension_semantics=("parallel", "arbitrary"), vmem_limit_bytes=VMEM_LIMIT),
        name="deltanet",
    )(main3d, gates3d, conv_w, gpar, normg)


def _swa_kernel(sinks_ref, q_ref, kp_ref, kc_ref, vp_ref, vc_ref, out_ref):
    n = pl.program_id(1)
    blk = SWA_BLOCK
    d = SWA_HEAD_DIM
    grp = SWA_Q_HEADS // SWA_KV_HEADS
    qi = lax.broadcasted_iota(I32, (blk, 2 * blk), 0)
    kj = lax.broadcasted_iota(I32, (blk, 2 * blk), 1)
    dist = qi + blk - kj
    valid = (dist >= 0) & (dist < SWA_WINDOW) & ((kj >= blk) | (n > 0))
    dist_f = dist.astype(F32)
    outs = []
    for hq in range(SWA_Q_HEADS):
        hk = hq // grp
        q = q_ref[:, hq * d:(hq + 1) * d]
        kband = jnp.concatenate([kp_ref[:, hk * d:(hk + 1) * d], kc_ref[:, hk * d:(hk + 1) * d]], axis=0)
        vband = jnp.concatenate([vp_ref[:, hk * d:(hk + 1) * d], vc_ref[:, hk * d:(hk + 1) * d]], axis=0)
        slope = 2.0 ** (-8.0 * (hq + 1.0) / SWA_Q_HEADS)
        s = _mm_nt(q, kband) * (d ** -0.5) - slope * dist_f
        s = jnp.where(valid, s, NEG_INF)
        sink = sinks_ref[hq]
        m = jnp.maximum(jnp.max(s, axis=-1, keepdims=True), sink)
        p = jnp.exp(s - m)
        denom = jnp.sum(p, axis=-1, keepdims=True) + jnp.exp(sink - m)
        outs.append(_mm(p, vband) / denom)
    out_ref[...] = jnp.concatenate(outs, axis=-1).astype(out_ref.dtype)


def _swa(main3d, sinks):
    b, t, _ = main3d.shape
    blk = SWA_BLOCK
    qb = COL_SQ // SWA_WIDTH
    kb = COL_SK // SWA_KV_WIDTH
    vb = COL_SV // SWA_KV_WIDTH
    grid_spec = pltpu.PrefetchScalarGridSpec(
        num_scalar_prefetch=1,
        grid=(b, t // blk),
        in_specs=[
            pl.BlockSpec((None, blk, SWA_WIDTH), lambda bi, ni, s: (bi, ni, qb)),
            pl.BlockSpec((None, blk, SWA_KV_WIDTH), lambda bi, ni, s: (bi, jnp.maximum(ni - 1, 0), kb)),
            pl.BlockSpec((None, blk, SWA_KV_WIDTH), lambda bi, ni, s: (bi, ni, kb)),
            pl.BlockSpec((None, blk, SWA_KV_WIDTH), lambda bi, ni, s: (bi, jnp.maximum(ni - 1, 0), vb)),
            pl.BlockSpec((None, blk, SWA_KV_WIDTH), lambda bi, ni, s: (bi, ni, vb)),
        ],
        out_specs=pl.BlockSpec((None, blk, SWA_WIDTH), lambda bi, ni, s: (bi, ni, 0)),
    )
    return pl.pallas_call(
        _swa_kernel,
        grid_spec=grid_spec,
        out_shape=jax.ShapeDtypeStruct((b, t, SWA_WIDTH), BF16),
        compiler_params=pltpu.CompilerParams(
            dimension_semantics=("parallel", "parallel"), vmem_limit_bytes=VMEM_LIMIT),
        name="swa",
    )(sinks, main3d, main3d, main3d, main3d, main3d)


def _layer_norm(y, g, b):
    mu = jnp.mean(y, axis=-1, keepdims=True)
    yc = y - mu
    var = jnp.mean(yc * yc, axis=-1, keepdims=True)
    return yc * lax.rsqrt(var + LN_EPS) * g + b


def _pack_bf16_pair(lo, hi):
    lo_bits = lax.bitcast_convert_type(lo.astype(BF16).astype(F32), U32)
    hi_bits = lax.bitcast_convert_type(hi.astype(BF16).astype(F32), U32)
    return (hi_bits & jnp.uint32(0xFFFF0000)) | (lo_bits >> 16)


def _unpack_bf16_pair(packed):
    lo = lax.bitcast_convert_type(packed << 16, F32)
    hi = lax.bitcast_convert_type(packed & jnp.uint32(0xFFFF0000), F32)
    return lo, hi


def _post_mix_kernel(x_ref, dn_ref, swa_ref, wo_dn_ref, wo_swa_ref, g_ref, b_ref,
                     rwh_ref, rwl_ref, sg_ref, su_ref, sd_ref,
                     base_ref, xpk_ref, logit_ref):
    mix = _dot(dn_ref[...], wo_dn_ref[...]) + _dot(swa_ref[...], wo_swa_ref[...])
    x1 = _layer_norm(DEEPNORM_ALPHA * x_ref[...] + mix, g_ref[...], b_ref[...])
    half = D_MODEL // 2
    xpk_ref[...] = _pack_bf16_pair(x1[:, :half], x1[:, half:])
    xh, xl = _split2(x1)
    nt = (((1,), (1,)), ((), ()))
    logit_ref[...] = (lax.dot_general(rwh_ref[...], xh, nt, preferred_element_type=F32)
                      + lax.dot_general(rwh_ref[...], xl, nt, preferred_element_type=F32)
                      + lax.dot_general(rwl_ref[...], xh, nt, preferred_element_type=F32))
    hmid = _silu(_dot(xh, sg_ref[...])) * _dot(xh, su_ref[...])
    base_ref[...] = DEEPNORM_ALPHA * x1 + _dot(hmid.astype(BF16), sd_ref[...])


def _post_mix(x2d, dn2d, swa2d, wo_dn, wo_swa, ln_g, ln_b, rwh, rwl, sg, su, sd):
    n = x2d.shape[0]
    tm = TM_POST
    full = lambda shape: pl.BlockSpec(shape, lambda i: (0, 0))
    return pl.pallas_call(
        _post_mix_kernel,
        grid=(n // tm,),
        in_specs=[
            pl.BlockSpec((tm, D_MODEL), lambda i: (i, 0)),
            pl.BlockSpec((tm, DN_WIDTH), lambda i: (i, 0)),
            pl.BlockSpec((tm, SWA_WIDTH), lambda i: (i, 0)),
            full((DN_WIDTH, D_MODEL)), full((SWA_WIDTH, D_MODEL)),
            full((1, D_MODEL)), full((1, D_MODEL)),
            full((N_EXPERTS, D_MODEL)), full((N_EXPERTS, D_MODEL)),
            full((D_MODEL, SHARED_FF)), full((D_MODEL, SHARED_FF)), full((SHARED_FF, D_MODEL)),
        ],
        out_specs=[
            pl.BlockSpec((tm, D_MODEL), lambda i: (i, 0)),
            pl.BlockSpec((tm, D_MODEL // 2), lambda i: (i, 0)),
            pl.BlockSpec((N_EXPERTS, tm), lambda i: (0, i)),
        ],
        out_shape=[
            jax.ShapeDtypeStruct((n, D_MODEL), F32),
            jax.ShapeDtypeStruct((n, D_MODEL // 2), U32),
            jax.ShapeDtypeStruct((N_EXPERTS, n), F32),
        ],
        compiler_params=pltpu.CompilerParams(
            dimension_semantics=("parallel",), vmem_limit_bytes=VMEM_LIMIT),
        name="post_mix",
    )(x2d, dn2d, swa2d, wo_dn, wo_swa, ln_g, ln_b, rwh, rwl, sg, su, sd)


def _route_kernel(lg_ref, bias_ref, eidx_ref, gate_ref, rank_ref, cnt_ref, carry_ref):
    @pl.when(pl.program_id(0) == 0)
    def _():
        carry_ref[...] = jnp.zeros_like(carry_ref)

    tt = lg_ref.shape[1]
    scores = _sigmoid(lg_ref[...])
    sel = scores + bias_ref[...]

    iog = lax.broadcasted_iota(I32, (GROUP_SIZE, tt), 0)
    grp_rows = []
    for g in range(N_GROUPS):
        blk = sel[g * GROUP_SIZE:(g + 1) * GROUP_SIZE, :]
        m1 = jnp.max(blk, axis=0, keepdims=True)
        i1 = jnp.min(jnp.where(blk == m1, iog, GROUP_SIZE), axis=0, keepdims=True)
        m2 = jnp.max(jnp.where(iog == i1, NEG_INF, blk), axis=0, keepdims=True)
        grp_rows.append(m1 + m2)
    gs = jnp.concatenate(grp_rows, axis=0)

    io8 = lax.broadcasted_iota(I32, (N_GROUPS, tt), 0)
    gsel = jnp.zeros((N_GROUPS, tt), F32)
    for _ in range(TOPK_GROUPS):
        mg = jnp.max(gs, axis=0, keepdims=True)
        ig = jnp.min(jnp.where(gs == mg, io8, N_GROUPS), axis=0, keepdims=True)
        hit = io8 == ig
        gsel = jnp.where(hit, 1.0, gsel)
        gs = jnp.where(hit, NEG_INF, gs)

    val = jnp.concatenate(
        [jnp.where(gsel[g:g + 1, :] > 0.0, sel[g * GROUP_SIZE:(g + 1) * GROUP_SIZE, :], NEG_INF)
         for g in range(N_GROUPS)], axis=0)

    ioe = lax.broadcasted_iota(I32, (N_EXPERTS, tt), 0)
    onehot = jnp.zeros((N_EXPERTS, tt), F32)
    idx_rows, gate_rows = [], []
    for _ in range(TOP_K):
        m = jnp.max(val, axis=0, keepdims=True)
        ik = jnp.min(jnp.where(val == m, ioe, N_EXPERTS), axis=0, keepdims=True)
        hit = ioe == ik
        gate_rows.append(jnp.sum(jnp.where(hit, scores, 0.0), axis=0, keepdims=True))
        idx_rows.append(ik)
        val = jnp.where(hit, NEG_INF, val)
        onehot = jnp.where(hit, 1.0, onehot)
    gsum = gate_rows[0]
    for r in gate_rows[1:]:
        gsum = gsum + r
    gate_ref[...] = jnp.concatenate(gate_rows, axis=0) / gsum * ROUTED_SCALE
    eidx_ref[...] = jnp.concatenate(idx_rows, axis=0)

    ti = lax.broadcasted_iota(I32, (tt, tt), 0)
    tj = lax.broadcasted_iota(I32, (tt, tt), 1)
    upper = jnp.where(ti < tj, 1.0, 0.0).astype(BF16)
    cum = _dot(onehot.astype(BF16), upper) + jnp.broadcast_to(carry_ref[:, 0:1], (N_EXPERTS, tt))
    rank_rows = [jnp.sum(jnp.where(ioe == ik, cum, 0.0), axis=0, keepdims=True) for ik in idx_rows]
    rank_ref[...] = jnp.concatenate(rank_rows, axis=0).astype(I32)
    carry_ref[...] = carry_ref[...] + jnp.broadcast_to(
        jnp.sum(onehot, axis=1, keepdims=True), carry_ref.shape)
    cnt_ref[...] = carry_ref[...].astype(I32)


def _route(logits_t, bias_col):
    n = logits_t.shape[1]
    tt = TT_ROUTE
    row_spec = pl.BlockSpec((TOP_K, tt), lambda i: (0, i))
    return pl.pallas_call(
        _route_kernel,
        grid=(n // tt,),
        in_specs=[
            pl.BlockSpec((N_EXPERTS, tt), lambda i: (0, i)),
            pl.BlockSpec((N_EXPERTS, 1), lambda i: (0, 0)),
        ],
        out_specs=[row_spec, row_spec, row_spec,
                   pl.BlockSpec((N_EXPERTS, 128), lambda i: (0, 0))],
        out_shape=[
            jax.ShapeDtypeStruct((TOP_K, n), I32),
            jax.ShapeDtypeStruct((TOP_K, n), F32),
            jax.ShapeDtypeStruct((TOP_K, n), I32),
            jax.ShapeDtypeStruct((N_EXPERTS, 128), I32),
        ],
        scratch_shapes=[pltpu.VMEM((N_EXPERTS, 128), F32)],
        compiler_params=pltpu.CompilerParams(
            dimension_semantics=("arbitrary",), vmem_limit_bytes=VMEM_LIMIT),
        name="route",
    )(logits_t, bias_col)


def _place_kernel(eidx_ref, rank_ref, pstart_ref, dest_ref):
    tt = eidx_ref.shape[1]
    ioe = lax.broadcasted_iota(I32, (N_EXPERTS, tt), 0)
    pstart = pstart_ref[...]
    rows = [jnp.sum(jnp.where(ioe == eidx_ref[k:k + 1, :], pstart, 0.0), axis=0, keepdims=True)
            for k in range(TOP_K)]
    dest_ref[...] = jnp.concatenate(rows, axis=0).astype(I32) + rank_ref[...]


def _place(eidx, rank, pstart_col):
    n = eidx.shape[1]
    tt = TT_ROUTE
    row_spec = pl.BlockSpec((TOP_K, tt), lambda i: (0, i))
    return pl.pallas_call(
        _place_kernel,
        grid=(n // tt,),
        in_specs=[row_spec, row_spec, pl.BlockSpec((N_EXPERTS, 1), lambda i: (0, 0))],
        out_specs=row_spec,
        out_shape=jax.ShapeDtypeStruct((TOP_K, n), I32),
        compiler_params=pltpu.CompilerParams(
            dimension_semantics=("parallel",), vmem_limit_bytes=VMEM_LIMIT),
        name="place",
    )(eidx, rank, pstart_col)


def _sc_mesh():
    return plsc.VectorSubcoreMesh(core_axis_name="c", subcore_axis_name="s",
                                  num_cores=SC_NC, num_subcores=SC_NS)


def _sc_scatter_rows(rows, idx, nrows_out):
    n, d = rows.shape
    nk = idx.shape[0]
    per_w = n // SC_NW
    nwin = per_w // SC_WIN
    assert per_w * SC_NW == n and nwin * SC_WIN == per_w and nwin % 2 == 0

    @functools.partial(
        pl.kernel, mesh=_sc_mesh(),
        out_type=jax.ShapeDtypeStruct((nrows_out, d), rows.dtype),
        scratch_types=[
            pltpu.VMEM((nwin, nk, SC_WIN), I32),
            pltpu.VMEM((2, SC_WIN, d), rows.dtype),
            pltpu.SemaphoreType.DMA((2,)),
            pltpu.SemaphoreType.DMA((2,)),
        ],
        compiler_params=pltpu.CompilerParams(use_tc_tiling_on_sc=True),
        name="sc_scatter_rows",
    )
    def scatter_kernel(rows_hbm, idx_hbm, out_hbm, idx_v, rows_v, lsem, ssem):
        wid = lax.axis_index("s") * SC_NC + lax.axis_index("c")
        base = wid * per_w
        pltpu.sync_copy(idx_hbm.at[wid], idx_v)

        def load(w, slot):
            return pltpu.make_async_copy(
                rows_hbm.at[pl.ds(base + w * SC_WIN, SC_WIN)], rows_v.at[slot], lsem.at[slot])

        def scat(w, k, slot):
            return pltpu.make_async_copy(rows_v.at[slot], out_hbm.at[idx_v.at[w, k]], ssem.at[slot])

        load(0, 0).start()

        @pl.loop(0, nwin, step=2)
        def _(w0):
            for slot in range(2):
                w = w0 + slot
                load(w, slot).wait()

                @pl.when(w + 1 < nwin)
                def _():
                    @pl.when(w >= 1)
                    def _():
                        for k in range(nk):
                            scat(w - 1, k, 1 - slot).wait()
                    load(w + 1, 1 - slot).start()

                for k in range(nk):
                    scat(w, k, slot).start()

        for k in range(nk):
            scat(nwin - 2, k, 0).wait()
        for k in range(nk):
            scat(nwin - 1, k, 1).wait()

    idx4 = idx.reshape(nk, SC_NW, nwin, SC_WIN).transpose(1, 2, 0, 3)
    return scatter_kernel(rows, idx4)


def _expert_kernel(blk_e_ref, blk_nv_ref, nused_ref, xs_ref, wg_ref, wu_ref, wd_ref, y_ref):
    i = pl.program_id(0)

    @pl.when(i >= nused_ref[0])
    def _():
        y_ref[...] = jnp.zeros_like(y_ref)

    @pl.when(i < nused_ref[0])
    def _():
        half = D_MODEL // 2
        row = lax.broadcasted_iota(I32, xs_ref.shape, 0)
        x_lo, x_hi = _unpack_bf16_pair(jnp.where(row < blk_nv_ref[i], xs_ref[...], jnp.uint32(0)))
        x_lo = x_lo.astype(BF16)
        x_hi = x_hi.astype(BF16)
        wg = wg_ref[...].astype(BF16)
        wu = wu_ref[...].astype(BF16)
        gate = _dot(x_lo, wg[:half]) + _dot(x_hi, wg[half:])
        up = _dot(x_lo, wu[:half]) + _dot(x_hi, wu[half:])
        hmid = (_silu(gate) * up).astype(BF16)
        y = _dot(hmid, wd_ref[...].astype(BF16))
        y_ref[...] = _pack_bf16_pair(y[:, :half], y[:, half:])


def _experts(blk_e, blk_nv, nused, xs, w_gate, w_up, w_down):
    bm = BM_EXP
    nblk = xs.shape[0] // bm
    half = D_MODEL // 2

    def last_used(i, nu):
        return jnp.minimum(i, nu[0] - 1)

    grid_spec = pltpu.PrefetchScalarGridSpec(
        num_scalar_prefetch=3,
        grid=(nblk,),
        in_specs=[
            pl.BlockSpec((bm, half), lambda i, be, nv, nu: (last_used(i, nu), 0)),
            pl.BlockSpec((None, D_MODEL, EXPERT_FF), lambda i, be, nv, nu: (be[last_used(i, nu)], 0, 0)),
            pl.BlockSpec((None, D_MODEL, EXPERT_FF), lambda i, be, nv, nu: (be[last_used(i, nu)], 0, 0)),
            pl.BlockSpec((None, EXPERT_FF, D_MODEL), lambda i, be, nv, nu: (be[last_used(i, nu)], 0, 0)),
        ],
        out_specs=pl.BlockSpec((bm, half), lambda i, be, nv, nu: (i, 0)),
    )
    return pl.pallas_call(
        _expert_kernel,
        grid_spec=grid_spec,
        out_shape=jax.ShapeDtypeStruct((nblk * bm, half), U32),
        compiler_params=pltpu.CompilerParams(
            dimension_semantics=("arbitrary",), vmem_limit_bytes=VMEM_LIMIT),
        name="experts",
    )(blk_e, blk_nv, nused, xs, w_gate, w_up, w_down)


def _sc_gather_rows(table, idx):
    nrows = idx.shape[0]
    d = table.shape[1]
    per_w = nrows // SC_NW
    nwin = per_w // SC_WIN
    assert per_w * SC_NW == nrows and nwin * SC_WIN == per_w and nwin % 2 == 0
    @functools.partial(
        pl.kernel, mesh=_sc_mesh(),
        out_type=jax.ShapeDtypeStruct((nrows, d), table.dtype),
        scratch_types=[
            pltpu.VMEM((nwin, SC_WIN), I32),
            pltpu.VMEM((2, SC_WIN, d), table.dtype),
            pltpu.SemaphoreType.DMA((2,)),
            pltpu.SemaphoreType.DMA((2,)),
        ],
        compiler_params=pltpu.CompilerParams(use_tc_tiling_on_sc=True),
        name="sc_gather_rows",
    )
    def gather_kernel(table_hbm, idx_hbm, out_hbm, idx_v, rows_v, gsem, wsem):
        wid = lax.axis_index("s") * SC_NC + lax.axis_index("c")
        base = wid * per_w
        pltpu.sync_copy(idx_hbm.at[wid], idx_v)

        def gather(w, slot):
            return pltpu.make_async_copy(table_hbm.at[idx_v.at[w]], rows_v.at[slot], gsem.at[slot])

        def put(w, slot):
            return pltpu.make_async_copy(
                rows_v.at[slot], out_hbm.at[pl.ds(base + w * SC_WIN, SC_WIN)], wsem.at[slot])

        gather(0, 0).start()

        @pl.loop(0, nwin, step=2)
        def _(w0):
            for slot in range(2):
                w = w0 + slot
                gather(w, slot).wait()

                @pl.when(w + 1 < nwin)
                def _():
                    @pl.when(w >= 1)
                    def _():
                        put(w - 1, 1 - slot).wait()
                    gather(w + 1, 1 - slot).start()

                put(w, slot).start()

        put(nwin - 2, 0).wait()
        put(nwin - 1, 1).wait()

    return gather_kernel(table, idx.reshape(SC_NW, nwin, SC_WIN))


def _combine_kernel(y_ref, base_ref, gate_ref, g_ref, b_ref, out_ref):
    half = D_MODEL // 2
    gates = gate_ref[...]
    acc_lo = base_ref[:, :half]
    acc_hi = base_ref[:, half:]
    for k in range(TOP_K):
        y_lo, y_hi = _unpack_bf16_pair(y_ref[k])
        gk = gates[:, k:k + 1]
        acc_lo = acc_lo + gk * y_lo
        acc_hi = acc_hi + gk * y_hi
    mu = (jnp.sum(acc_lo, axis=-1, keepdims=True) + jnp.sum(acc_hi, axis=-1, keepdims=True)) / D_MODEL
    c_lo = acc_lo - mu
    c_hi = acc_hi - mu
    var = (jnp.sum(c_lo * c_lo, axis=-1, keepdims=True)
           + jnp.sum(c_hi * c_hi, axis=-1, keepdims=True)) / D_MODEL
    inv = lax.rsqrt(var + LN_EPS)
    out_ref[:, :half] = c_lo * inv * g_ref[:, :half] + b_ref[:, :half]
    out_ref[:, half:] = c_hi * inv * g_ref[:, half:] + b_ref[:, half:]


def _combine(ybuf, base, gate_tok, ln_g, ln_b):
    n = base.shape[0]
    tt = TT_COMB
    half = D_MODEL // 2
    return pl.pallas_call(
        _combine_kernel,
        grid=(n // tt,),
        in_specs=[
            pl.BlockSpec((TOP_K, tt, half), lambda i: (0, i, 0)),
            pl.BlockSpec((tt, D_MODEL), lambda i: (i, 0)),
            pl.BlockSpec((tt, TOP_K), lambda i: (i, 0)),
            pl.BlockSpec((1, D_MODEL), lambda i: (0, 0)),
            pl.BlockSpec((1, D_MODEL), lambda i: (0, 0)),
        ],
        out_specs=pl.BlockSpec((tt, D_MODEL), lambda i: (i, 0)),
        out_shape=jax.ShapeDtypeStruct((n, D_MODEL), F32),
        compiler_params=pltpu.CompilerParams(
            dimension_semantics=("parallel",), vmem_limit_bytes=VMEM_LIMIT),
        name="combine",
    )(ybuf, base, gate_tok, ln_g, ln_b)


def _regroup_w_in(w_in):
    o = 0
    cols = {}
    for name, width in (("dnq", DN_WIDTH), ("dnk", DN_WIDTH), ("dnv", DN_WIDTH), ("sq", SWA_WIDTH),
                        ("sk", SWA_KV_WIDTH), ("sv", SWA_KV_WIDTH), ("z", DN_WIDTH),
                        ("b", DN_HEADS), ("a", DN_HEADS)):
        cols[name] = w_in[:, o:o + width]
        o += width
    w_main = jnp.concatenate([cols[k] for k in ("dnq", "dnk", "dnv", "z", "sq", "sk", "sv")], axis=1)
    w_gates = jnp.concatenate(
        [cols["b"], cols["a"], jnp.zeros((D_MODEL, GATE_COLS - 2 * DN_HEADS), w_in.dtype)], axis=1)
    return w_main.astype(BF16), w_gates.astype(BF16)


def _layer(x, w_in, conv_w, a_log, dt_bias, dn_norm_g, sinks, w_out, ln1_g, ln1_b,
           router_w, router_bias, w_gate, w_up, w_down, sh_gate, sh_up, sh_down, ln2_g, ln2_b):
    b, t, d = x.shape
    n = b * t
    x2d = x.reshape(n, d)

    w_main, w_gates = _regroup_w_in(w_in)
    main, gates = _in_proj(x2d, w_main, w_gates)
    main3d = main.reshape(b, t, MAIN_COLS)

    pad = jnp.zeros((GATE_COLS - 2 * DN_HEADS,), F32)
    gpar = jnp.stack([jnp.concatenate([jnp.zeros((DN_HEADS,), F32), a_log.astype(F32), pad]),
                      jnp.concatenate([jnp.zeros((DN_HEADS,), F32), dt_bias.astype(F32), pad])])
    dn_out = _deltanet(main3d, gates.reshape(b, t, GATE_COLS), conv_w.astype(F32), gpar,
                       dn_norm_g.astype(F32).reshape(1, DN_HEAD_DIM))
    swa_out = _swa(main3d, sinks.astype(F32))

    rw_t = router_w.T.astype(F32)
    rwh = rw_t.astype(BF16)
    rwl = (rw_t - rwh.astype(F32)).astype(BF16)
    base, xpk, logits_t = _post_mix(
        x2d, dn_out.reshape(n, DN_WIDTH), swa_out.reshape(n, SWA_WIDTH),
        w_out[:DN_WIDTH].astype(BF16), w_out[DN_WIDTH:].astype(BF16),
        ln1_g.reshape(1, d).astype(F32), ln1_b.reshape(1, d).astype(F32), rwh, rwl,
        sh_gate.astype(BF16), sh_up.astype(BF16), sh_down.astype(BF16))

    eidx, gate, rank, cnt = _route(logits_t, router_bias.astype(F32).reshape(N_EXPERTS, 1))

    bm = BM_EXP
    counts = cnt[:, 0]
    padded = (counts + bm - 1) // bm * bm
    pend = jnp.cumsum(padded)
    pstart = pend - padded
    nblk = -(-(n * TOP_K) // bm) + N_EXPERTS
    blk_row0 = jnp.arange(nblk, dtype=I32) * bm
    blk_e = jnp.clip(jnp.searchsorted(pend, blk_row0, side="right"), 0, N_EXPERTS - 1).astype(I32)
    blk_nv = jnp.clip(counts[blk_e] - (blk_row0 - pstart[blk_e]), 0, bm).astype(I32)
    nused = (pend[-1:] // bm).astype(I32)

    dest = _place(eidx, rank, pstart.astype(F32).reshape(N_EXPERTS, 1))
    xs = _sc_scatter_rows(xpk, dest, nblk * bm)
    ypk = _experts(blk_e, blk_nv, nused, xs, w_gate, w_up, w_down)
    ybuf = _sc_gather_rows(ypk, dest.reshape(-1)).reshape(TOP_K, n, d // 2)
    out = _combine(ybuf, base, gate.T, ln2_g.reshape(1, d).astype(F32), ln2_b.reshape(1, d).astype(F32))
    return out.reshape(b, t, d)


def kernel(x, w_in, conv_w, a_log, dt_bias, dn_norm_g, sinks, w_out, ln1_g, ln1_b, router_w, router_bias,
           w_gate, w_up, w_down, shared_w_gate, shared_w_up, shared_w_down, ln2_g, ln2_b):
    depth = w_in.shape[0]
    for l in range(depth):
        x = _layer(x, w_in[l], conv_w[l], a_log[l], dt_bias[l], dn_norm_g[l], sinks[l], w_out[l],
                   ln1_g[l], ln1_b[l], router_w[l], router_bias[l], w_gate[l], w_up[l], w_down[l],
                   shared_w_gate[l], shared_w_up[l], shared_w_down[l], ln2_g[l], ln2_b[l])
    return x
```

```python
import functools

import jax
import jax.numpy as jnp
from jax import lax
from jax.experimental import pallas as pl
from jax.experimental.pallas import tpu as pltpu
from jax.experimental.pallas import tpu_sc as plsc

F32 = jnp.float32
BF16 = jnp.bfloat16
I32 = jnp.int32
U32 = jnp.uint32

D_MODEL = 1024
DN_HEADS = 4
DN_HEAD_DIM = 128
DN_WIDTH = DN_HEADS * DN_HEAD_DIM
CONV_WIDTH = 4
DN_CHUNK = 64
SWA_Q_HEADS = 8
SWA_KV_HEADS = 2
SWA_HEAD_DIM = 64
SWA_WIDTH = SWA_Q_HEADS * SWA_HEAD_DIM
SWA_KV_WIDTH = SWA_KV_HEADS * SWA_HEAD_DIM
SWA_WINDOW = 128
SWA_BLOCK = 128
N_EXPERTS = 256
N_GROUPS = 8
GROUP_SIZE = N_EXPERTS // N_GROUPS
TOPK_GROUPS = 4
TOP_K = 8
EXPERT_FF = 256
SHARED_FF = 256
ROUTED_SCALE = 2.5
DEEPNORM_ALPHA = 2.0 ** 0.25
LN_EPS = 1e-5
RMS_EPS = 1e-6
L2_EPS = 1e-6

COL_DNQ = 0
COL_DNK = DN_WIDTH
COL_DNV = 2 * DN_WIDTH
COL_Z = 3 * DN_WIDTH
COL_SQ = 4 * DN_WIDTH
COL_SK = COL_SQ + SWA_WIDTH
COL_SV = COL_SK + SWA_KV_WIDTH
MAIN_COLS = COL_SV + SWA_KV_WIDTH
GATE_COLS = 128

TM_PROJ = 512
TS_DN = 512
DN_A_UNROLL = 2
TM_POST = 512
TT_ROUTE = 512
BM_EXP = 256
TT_COMB = 256
SC_NC = 2
SC_NS = 16
SC_NW = SC_NC * SC_NS
SC_WIN = 64
VMEM_LIMIT = 56 * 1024 * 1024
NEG_INF = float("-inf")


def _dot(a, b):
    return jnp.dot(a, b, preferred_element_type=F32)


def _mm(a, b):
    return _dot(a.astype(BF16), b.astype(BF16))


def _mm_nt(a, b):
    return lax.dot_general(a.astype(BF16), b.astype(BF16), (((1,), (1,)), ((), ())),
                           preferred_element_type=F32)


def _mm_tn(a, b):
    return lax.dot_general(a.astype(BF16), b.astype(BF16), (((0,), (0,)), ((), ())),
                           preferred_element_type=F32)


def _split2(a):
    hi = a.astype(BF16)
    lo = (a - hi.astype(F32)).astype(BF16)
    return hi, lo


def _mm3(a, b):
    ah, al = _split2(a)
    bh, bl = _split2(b)
    return _dot(ah, bh) + _dot(ah, bl) + _dot(al, bh)


def _mm_exact_lhs(l_bf16, g):
    g1 = g.astype(BF16)
    r1 = g - g1.astype(F32)
    g2 = r1.astype(BF16)
    g3 = (r1 - g2.astype(F32)).astype(BF16)
    return _dot(l_bf16, g1) + _dot(l_bf16, g2) + _dot(l_bf16, g3)


def _sigmoid(x):
    return 1.0 / (1.0 + jnp.exp(-x))


def _silu(x):
    return x * _sigmoid(x)


def _in_proj_kernel(x_ref, w_ref, wg_ref, main_ref, gates_ref):
    xb = x_ref[...].astype(BF16)
    main_ref[...] = _dot(xb, w_ref[...]).astype(BF16)
    gates_ref[...] = _dot(xb, wg_ref[...])


def _in_proj(x2d, w_main, w_gates):
    n = x2d.shape[0]
    return pl.pallas_call(
        _in_proj_kernel,
        grid=(n // TM_PROJ,),
        in_specs=[
            pl.BlockSpec((TM_PROJ, D_MODEL), lambda i: (i, 0)),
            pl.BlockSpec((D_MODEL, MAIN_COLS), lambda i: (0, 0)),
            pl.BlockSpec((D_MODEL, GATE_COLS), lambda i: (0, 0)),
        ],
        out_specs=[
            pl.BlockSpec((TM_PROJ, MAIN_COLS), lambda i: (i, 0)),
            pl.BlockSpec((TM_PROJ, GATE_COLS), lambda i: (i, 0)),
        ],
        out_shape=[
            jax.ShapeDtypeStruct((n, MAIN_COLS), BF16),
            jax.ShapeDtypeStruct((n, GATE_COLS), F32),
        ],
        compiler_params=pltpu.CompilerParams(
            dimension_semantics=("parallel",), vmem_limit_bytes=VMEM_LIMIT),
        name="in_proj",
    )(x2d, w_main, w_gates)


def _dn_kernel(x_ref, gates_ref, convw_ref, gpar_ref, normg_ref, out_ref,
               xc_ref, gl_ref, gc_ref, wq_ref, u_ref, kt_ref, attn_ref, egl_ref, s_ref, hist_ref):
    ts = x_ref.shape[0]
    c = DN_CHUNK
    hd = DN_HEAD_DIM
    qkv_w = 3 * DN_WIDTH
    nch = ts // c

    @pl.when(pl.program_id(1) == 0)
    def _():
        s_ref[...] = jnp.zeros_like(s_ref)
        hist_ref[...] = jnp.zeros_like(hist_ref)

    xc_ref[0:8, :] = hist_ref[...]
    xc_ref[8:ts + 8, :] = x_ref[:, 0:qkv_w].astype(F32)
    hist_ref[...] = xc_ref[ts:ts + 8, :]

    gsl = gates_ref[...]
    sp_in = gsl + gpar_ref[1:2, :]
    softplus = jnp.maximum(sp_in, 0.0) + jnp.log(1.0 + jnp.exp(-jnp.abs(sp_in)))
    lane = lax.broadcasted_iota(I32, gsl.shape, 1)
    gl = jnp.where(lane < DN_HEADS, _sigmoid(gsl), -jnp.exp(gpar_ref[0:1, :]) * softplus)
    gl_ref[...] = gl
    row_in_chunk = lax.broadcasted_iota(I32, gsl.shape, 0) % c
    gc = gl
    shift = 1
    while shift < c:
        gc = gc + jnp.where(row_in_chunk >= shift, pltpu.roll(gc, shift, 0), 0.0)
        shift *= 2
    gc_ref[...] = gc

    ii = lax.broadcasted_iota(I32, (c, c), 0)
    jj = lax.broadcasted_iota(I32, (c, c), 1)
    tri_incl = ii >= jj
    tri_strict = ii > jj
    eye = jnp.where(ii == jj, 1.0, 0.0).astype(F32)
    heads = range(DN_HEADS)

    def conv_silu(r0, col):
        xt = xc_ref[pl.ds(r0, c + 8), col:col + hd]
        w = convw_ref[:, col:col + hd]
        y = w[CONV_WIDTH - 1:CONV_WIDTH, :] * xt[8:8 + c, :]
        for j in range(CONV_WIDTH - 1):
            off = 8 - (CONV_WIDTH - 1) + j
            y = y + w[j:j + 1, :] * xt[off:off + c, :]
        return _silu(y)

    def l2n(t, scale):
        return t * (lax.rsqrt(jnp.sum(t * t, axis=-1, keepdims=True) + L2_EPS) * scale)

    def phase_a(i, carry):
        chains = []
        for sub in range(DN_A_UNROLL):
            ci = i * DN_A_UNROLL + sub
            r0 = pl.multiple_of(ci * c, c)
            glc = gl_ref[pl.ds(r0, c), :]
            gcc = gc_ref[pl.ds(r0, c), :]
            gct = jnp.concatenate([gcc, gcc], axis=0).T
            egl_ref[ci] = jnp.exp(gcc[c - 8:c, :])
            for h in heads:
                chains.append((ci, r0, h, glc, gcc, gct))
        nchain = len(chains)
        q = [l2n(conv_silu(r0, COL_DNQ + h * hd), hd ** -0.5) for (ci, r0, h, _, _, _) in chains]
        k = [l2n(conv_silu(r0, COL_DNK + h * hd), 1.0) for (ci, r0, h, _, _, _) in chains]
        v = [conv_silu(r0, COL_DNV + h * hd) for (ci, r0, h, _, _, _) in chains]
        kb, vb, decay, egc = [], [], [], []
        for n_, (ci, r0, h, glc, gcc, gct) in enumerate(chains):
            beta = glc[:, h:h + 1]
            gc_col = gcc[:, DN_HEADS + h:DN_HEADS + h + 1]
            gc_row = gct[DN_HEADS + h:DN_HEADS + h + 1, 0:c]
            decay.append(jnp.where(tri_incl, jnp.exp(jnp.minimum(gc_col - gc_row, 0.0)), 0.0))
            egc.append(jnp.exp(gc_col))
            e_tail = jnp.exp(gcc[c - 1:c, DN_HEADS + h:DN_HEADS + h + 1] - gc_col)
            kb.append(k[n_] * beta)
            vb.append(v[n_] * beta)
            kt_ref[ci, h] = (k[n_] * e_tail).astype(BF16)
        kq = [_mm_nt(jnp.concatenate([kb[n_], q[n_]], axis=0), k[n_]) for n_ in range(nchain)]
        a_mat = [jnp.where(tri_strict, kq[n_][0:c] * decay[n_], 0.0) for n_ in range(nchain)]
        for n_, (ci, r0, h, _, _, _) in enumerate(chains):
            attn_ref[ci, h] = (kq[n_][c:2 * c] * decay[n_]).astype(BF16)
        t_inv = [eye - a for a in a_mat]
        p = a_mat
        for _ in range(5):
            p = [_mm(x, x) for x in p]
            t_inv = [t + _mm(t, x) for t, x in zip(t_inv, p)]
        for n_, (ci, r0, h, _, _, _) in enumerate(chains):
            uw = _mm3(t_inv[n_], jnp.concatenate([vb[n_], kb[n_] * egc[n_]], axis=1))
            u_ref[ci, h] = uw[:, 0:hd]
            wq_ref[ci, h, 0:c, :] = uw[:, hd:2 * hd].astype(BF16)
            wq_ref[ci, h, c:2 * c, :] = (q[n_] * egc[n_]).astype(BF16)
        return carry

    lax.fori_loop(0, nch // DN_A_UNROLL, phase_a, 0)

    normg = normg_ref[...]

    def phase_b(ci, carry):
        r0 = pl.multiple_of(ci * c, c)
        rows = pl.ds(r0, c)
        egl = egl_ref[ci]
        s_old = [s_ref[h] for h in heads]
        ws = [_dot(wq_ref[ci, h], s_old[h].astype(BF16)) for h in heads]
        v_new = [(u_ref[ci, h] - ws[h][0:c]).astype(BF16) for h in heads]
        for h in heads:
            s_ref[h] = (s_old[h] * egl[7:8, DN_HEADS + h:DN_HEADS + h + 1]
                        + lax.dot_general(kt_ref[ci, h], v_new[h], (((0,), (0,)), ((), ())),
                                          preferred_element_type=F32))
        for h in heads:
            o = ws[h][c:2 * c] + _dot(attn_ref[ci, h], v_new[h])
            o = o * lax.rsqrt(jnp.mean(o * o, axis=-1, keepdims=True) + RMS_EPS) * normg
            z = x_ref[rows, COL_Z + h * hd:COL_Z + (h + 1) * hd].astype(F32)
            out_ref[rows, h * hd:(h + 1) * hd] = (o * _silu(z)).astype(out_ref.dtype)
        return carry

    lax.fori_loop(0, nch, phase_b, 0)


def _deltanet(main3d, gates3d, conv_w, gpar, normg):
    b, t, _ = main3d.shape
    ts = TS_DN
    nch = ts // DN_CHUNK
    dn_in = COL_Z + DN_WIDTH
    return pl.pallas_call(
        _dn_kernel,
        grid=(b, t // ts),
        in_specs=[
            pl.BlockSpec((None, ts, dn_in), lambda bi, si: (bi, si, 0)),
            pl.BlockSpec((None, ts, GATE_COLS), lambda bi, si: (bi, si, 0)),
            pl.BlockSpec((CONV_WIDTH, 3 * DN_WIDTH), lambda bi, si: (0, 0)),
            pl.BlockSpec((2, GATE_COLS), lambda bi, si: (0, 0)),
            pl.BlockSpec((1, DN_HEAD_DIM), lambda bi, si: (0, 0)),
        ],
        out_specs=pl.BlockSpec((None, ts, DN_WIDTH), lambda bi, si: (bi, si, 0)),
        out_shape=jax.ShapeDtypeStruct((b, t, DN_WIDTH), BF16),
        scratch_shapes=[
            pltpu.VMEM((ts + 8, 3 * DN_WIDTH), F32),
            pltpu.VMEM((ts, GATE_COLS), F32),
            pltpu.VMEM((ts, GATE_COLS), F32),
            pltpu.VMEM((nch, DN_HEADS, 2 * DN_CHUNK, DN_HEAD_DIM), BF16),
            pltpu.VMEM((nch, DN_HEADS, DN_CHUNK, DN_HEAD_DIM), F32),
            pltpu.VMEM((nch, DN_HEADS, DN_CHUNK, DN_HEAD_DIM), BF16),
            pltpu.VMEM((nch, DN_HEADS, DN_CHUNK, DN_CHUNK), BF16),
            pltpu.VMEM((nch, 8, GATE_COLS), F32),
            pltpu.VMEM((DN_HEADS, DN_HEAD_DIM, DN_HEAD_DIM), F32),
            pltpu.VMEM((8, 3 * DN_WIDTH), F32),
        ],
        compiler_params=pltpu.CompilerParams(
            dimension_semantics=("parallel", "arbitrary"), vmem_limit_bytes=VMEM_LIMIT),
        name="deltanet",
    )(main3d, gates3d, conv_w, gpar, normg)


def _swa_kernel(sinks_ref, q_ref, kp_ref, kc_ref, vp_ref, vc_ref, out_ref):
    n = pl.program_id(1)
    blk = SWA_BLOCK
    d = SWA_HEAD_DIM
    grp = SWA_Q_HEADS // SWA_KV_HEADS
    qi = lax.broadcasted_iota(I32, (blk, 2 * blk), 0)
    kj = lax.broadcasted_iota(I32, (blk, 2 * blk), 1)
    dist = qi + blk - kj
    valid = (dist >= 0) & (dist < SWA_WINDOW) & ((kj >= blk) | (n > 0))
    dist_f = dist.astype(F32)
    outs = []
    for hq in range(SWA_Q_HEADS):
        hk = hq // grp
        q = q_ref[:, hq * d:(hq + 1) * d]
        kband = jnp.concatenate([kp_ref[:, hk * d:(hk + 1) * d], kc_ref[:, hk * d:(hk + 1) * d]], axis=0)
        vband = jnp.concatenate([vp_ref[:, hk * d:(hk + 1) * d], vc_ref[:, hk * d:(hk + 1) * d]], axis=0)
        slope = 2.0 ** (-8.0 * (hq + 1.0) / SWA_Q_HEADS)
        s = _mm_nt(q, kband) * (d ** -0.5) - slope * dist_f
        s = jnp.where(valid, s, NEG_INF)
        sink = sinks_ref[hq]
        m = jnp.maximum(jnp.max(s, axis=-1, keepdims=True), sink)
        p = jnp.exp(s - m)
        denom = jnp.sum(p, axis=-1, keepdims=True) + jnp.exp(sink - m)
        outs.append(_mm(p, vband) / denom)
    out_ref[...] = jnp.concatenate(outs, axis=-1).astype(out_ref.dtype)


def _swa(main3d, sinks):
    b, t, _ = main3d.shape
    blk = SWA_BLOCK
    qb = COL_SQ // SWA_WIDTH
    kb = COL_SK // SWA_KV_WIDTH
    vb = COL_SV // SWA_KV_WIDTH
    grid_spec = pltpu.PrefetchScalarGridSpec(
        num_scalar_prefetch=1,
        grid=(b, t // blk),
        in_specs=[
            pl.BlockSpec((None, blk, SWA_WIDTH), lambda bi, ni, s: (bi, ni, qb)),
            pl.BlockSpec((None, blk, SWA_KV_WIDTH), lambda bi, ni, s: (bi, jnp.maximum(ni - 1, 0), kb)),
            pl.BlockSpec((None, blk, SWA_KV_WIDTH), lambda bi, ni, s: (bi, ni, kb)),
            pl.BlockSpec((None, blk, SWA_KV_WIDTH), lambda bi, ni, s: (bi, jnp.maximum(ni - 1, 0), vb)),
            pl.BlockSpec((None, blk, SWA_KV_WIDTH), lambda bi, ni, s: (bi, ni, vb)),
        ],
        out_specs=pl.BlockSpec((None, blk, SWA_WIDTH), lambda bi, ni, s: (bi, ni, 0)),
    )
    return pl.pallas_call(
        _swa_kernel,
        grid_spec=grid_spec,
        out_shape=jax.ShapeDtypeStruct((b, t, SWA_WIDTH), BF16),
        compiler_params=pltpu.CompilerParams(
            dimension_semantics=("parallel", "parallel"), vmem_limit_bytes=VMEM_LIMIT),
        name="swa",
    )(sinks, main3d, main3d, main3d, main3d, main3d)


def _layer_norm(y, g, b):
    mu = jnp.mean(y, axis=-1, keepdims=True)
    yc = y - mu
    var = jnp.mean(yc * yc, axis=-1, keepdims=True)
    return yc * lax.rsqrt(var + LN_EPS) * g + b


def _pack_bf16_pair(lo, hi):
    lo_bits = lax.bitcast_convert_type(lo.astype(BF16).astype(F32), U32)
    hi_bits = lax.bitcast_convert_type(hi.astype(BF16).astype(F32), U32)
    return (hi_bits & jnp.uint32(0xFFFF0000)) | (lo_bits >> 16)


def _unpack_bf16_pair(packed):
    lo = lax.bitcast_convert_type(packed << 16, F32)
    hi = lax.bitcast_convert_type(packed & jnp.uint32(0xFFFF0000), F32)
    return lo, hi


def _post_mix_kernel(x_ref, dn_ref, swa_ref, wo_dn_ref, wo_swa_ref, g_ref, b_ref,
                     rwh_ref, rwl_ref, sg_ref, su_ref, sd_ref,
                     base_ref, xpk_ref, logit_ref):
    mix = _dot(dn_ref[...], wo_dn_ref[...]) + _dot(swa_ref[...], wo_swa_ref[...])
    x1 = _layer_norm(DEEPNORM_ALPHA * x_ref[...] + mix, g_ref[...], b_ref[...])
    half = D_MODEL // 2
    xpk_ref[...] = _pack_bf16_pair(x1[:, :half], x1[:, half:])
    xh, xl = _split2(x1)
    nt = (((1,), (1,)), ((), ()))
    logit_ref[...] = (lax.dot_general(rwh_ref[...], xh, nt, preferred_element_type=F32)
                      + lax.dot_general(rwh_ref[...], xl, nt, preferred_element_type=F32)
                      + lax.dot_general(rwl_ref[...], xh, nt, preferred_element_type=F32))
    hmid = _silu(_dot(xh, sg_ref[...])) * _dot(xh, su_ref[...])
    base_ref[...] = DEEPNORM_ALPHA * x1 + _dot(hmid.astype(BF16), sd_ref[...])


def _post_mix(x2d, dn2d, swa2d, wo_dn, wo_swa, ln_g, ln_b, rwh, rwl, sg, su, sd):
    n = x2d.shape[0]
    tm = TM_POST
    full = lambda shape: pl.BlockSpec(shape, lambda i: (0, 0))
    return pl.pallas_call(
        _post_mix_kernel,
        grid=(n // tm,),
        in_specs=[
            pl.BlockSpec((tm, D_MODEL), lambda i: (i, 0)),
            pl.BlockSpec((tm, DN_WIDTH), lambda i: (i, 0)),
            pl.BlockSpec((tm, SWA_WIDTH), lambda i: (i, 0)),
            full((DN_WIDTH, D_MODEL)), full((SWA_WIDTH, D_MODEL)),
            full((1, D_MODEL)), full((1, D_MODEL)),
            full((N_EXPERTS, D_MODEL)), full((N_EXPERTS, D_MODEL)),
            full((D_MODEL, SHARED_FF)), full((D_MODEL, SHARED_FF)), full((SHARED_FF, D_MODEL)),
        ],
        out_specs=[
            pl.BlockSpec((tm, D_MODEL), lambda i: (i, 0)),
            pl.BlockSpec((tm, D_MODEL // 2), lambda i: (i, 0)),
            pl.BlockSpec((N_EXPERTS, tm), lambda i: (0, i)),
        ],
        out_shape=[
            jax.ShapeDtypeStruct((n, D_MODEL), F32),
            jax.ShapeDtypeStruct((n, D_MODEL // 2), U32),
            jax.ShapeDtypeStruct((N_EXPERTS, n), F32),
        ],
        compiler_params=pltpu.CompilerParams(
            dimension_semantics=("parallel",), vmem_limit_bytes=VMEM_LIMIT),
        name="post_mix",
    )(x2d, dn2d, swa2d, wo_dn, wo_swa, ln_g, ln_b, rwh, rwl, sg, su, sd)


def _route_kernel(lg_ref, bias_ref, eidx_ref, gate_ref, rank_ref, cnt_ref, carry_ref):
    @pl.when(pl.program_id(0) == 0)
    def _():
        carry_ref[...] = jnp.zeros_like(carry_ref)

    tt = lg_ref.shape[1]
    scores = _sigmoid(lg_ref[...])
    sel = scores + bias_ref[...]

    iog = lax.broadcasted_iota(I32, (GROUP_SIZE, tt), 0)
    grp_rows = []
    for g in range(N_GROUPS):
        blk = sel[g * GROUP_SIZE:(g + 1) * GROUP_SIZE, :]
        m1 = jnp.max(blk, axis=0, keepdims=True)
        i1 = jnp.min(jnp.where(blk == m1, iog, GROUP_SIZE), axis=0, keepdims=True)
        m2 = jnp.max(jnp.where(iog == i1, NEG_INF, blk), axis=0, keepdims=True)
        grp_rows.append(m1 + m2)
    gs = jnp.concatenate(grp_rows, axis=0)

    io8 = lax.broadcasted_iota(I32, (N_GROUPS, tt), 0)
    gsel = jnp.zeros((N_GROUPS, tt), F32)
    for _ in range(TOPK_GROUPS):
        mg = jnp.max(gs, axis=0, keepdims=True)
        ig = jnp.min(jnp.where(gs == mg, io8, N_GROUPS), axis=0, keepdims=True)
        hit = io8 == ig
        gsel = jnp.where(hit, 1.0, gsel)
        gs = jnp.where(hit, NEG_INF, gs)

    val = jnp.concatenate(
        [jnp.where(gsel[g:g + 1, :] > 0.0, sel[g * GROUP_SIZE:(g + 1) * GROUP_SIZE, :], NEG_INF)
         for g in range(N_GROUPS)], axis=0)

    ioe = lax.broadcasted_iota(I32, (N_EXPERTS, tt), 0)
    onehot = jnp.zeros((N_EXPERTS, tt), F32)
    idx_rows, gate_rows = [], []
    for _ in range(TOP_K):
        m = jnp.max(val, axis=0, keepdims=True)
        ik = jnp.min(jnp.where(val == m, ioe, N_EXPERTS), axis=0, keepdims=True)
        hit = ioe == ik
        gate_rows.append(jnp.sum(jnp.where(hit, scores, 0.0), axis=0, keepdims=True))
        idx_rows.append(ik)
        val = jnp.where(hit, NEG_INF, val)
        onehot = jnp.where(hit, 1.0, onehot)
    gsum = gate_rows[0]
    for r in gate_rows[1:]:
        gsum = gsum + r
    gate_ref[...] = jnp.concatenate(gate_rows, axis=0) / gsum * ROUTED_SCALE
    eidx_ref[...] = jnp.concatenate(idx_rows, axis=0)

    ti = lax.broadcasted_iota(I32, (tt, tt), 0)
    tj = lax.broadcasted_iota(I32, (tt, tt), 1)
    upper = jnp.where(ti < tj, 1.0, 0.0).astype(BF16)
    cum = _dot(onehot.astype(BF16), upper) + jnp.broadcast_to(carry_ref[:, 0:1], (N_EXPERTS, tt))
    rank_rows = [jnp.sum(jnp.where(ioe == ik, cum, 0.0), axis=0, keepdims=True) for ik in idx_rows]
    rank_ref[...] = jnp.concatenate(rank_rows, axis=0).astype(I32)
    carry_ref[...] = carry_ref[...] + jnp.broadcast_to(
        jnp.sum(onehot, axis=1, keepdims=True), carry_ref.shape)
    cnt_ref[...] = carry_ref[...].astype(I32)


def _route(logits_t, bias_col):
    n = logits_t.shape[1]
    tt = TT_ROUTE
    row_spec = pl.BlockSpec((TOP_K, tt), lambda i: (0, i))
    return pl.pallas_call(
        _route_kernel,
        grid=(n // tt,),
        in_specs=[
            pl.BlockSpec((N_EXPERTS, tt), lambda i: (0, i)),
            pl.BlockSpec((N_EXPERTS, 1), lambda i: (0, 0)),
        ],
        out_specs=[row_spec, row_spec, row_spec,
                   pl.BlockSpec((N_EXPERTS, 128), lambda i: (0, 0))],
        out_shape=[
            jax.ShapeDtypeStruct((TOP_K, n), I32),
            jax.ShapeDtypeStruct((TOP_K, n), F32),
            jax.ShapeDtypeStruct((TOP_K, n), I32),
            jax.ShapeDtypeStruct((N_EXPERTS, 128), I32),
        ],
        scratch_shapes=[pltpu.VMEM((N_EXPERTS, 128), F32)],
        compiler_params=pltpu.CompilerParams(
            dimension_semantics=("arbitrary",), vmem_limit_bytes=VMEM_LIMIT),
        name="route",
    )(logits_t, bias_col)


def _place_kernel(eidx_ref, rank_ref, pstart_ref, dest_ref):
    tt = eidx_ref.shape[1]
    ioe = lax.broadcasted_iota(I32, (N_EXPERTS, tt), 0)
    pstart = pstart_ref[...]
    rows = [jnp.sum(jnp.where(ioe == eidx_ref[k:k + 1, :], pstart, 0.0), axis=0, keepdims=True)
            for k in range(TOP_K)]
    dest_ref[...] = jnp.concatenate(rows, axis=0).astype(I32) + rank_ref[...]


def _place(eidx, rank, pstart_col):
    n = eidx.shape[1]
    tt = TT_ROUTE
    row_spec = pl.BlockSpec((TOP_K, tt), lambda i: (0, i))
    return pl.pallas_call(
        _place_kernel,
        grid=(n // tt,),
        in_specs=[row_spec, row_spec, pl.BlockSpec((N_EXPERTS, 1), lambda i: (0, 0))],
        out_specs=row_spec,
        out_shape=jax.ShapeDtypeStruct((TOP_K, n), I32),
        compiler_params=pltpu.CompilerParams(
            dimension_semantics=("parallel",), vmem_limit_bytes=VMEM_LIMIT),
        name="place",
    )(eidx, rank, pstart_col)


def _sc_mesh():
    return plsc.VectorSubcoreMesh(core_axis_name="c", subcore_axis_name="s",
                                  num_cores=SC_NC, num_subcores=SC_NS)


def _sc_scatter_rows(rows, idx, nrows_out):
    n, d = rows.shape
    nk = idx.shape[0]
    per_w = n // SC_NW
    nwin = per_w // SC_WIN
    assert per_w * SC_NW == n and nwin * SC_WIN == per_w and nwin % 2 == 0

    @functools.partial(
        pl.kernel, mesh=_sc_mesh(),
        out_type=jax.ShapeDtypeStruct((nrows_out, d), rows.dtype),
        scratch_types=[
            pltpu.VMEM((nwin, nk, SC_WIN), I32),
            pltpu.VMEM((2, SC_WIN, d), rows.dtype),
            pltpu.SemaphoreType.DMA((2,)),
            pltpu.SemaphoreType.DMA((2,)),
        ],
        compiler_params=pltpu.CompilerParams(use_tc_tiling_on_sc=True),
        name="sc_scatter_rows",
    )
    def scatter_kernel(rows_hbm, idx_hbm, out_hbm, idx_v, rows_v, lsem, ssem):
        wid = lax.axis_index("s") * SC_NC + lax.axis_index("c")
        base = wid * per_w
        pltpu.sync_copy(idx_hbm.at[wid], idx_v)

        def load(w, slot):
            return pltpu.make_async_copy(
                rows_hbm.at[pl.ds(base + w * SC_WIN, SC_WIN)], rows_v.at[slot], lsem.at[slot])

        def scat(w, k, slot):
            return pltpu.make_async_copy(rows_v.at[slot], out_hbm.at[idx_v.at[w, k]], ssem.at[slot])

        load(0, 0).start()

        @pl.loop(0, nwin, step=2)
        def _(w0):
            for slot in range(2):
                w = w0 + slot
                load(w, slot).wait()

                @pl.when(w + 1 < nwin)
                def _():
                    @pl.when(w >= 1)
                    def _():
                        for k in range(nk):
                            scat(w - 1, k, 1 - slot).wait()
                    load(w + 1, 1 - slot).start()

                for k in range(nk):
                    scat(w, k, slot).start()

        for k in range(nk):
            scat(nwin - 2, k, 0).wait()
        for k in range(nk):
            scat(nwin - 1, k, 1).wait()

    idx4 = idx.reshape(nk, SC_NW, nwin, SC_WIN).transpose(1, 2, 0, 3)
    return scatter_kernel(rows, idx4)


def _expert_kernel(gstart_ref, cnt_ref, xs_hbm, wg_ref, wu_ref, wd_ref, y_hbm,
                   wgb_ref, wub_ref, wdb_ref, xbuf_ref, ybuf_ref, xsem, ysem):
    e = pl.program_id(0)
    ne = pl.num_programs(0)
    bm = xbuf_ref.shape[1]
    nblk = y_hbm.shape[0] // bm
    half = D_MODEL // 2
    g_lo = gstart_ref[e]
    g_hi = gstart_ref[e + 1]
    g_end = gstart_ref[ne]

    def x_copy(g, slot):
        return pltpu.make_async_copy(xs_hbm.at[pl.ds(g * bm, bm), :], xbuf_ref.at[slot], xsem.at[slot])

    def y_copy(g, slot):
        return pltpu.make_async_copy(ybuf_ref.at[slot], y_hbm.at[pl.ds(g * bm, bm), :], ysem.at[slot])

    @pl.when((e == 0) & (g_end > 0))
    def _():
        x_copy(0, 0).start()

    @pl.when(g_hi > g_lo)
    def _():
        wgb_ref[...] = wg_ref[...].astype(BF16)
        wub_ref[...] = wu_ref[...].astype(BF16)
        wdb_ref[...] = wd_ref[...].astype(BF16)

    row = lax.broadcasted_iota(I32, (bm, half), 0)

    def block(g, carry):
        slot = g % 2
        x_copy(g, slot).wait()

        @pl.when(g + 1 < g_end)
        def _():
            x_copy(g + 1, 1 - slot).start()

        n_valid = cnt_ref[e] - (g - g_lo) * bm
        x_lo, x_hi = _unpack_bf16_pair(jnp.where(row < n_valid, xbuf_ref[slot], jnp.uint32(0)))
        x_lo = x_lo.astype(BF16)
        x_hi = x_hi.astype(BF16)
        gate = _dot(x_lo, wgb_ref[:half, :]) + _dot(x_hi, wgb_ref[half:, :])
        up = _dot(x_lo, wub_ref[:half, :]) + _dot(x_hi, wub_ref[half:, :])
        hmid = (_silu(gate) * up).astype(BF16)
        y = _dot(hmid, wdb_ref[...])

        @pl.when(g >= 2)
        def _():
            y_copy(g - 2, slot).wait()

        ybuf_ref[slot] = _pack_bf16_pair(y[:, :half], y[:, half:])
        y_copy(g, slot).start()
        return carry

    lax.fori_loop(g_lo, g_hi, block, 0)

    @pl.when(e == ne - 1)
    def _():
        @pl.when(g_end >= 2)
        def _():
            y_copy(g_end - 2, g_end % 2).wait()

        @pl.when(g_end >= 1)
        def _():
            y_copy(g_end - 1, (g_end - 1) % 2).wait()

        ybuf_ref[0] = jnp.zeros((bm, half), U32)

        def fill(g, carry):
            y_copy(g, 0).start()
            return carry

        def drain(g, carry):
            y_copy(g, 0).wait()
            return carry

        lax.fori_loop(g_end, nblk, fill, 0)
        lax.fori_loop(g_end, nblk, drain, 0)


def _experts(gstart, counts, xs, w_gate, w_up, w_down):
    bm = BM_EXP
    nblk = xs.shape[0] // bm
    half = D_MODEL // 2
    grid_spec = pltpu.PrefetchScalarGridSpec(
        num_scalar_prefetch=2,
        grid=(N_EXPERTS,),
        in_specs=[
            pl.BlockSpec(memory_space=pl.ANY),
            pl.BlockSpec((None, D_MODEL, EXPERT_FF), lambda e, gs, cn: (e, 0, 0)),
            pl.BlockSpec((None, D_MODEL, EXPERT_FF), lambda e, gs, cn: (e, 0, 0)),
            pl.BlockSpec((None, EXPERT_FF, D_MODEL), lambda e, gs, cn: (e, 0, 0)),
        ],
        out_specs=pl.BlockSpec(memory_space=pl.ANY),
        scratch_shapes=[
            pltpu.VMEM((D_MODEL, EXPERT_FF), BF16),
            pltpu.VMEM((D_MODEL, EXPERT_FF), BF16),
            pltpu.VMEM((EXPERT_FF, D_MODEL), BF16),
            pltpu.VMEM((2, bm, half), U32),
            pltpu.VMEM((2, bm, half), U32),
            pltpu.SemaphoreType.DMA((2,)),
            pltpu.SemaphoreType.DMA((2,)),
        ],
    )
    return pl.pallas_call(
        _expert_kernel,
        grid_spec=grid_spec,
        out_shape=jax.ShapeDtypeStruct((nblk * bm, half), U32),
        compiler_params=pltpu.CompilerParams(
            dimension_semantics=("arbitrary",), vmem_limit_bytes=VMEM_LIMIT),
        name="experts",
    )(gstart, counts, xs, w_gate, w_up, w_down)


def _sc_gather_rows(table, idx):
    nrows = idx.shape[0]
    d = table.shape[1]
    per_w = nrows // SC_NW
    nwin = per_w // SC_WIN
    assert per_w * SC_NW == nrows and nwin * SC_WIN == per_w and nwin % 2 == 0
    @functools.partial(
        pl.kernel, mesh=_sc_mesh(),
        out_type=jax.ShapeDtypeStruct((nrows, d), table.dtype),
        scratch_types=[
            pltpu.VMEM((nwin, SC_WIN), I32),
            pltpu.VMEM((2, SC_WIN, d), table.dtype),
            pltpu.SemaphoreType.DMA((2,)),
            pltpu.SemaphoreType.DMA((2,)),
        ],
        compiler_params=pltpu.CompilerParams(use_tc_tiling_on_sc=True),
        name="sc_gather_rows",
    )
    def gather_kernel(table_hbm, idx_hbm, out_hbm, idx_v, rows_v, gsem, wsem):
        wid = lax.axis_index("s") * SC_NC + lax.axis_index("c")
        base = wid * per_w
        pltpu.sync_copy(idx_hbm.at[wid], idx_v)

        def gather(w, slot):
            return pltpu.make_async_copy(table_hbm.at[idx_v.at[w]], rows_v.at[slot], gsem.at[slot])

        def put(w, slot):
            return pltpu.make_async_copy(
                rows_v.at[slot], out_hbm.at[pl.ds(base + w * SC_WIN, SC_WIN)], wsem.at[slot])

        gather(0, 0).start()

        @pl.loop(0, nwin, step=2)
        def _(w0):
            for slot in range(2):
                w = w0 + slot
                gather(w, slot).wait()

                @pl.when(w + 1 < nwin)
                def _():
                    @pl.when(w >= 1)
                    def _():
                        put(w - 1, 1 - slot).wait()
                    gather(w + 1, 1 - slot).start()

                put(w, slot).start()

        put(nwin - 2, 0).wait()
        put(nwin - 1, 1).wait()

    return gather_kernel(table, idx.reshape(SC_NW, nwin, SC_WIN))


def _combine_kernel(y_ref, base_ref, gate_ref, g_ref, b_ref, out_ref):
    half = D_MODEL // 2
    gates = gate_ref[...]
    acc_lo = base_ref[:, :half]
    acc_hi = base_ref[:, half:]
    for k in range(TOP_K):
        y_lo, y_hi = _unpack_bf16_pair(y_ref[k])
        gk = gates[:, k:k + 1]
        acc_lo = acc_lo + gk * y_lo
        acc_hi = acc_hi + gk * y_hi
    mu = (jnp.sum(acc_lo, axis=-1, keepdims=True) + jnp.sum(acc_hi, axis=-1, keepdims=True)) / D_MODEL
    c_lo = acc_lo - mu
    c_hi = acc_hi - mu
    var = (jnp.sum(c_lo * c_lo, axis=-1, keepdims=True)
           + jnp.sum(c_hi * c_hi, axis=-1, keepdims=True)) / D_MODEL
    inv = lax.rsqrt(var + LN_EPS)
    out_ref[:, :half] = c_lo * inv * g_ref[:, :half] + b_ref[:, :half]
    out_ref[:, half:] = c_hi * inv * g_ref[:, half:] + b_ref[:, half:]


def _combine(ybuf, base, gate_tok, ln_g, ln_b):
    n = base.shape[0]
    tt = TT_COMB
    half = D_MODEL // 2
    return pl.pallas_call(
        _combine_kernel,
        grid=(n // tt,),
        in_specs=[
            pl.BlockSpec((TOP_K, tt, half), lambda i: (0, i, 0)),
            pl.BlockSpec((tt, D_MODEL), lambda i: (i, 0)),
            pl.BlockSpec((tt, TOP_K), lambda i: (i, 0)),
            pl.BlockSpec((1, D_MODEL), lambda i: (0, 0)),
            pl.BlockSpec((1, D_MODEL), lambda i: (0, 0)),
        ],
        out_specs=pl.BlockSpec((tt, D_MODEL), lambda i: (i, 0)),
        out_shape=jax.ShapeDtypeStruct((n, D_MODEL), F32),
        compiler_params=pltpu.CompilerParams(
            dimension_semantics=("parallel",), vmem_limit_bytes=VMEM_LIMIT),
        name="combine",
    )(ybuf, base, gate_tok, ln_g, ln_b)


def _regroup_w_in(w_in):
    o = 0
    cols = {}
    for name, width in (("dnq", DN_WIDTH), ("dnk", DN_WIDTH), ("dnv", DN_WIDTH), ("sq", SWA_WIDTH),
                        ("sk", SWA_KV_WIDTH), ("sv", SWA_KV_WIDTH), ("z", DN_WIDTH),
                        ("b", DN_HEADS), ("a", DN_HEADS)):
        cols[name] = w_in[:, o:o + width]
        o += width
    w_main = jnp.concatenate([cols[k] for k in ("dnq", "dnk", "dnv", "z", "sq", "sk", "sv")], axis=1)
    w_gates = jnp.concatenate(
        [cols["b"], cols["a"], jnp.zeros((D_MODEL, GATE_COLS - 2 * DN_HEADS), w_in.dtype)], axis=1)
    return w_main.astype(BF16), w_gates.astype(BF16)


def _layer(x, w_in, conv_w, a_log, dt_bias, dn_norm_g, sinks, w_out, ln1_g, ln1_b,
           router_w, router_bias, w_gate, w_up, w_down, sh_gate, sh_up, sh_down, ln2_g, ln2_b):
    b, t, d = x.shape
    n = b * t
    x2d = x.reshape(n, d)

    w_main, w_gates = _regroup_w_in(w_in)
    main, gates = _in_proj(x2d, w_main, w_gates)
    main3d = main.reshape(b, t, MAIN_COLS)

    pad = jnp.zeros((GATE_COLS - 2 * DN_HEADS,), F32)
    gpar = jnp.stack([jnp.concatenate([jnp.zeros((DN_HEADS,), F32), a_log.astype(F32), pad]),
                      jnp.concatenate([jnp.zeros((DN_HEADS,), F32), dt_bias.astype(F32), pad])])
    dn_out = _deltanet(main3d, gates.reshape(b, t, GATE_COLS), conv_w.astype(F32), gpar,
                       dn_norm_g.astype(F32).reshape(1, DN_HEAD_DIM))
    swa_out = _swa(main3d, sinks.astype(F32))

    rw_t = router_w.T.astype(F32)
    rwh = rw_t.astype(BF16)
    rwl = (rw_t - rwh.astype(F32)).astype(BF16)
    base, xpk, logits_t = _post_mix(
        x2d, dn_out.reshape(n, DN_WIDTH), swa_out.reshape(n, SWA_WIDTH),
        w_out[:DN_WIDTH].astype(BF16), w_out[DN_WIDTH:].astype(BF16),
        ln1_g.reshape(1, d).astype(F32), ln1_b.reshape(1, d).astype(F32), rwh, rwl,
        sh_gate.astype(BF16), sh_up.astype(BF16), sh_down.astype(BF16))

    eidx, gate, rank, cnt = _route(logits_t, router_bias.astype(F32).reshape(N_EXPERTS, 1))

    bm = BM_EXP
    counts = cnt[:, 0]
    padded = (counts + bm - 1) // bm * bm
    pend = jnp.cumsum(padded)
    pstart = pend - padded
    nblk = -(-(n * TOP_K) // bm) + N_EXPERTS
    gstart = (jnp.concatenate([pstart, pend[-1:]]) // bm).astype(I32)

    dest = _place(eidx, rank, pstart.astype(F32).reshape(N_EXPERTS, 1))
    xs = _sc_scatter_rows(xpk, dest, nblk * bm)
    ypk = _experts(gstart, counts, xs, w_gate, w_up, w_down)
    ybuf = _sc_gather_rows(ypk, dest.reshape(-1)).reshape(TOP_K, n, d // 2)
    out = _combine(ybuf, base, gate.T, ln2_g.reshape(1, d).astype(F32), ln2_b.reshape(1, d).astype(F32))
    return out.reshape(b, t, d)


def kernel(x, w_in, conv_w, a_log, dt_bias, dn_norm_g, sinks, w_out, ln1_g, ln1_b, router_w, router_bias,
           w_gate, w_up, w_down, shared_w_gate, shared_w_up, shared_w_down, ln2_g, ln2_b):
    depth = w_in.shape[0]
    for l in range(depth):
        x = _layer(x, w_in[l], conv_w[l], a_log[l], dt_bias[l], dn_norm_g[l], sinks[l], w_out[l],
                   ln1_g[l], ln1_b[l], router_w[l], router_bias[l], w_gate[l], w_up[l], w_down[l],
                   shared_w_gate[l], shared_w_up[l], shared_w_down[l], ln2_g[l], ln2_b[l])
    return x
```

```python
import functools

import jax
import jax.numpy as jnp
from jax import lax
from jax.experimental import pallas as pl
from jax.experimental.pallas import tpu as pltpu
from jax.experimental.pallas import tpu_sc as plsc

F32 = jnp.float32
BF16 = jnp.bfloat16
I32 = jnp.int32
U32 = jnp.uint32

D_MODEL = 1024
DN_HEADS = 4
DN_HEAD_DIM = 128
DN_WIDTH = DN_HEADS * DN_HEAD_DIM
CONV_WIDTH = 4
DN_CHUNK = 64
SWA_Q_HEADS = 8
SWA_KV_HEADS = 2
SWA_HEAD_DIM = 64
SWA_WIDTH = SWA_Q_HEADS * SWA_HEAD_DIM
SWA_KV_WIDTH = SWA_KV_HEADS * SWA_HEAD_DIM
SWA_WINDOW = 128
SWA_BLOCK = 128
N_EXPERTS = 256
N_GROUPS = 8
GROUP_SIZE = N_EXPERTS // N_GROUPS
TOPK_GROUPS = 4
TOP_K = 8
EXPERT_FF = 256
SHARED_FF = 256
ROUTED_SCALE = 2.5
DEEPNORM_ALPHA = 2.0 ** 0.25
LN_EPS = 1e-5
RMS_EPS = 1e-6
L2_EPS = 1e-6

COL_DNQ = 0
COL_DNK = DN_WIDTH
COL_DNV = 2 * DN_WIDTH
COL_Z = 3 * DN_WIDTH
COL_SQ = 4 * DN_WIDTH
COL_SK = COL_SQ + SWA_WIDTH
COL_SV = COL_SK + SWA_KV_WIDTH
MAIN_COLS = COL_SV + SWA_KV_WIDTH
GATE_COLS = 128

TM_PROJ = 512
TS_DN = 512
DN_A_UNROLL = 2
TM_POST = 512
TT_ROUTE = 512
BM_EXP = 256
EXP_SLOTS = 4
TT_COMB = 256
SC_NC = 2
SC_NS = 16
SC_NW = SC_NC * SC_NS
SC_WIN = 64
VMEM_LIMIT = 56 * 1024 * 1024
NEG_INF = float("-inf")


def _dot(a, b):
    return jnp.dot(a, b, preferred_element_type=F32)


def _mm(a, b):
    return _dot(a.astype(BF16), b.astype(BF16))


def _mm_nt(a, b):
    return lax.dot_general(a.astype(BF16), b.astype(BF16), (((1,), (1,)), ((), ())),
                           preferred_element_type=F32)


def _mm_tn(a, b):
    return lax.dot_general(a.astype(BF16), b.astype(BF16), (((0,), (0,)), ((), ())),
                           preferred_element_type=F32)


def _split2(a):
    hi = a.astype(BF16)
    lo = (a - hi.astype(F32)).astype(BF16)
    return hi, lo


def _mm3(a, b):
    ah, al = _split2(a)
    bh, bl = _split2(b)
    return _dot(ah, bh) + _dot(ah, bl) + _dot(al, bh)


def _mm_exact_lhs(l_bf16, g):
    g1 = g.astype(BF16)
    r1 = g - g1.astype(F32)
    g2 = r1.astype(BF16)
    g3 = (r1 - g2.astype(F32)).astype(BF16)
    return _dot(l_bf16, g1) + _dot(l_bf16, g2) + _dot(l_bf16, g3)


def _sigmoid(x):
    return 1.0 / (1.0 + jnp.exp(-x))


def _silu(x):
    return x * _sigmoid(x)


def _in_proj_kernel(x_ref, w_ref, wg_ref, main_ref, gates_ref):
    xb = x_ref[...].astype(BF16)
    main_ref[...] = _dot(xb, w_ref[...]).astype(BF16)
    gates_ref[...] = _dot(xb, wg_ref[...])


def _in_proj(x2d, w_main, w_gates):
    n = x2d.shape[0]
    return pl.pallas_call(
        _in_proj_kernel,
        grid=(n // TM_PROJ,),
        in_specs=[
            pl.BlockSpec((TM_PROJ, D_MODEL), lambda i: (i, 0)),
            pl.BlockSpec((D_MODEL, MAIN_COLS), lambda i: (0, 0)),
            pl.BlockSpec((D_MODEL, GATE_COLS), lambda i: (0, 0)),
        ],
        out_specs=[
            pl.BlockSpec((TM_PROJ, MAIN_COLS), lambda i: (i, 0)),
            pl.BlockSpec((TM_PROJ, GATE_COLS), lambda i: (i, 0)),
        ],
        out_shape=[
            jax.ShapeDtypeStruct((n, MAIN_COLS), BF16),
            jax.ShapeDtypeStruct((n, GATE_COLS), F32),
        ],
        compiler_params=pltpu.CompilerParams(
            dimension_semantics=("parallel",), vmem_limit_bytes=VMEM_LIMIT),
        name="in_proj",
    )(x2d, w_main, w_gates)


def _dn_kernel(x_ref, gates_ref, convw_ref, gpar_ref, normg_ref, out_ref,
               xc_ref, gl_ref, gc_ref, wq_ref, u_ref, kt_ref, attn_ref, egl_ref, s_ref, hist_ref):
    ts = x_ref.shape[0]
    c = DN_CHUNK
    hd = DN_HEAD_DIM
    qkv_w = 3 * DN_WIDTH
    nch = ts // c

    @pl.when(pl.program_id(1) == 0)
    def _():
        s_ref[...] = jnp.zeros_like(s_ref)
        hist_ref[...] = jnp.zeros_like(hist_ref)

    xc_ref[0:8, :] = hist_ref[...]
    xc_ref[8:ts + 8, :] = x_ref[:, 0:qkv_w].astype(F32)
    hist_ref[...] = xc_ref[ts:ts + 8, :]

    gsl = gates_ref[...]
    sp_in = gsl + gpar_ref[1:2, :]
    softplus = jnp.maximum(sp_in, 0.0) + jnp.log(1.0 + jnp.exp(-jnp.abs(sp_in)))
    lane = lax.broadcasted_iota(I32, gsl.shape, 1)
    gl = jnp.where(lane < DN_HEADS, _sigmoid(gsl), -jnp.exp(gpar_ref[0:1, :]) * softplus)
    gl_ref[...] = gl
    row_in_chunk = lax.broadcasted_iota(I32, gsl.shape, 0) % c
    gc = gl
    shift = 1
    while shift < c:
        gc = gc + jnp.where(row_in_chunk >= shift, pltpu.roll(gc, shift, 0), 0.0)
        shift *= 2
    gc_ref[...] = gc

    ii = lax.broadcasted_iota(I32, (c, c), 0)
    jj = lax.broadcasted_iota(I32, (c, c), 1)
    tri_incl = ii >= jj
    tri_strict = ii > jj
    eye = jnp.where(ii == jj, 1.0, 0.0).astype(F32)
    heads = range(DN_HEADS)

    def conv_silu(r0, col):
        xt = xc_ref[pl.ds(r0, c + 8), col:col + hd]
        w = convw_ref[:, col:col + hd]
        y = w[CONV_WIDTH - 1:CONV_WIDTH, :] * xt[8:8 + c, :]
        for j in range(CONV_WIDTH - 1):
            off = 8 - (CONV_WIDTH - 1) + j
            y = y + w[j:j + 1, :] * xt[off:off + c, :]
        return _silu(y)

    def l2n(t, scale):
        return t * (lax.rsqrt(jnp.sum(t * t, axis=-1, keepdims=True) + L2_EPS) * scale)

    def phase_a(i, carry):
        chains = []
        for sub in range(DN_A_UNROLL):
            ci = i * DN_A_UNROLL + sub
            r0 = pl.multiple_of(ci * c, c)
            glc = gl_ref[pl.ds(r0, c), :]
            gcc = gc_ref[pl.ds(r0, c), :]
            gct = jnp.concatenate([gcc, gcc], axis=0).T
            egl_ref[ci] = jnp.exp(gcc[c - 8:c, :])
            for h in heads:
                chains.append((ci, r0, h, glc, gcc, gct))
        nchain = len(chains)
        q = [l2n(conv_silu(r0, COL_DNQ + h * hd), hd ** -0.5) for (ci, r0, h, _, _, _) in chains]
        k = [l2n(conv_silu(r0, COL_DNK + h * hd), 1.0) for (ci, r0, h, _, _, _) in chains]
        v = [conv_silu(r0, COL_DNV + h * hd) for (ci, r0, h, _, _, _) in chains]
        kb, vb, decay, egc = [], [], [], []
        for n_, (ci, r0, h, glc, gcc, gct) in enumerate(chains):
            beta = glc[:, h:h + 1]
            gc_col = gcc[:, DN_HEADS + h:DN_HEADS + h + 1]
            gc_row = gct[DN_HEADS + h:DN_HEADS + h + 1, 0:c]
            decay.append(jnp.where(tri_incl, jnp.exp(jnp.minimum(gc_col - gc_row, 0.0)), 0.0))
            egc.append(jnp.exp(gc_col))
            e_tail = jnp.exp(gcc[c - 1:c, DN_HEADS + h:DN_HEADS + h + 1] - gc_col)
            kb.append(k[n_] * beta)
            vb.append(v[n_] * beta)
            kt_ref[ci, h] = (k[n_] * e_tail).astype(BF16)
        kq = [_mm_nt(jnp.concatenate([kb[n_], q[n_]], axis=0), k[n_]) for n_ in range(nchain)]
        a_mat = [jnp.where(tri_strict, kq[n_][0:c] * decay[n_], 0.0) for n_ in range(nchain)]
        for n_, (ci, r0, h, _, _, _) in enumerate(chains):
            attn_ref[ci, h] = (kq[n_][c:2 * c] * decay[n_]).astype(BF16)
        t_inv = [eye - a for a in a_mat]
        p = a_mat
        for _ in range(5):
            p = [_mm(x, x) for x in p]
            t_inv = [t + _mm(t, x) for t, x in zip(t_inv, p)]
        for n_, (ci, r0, h, _, _, _) in enumerate(chains):
            uw = _mm3(t_inv[n_], jnp.concatenate([vb[n_], kb[n_] * egc[n_]], axis=1))
            u_ref[ci, h] = uw[:, 0:hd]
            wq_ref[ci, h, 0:c, :] = uw[:, hd:2 * hd].astype(BF16)
            wq_ref[ci, h, c:2 * c, :] = (q[n_] * egc[n_]).astype(BF16)
        return carry

    lax.fori_loop(0, nch // DN_A_UNROLL, phase_a, 0)

    normg = normg_ref[...]

    def phase_b(ci, carry):
        r0 = pl.multiple_of(ci * c, c)
        rows = pl.ds(r0, c)
        egl = egl_ref[ci]
        s_old = [s_ref[h] for h in heads]
        ws = [_dot(wq_ref[ci, h], s_old[h].astype(BF16)) for h in heads]
        v_new = [(u_ref[ci, h] - ws[h][0:c]).astype(BF16) for h in heads]
        for h in heads:
            s_ref[h] = (s_old[h] * egl[7:8, DN_HEADS + h:DN_HEADS + h + 1]
                        + lax.dot_general(kt_ref[ci, h], v_new[h], (((0,), (0,)), ((), ())),
                                          preferred_element_type=F32))
        for h in heads:
            o = ws[h][c:2 * c] + _dot(attn_ref[ci, h], v_new[h])
            o = o * lax.rsqrt(jnp.mean(o * o, axis=-1, keepdims=True) + RMS_EPS) * normg
            z = x_ref[rows, COL_Z + h * hd:COL_Z + (h + 1) * hd].astype(F32)
            out_ref[rows, h * hd:(h + 1) * hd] = (o * _silu(z)).astype(out_ref.dtype)
        return carry

    lax.fori_loop(0, nch, phase_b, 0)


def _deltanet(main3d, gates3d, conv_w, gpar, normg):
    b, t, _ = main3d.shape
    ts = TS_DN
    nch = ts // DN_CHUNK
    dn_in = COL_Z + DN_WIDTH
    return pl.pallas_call(
        _dn_kernel,
        grid=(b, t // ts),
        in_specs=[
            pl.BlockSpec((None, ts, dn_in), lambda bi, si: (bi, si, 0)),
            pl.BlockSpec((None, ts, GATE_COLS), lambda bi, si: (bi, si, 0)),
            pl.BlockSpec((CONV_WIDTH, 3 * DN_WIDTH), lambda bi, si: (0, 0)),
            pl.BlockSpec((2, GATE_COLS), lambda bi, si: (0, 0)),
            pl.BlockSpec((1, DN_HEAD_DIM), lambda bi, si: (0, 0)),
        ],
        out_specs=pl.BlockSpec((None, ts, DN_WIDTH), lambda bi, si: (bi, si, 0)),
        out_shape=jax.ShapeDtypeStruct((b, t, DN_WIDTH), BF16),
        scratch_shapes=[
            pltpu.VMEM((ts + 8, 3 * DN_WIDTH), F32),
            pltpu.VMEM((ts, GATE_COLS), F32),
            pltpu.VMEM((ts, GATE_COLS), F32),
            pltpu.VMEM((nch, DN_HEADS, 2 * DN_CHUNK, DN_HEAD_DIM), BF16),
            pltpu.VMEM((nch, DN_HEADS, DN_CHUNK, DN_HEAD_DIM), F32),
            pltpu.VMEM((nch, DN_HEADS, DN_CHUNK, DN_HEAD_DIM), BF16),
            pltpu.VMEM((nch, DN_HEADS, DN_CHUNK, DN_CHUNK), BF16),
            pltpu.VMEM((nch, 8, GATE_COLS), F32),
            pltpu.VMEM((DN_HEADS, DN_HEAD_DIM, DN_HEAD_DIM), F32),
            pltpu.VMEM((8, 3 * DN_WIDTH), F32),
        ],
        compiler_params=pltpu.CompilerParams(
            dimension_semantics=("parallel", "arbitrary"), vmem_limit_bytes=VMEM_LIMIT),
        name="deltanet",
    )(main3d, gates3d, conv_w, gpar, normg)


def _swa_kernel(sinks_ref, q_ref, kp_ref, kc_ref, vp_ref, vc_ref, out_ref):
    n = pl.program_id(1)
    blk = SWA_BLOCK
    d = SWA_HEAD_DIM
    grp = SWA_Q_HEADS // SWA_KV_HEADS
    qi = lax.broadcasted_iota(I32, (blk, 2 * blk), 0)
    kj = lax.broadcasted_iota(I32, (blk, 2 * blk), 1)
    dist = qi + blk - kj
    valid = (dist >= 0) & (dist < SWA_WINDOW) & ((kj >= blk) | (n > 0))
    dist_f = dist.astype(F32)
    outs = []
    for hq in range(SWA_Q_HEADS):
        hk = hq // grp
        q = q_ref[:, hq * d:(hq + 1) * d]
        kband = jnp.concatenate([kp_ref[:, hk * d:(hk + 1) * d], kc_ref[:, hk * d:(hk + 1) * d]], axis=0)
        vband = jnp.concatenate([vp_ref[:, hk * d:(hk + 1) * d], vc_ref[:, hk * d:(hk + 1) * d]], axis=0)
        slope = 2.0 ** (-8.0 * (hq + 1.0) / SWA_Q_HEADS)
        s = _mm_nt(q, kband) * (d ** -0.5) - slope * dist_f
        s = jnp.where(valid, s, NEG_INF)
        sink = sinks_ref[hq]
        m = jnp.maximum(jnp.max(s, axis=-1, keepdims=True), sink)
        p = jnp.exp(s - m)
        denom = jnp.sum(p, axis=-1, keepdims=True) + jnp.exp(sink - m)
        outs.append(_mm(p, vband) / denom)
    out_ref[...] = jnp.concatenate(outs, axis=-1).astype(out_ref.dtype)


def _swa(main3d, sinks):
    b, t, _ = main3d.shape
    blk = SWA_BLOCK
    qb = COL_SQ // SWA_WIDTH
    kb = COL_SK // SWA_KV_WIDTH
    vb = COL_SV // SWA_KV_WIDTH
    grid_spec = pltpu.PrefetchScalarGridSpec(
        num_scalar_prefetch=1,
        grid=(b, t // blk),
        in_specs=[
            pl.BlockSpec((None, blk, SWA_WIDTH), lambda bi, ni, s: (bi, ni, qb)),
            pl.BlockSpec((None, blk, SWA_KV_WIDTH), lambda bi, ni, s: (bi, jnp.maximum(ni - 1, 0), kb)),
            pl.BlockSpec((None, blk, SWA_KV_WIDTH), lambda bi, ni, s: (bi, ni, kb)),
            pl.BlockSpec((None, blk, SWA_KV_WIDTH), lambda bi, ni, s: (bi, jnp.maximum(ni - 1, 0), vb)),
            pl.BlockSpec((None, blk, SWA_KV_WIDTH), lambda bi, ni, s: (bi, ni, vb)),
        ],
        out_specs=pl.BlockSpec((None, blk, SWA_WIDTH), lambda bi, ni, s: (bi, ni, 0)),
    )
    return pl.pallas_call(
        _swa_kernel,
        grid_spec=grid_spec,
        out_shape=jax.ShapeDtypeStruct((b, t, SWA_WIDTH), BF16),
        compiler_params=pltpu.CompilerParams(
            dimension_semantics=("parallel", "parallel"), vmem_limit_bytes=VMEM_LIMIT),
        name="swa",
    )(sinks, main3d, main3d, main3d, main3d, main3d)


def _layer_norm(y, g, b):
    mu = jnp.mean(y, axis=-1, keepdims=True)
    yc = y - mu
    var = jnp.mean(yc * yc, axis=-1, keepdims=True)
    return yc * lax.rsqrt(var + LN_EPS) * g + b


def _pack_bf16_pair(lo, hi):
    lo_bits = lax.bitcast_convert_type(lo.astype(BF16).astype(F32), U32)
    hi_bits = lax.bitcast_convert_type(hi.astype(BF16).astype(F32), U32)
    return (hi_bits & jnp.uint32(0xFFFF0000)) | (lo_bits >> 16)


def _unpack_bf16_pair(packed):
    lo = lax.bitcast_convert_type(packed << 16, F32)
    hi = lax.bitcast_convert_type(packed & jnp.uint32(0xFFFF0000), F32)
    return lo, hi


def _post_mix_kernel(x_ref, dn_ref, swa_ref, wo_dn_ref, wo_swa_ref, g_ref, b_ref,
                     rwh_ref, rwl_ref, sg_ref, su_ref, sd_ref,
                     base_ref, xpk_ref, logit_ref):
    mix = _dot(dn_ref[...], wo_dn_ref[...]) + _dot(swa_ref[...], wo_swa_ref[...])
    x1 = _layer_norm(DEEPNORM_ALPHA * x_ref[...] + mix, g_ref[...], b_ref[...])
    half = D_MODEL // 2
    xpk_ref[...] = _pack_bf16_pair(x1[:, :half], x1[:, half:])
    xh, xl = _split2(x1)
    nt = (((1,), (1,)), ((), ()))
    logit_ref[...] = (lax.dot_general(rwh_ref[...], xh, nt, preferred_element_type=F32)
                      + lax.dot_general(rwh_ref[...], xl, nt, preferred_element_type=F32)
                      + lax.dot_general(rwl_ref[...], xh, nt, preferred_element_type=F32))
    hmid = _silu(_dot(xh, sg_ref[...])) * _dot(xh, su_ref[...])
    base_ref[...] = DEEPNORM_ALPHA * x1 + _dot(hmid.astype(BF16), sd_ref[...])


def _post_mix(x2d, dn2d, swa2d, wo_dn, wo_swa, ln_g, ln_b, rwh, rwl, sg, su, sd):
    n = x2d.shape[0]
    tm = TM_POST
    full = lambda shape: pl.BlockSpec(shape, lambda i: (0, 0))
    return pl.pallas_call(
        _post_mix_kernel,
        grid=(n // tm,),
        in_specs=[
            pl.BlockSpec((tm, D_MODEL), lambda i: (i, 0)),
            pl.BlockSpec((tm, DN_WIDTH), lambda i: (i, 0)),
            pl.BlockSpec((tm, SWA_WIDTH), lambda i: (i, 0)),
            full((DN_WIDTH, D_MODEL)), full((SWA_WIDTH, D_MODEL)),
            full((1, D_MODEL)), full((1, D_MODEL)),
            full((N_EXPERTS, D_MODEL)), full((N_EXPERTS, D_MODEL)),
            full((D_MODEL, SHARED_FF)), full((D_MODEL, SHARED_FF)), full((SHARED_FF, D_MODEL)),
        ],
        out_specs=[
            pl.BlockSpec((tm, D_MODEL), lambda i: (i, 0)),
            pl.BlockSpec((tm, D_MODEL // 2), lambda i: (i, 0)),
            pl.BlockSpec((N_EXPERTS, tm), lambda i: (0, i)),
        ],
        out_shape=[
            jax.ShapeDtypeStruct((n, D_MODEL), F32),
            jax.ShapeDtypeStruct((n, D_MODEL // 2), U32),
            jax.ShapeDtypeStruct((N_EXPERTS, n), F32),
        ],
        compiler_params=pltpu.CompilerParams(
            dimension_semantics=("parallel",), vmem_limit_bytes=VMEM_LIMIT),
        name="post_mix",
    )(x2d, dn2d, swa2d, wo_dn, wo_swa, ln_g, ln_b, rwh, rwl, sg, su, sd)


def _route_kernel(lg_ref, bias_ref, eidx_ref, gate_ref, rank_ref, cnt_ref, carry_ref):
    @pl.when(pl.program_id(0) == 0)
    def _():
        carry_ref[...] = jnp.zeros_like(carry_ref)

    tt = lg_ref.shape[1]
    scores = _sigmoid(lg_ref[...])
    sel = scores + bias_ref[...]

    iog = lax.broadcasted_iota(I32, (GROUP_SIZE, tt), 0)
    grp_rows = []
    for g in range(N_GROUPS):
        blk = sel[g * GROUP_SIZE:(g + 1) * GROUP_SIZE, :]
        m1 = jnp.max(blk, axis=0, keepdims=True)
        i1 = jnp.min(jnp.where(blk == m1, iog, GROUP_SIZE), axis=0, keepdims=True)
        m2 = jnp.max(jnp.where(iog == i1, NEG_INF, blk), axis=0, keepdims=True)
        grp_rows.append(m1 + m2)
    gs = jnp.concatenate(grp_rows, axis=0)

    io8 = lax.broadcasted_iota(I32, (N_GROUPS, tt), 0)
    gsel = jnp.zeros((N_GROUPS, tt), F32)
    for _ in range(TOPK_GROUPS):
        mg = jnp.max(gs, axis=0, keepdims=True)
        ig = jnp.min(jnp.where(gs == mg, io8, N_GROUPS), axis=0, keepdims=True)
        hit = io8 == ig
        gsel = jnp.where(hit, 1.0, gsel)
        gs = jnp.where(hit, NEG_INF, gs)

    val = jnp.concatenate(
        [jnp.where(gsel[g:g + 1, :] > 0.0, sel[g * GROUP_SIZE:(g + 1) * GROUP_SIZE, :], NEG_INF)
         for g in range(N_GROUPS)], axis=0)

    ioe = lax.broadcasted_iota(I32, (N_EXPERTS, tt), 0)
    onehot = jnp.zeros((N_EXPERTS, tt), F32)
    idx_rows, gate_rows = [], []
    for _ in range(TOP_K):
        m = jnp.max(val, axis=0, keepdims=True)
        ik = jnp.min(jnp.where(val == m, ioe, N_EXPERTS), axis=0, keepdims=True)
        hit = ioe == ik
        gate_rows.append(jnp.sum(jnp.where(hit, scores, 0.0), axis=0, keepdims=True))
        idx_rows.append(ik)
        val = jnp.where(hit, NEG_INF, val)
        onehot = jnp.where(hit, 1.0, onehot)
    gsum = gate_rows[0]
    for r in gate_rows[1:]:
        gsum = gsum + r
    gate_ref[...] = jnp.concatenate(gate_rows, axis=0) / gsum * ROUTED_SCALE
    eidx_ref[...] = jnp.concatenate(idx_rows, axis=0)

    ti = lax.broadcasted_iota(I32, (tt, tt), 0)
    tj = lax.broadcasted_iota(I32, (tt, tt), 1)
    upper = jnp.where(ti < tj, 1.0, 0.0).astype(BF16)
    cum = _dot(onehot.astype(BF16), upper) + jnp.broadcast_to(carry_ref[:, 0:1], (N_EXPERTS, tt))
    rank_rows = [jnp.sum(jnp.where(ioe == ik, cum, 0.0), axis=0, keepdims=True) for ik in idx_rows]
    rank_ref[...] = jnp.concatenate(rank_rows, axis=0).astype(I32)
    carry_ref[...] = carry_ref[...] + jnp.broadcast_to(
        jnp.sum(onehot, axis=1, keepdims=True), carry_ref.shape)
    cnt_ref[...] = carry_ref[...].astype(I32)


def _route(logits_t, bias_col):
    n = logits_t.shape[1]
    tt = TT_ROUTE
    row_spec = pl.BlockSpec((TOP_K, tt), lambda i: (0, i))
    return pl.pallas_call(
        _route_kernel,
        grid=(n // tt,),
        in_specs=[
            pl.BlockSpec((N_EXPERTS, tt), lambda i: (0, i)),
            pl.BlockSpec((N_EXPERTS, 1), lambda i: (0, 0)),
        ],
        out_specs=[row_spec, row_spec, row_spec,
                   pl.BlockSpec((N_EXPERTS, 128), lambda i: (0, 0))],
        out_shape=[
            jax.ShapeDtypeStruct((TOP_K, n), I32),
            jax.ShapeDtypeStruct((TOP_K, n), F32),
            jax.ShapeDtypeStruct((TOP_K, n), I32),
            jax.ShapeDtypeStruct((N_EXPERTS, 128), I32),
        ],
        scratch_shapes=[pltpu.VMEM((N_EXPERTS, 128), F32)],
        compiler_params=pltpu.CompilerParams(
            dimension_semantics=("arbitrary",), vmem_limit_bytes=VMEM_LIMIT),
        name="route",
    )(logits_t, bias_col)


def _place_kernel(eidx_ref, rank_ref, pstart_ref, dest_ref):
    tt = eidx_ref.shape[1]
    ioe = lax.broadcasted_iota(I32, (N_EXPERTS, tt), 0)
    pstart = pstart_ref[...]
    rows = [jnp.sum(jnp.where(ioe == eidx_ref[k:k + 1, :], pstart, 0.0), axis=0, keepdims=True)
            for k in range(TOP_K)]
    dest_ref[...] = jnp.concatenate(rows, axis=0).astype(I32) + rank_ref[...]


def _place(eidx, rank, pstart_col):
    n = eidx.shape[1]
    tt = TT_ROUTE
    row_spec = pl.BlockSpec((TOP_K, tt), lambda i: (0, i))
    return pl.pallas_call(
        _place_kernel,
        grid=(n // tt,),
        in_specs=[row_spec, row_spec, pl.BlockSpec((N_EXPERTS, 1), lambda i: (0, 0))],
        out_specs=row_spec,
        out_shape=jax.ShapeDtypeStruct((TOP_K, n), I32),
        compiler_params=pltpu.CompilerParams(
            dimension_semantics=("parallel",), vmem_limit_bytes=VMEM_LIMIT),
        name="place",
    )(eidx, rank, pstart_col)


def _sc_mesh():
    return plsc.VectorSubcoreMesh(core_axis_name="c", subcore_axis_name="s",
                                  num_cores=SC_NC, num_subcores=SC_NS)


def _sc_scatter_rows(rows, idx, nrows_out):
    n, d = rows.shape
    nk = idx.shape[0]
    per_w = n // SC_NW
    nwin = per_w // SC_WIN
    assert per_w * SC_NW == n and nwin * SC_WIN == per_w and nwin % 2 == 0

    @functools.partial(
        pl.kernel, mesh=_sc_mesh(),
        out_type=jax.ShapeDtypeStruct((nrows_out, d), rows.dtype),
        scratch_types=[
            pltpu.VMEM((nwin, nk, SC_WIN), I32),
            pltpu.VMEM((2, SC_WIN, d), rows.dtype),
            pltpu.SemaphoreType.DMA((2,)),
            pltpu.SemaphoreType.DMA((2,)),
        ],
        compiler_params=pltpu.CompilerParams(use_tc_tiling_on_sc=True),
        name="sc_scatter_rows",
    )
    def scatter_kernel(rows_hbm, idx_hbm, out_hbm, idx_v, rows_v, lsem, ssem):
        wid = lax.axis_index("s") * SC_NC + lax.axis_index("c")
        base = wid * per_w
        pltpu.sync_copy(idx_hbm.at[wid], idx_v)

        def load(w, slot):
            return pltpu.make_async_copy(
                rows_hbm.at[pl.ds(base + w * SC_WIN, SC_WIN)], rows_v.at[slot], lsem.at[slot])

        def scat(w, k, slot):
            return pltpu.make_async_copy(rows_v.at[slot], out_hbm.at[idx_v.at[w, k]], ssem.at[slot])

        load(0, 0).start()

        @pl.loop(0, nwin, step=2)
        def _(w0):
            for slot in range(2):
                w = w0 + slot
                load(w, slot).wait()

                @pl.when(w + 1 < nwin)
                def _():
                    @pl.when(w >= 1)
                    def _():
                        for k in range(nk):
                            scat(w - 1, k, 1 - slot).wait()
                    load(w + 1, 1 - slot).start()

                for k in range(nk):
                    scat(w, k, slot).start()

        for k in range(nk):
            scat(nwin - 2, k, 0).wait()
        for k in range(nk):
            scat(nwin - 1, k, 1).wait()

    idx4 = idx.reshape(nk, SC_NW, nwin, SC_WIN).transpose(1, 2, 0, 3)
    return scatter_kernel(rows, idx4)


def _expert_kernel(gstart_ref, cnt_ref, xs_hbm, wg_ref, wu_ref, wd_ref, y_hbm,
                   wgb_ref, wub_ref, wdb_ref, xbuf_ref, ybuf_ref, xsem, ysem):
    e = pl.program_id(0)
    ne = pl.num_programs(0)
    bm = xbuf_ref.shape[1]
    nblk = y_hbm.shape[0] // bm
    half = D_MODEL // 2
    g_lo = gstart_ref[e]
    g_hi = gstart_ref[e + 1]
    g_end = gstart_ref[ne]

    def x_copy(g, slot):
        return pltpu.make_async_copy(xs_hbm.at[pl.ds(g * bm, bm), :], xbuf_ref.at[slot], xsem.at[slot])

    def y_copy(g, slot):
        return pltpu.make_async_copy(ybuf_ref.at[slot], y_hbm.at[pl.ds(g * bm, bm), :], ysem.at[slot])

    nslot = xbuf_ref.shape[0]

    @pl.when(e == 0)
    def _():
        for g0 in range(nslot - 1):
            @pl.when(g0 < g_end)
            def _():
                x_copy(g0, g0).start()

    @pl.when(g_hi > g_lo)
    def _():
        wgb_ref[...] = wg_ref[...].astype(BF16)
        wub_ref[...] = wu_ref[...].astype(BF16)
        wdb_ref[...] = wd_ref[...].astype(BF16)

    row = lax.broadcasted_iota(I32, (bm, half), 0)

    def block(g, carry):
        slot = g % nslot
        x_copy(g, slot).wait()

        @pl.when(g + (nslot - 1) < g_end)
        def _():
            x_copy(g + (nslot - 1), (g + (nslot - 1)) % nslot).start()

        n_valid = cnt_ref[e] - (g - g_lo) * bm
        x_lo, x_hi = _unpack_bf16_pair(jnp.where(row < n_valid, xbuf_ref[slot], jnp.uint32(0)))
        x_lo = x_lo.astype(BF16)
        x_hi = x_hi.astype(BF16)
        gate = _dot(x_lo, wgb_ref[:half, :]) + _dot(x_hi, wgb_ref[half:, :])
        up = _dot(x_lo, wub_ref[:half, :]) + _dot(x_hi, wub_ref[half:, :])
        hmid = (_silu(gate) * up).astype(BF16)
        y = _dot(hmid, wdb_ref[...])

        @pl.when(g >= nslot)
        def _():
            y_copy(g - nslot, slot).wait()

        ybuf_ref[slot] = _pack_bf16_pair(y[:, :half], y[:, half:])
        y_copy(g, slot).start()
        return carry

    lax.fori_loop(g_lo, g_hi, block, 0)

    @pl.when(e == ne - 1)
    def _():
        for back in range(nslot, 0, -1):
            @pl.when(g_end >= back)
            def _():
                y_copy(g_end - back, (g_end - back) % nslot).wait()

        ybuf_ref[0] = jnp.zeros((bm, half), U32)

        def fill(g, carry):
            y_copy(g, 0).start()
            return carry

        def drain(g, carry):
            y_copy(g, 0).wait()
            return carry

        lax.fori_loop(g_end, nblk, fill, 0)
        lax.fori_loop(g_end, nblk, drain, 0)


def _experts(gstart, counts, xs, w_gate, w_up, w_down):
    bm = BM_EXP
    nblk = xs.shape[0] // bm
    half = D_MODEL // 2
    grid_spec = pltpu.PrefetchScalarGridSpec(
        num_scalar_prefetch=2,
        grid=(N_EXPERTS,),
        in_specs=[
            pl.BlockSpec(memory_space=pl.ANY),
            pl.BlockSpec((None, D_MODEL, EXPERT_FF), lambda e, gs, cn: (e, 0, 0)),
            pl.BlockSpec((None, D_MODEL, EXPERT_FF), lambda e, gs, cn: (e, 0, 0)),
            pl.BlockSpec((None, EXPERT_FF, D_MODEL), lambda e, gs, cn: (e, 0, 0)),
        ],
        out_specs=pl.BlockSpec(memory_space=pl.ANY),
        scratch_shapes=[
            pltpu.VMEM((D_MODEL, EXPERT_FF), BF16),
            pltpu.VMEM((D_MODEL, EXPERT_FF), BF16),
            pltpu.VMEM((EXPERT_FF, D_MODEL), BF16),
            pltpu.VMEM((EXP_SLOTS, bm, half), U32),
            pltpu.VMEM((EXP_SLOTS, bm, half), U32),
            pltpu.SemaphoreType.DMA((EXP_SLOTS,)),
            pltpu.SemaphoreType.DMA((EXP_SLOTS,)),
        ],
    )
    return pl.pallas_call(
        _expert_kernel,
        grid_spec=grid_spec,
        out_shape=jax.ShapeDtypeStruct((nblk * bm, half), U32),
        compiler_params=pltpu.CompilerParams(
            dimension_semantics=("arbitrary",), vmem_limit_bytes=VMEM_LIMIT),
        name="experts",
    )(gstart, counts, xs, w_gate, w_up, w_down)


def _sc_gather_rows(table, idx):
    nrows = idx.shape[0]
    d = table.shape[1]
    per_w = nrows // SC_NW
    nwin = per_w // SC_WIN
    assert per_w * SC_NW == nrows and nwin * SC_WIN == per_w and nwin % 2 == 0
    @functools.partial(
        pl.kernel, mesh=_sc_mesh(),
        out_type=jax.ShapeDtypeStruct((nrows, d), table.dtype),
        scratch_types=[
            pltpu.VMEM((nwin, SC_WIN), I32),
            pltpu.VMEM((2, SC_WIN, d), table.dtype),
            pltpu.SemaphoreType.DMA((2,)),
            pltpu.SemaphoreType.DMA((2,)),
        ],
        compiler_params=pltpu.CompilerParams(use_tc_tiling_on_sc=True),
        name="sc_gather_rows",
    )
    def gather_kernel(table_hbm, idx_hbm, out_hbm, idx_v, rows_v, gsem, wsem):
        wid = lax.axis_index("s") * SC_NC + lax.axis_index("c")
        base = wid * per_w
        pltpu.sync_copy(idx_hbm.at[wid], idx_v)

        def gather(w, slot):
            return pltpu.make_async_copy(table_hbm.at[idx_v.at[w]], rows_v.at[slot], gsem.at[slot])

        def put(w, slot):
            return pltpu.make_async_copy(
                rows_v.at[slot], out_hbm.at[pl.ds(base + w * SC_WIN, SC_WIN)], wsem.at[slot])

        gather(0, 0).start()

        @pl.loop(0, nwin, step=2)
        def _(w0):
            for slot in range(2):
                w = w0 + slot
                gather(w, slot).wait()

                @pl.when(w + 1 < nwin)
                def _():
                    @pl.when(w >= 1)
                    def _():
                        put(w - 1, 1 - slot).wait()
                    gather(w + 1, 1 - slot).start()

                put(w, slot).start()

        put(nwin - 2, 0).wait()
        put(nwin - 1, 1).wait()

    return gather_kernel(table, idx.reshape(SC_NW, nwin, SC_WIN))


def _combine_kernel(y_ref, base_ref, gate_ref, g_ref, b_ref, out_ref):
    half = D_MODEL // 2
    gates = gate_ref[...]
    acc_lo = base_ref[:, :half]
    acc_hi = base_ref[:, half:]
    for k in range(TOP_K):
        y_lo, y_hi = _unpack_bf16_pair(y_ref[k])
        gk = gates[:, k:k + 1]
        acc_lo = acc_lo + gk * y_lo
        acc_hi = acc_hi + gk * y_hi
    mu = (jnp.sum(acc_lo, axis=-1, keepdims=True) + jnp.sum(acc_hi, axis=-1, keepdims=True)) / D_MODEL
    c_lo = acc_lo - mu
    c_hi = acc_hi - mu
    var = (jnp.sum(c_lo * c_lo, axis=-1, keepdims=True)
           + jnp.sum(c_hi * c_hi, axis=-1, keepdims=True)) / D_MODEL
    inv = lax.rsqrt(var + LN_EPS)
    out_ref[:, :half] = c_lo * inv * g_ref[:, :half] + b_ref[:, :half]
    out_ref[:, half:] = c_hi * inv * g_ref[:, half:] + b_ref[:, half:]


def _combine(ybuf, base, gate_tok, ln_g, ln_b):
    n = base.shape[0]
    tt = TT_COMB
    half = D_MODEL // 2
    return pl.pallas_call(
        _combine_kernel,
        grid=(n // tt,),
        in_specs=[
            pl.BlockSpec((TOP_K, tt, half), lambda i: (0, i, 0)),
            pl.BlockSpec((tt, D_MODEL), lambda i: (i, 0)),
            pl.BlockSpec((tt, TOP_K), lambda i: (i, 0)),
            pl.BlockSpec((1, D_MODEL), lambda i: (0, 0)),
            pl.BlockSpec((1, D_MODEL), lambda i: (0, 0)),
        ],
        out_specs=pl.BlockSpec((tt, D_MODEL), lambda i: (i, 0)),
        out_shape=jax.ShapeDtypeStruct((n, D_MODEL), F32),
        compiler_params=pltpu.CompilerParams(
            dimension_semantics=("parallel",), vmem_limit_bytes=VMEM_LIMIT),
        name="combine",
    )(ybuf, base, gate_tok, ln_g, ln_b)


def _regroup_w_in(w_in):
    o = 0
    cols = {}
    for name, width in (("dnq", DN_WIDTH), ("dnk", DN_WIDTH), ("dnv", DN_WIDTH), ("sq", SWA_WIDTH),
                        ("sk", SWA_KV_WIDTH), ("sv", SWA_KV_WIDTH), ("z", DN_WIDTH),
                        ("b", DN_HEADS), ("a", DN_HEADS)):
        cols[name] = w_in[:, o:o + width]
        o += width
    w_main = jnp.concatenate([cols[k] for k in ("dnq", "dnk", "dnv", "z", "sq", "sk", "sv")], axis=1)
    w_gates = jnp.concatenate(
        [cols["b"], cols["a"], jnp.zeros((D_MODEL, GATE_COLS - 2 * DN_HEADS), w_in.dtype)], axis=1)
    return w_main.astype(BF16), w_gates.astype(BF16)


def _layer(x, w_in, conv_w, a_log, dt_bias, dn_norm_g, sinks, w_out, ln1_g, ln1_b,
           router_w, router_bias, w_gate, w_up, w_down, sh_gate, sh_up, sh_down, ln2_g, ln2_b):
    b, t, d = x.shape
    n = b * t
    x2d = x.reshape(n, d)

    w_main, w_gates = _regroup_w_in(w_in)
    main, gates = _in_proj(x2d, w_main, w_gates)
    main3d = main.reshape(b, t, MAIN_COLS)

    pad = jnp.zeros((GATE_COLS - 2 * DN_HEADS,), F32)
    gpar = jnp.stack([jnp.concatenate([jnp.zeros((DN_HEADS,), F32), a_log.astype(F32), pad]),
                      jnp.concatenate([jnp.zeros((DN_HEADS,), F32), dt_bias.astype(F32), pad])])
    dn_out = _deltanet(main3d, gates.reshape(b, t, GATE_COLS), conv_w.astype(F32), gpar,
                       dn_norm_g.astype(F32).reshape(1, DN_HEAD_DIM))
    swa_out = _swa(main3d, sinks.astype(F32))

    rw_t = router_w.T.astype(F32)
    rwh = rw_t.astype(BF16)
    rwl = (rw_t - rwh.astype(F32)).astype(BF16)
    base, xpk, logits_t = _post_mix(
        x2d, dn_out.reshape(n, DN_WIDTH), swa_out.reshape(n, SWA_WIDTH),
        w_out[:DN_WIDTH].astype(BF16), w_out[DN_WIDTH:].astype(BF16),
        ln1_g.reshape(1, d).astype(F32), ln1_b.reshape(1, d).astype(F32), rwh, rwl,
        sh_gate.astype(BF16), sh_up.astype(BF16), sh_down.astype(BF16))

    eidx, gate, rank, cnt = _route(logits_t, router_bias.astype(F32).reshape(N_EXPERTS, 1))

    bm = BM_EXP
    counts = cnt[:, 0]
    padded = (counts + bm - 1) // bm * bm
    pend = jnp.cumsum(padded)
    pstart = pend - padded
    nblk = -(-(n * TOP_K) // bm) + N_EXPERTS
    gstart = (jnp.concatenate([pstart, pend[-1:]]) // bm).astype(I32)

    dest = _place(eidx, rank, pstart.astype(F32).reshape(N_EXPERTS, 1))
    xs = _sc_scatter_rows(xpk, dest, nblk * bm)
    ypk = _experts(gstart, counts, xs, w_gate, w_up, w_down)
    ybuf = _sc_gather_rows(ypk, dest.reshape(-1)).reshape(TOP_K, n, d // 2)
    out = _combine(ybuf, base, gate.T, ln2_g.reshape(1, d).astype(F32), ln2_b.reshape(1, d).astype(F32))
    return out.reshape(b, t, d)


def kernel(x, w_in, conv_w, a_log, dt_bias, dn_norm_g, sinks, w_out, ln1_g, ln1_b, router_w, router_bias,
           w_gate, w_up, w_down, shared_w_gate, shared_w_up, shared_w_down, ln2_g, ln2_b):
    depth = w_in.shape[0]
    for l in range(depth):
        x = _layer(x, w_in[l], conv_w[l], a_log[l], dt_bias[l], dn_norm_g[l], sinks[l], w_out[l],
                   ln1_g[l], ln1_b[l], router_w[l], router_bias[l], w_gate[l], w_up[l], w_down[l],
                   shared_w_gate[l], shared_w_up[l], shared_w_down[l], ln2_g[l], ln2_b[l])
    return x
```

```python
import functools

import jax
import jax.numpy as jnp
from jax import lax
from jax.experimental import pallas as pl
from jax.experimental.pallas import tpu as pltpu
from jax.experimental.pallas import tpu_sc as plsc

F32 = jnp.float32
BF16 = jnp.bfloat16
I32 = jnp.int32
U32 = jnp.uint32

D_MODEL = 1024
DN_HEADS = 4
DN_HEAD_DIM = 128
DN_WIDTH = DN_HEADS * DN_HEAD_DIM
CONV_WIDTH = 4
DN_CHUNK = 64
SWA_Q_HEADS = 8
SWA_KV_HEADS = 2
SWA_HEAD_DIM = 64
SWA_WIDTH = SWA_Q_HEADS * SWA_HEAD_DIM
SWA_KV_WIDTH = SWA_KV_HEADS * SWA_HEAD_DIM
SWA_WINDOW = 128
SWA_BLOCK = 128
N_EXPERTS = 256
N_GROUPS = 8
GROUP_SIZE = N_EXPERTS // N_GROUPS
TOPK_GROUPS = 4
TOP_K = 8
EXPERT_FF = 256
SHARED_FF = 256
ROUTED_SCALE = 2.5
DEEPNORM_ALPHA = 2.0 ** 0.25
LN_EPS = 1e-5
RMS_EPS = 1e-6
L2_EPS = 1e-6

COL_DNQ = 0
COL_DNK = DN_WIDTH
COL_DNV = 2 * DN_WIDTH
COL_Z = 3 * DN_WIDTH
COL_SQ = 4 * DN_WIDTH
COL_SK = COL_SQ + SWA_WIDTH
COL_SV = COL_SK + SWA_KV_WIDTH
MAIN_COLS = COL_SV + SWA_KV_WIDTH
GATE_COLS = 128

TM_PROJ = 512
TS_DN = 512
DN_A_UNROLL = 4
TM_POST = 512
TT_ROUTE = 512
BM_EXP = 256
EXP_SLOTS = 6
TT_COMB = 256
SC_NC = 2
SC_NS = 16
SC_NW = SC_NC * SC_NS
SC_WIN = 64
VMEM_LIMIT = 56 * 1024 * 1024
NEG_INF = float("-inf")


def _dot(a, b):
    return jnp.dot(a, b, preferred_element_type=F32)


def _mm(a, b):
    return _dot(a.astype(BF16), b.astype(BF16))


def _mm_nt(a, b):
    return lax.dot_general(a.astype(BF16), b.astype(BF16), (((1,), (1,)), ((), ())),
                           preferred_element_type=F32)


def _mm_tn(a, b):
    return lax.dot_general(a.astype(BF16), b.astype(BF16), (((0,), (0,)), ((), ())),
                           preferred_element_type=F32)


def _split2(a):
    hi = a.astype(BF16)
    lo = (a - hi.astype(F32)).astype(BF16)
    return hi, lo


def _mm3(a, b):
    ah, al = _split2(a)
    bh, bl = _split2(b)
    return _dot(ah, bh) + _dot(ah, bl) + _dot(al, bh)


def _mm_exact_lhs(l_bf16, g):
    g1 = g.astype(BF16)
    r1 = g - g1.astype(F32)
    g2 = r1.astype(BF16)
    g3 = (r1 - g2.astype(F32)).astype(BF16)
    return _dot(l_bf16, g1) + _dot(l_bf16, g2) + _dot(l_bf16, g3)


def _sigmoid(x):
    return 1.0 / (1.0 + jnp.exp(-x))


def _silu(x):
    return x * _sigmoid(x)


def _in_proj_kernel(x_ref, w_ref, wg_ref, main_ref, gates_ref):
    xb = x_ref[...].astype(BF16)
    main_ref[...] = _dot(xb, w_ref[...]).astype(BF16)
    gates_ref[...] = _dot(xb, wg_ref[...])


def _in_proj(x2d, w_main, w_gates):
    n = x2d.shape[0]
    return pl.pallas_call(
        _in_proj_kernel,
        grid=(n // TM_PROJ,),
        in_specs=[
            pl.BlockSpec((TM_PROJ, D_MODEL), lambda i: (i, 0)),
            pl.BlockSpec((D_MODEL, MAIN_COLS), lambda i: (0, 0)),
            pl.BlockSpec((D_MODEL, GATE_COLS), lambda i: (0, 0)),
        ],
        out_specs=[
            pl.BlockSpec((TM_PROJ, MAIN_COLS), lambda i: (i, 0)),
            pl.BlockSpec((TM_PROJ, GATE_COLS), lambda i: (i, 0)),
        ],
        out_shape=[
            jax.ShapeDtypeStruct((n, MAIN_COLS), BF16),
            jax.ShapeDtypeStruct((n, GATE_COLS), F32),
        ],
        compiler_params=pltpu.CompilerParams(
            dimension_semantics=("parallel",), vmem_limit_bytes=VMEM_LIMIT),
        name="in_proj",
    )(x2d, w_main, w_gates)


def _dn_kernel(x_ref, gates_ref, convw_ref, gpar_ref, normg_ref, out_ref,
               xc_ref, gl_ref, gc_ref, wq_ref, u_ref, kt_ref, attn_ref, egl_ref, s_ref, hist_ref):
    ts = x_ref.shape[0]
    c = DN_CHUNK
    hd = DN_HEAD_DIM
    qkv_w = 3 * DN_WIDTH
    nch = ts // c

    @pl.when(pl.program_id(1) == 0)
    def _():
        s_ref[...] = jnp.zeros_like(s_ref)
        hist_ref[...] = jnp.zeros_like(hist_ref)

    xc_ref[0:8, :] = hist_ref[...]
    xc_ref[8:ts + 8, :] = x_ref[:, 0:qkv_w].astype(F32)
    hist_ref[...] = xc_ref[ts:ts + 8, :]

    gsl = gates_ref[...]
    sp_in = gsl + gpar_ref[1:2, :]
    softplus = jnp.maximum(sp_in, 0.0) + jnp.log(1.0 + jnp.exp(-jnp.abs(sp_in)))
    lane = lax.broadcasted_iota(I32, gsl.shape, 1)
    gl = jnp.where(lane < DN_HEADS, _sigmoid(gsl), -jnp.exp(gpar_ref[0:1, :]) * softplus)
    gl_ref[...] = gl
    row_in_chunk = lax.broadcasted_iota(I32, gsl.shape, 0) % c
    gc = gl
    shift = 1
    while shift < c:
        gc = gc + jnp.where(row_in_chunk >= shift, pltpu.roll(gc, shift, 0), 0.0)
        shift *= 2
    gc_ref[...] = gc

    ii = lax.broadcasted_iota(I32, (c, c), 0)
    jj = lax.broadcasted_iota(I32, (c, c), 1)
    tri_incl = ii >= jj
    tri_strict = ii > jj
    eye = jnp.where(ii == jj, 1.0, 0.0).astype(F32)
    heads = range(DN_HEADS)

    def conv_silu(r0, col):
        w = convw_ref[:, col:col + hd]
        y = w[CONV_WIDTH - 1:CONV_WIDTH, :] * xc_ref[pl.ds(r0 + 8, c), col:col + hd]
        for j in range(CONV_WIDTH - 1):
            off = 8 - (CONV_WIDTH - 1) + j
            y = y + w[j:j + 1, :] * xc_ref[pl.ds(r0 + off, c), col:col + hd]
        return _silu(y)

    def l2n(t, scale):
        return t * (lax.rsqrt(jnp.sum(t * t, axis=-1, keepdims=True) + L2_EPS) * scale)

    def phase_a(i, carry):
        chains = []
        for sub in range(DN_A_UNROLL):
            ci = i * DN_A_UNROLL + sub
            r0 = ci * c
            glc = gl_ref[pl.ds(r0, c), :]
            gcc = gc_ref[pl.ds(r0, c), :]
            gct = jnp.concatenate([gcc, gcc], axis=0).T
            egl_ref[ci] = jnp.exp(gcc[c - 8:c, :])
            for h in heads:
                chains.append((ci, r0, h, glc, gcc, gct))
        nchain = len(chains)
        q = [l2n(conv_silu(r0, COL_DNQ + h * hd), hd ** -0.5) for (ci, r0, h, _, _, _) in chains]
        k = [l2n(conv_silu(r0, COL_DNK + h * hd), 1.0) for (ci, r0, h, _, _, _) in chains]
        v = [conv_silu(r0, COL_DNV + h * hd) for (ci, r0, h, _, _, _) in chains]
        kb, vb, decay, egc = [], [], [], []
        for n_, (ci, r0, h, glc, gcc, gct) in enumerate(chains):
            beta = glc[:, h:h + 1]
            gc_col = gcc[:, DN_HEADS + h:DN_HEADS + h + 1]
            gc_row = gct[DN_HEADS + h:DN_HEADS + h + 1, 0:c]
            decay.append(jnp.where(tri_incl, jnp.exp(jnp.minimum(gc_col - gc_row, 0.0)), 0.0))
            egc.append(jnp.exp(gc_col))
            e_tail = jnp.exp(gcc[c - 1:c, DN_HEADS + h:DN_HEADS + h + 1] - gc_col)
            kb.append(k[n_] * beta)
            vb.append(v[n_] * beta)
            kt_ref[ci, h] = (k[n_] * e_tail).astype(BF16)
        kq = [_mm_nt(jnp.concatenate([kb[n_], q[n_]], axis=0), k[n_]) for n_ in range(nchain)]
        a_mat = [jnp.where(tri_strict, kq[n_][0:c] * decay[n_], 0.0) for n_ in range(nchain)]
        for n_, (ci, r0, h, _, _, _) in enumerate(chains):
            attn_ref[ci, h] = (kq[n_][c:2 * c] * decay[n_]).astype(BF16)
        t_inv = [eye - a for a in a_mat]
        p = a_mat
        for _ in range(5):
            p = [_mm(x, x) for x in p]
            t_inv = [t + _mm(t, x) for t, x in zip(t_inv, p)]
        for n_, (ci, r0, h, _, _, _) in enumerate(chains):
            uw = _mm3(t_inv[n_], jnp.concatenate([vb[n_], kb[n_] * egc[n_]], axis=1))
            u_ref[ci, h] = uw[:, 0:hd]
            wq_ref[ci, h, 0:c, :] = uw[:, hd:2 * hd].astype(BF16)
            wq_ref[ci, h, c:2 * c, :] = (q[n_] * egc[n_]).astype(BF16)
        return carry

    for i in range(nch // DN_A_UNROLL):
        phase_a(i, 0)

    normg = normg_ref[...]

    def phase_b(ci, carry):
        r0 = pl.multiple_of(ci * c, c)
        rows = pl.ds(r0, c)
        egl = egl_ref[ci]
        s_old = [s_ref[h] for h in heads]
        ws = [_dot(wq_ref[ci, h], s_old[h].astype(BF16)) for h in heads]
        v_new = [(u_ref[ci, h] - ws[h][0:c]).astype(BF16) for h in heads]
        for h in heads:
            s_ref[h] = (s_old[h] * egl[7:8, DN_HEADS + h:DN_HEADS + h + 1]
                        + lax.dot_general(kt_ref[ci, h], v_new[h], (((0,), (0,)), ((), ())),
                                          preferred_element_type=F32))
        for h in heads:
            o = ws[h][c:2 * c] + _dot(attn_ref[ci, h], v_new[h])
            o = o * lax.rsqrt(jnp.mean(o * o, axis=-1, keepdims=True) + RMS_EPS) * normg
            z = x_ref[rows, COL_Z + h * hd:COL_Z + (h + 1) * hd].astype(F32)
            out_ref[rows, h * hd:(h + 1) * hd] = (o * _silu(z)).astype(out_ref.dtype)
        return carry

    lax.fori_loop(0, nch, phase_b, 0)


def _deltanet(main3d, gates3d, conv_w, gpar, normg):
    b, t, _ = main3d.shape
    ts = TS_DN
    nch = ts // DN_CHUNK
    dn_in = COL_Z + DN_WIDTH
    return pl.pallas_call(
        _dn_kernel,
        grid=(b, t // ts),
        in_specs=[
            pl.BlockSpec((None, ts, dn_in), lambda bi, si: (bi, si, 0)),
            pl.BlockSpec((None, ts, GATE_COLS), lambda bi, si: (bi, si, 0)),
            pl.BlockSpec((CONV_WIDTH, 3 * DN_WIDTH), lambda bi, si: (0, 0)),
            pl.BlockSpec((2, GATE_COLS), lambda bi, si: (0, 0)),
            pl.BlockSpec((1, DN_HEAD_DIM), lambda bi, si: (0, 0)),
        ],
        out_specs=pl.BlockSpec((None, ts, DN_WIDTH), lambda bi, si: (bi, si, 0)),
        out_shape=jax.ShapeDtypeStruct((b, t, DN_WIDTH), BF16),
        scratch_shapes=[
            pltpu.VMEM((ts + 8, 3 * DN_WIDTH), F32),
            pltpu.VMEM((ts, GATE_COLS), F32),
            pltpu.VMEM((ts, GATE_COLS), F32),
            pltpu.VMEM((nch, DN_HEADS, 2 * DN_CHUNK, DN_HEAD_DIM), BF16),
            pltpu.VMEM((nch, DN_HEADS, DN_CHUNK, DN_HEAD_DIM), F32),
            pltpu.VMEM((nch, DN_HEADS, DN_CHUNK, DN_HEAD_DIM), BF16),
            pltpu.VMEM((nch, DN_HEADS, DN_CHUNK, DN_CHUNK), BF16),
            pltpu.VMEM((nch, 8, GATE_COLS), F32),
            pltpu.VMEM((DN_HEADS, DN_HEAD_DIM, DN_HEAD_DIM), F32),
            pltpu.VMEM((8, 3 * DN_WIDTH), F32),
        ],
        compiler_params=pltpu.CompilerParams(
            dimension_semantics=("parallel", "arbitrary"), vmem_limit_bytes=VMEM_LIMIT),
        name="deltanet",
    )(main3d, gates3d, conv_w, gpar, normg)


def _swa_kernel(sinks_ref, q_ref, kp_ref, kc_ref, vp_ref, vc_ref, out_ref):
    n = pl.program_id(1)
    blk = SWA_BLOCK
    d = SWA_HEAD_DIM
    grp = SWA_Q_HEADS // SWA_KV_HEADS
    qi = lax.broadcasted_iota(I32, (blk, 2 * blk), 0)
    kj = lax.broadcasted_iota(I32, (blk, 2 * blk), 1)
    dist = qi + blk - kj
    valid = (dist >= 0) & (dist < SWA_WINDOW) & ((kj >= blk) | (n > 0))
    dist_f = dist.astype(F32)
    outs = []
    for hq in range(SWA_Q_HEADS):
        hk = hq // grp
        q = q_ref[:, hq * d:(hq + 1) * d]
        kband = jnp.concatenate([kp_ref[:, hk * d:(hk + 1) * d], kc_ref[:, hk * d:(hk + 1) * d]], axis=0)
        vband = jnp.concatenate([vp_ref[:, hk * d:(hk + 1) * d], vc_ref[:, hk * d:(hk + 1) * d]], axis=0)
        slope = 2.0 ** (-8.0 * (hq + 1.0) / SWA_Q_HEADS)
        s = _mm_nt(q, kband) * (d ** -0.5) - slope * dist_f
        s = jnp.where(valid, s, NEG_INF)
        sink = sinks_ref[hq]
        m = jnp.maximum(jnp.max(s, axis=-1, keepdims=True), sink)
        p = jnp.exp(s - m)
        denom = jnp.sum(p, axis=-1, keepdims=True) + jnp.exp(sink - m)
        outs.append(_mm(p, vband) / denom)
    out_ref[...] = jnp.concatenate(outs, axis=-1).astype(out_ref.dtype)


def _swa(main3d, sinks):
    b, t, _ = main3d.shape
    blk = SWA_BLOCK
    qb = COL_SQ // SWA_WIDTH
    kb = COL_SK // SWA_KV_WIDTH
    vb = COL_SV // SWA_KV_WIDTH
    grid_spec = pltpu.PrefetchScalarGridSpec(
        num_scalar_prefetch=1,
        grid=(b, t // blk),
        in_specs=[
            pl.BlockSpec((None, blk, SWA_WIDTH), lambda bi, ni, s: (bi, ni, qb)),
            pl.BlockSpec((None, blk, SWA_KV_WIDTH), lambda bi, ni, s: (bi, jnp.maximum(ni - 1, 0), kb)),
            pl.BlockSpec((None, blk, SWA_KV_WIDTH), lambda bi, ni, s: (bi, ni, kb)),
            pl.BlockSpec((None, blk, SWA_KV_WIDTH), lambda bi, ni, s: (bi, jnp.maximum(ni - 1, 0), vb)),
            pl.BlockSpec((None, blk, SWA_KV_WIDTH), lambda bi, ni, s: (bi, ni, vb)),
        ],
        out_specs=pl.BlockSpec((None, blk, SWA_WIDTH), lambda bi, ni, s: (bi, ni, 0)),
    )
    return pl.pallas_call(
        _swa_kernel,
        grid_spec=grid_spec,
        out_shape=jax.ShapeDtypeStruct((b, t, SWA_WIDTH), BF16),
        compiler_params=pltpu.CompilerParams(
            dimension_semantics=("parallel", "parallel"), vmem_limit_bytes=VMEM_LIMIT),
        name="swa",
    )(sinks, main3d, main3d, main3d, main3d, main3d)


def _layer_norm(y, g, b):
    mu = jnp.mean(y, axis=-1, keepdims=True)
    yc = y - mu
    var = jnp.mean(yc * yc, axis=-1, keepdims=True)
    return yc * lax.rsqrt(var + LN_EPS) * g + b


def _pack_bf16_pair(lo, hi):
    lo_bits = lax.bitcast_convert_type(lo.astype(BF16).astype(F32), U32)
    hi_bits = lax.bitcast_convert_type(hi.astype(BF16).astype(F32), U32)
    return (hi_bits & jnp.uint32(0xFFFF0000)) | (lo_bits >> 16)


def _unpack_bf16_pair(packed):
    lo = lax.bitcast_convert_type(packed << 16, F32)
    hi = lax.bitcast_convert_type(packed & jnp.uint32(0xFFFF0000), F32)
    return lo, hi


def _post_mix_kernel(x_ref, dn_ref, swa_ref, wo_dn_ref, wo_swa_ref, g_ref, b_ref,
                     rwh_ref, rwl_ref, sg_ref, su_ref, sd_ref,
                     base_ref, xpk_ref, logit_ref):
    mix = _dot(dn_ref[...], wo_dn_ref[...]) + _dot(swa_ref[...], wo_swa_ref[...])
    x1 = _layer_norm(DEEPNORM_ALPHA * x_ref[...] + mix, g_ref[...], b_ref[...])
    half = D_MODEL // 2
    xpk_ref[...] = _pack_bf16_pair(x1[:, :half], x1[:, half:])
    xh, xl = _split2(x1)
    nt = (((1,), (1,)), ((), ()))
    logit_ref[...] = (lax.dot_general(rwh_ref[...], xh, nt, preferred_element_type=F32)
                      + lax.dot_general(rwh_ref[...], xl, nt, preferred_element_type=F32)
                      + lax.dot_general(rwl_ref[...], xh, nt, preferred_element_type=F32))
    hmid = _silu(_dot(xh, sg_ref[...])) * _dot(xh, su_ref[...])
    base_ref[...] = DEEPNORM_ALPHA * x1 + _dot(hmid.astype(BF16), sd_ref[...])


def _post_mix(x2d, dn2d, swa2d, wo_dn, wo_swa, ln_g, ln_b, rwh, rwl, sg, su, sd):
    n = x2d.shape[0]
    tm = TM_POST
    full = lambda shape: pl.BlockSpec(shape, lambda i: (0, 0))
    return pl.pallas_call(
        _post_mix_kernel,
        grid=(n // tm,),
        in_specs=[
            pl.BlockSpec((tm, D_MODEL), lambda i: (i, 0)),
            pl.BlockSpec((tm, DN_WIDTH), lambda i: (i, 0)),
            pl.BlockSpec((tm, SWA_WIDTH), lambda i: (i, 0)),
            full((DN_WIDTH, D_MODEL)), full((SWA_WIDTH, D_MODEL)),
            full((1, D_MODEL)), full((1, D_MODEL)),
            full((N_EXPERTS, D_MODEL)), full((N_EXPERTS, D_MODEL)),
            full((D_MODEL, SHARED_FF)), full((D_MODEL, SHARED_FF)), full((SHARED_FF, D_MODEL)),
        ],
        out_specs=[
            pl.BlockSpec((tm, D_MODEL), lambda i: (i, 0)),
            pl.BlockSpec((tm, D_MODEL // 2), lambda i: (i, 0)),
            pl.BlockSpec((N_EXPERTS, tm), lambda i: (0, i)),
        ],
        out_shape=[
            jax.ShapeDtypeStruct((n, D_MODEL), F32),
            jax.ShapeDtypeStruct((n, D_MODEL // 2), U32),
            jax.ShapeDtypeStruct((N_EXPERTS, n), F32),
        ],
        compiler_params=pltpu.CompilerParams(
            dimension_semantics=("parallel",), vmem_limit_bytes=VMEM_LIMIT),
        name="post_mix",
    )(x2d, dn2d, swa2d, wo_dn, wo_swa, ln_g, ln_b, rwh, rwl, sg, su, sd)


def _route_kernel(lg_ref, bias_ref, eidx_ref, gate_ref, rank_ref, cnt_ref, carry_ref):
    @pl.when(pl.program_id(0) == 0)
    def _():
        carry_ref[...] = jnp.zeros_like(carry_ref)

    tt = lg_ref.shape[1]
    scores = _sigmoid(lg_ref[...])
    sel = scores + bias_ref[...]

    iog = lax.broadcasted_iota(I32, (GROUP_SIZE, tt), 0)
    grp_rows = []
    for g in range(N_GROUPS):
        blk = sel[g * GROUP_SIZE:(g + 1) * GROUP_SIZE, :]
        m1 = jnp.max(blk, axis=0, keepdims=True)
        i1 = jnp.min(jnp.where(blk == m1, iog, GROUP_SIZE), axis=0, keepdims=True)
        m2 = jnp.max(jnp.where(iog == i1, NEG_INF, blk), axis=0, keepdims=True)
        grp_rows.append(m1 + m2)
    gs = jnp.concatenate(grp_rows, axis=0)

    io8 = lax.broadcasted_iota(I32, (N_GROUPS, tt), 0)
    gsel = jnp.zeros((N_GROUPS, tt), F32)
    for _ in range(TOPK_GROUPS):
        mg = jnp.max(gs, axis=0, keepdims=True)
        ig = jnp.min(jnp.where(gs == mg, io8, N_GROUPS), axis=0, keepdims=True)
        hit = io8 == ig
        gsel = jnp.where(hit, 1.0, gsel)
        gs = jnp.where(hit, NEG_INF, gs)

    val = jnp.concatenate(
        [jnp.where(gsel[g:g + 1, :] > 0.0, sel[g * GROUP_SIZE:(g + 1) * GROUP_SIZE, :], NEG_INF)
         for g in range(N_GROUPS)], axis=0)

    ioe = lax.broadcasted_iota(I32, (N_EXPERTS, tt), 0)
    onehot = jnp.zeros((N_EXPERTS, tt), F32)
    idx_rows, gate_rows = [], []
    for _ in range(TOP_K):
        m = jnp.max(val, axis=0, keepdims=True)
        ik = jnp.min(jnp.where(val == m, ioe, N_EXPERTS), axis=0, keepdims=True)
        hit = ioe == ik
        gate_rows.append(jnp.sum(jnp.where(hit, scores, 0.0), axis=0, keepdims=True))
        idx_rows.append(ik)
        val = jnp.where(hit, NEG_INF, val)
        onehot = jnp.where(hit, 1.0, onehot)
    gsum = gate_rows[0]
    for r in gate_rows[1:]:
        gsum = gsum + r
    gate_ref[...] = jnp.concatenate(gate_rows, axis=0) / gsum * ROUTED_SCALE
    eidx_ref[...] = jnp.concatenate(idx_rows, axis=0)

    ti = lax.broadcasted_iota(I32, (tt, tt), 0)
    tj = lax.broadcasted_iota(I32, (tt, tt), 1)
    upper = jnp.where(ti < tj, 1.0, 0.0).astype(BF16)
    cum = _dot(onehot.astype(BF16), upper) + jnp.broadcast_to(carry_ref[:, 0:1], (N_EXPERTS, tt))
    rank_rows = [jnp.sum(jnp.where(ioe == ik, cum, 0.0), axis=0, keepdims=True) for ik in idx_rows]
    rank_ref[...] = jnp.concatenate(rank_rows, axis=0).astype(I32)
    carry_ref[...] = carry_ref[...] + jnp.broadcast_to(
        jnp.sum(onehot, axis=1, keepdims=True), carry_ref.shape)
    cnt_ref[...] = carry_ref[...].astype(I32)


def _route(logits_t, bias_col):
    n = logits_t.shape[1]
    tt = TT_ROUTE
    row_spec = pl.BlockSpec((TOP_K, tt), lambda i: (0, i))
    return pl.pallas_call(
        _route_kernel,
        grid=(n // tt,),
        in_specs=[
            pl.BlockSpec((N_EXPERTS, tt), lambda i: (0, i)),
            pl.BlockSpec((N_EXPERTS, 1), lambda i: (0, 0)),
        ],
        out_specs=[row_spec, row_spec, row_spec,
                   pl.BlockSpec((N_EXPERTS, 128), lambda i: (0, 0))],
        out_shape=[
            jax.ShapeDtypeStruct((TOP_K, n), I32),
            jax.ShapeDtypeStruct((TOP_K, n), F32),
            jax.ShapeDtypeStruct((TOP_K, n), I32),
            jax.ShapeDtypeStruct((N_EXPERTS, 128), I32),
        ],
        scratch_shapes=[pltpu.VMEM((N_EXPERTS, 128), F32)],
        compiler_params=pltpu.CompilerParams(
            dimension_semantics=("arbitrary",), vmem_limit_bytes=VMEM_LIMIT),
        name="route",
    )(logits_t, bias_col)


def _place_kernel(eidx_ref, rank_ref, pstart_ref, dest_ref):
    tt = eidx_ref.shape[1]
    ioe = lax.broadcasted_iota(I32, (N_EXPERTS, tt), 0)
    pstart = pstart_ref[...]
    rows = [jnp.sum(jnp.where(ioe == eidx_ref[k:k + 1, :], pstart, 0.0), axis=0, keepdims=True)
            for k in range(TOP_K)]
    dest_ref[...] = jnp.concatenate(rows, axis=0).astype(I32) + rank_ref[...]


def _place(eidx, rank, pstart_col):
    n = eidx.shape[1]
    tt = TT_ROUTE
    row_spec = pl.BlockSpec((TOP_K, tt), lambda i: (0, i))
    return pl.pallas_call(
        _place_kernel,
        grid=(n // tt,),
        in_specs=[row_spec, row_spec, pl.BlockSpec((N_EXPERTS, 1), lambda i: (0, 0))],
        out_specs=row_spec,
        out_shape=jax.ShapeDtypeStruct((TOP_K, n), I32),
        compiler_params=pltpu.CompilerParams(
            dimension_semantics=("parallel",), vmem_limit_bytes=VMEM_LIMIT),
        name="place",
    )(eidx, rank, pstart_col)


def _sc_mesh():
    return plsc.VectorSubcoreMesh(core_axis_name="c", subcore_axis_name="s",
                                  num_cores=SC_NC, num_subcores=SC_NS)


def _sc_scatter_rows(rows, idx, nrows_out):
    n, d = rows.shape
    nk = idx.shape[0]
    per_w = n // SC_NW
    nwin = per_w // SC_WIN
    assert per_w * SC_NW == n and nwin * SC_WIN == per_w and nwin % 2 == 0

    @functools.partial(
        pl.kernel, mesh=_sc_mesh(),
        out_type=jax.ShapeDtypeStruct((nrows_out, d), rows.dtype),
        scratch_types=[
            pltpu.VMEM((nwin, nk, SC_WIN), I32),
            pltpu.VMEM((2, SC_WIN, d), rows.dtype),
            pltpu.SemaphoreType.DMA((2,)),
            pltpu.SemaphoreType.DMA((2,)),
        ],
        compiler_params=pltpu.CompilerParams(use_tc_tiling_on_sc=True),
        name="sc_scatter_rows",
    )
    def scatter_kernel(rows_hbm, idx_hbm, out_hbm, idx_v, rows_v, lsem, ssem):
        wid = lax.axis_index("s") * SC_NC + lax.axis_index("c")
        base = wid * per_w
        pltpu.sync_copy(idx_hbm.at[wid], idx_v)

        def load(w, slot):
            return pltpu.make_async_copy(
                rows_hbm.at[pl.ds(base + w * SC_WIN, SC_WIN)], rows_v.at[slot], lsem.at[slot])

        def scat(w, k, slot):
            return pltpu.make_async_copy(rows_v.at[slot], out_hbm.at[idx_v.at[w, k]], ssem.at[slot])

        load(0, 0).start()

        @pl.loop(0, nwin, step=2)
        def _(w0):
            for slot in range(2):
                w = w0 + slot
                load(w, slot).wait()

                @pl.when(w + 1 < nwin)
                def _():
                    @pl.when(w >= 1)
                    def _():
                        for k in range(nk):
                            scat(w - 1, k, 1 - slot).wait()
                    load(w + 1, 1 - slot).start()

                for k in range(nk):
                    scat(w, k, slot).start()

        for k in range(nk):
            scat(nwin - 2, k, 0).wait()
        for k in range(nk):
            scat(nwin - 1, k, 1).wait()

    idx4 = idx.reshape(nk, SC_NW, nwin, SC_WIN).transpose(1, 2, 0, 3)
    return scatter_kernel(rows, idx4)


def _expert_kernel(gstart_ref, cnt_ref, xs_hbm, wg_ref, wu_ref, wd_ref, y_hbm,
                   wgb_ref, wub_ref, wdb_ref, xbuf_ref, ybuf_ref, xsem, ysem):
    e = pl.program_id(0)
    ne = pl.num_programs(0)
    bm = xbuf_ref.shape[1]
    nblk = y_hbm.shape[0] // bm
    half = D_MODEL // 2
    g_lo = gstart_ref[e]
    g_hi = gstart_ref[e + 1]
    g_end = gstart_ref[ne]

    def x_copy(g, slot):
        return pltpu.make_async_copy(xs_hbm.at[pl.ds(g * bm, bm), :], xbuf_ref.at[slot], xsem.at[slot])

    def y_copy(g, slot):
        return pltpu.make_async_copy(ybuf_ref.at[slot], y_hbm.at[pl.ds(g * bm, bm), :], ysem.at[slot])

    nslot = xbuf_ref.shape[0]

    ahead = nslot - 2

    @pl.when(e == 0)
    def _():
        for g0 in range(ahead):
            @pl.when(g0 < g_end)
            def _():
                x_copy(g0, g0).start()

    @pl.when(g_hi > g_lo)
    def _():
        wgb_ref[...] = wg_ref[...].astype(BF16)
        wub_ref[...] = wu_ref[...].astype(BF16)
        wdb_ref[...] = wd_ref[...].astype(BF16)

    row = lax.broadcasted_iota(I32, (bm, half), 0)

    def acquire(g):
        x_copy(g, g % nslot).wait()

        @pl.when(g + ahead < g_end)
        def _():
            x_copy(g + ahead, (g + ahead) % nslot).start()

        @pl.when(g >= nslot)
        def _():
            y_copy(g - nslot, g % nslot).wait()

    def load(g):
        n_valid = cnt_ref[e] - (g - g_lo) * bm
        x_lo, x_hi = _unpack_bf16_pair(jnp.where(row < n_valid, xbuf_ref[g % nslot], jnp.uint32(0)))
        return x_lo.astype(BF16), x_hi.astype(BF16)

    def gate_up(x):
        x_lo, x_hi = x
        gate = _dot(x_lo, wgb_ref[:half, :]) + _dot(x_hi, wgb_ref[half:, :])
        up = _dot(x_lo, wub_ref[:half, :]) + _dot(x_hi, wub_ref[half:, :])
        return gate, up

    def down(gu):
        gate, up = gu
        return _dot((_silu(gate) * up).astype(BF16), wdb_ref[...])

    def store(g, y):
        ybuf_ref[g % nslot] = _pack_bf16_pair(y[:, :half], y[:, half:])
        y_copy(g, g % nslot).start()

    def pair(p, carry):
        g = g_lo + 2 * p
        acquire(g)
        acquire(g + 1)
        gu_a = gate_up(load(g))
        gu_b = gate_up(load(g + 1))
        y_a = down(gu_a)
        y_b = down(gu_b)
        store(g, y_a)
        store(g + 1, y_b)
        return carry

    n_own = g_hi - g_lo
    lax.fori_loop(0, n_own // 2, pair, 0)

    @pl.when(n_own % 2 == 1)
    def _():
        acquire(g_hi - 1)
        store(g_hi - 1, down(gate_up(load(g_hi - 1))))

    @pl.when(e == ne - 1)
    def _():
        for back in range(nslot, 0, -1):
            @pl.when(g_end >= back)
            def _():
                y_copy(g_end - back, (g_end - back) % nslot).wait()

        ybuf_ref[0] = jnp.zeros((bm, half), U32)

        def fill(g, carry):
            y_copy(g, 0).start()
            return carry

        def drain(g, carry):
            y_copy(g, 0).wait()
            return carry

        lax.fori_loop(g_end, nblk, fill, 0)
        lax.fori_loop(g_end, nblk, drain, 0)


def _experts(gstart, counts, xs, w_gate, w_up, w_down):
    bm = BM_EXP
    nblk = xs.shape[0] // bm
    half = D_MODEL // 2
    grid_spec = pltpu.PrefetchScalarGridSpec(
        num_scalar_prefetch=2,
        grid=(N_EXPERTS,),
        in_specs=[
            pl.BlockSpec(memory_space=pl.ANY),
            pl.BlockSpec((None, D_MODEL, EXPERT_FF), lambda e, gs, cn: (e, 0, 0)),
            pl.BlockSpec((None, D_MODEL, EXPERT_FF), lambda e, gs, cn: (e, 0, 0)),
            pl.BlockSpec((None, EXPERT_FF, D_MODEL), lambda e, gs, cn: (e, 0, 0)),
        ],
        out_specs=pl.BlockSpec(memory_space=pl.ANY),
        scratch_shapes=[
            pltpu.VMEM((D_MODEL, EXPERT_FF), BF16),
            pltpu.VMEM((D_MODEL, EXPERT_FF), BF16),
            pltpu.VMEM((EXPERT_FF, D_MODEL), BF16),
            pltpu.VMEM((EXP_SLOTS, bm, half), U32),
            pltpu.VMEM((EXP_SLOTS, bm, half), U32),
            pltpu.SemaphoreType.DMA((EXP_SLOTS,)),
            pltpu.SemaphoreType.DMA((EXP_SLOTS,)),
        ],
    )
    return pl.pallas_call(
        _expert_kernel,
        grid_spec=grid_spec,
        out_shape=jax.ShapeDtypeStruct((nblk * bm, half), U32),
        compiler_params=pltpu.CompilerParams(
            dimension_semantics=("arbitrary",), vmem_limit_bytes=VMEM_LIMIT),
        name="experts",
    )(gstart, counts, xs, w_gate, w_up, w_down)


def _sc_gather_rows(table, idx):
    nrows = idx.shape[0]
    d = table.shape[1]
    per_w = nrows // SC_NW
    nwin = per_w // SC_WIN
    assert per_w * SC_NW == nrows and nwin * SC_WIN == per_w and nwin % 2 == 0
    @functools.partial(
        pl.kernel, mesh=_sc_mesh(),
        out_type=jax.ShapeDtypeStruct((nrows, d), table.dtype),
        scratch_types=[
            pltpu.VMEM((nwin, SC_WIN), I32),
            pltpu.VMEM((2, SC_WIN, d), table.dtype),
            pltpu.SemaphoreType.DMA((2,)),
            pltpu.SemaphoreType.DMA((2,)),
        ],
        compiler_params=pltpu.CompilerParams(use_tc_tiling_on_sc=True),
        name="sc_gather_rows",
    )
    def gather_kernel(table_hbm, idx_hbm, out_hbm, idx_v, rows_v, gsem, wsem):
        wid = lax.axis_index("s") * SC_NC + lax.axis_index("c")
        base = wid * per_w
        pltpu.sync_copy(idx_hbm.at[wid], idx_v)

        def gather(w, slot):
            return pltpu.make_async_copy(table_hbm.at[idx_v.at[w]], rows_v.at[slot], gsem.at[slot])

        def put(w, slot):
            return pltpu.make_async_copy(
                rows_v.at[slot], out_hbm.at[pl.ds(base + w * SC_WIN, SC_WIN)], wsem.at[slot])

        gather(0, 0).start()

        @pl.loop(0, nwin, step=2)
        def _(w0):
            for slot in range(2):
                w = w0 + slot
                gather(w, slot).wait()

                @pl.when(w + 1 < nwin)
                def _():
                    @pl.when(w >= 1)
                    def _():
                        put(w - 1, 1 - slot).wait()
                    gather(w + 1, 1 - slot).start()

                put(w, slot).start()

        put(nwin - 2, 0).wait()
        put(nwin - 1, 1).wait()

    return gather_kernel(table, idx.reshape(SC_NW, nwin, SC_WIN))


def _combine_kernel(y_ref, base_ref, gate_ref, g_ref, b_ref, out_ref):
    half = D_MODEL // 2
    gates = gate_ref[...]
    acc_lo = base_ref[:, :half]
    acc_hi = base_ref[:, half:]
    for k in range(TOP_K):
        y_lo, y_hi = _unpack_bf16_pair(y_ref[k])
        gk = gates[:, k:k + 1]
        acc_lo = acc_lo + gk * y_lo
        acc_hi = acc_hi + gk * y_hi
    mu = (jnp.sum(acc_lo, axis=-1, keepdims=True) + jnp.sum(acc_hi, axis=-1, keepdims=True)) / D_MODEL
    c_lo = acc_lo - mu
    c_hi = acc_hi - mu
    var = (jnp.sum(c_lo * c_lo, axis=-1, keepdims=True)
           + jnp.sum(c_hi * c_hi, axis=-1, keepdims=True)) / D_MODEL
    inv = lax.rsqrt(var + LN_EPS)
    out_ref[:, :half] = c_lo * inv * g_ref[:, :half] + b_ref[:, :half]
    out_ref[:, half:] = c_hi * inv * g_ref[:, half:] + b_ref[:, half:]


def _combine(ybuf, base, gate_tok, ln_g, ln_b):
    n = base.shape[0]
    tt = TT_COMB
    half = D_MODEL // 2
    return pl.pallas_call(
        _combine_kernel,
        grid=(n // tt,),
        in_specs=[
            pl.BlockSpec((TOP_K, tt, half), lambda i: (0, i, 0)),
            pl.BlockSpec((tt, D_MODEL), lambda i: (i, 0)),
            pl.BlockSpec((tt, TOP_K), lambda i: (i, 0)),
            pl.BlockSpec((1, D_MODEL), lambda i: (0, 0)),
            pl.BlockSpec((1, D_MODEL), lambda i: (0, 0)),
        ],
        out_specs=pl.BlockSpec((tt, D_MODEL), lambda i: (i, 0)),
        out_shape=jax.ShapeDtypeStruct((n, D_MODEL), F32),
        compiler_params=pltpu.CompilerParams(
            dimension_semantics=("parallel",), vmem_limit_bytes=VMEM_LIMIT),
        name="combine",
    )(ybuf, base, gate_tok, ln_g, ln_b)


def _regroup_w_in(w_in):
    o = 0
    cols = {}
    for name, width in (("dnq", DN_WIDTH), ("dnk", DN_WIDTH), ("dnv", DN_WIDTH), ("sq", SWA_WIDTH),
                        ("sk", SWA_KV_WIDTH), ("sv", SWA_KV_WIDTH), ("z", DN_WIDTH),
                        ("b", DN_HEADS), ("a", DN_HEADS)):
        cols[name] = w_in[:, o:o + width]
        o += width
    w_main = jnp.concatenate([cols[k] for k in ("dnq", "dnk", "dnv", "z", "sq", "sk", "sv")], axis=1)
    w_gates = jnp.concatenate(
        [cols["b"], cols["a"], jnp.zeros((D_MODEL, GATE_COLS - 2 * DN_HEADS), w_in.dtype)], axis=1)
    return w_main.astype(BF16), w_gates.astype(BF16)


def _layer(x, w_in, conv_w, a_log, dt_bias, dn_norm_g, sinks, w_out, ln1_g, ln1_b,
           router_w, router_bias, w_gate, w_up, w_down, sh_gate, sh_up, sh_down, ln2_g, ln2_b):
    b, t, d = x.shape
    n = b * t
    x2d = x.reshape(n, d)

    w_main, w_gates = _regroup_w_in(w_in)
    main, gates = _in_proj(x2d, w_main, w_gates)
    main3d = main.reshape(b, t, MAIN_COLS)

    pad = jnp.zeros((GATE_COLS - 2 * DN_HEADS,), F32)
    gpar = jnp.stack([jnp.concatenate([jnp.zeros((DN_HEADS,), F32), a_log.astype(F32), pad]),
                      jnp.concatenate([jnp.zeros((DN_HEADS,), F32), dt_bias.astype(F32), pad])])
    dn_out = _deltanet(main3d, gates.reshape(b, t, GATE_COLS), conv_w.astype(F32), gpar,
                       dn_norm_g.astype(F32).reshape(1, DN_HEAD_DIM))
    swa_out = _swa(main3d, sinks.astype(F32))

    rw_t = router_w.T.astype(F32)
    rwh = rw_t.astype(BF16)
    rwl = (rw_t - rwh.astype(F32)).astype(BF16)
    base, xpk, logits_t = _post_mix(
        x2d, dn_out.reshape(n, DN_WIDTH), swa_out.reshape(n, SWA_WIDTH),
        w_out[:DN_WIDTH].astype(BF16), w_out[DN_WIDTH:].astype(BF16),
        ln1_g.reshape(1, d).astype(F32), ln1_b.reshape(1, d).astype(F32), rwh, rwl,
        sh_gate.astype(BF16), sh_up.astype(BF16), sh_down.astype(BF16))

    eidx, gate, rank, cnt = _route(logits_t, router_bias.astype(F32).reshape(N_EXPERTS, 1))

    bm = BM_EXP
    counts = cnt[:, 0]
    padded = (counts + bm - 1) // bm * bm
    pend = jnp.cumsum(padded)
    pstart = pend - padded
    nblk = -(-(n * TOP_K) // bm) + N_EXPERTS
    gstart = (jnp.concatenate([pstart, pend[-1:]]) // bm).astype(I32)

    dest = _place(eidx, rank, pstart.astype(F32).reshape(N_EXPERTS, 1))
    xs = _sc_scatter_rows(xpk, dest, nblk * bm)
    ypk = _experts(gstart, counts, xs, w_gate, w_up, w_down)
    ybuf = _sc_gather_rows(ypk, dest.reshape(-1)).reshape(TOP_K, n, d // 2)
    out = _combine(ybuf, base, gate.T, ln2_g.reshape(1, d).astype(F32), ln2_b.reshape(1, d).astype(F32))
    return out.reshape(b, t, d)


def kernel(x, w_in, conv_w, a_log, dt_bias, dn_norm_g, sinks, w_out, ln1_g, ln1_b, router_w, router_bias,
           w_gate, w_up, w_down, shared_w_gate, shared_w_up, shared_w_down, ln2_g, ln2_b):
    depth = w_in.shape[0]
    for l in range(depth):
        x = _layer(x, w_in[l], conv_w[l], a_log[l], dt_bias[l], dn_norm_g[l], sinks[l], w_out[l],
                   ln1_g[l], ln1_b[l], router_w[l], router_bias[l], w_gate[l], w_up[l], w_down[l],
                   shared_w_gate[l], shared_w_up[l], shared_w_down[l], ln2_g[l], ln2_b[l])
    return x
```

```python
import functools

import jax
import jax.numpy as jnp
from jax import lax
from jax.experimental import pallas as pl
from jax.experimental.pallas import tpu as pltpu
from jax.experimental.pallas import tpu_sc as plsc

F32 = jnp.float32
BF16 = jnp.bfloat16
I32 = jnp.int32
U32 = jnp.uint32

D_MODEL = 1024
DN_HEADS = 4
DN_HEAD_DIM = 128
DN_WIDTH = DN_HEADS * DN_HEAD_DIM
CONV_WIDTH = 4
DN_CHUNK = 64
SWA_Q_HEADS = 8
SWA_KV_HEADS = 2
SWA_HEAD_DIM = 64
SWA_WIDTH = SWA_Q_HEADS * SWA_HEAD_DIM
SWA_KV_WIDTH = SWA_KV_HEADS * SWA_HEAD_DIM
SWA_WINDOW = 128
SWA_BLOCK = 128
N_EXPERTS = 256
N_GROUPS = 8
GROUP_SIZE = N_EXPERTS // N_GROUPS
TOPK_GROUPS = 4
TOP_K = 8
EXPERT_FF = 256
SHARED_FF = 256
ROUTED_SCALE = 2.5
DEEPNORM_ALPHA = 2.0 ** 0.25
LN_EPS = 1e-5
RMS_EPS = 1e-6
L2_EPS = 1e-6

COL_DNQ = 0
COL_DNK = DN_WIDTH
COL_DNV = 2 * DN_WIDTH
COL_Z = 3 * DN_WIDTH
COL_SQ = 4 * DN_WIDTH
COL_SK = COL_SQ + SWA_WIDTH
COL_SV = COL_SK + SWA_KV_WIDTH
MAIN_COLS = COL_SV + SWA_KV_WIDTH
GATE_COLS = 128

TM_PROJ = 512
TS_DN = 512
DN_A_UNROLL = 4
TM_POST = 512
TT_ROUTE = 512
BM_EXP = 256
EXP_SLOTS = 6
TT_COMB = 256
COMB_PARTS = 4
SC_NC = 2
SC_NS = 16
SC_NW = SC_NC * SC_NS
SC_WIN = 64
VMEM_LIMIT = 56 * 1024 * 1024
NEG_INF = float("-inf")


def _dot(a, b):
    return jnp.dot(a, b, preferred_element_type=F32)


def _mm(a, b):
    return _dot(a.astype(BF16), b.astype(BF16))


def _mm_nt(a, b):
    return lax.dot_general(a.astype(BF16), b.astype(BF16), (((1,), (1,)), ((), ())),
                           preferred_element_type=F32)


def _mm_tn(a, b):
    return lax.dot_general(a.astype(BF16), b.astype(BF16), (((0,), (0,)), ((), ())),
                           preferred_element_type=F32)


def _split2(a):
    hi = a.astype(BF16)
    lo = (a - hi.astype(F32)).astype(BF16)
    return hi, lo


def _mm3(a, b):
    ah, al = _split2(a)
    bh, bl = _split2(b)
    return _dot(ah, bh) + _dot(ah, bl) + _dot(al, bh)


def _mm_exact_lhs(l_bf16, g):
    g1 = g.astype(BF16)
    r1 = g - g1.astype(F32)
    g2 = r1.astype(BF16)
    g3 = (r1 - g2.astype(F32)).astype(BF16)
    return _dot(l_bf16, g1) + _dot(l_bf16, g2) + _dot(l_bf16, g3)


def _sigmoid(x):
    return 1.0 / (1.0 + jnp.exp(-x))


def _silu(x):
    return x * _sigmoid(x)


def _in_proj_kernel(x_ref, w_ref, wg_ref, main_ref, gates_ref):
    xb = x_ref[...].astype(BF16)
    main_ref[...] = _dot(xb, w_ref[...]).astype(BF16)
    gates_ref[...] = _dot(xb, wg_ref[...])


def _in_proj(x2d, w_main, w_gates):
    n = x2d.shape[0]
    return pl.pallas_call(
        _in_proj_kernel,
        grid=(n // TM_PROJ,),
        in_specs=[
            pl.BlockSpec((TM_PROJ, D_MODEL), lambda i: (i, 0)),
            pl.BlockSpec((D_MODEL, MAIN_COLS), lambda i: (0, 0)),
            pl.BlockSpec((D_MODEL, GATE_COLS), lambda i: (0, 0)),
        ],
        out_specs=[
            pl.BlockSpec((TM_PROJ, MAIN_COLS), lambda i: (i, 0)),
            pl.BlockSpec((TM_PROJ, GATE_COLS), lambda i: (i, 0)),
        ],
        out_shape=[
            jax.ShapeDtypeStruct((n, MAIN_COLS), BF16),
            jax.ShapeDtypeStruct((n, GATE_COLS), F32),
        ],
        compiler_params=pltpu.CompilerParams(
            dimension_semantics=("parallel",), vmem_limit_bytes=VMEM_LIMIT),
        name="in_proj",
    )(x2d, w_main, w_gates)


def _dn_kernel(x_ref, gates_ref, convw_ref, gpar_ref, normg_ref, out_ref,
               xc_ref, gl_ref, gc_ref, wq_ref, u_ref, kt_ref, attn_ref, egl_ref, s_ref, hist_ref):
    ts = x_ref.shape[0]
    c = DN_CHUNK
    hd = DN_HEAD_DIM
    qkv_w = 3 * DN_WIDTH
    nch = ts // c

    @pl.when(pl.program_id(1) == 0)
    def _():
        s_ref[...] = jnp.zeros_like(s_ref)
        hist_ref[...] = jnp.zeros_like(hist_ref)

    xc_ref[0:8, :] = hist_ref[...]
    xc_ref[8:ts + 8, :] = x_ref[:, 0:qkv_w].astype(F32)
    hist_ref[...] = xc_ref[ts:ts + 8, :]

    gsl = gates_ref[...]
    sp_in = gsl + gpar_ref[1:2, :]
    softplus = jnp.maximum(sp_in, 0.0) + jnp.log(1.0 + jnp.exp(-jnp.abs(sp_in)))
    lane = lax.broadcasted_iota(I32, gsl.shape, 1)
    gl = jnp.where(lane < DN_HEADS, _sigmoid(gsl), -jnp.exp(gpar_ref[0:1, :]) * softplus)
    gl_ref[...] = gl
    row_in_chunk = lax.broadcasted_iota(I32, gsl.shape, 0) % c
    gc = gl
    shift = 1
    while shift < c:
        gc = gc + jnp.where(row_in_chunk >= shift, pltpu.roll(gc, shift, 0), 0.0)
        shift *= 2
    gc_ref[...] = gc

    ii = lax.broadcasted_iota(I32, (c, c), 0)
    jj = lax.broadcasted_iota(I32, (c, c), 1)
    tri_incl = ii >= jj
    tri_strict = ii > jj
    eye = jnp.where(ii == jj, 1.0, 0.0).astype(F32)
    heads = range(DN_HEADS)

    def conv_silu(r0, col):
        w = convw_ref[:, col:col + hd]
        y = w[CONV_WIDTH - 1:CONV_WIDTH, :] * xc_ref[pl.ds(r0 + 8, c), col:col + hd]
        for j in range(CONV_WIDTH - 1):
            off = 8 - (CONV_WIDTH - 1) + j
            y = y + w[j:j + 1, :] * xc_ref[pl.ds(r0 + off, c), col:col + hd]
        return _silu(y)

    def l2n(t, scale):
        return t * (lax.rsqrt(jnp.sum(t * t, axis=-1, keepdims=True) + L2_EPS) * scale)

    def phase_a(i, carry):
        chains = []
        for sub in range(DN_A_UNROLL):
            ci = i * DN_A_UNROLL + sub
            r0 = ci * c
            glc = gl_ref[pl.ds(r0, c), :]
            gcc = gc_ref[pl.ds(r0, c), :]
            gct = jnp.concatenate([gcc, gcc], axis=0).T
            egl_ref[ci] = jnp.exp(gcc[c - 8:c, :])
            for h in heads:
                chains.append((ci, r0, h, glc, gcc, gct))
        nchain = len(chains)
        q = [l2n(conv_silu(r0, COL_DNQ + h * hd), hd ** -0.5) for (ci, r0, h, _, _, _) in chains]
        k = [l2n(conv_silu(r0, COL_DNK + h * hd), 1.0) for (ci, r0, h, _, _, _) in chains]
        v = [conv_silu(r0, COL_DNV + h * hd) for (ci, r0, h, _, _, _) in chains]
        kb, vb, decay, egc = [], [], [], []
        for n_, (ci, r0, h, glc, gcc, gct) in enumerate(chains):
            beta = glc[:, h:h + 1]
            gc_col = gcc[:, DN_HEADS + h:DN_HEADS + h + 1]
            gc_row = gct[DN_HEADS + h:DN_HEADS + h + 1, 0:c]
            decay.append(jnp.where(tri_incl, jnp.exp(jnp.minimum(gc_col - gc_row, 0.0)), 0.0))
            egc.append(jnp.exp(gc_col))
            e_tail = jnp.exp(gcc[c - 1:c, DN_HEADS + h:DN_HEADS + h + 1] - gc_col)
            kb.append(k[n_] * beta)
            vb.append(v[n_] * beta)
            kt_ref[ci, h] = (k[n_] * e_tail).astype(BF16)
        kq = [_mm_nt(jnp.concatenate([kb[n_], q[n_]], axis=0), k[n_]) for n_ in range(nchain)]
        a_mat = [jnp.where(tri_strict, kq[n_][0:c] * decay[n_], 0.0) for n_ in range(nchain)]
        for n_, (ci, r0, h, _, _, _) in enumerate(chains):
            attn_ref[ci, h] = (kq[n_][c:2 * c] * decay[n_]).astype(BF16)
        t_inv = [eye - a for a in a_mat]
        p = a_mat
        for _ in range(5):
            p = [_mm(x, x) for x in p]
            t_inv = [t + _mm(t, x) for t, x in zip(t_inv, p)]
        for n_, (ci, r0, h, _, _, _) in enumerate(chains):
            uw = _mm3(t_inv[n_], jnp.concatenate([vb[n_], kb[n_] * egc[n_]], axis=1))
            u_ref[ci, h] = uw[:, 0:hd]
            wq_ref[ci, h, 0:c, :] = uw[:, hd:2 * hd].astype(BF16)
            wq_ref[ci, h, c:2 * c, :] = (q[n_] * egc[n_]).astype(BF16)
        return carry

    for i in range(nch // DN_A_UNROLL):
        phase_a(i, 0)

    normg = normg_ref[...]

    def phase_b(ci, carry):
        r0 = pl.multiple_of(ci * c, c)
        rows = pl.ds(r0, c)
        egl = egl_ref[ci]
        s_old = [s_ref[h] for h in heads]
        ws = [_dot(wq_ref[ci, h], s_old[h].astype(BF16)) for h in heads]
        v_new = [(u_ref[ci, h] - ws[h][0:c]).astype(BF16) for h in heads]
        for h in heads:
            s_ref[h] = (s_old[h] * egl[7:8, DN_HEADS + h:DN_HEADS + h + 1]
                        + lax.dot_general(kt_ref[ci, h], v_new[h], (((0,), (0,)), ((), ())),
                                          preferred_element_type=F32))
        for h in heads:
            o = ws[h][c:2 * c] + _dot(attn_ref[ci, h], v_new[h])
            o = o * lax.rsqrt(jnp.mean(o * o, axis=-1, keepdims=True) + RMS_EPS) * normg
            z = x_ref[rows, COL_Z + h * hd:COL_Z + (h + 1) * hd].astype(F32)
            out_ref[rows, h * hd:(h + 1) * hd] = (o * _silu(z)).astype(out_ref.dtype)
        return carry

    lax.fori_loop(0, nch, phase_b, 0)


def _deltanet(main3d, gates3d, conv_w, gpar, normg):
    b, t, _ = main3d.shape
    ts = TS_DN
    nch = ts // DN_CHUNK
    dn_in = COL_Z + DN_WIDTH
    return pl.pallas_call(
        _dn_kernel,
        grid=(b, t // ts),
        in_specs=[
            pl.BlockSpec((None, ts, dn_in), lambda bi, si: (bi, si, 0)),
            pl.BlockSpec((None, ts, GATE_COLS), lambda bi, si: (bi, si, 0)),
            pl.BlockSpec((CONV_WIDTH, 3 * DN_WIDTH), lambda bi, si: (0, 0)),
            pl.BlockSpec((2, GATE_COLS), lambda bi, si: (0, 0)),
            pl.BlockSpec((1, DN_HEAD_DIM), lambda bi, si: (0, 0)),
        ],
        out_specs=pl.BlockSpec((None, ts, DN_WIDTH), lambda bi, si: (bi, si, 0)),
        out_shape=jax.ShapeDtypeStruct((b, t, DN_WIDTH), BF16),
        scratch_shapes=[
            pltpu.VMEM((ts + 8, 3 * DN_WIDTH), F32),
            pltpu.VMEM((ts, GATE_COLS), F32),
            pltpu.VMEM((ts, GATE_COLS), F32),
            pltpu.VMEM((nch, DN_HEADS, 2 * DN_CHUNK, DN_HEAD_DIM), BF16),
            pltpu.VMEM((nch, DN_HEADS, DN_CHUNK, DN_HEAD_DIM), F32),
            pltpu.VMEM((nch, DN_HEADS, DN_CHUNK, DN_HEAD_DIM), BF16),
            pltpu.VMEM((nch, DN_HEADS, DN_CHUNK, DN_CHUNK), BF16),
            pltpu.VMEM((nch, 8, GATE_COLS), F32),
            pltpu.VMEM((DN_HEADS, DN_HEAD_DIM, DN_HEAD_DIM), F32),
            pltpu.VMEM((8, 3 * DN_WIDTH), F32),
        ],
        compiler_params=pltpu.CompilerParams(
            dimension_semantics=("parallel", "arbitrary"), vmem_limit_bytes=VMEM_LIMIT),
        name="deltanet",
    )(main3d, gates3d, conv_w, gpar, normg)


def _swa_kernel(sinks_ref, q_ref, kp_ref, kc_ref, vp_ref, vc_ref, out_ref, bias_ref):
    n = pl.program_id(1)
    blk = SWA_BLOCK
    d = SWA_HEAD_DIM
    grp = SWA_Q_HEADS // SWA_KV_HEADS

    @pl.when((pl.program_id(0) == 0) & (n == 0))
    def _():
        qi = lax.broadcasted_iota(I32, (blk, 2 * blk), 0)
        kj = lax.broadcasted_iota(I32, (blk, 2 * blk), 1)
        dist = qi + blk - kj
        valid = (dist >= 0) & (dist < SWA_WINDOW)
        dist_f = dist.astype(F32)
        for hq in range(SWA_Q_HEADS):
            slope = 2.0 ** (-8.0 * (hq + 1.0) / SWA_Q_HEADS)
            bias_ref[hq] = jnp.where(valid, -slope * dist_f, NEG_INF)

    kcol = lax.broadcasted_iota(I32, (1, 2 * blk), 1)
    colmask = jnp.where((kcol >= blk) | (n > 0), 0.0, NEG_INF)
    q_all = q_ref[...] * (d ** -0.5)
    kband = [jnp.concatenate([kp_ref[:, hk * d:(hk + 1) * d], kc_ref[:, hk * d:(hk + 1) * d]], axis=0)
             for hk in range(SWA_KV_HEADS)]
    vband = [jnp.concatenate([vp_ref[:, hk * d:(hk + 1) * d], vc_ref[:, hk * d:(hk + 1) * d]], axis=0)
             for hk in range(SWA_KV_HEADS)]
    heads = range(SWA_Q_HEADS)
    scores = [_mm_nt(q_all[:, hq * d:(hq + 1) * d], kband[hq // grp]) for hq in heads]
    probs, denoms = [], []
    for hq in heads:
        s = scores[hq] + bias_ref[hq] + colmask
        sink = sinks_ref[hq]
        m = jnp.maximum(jnp.max(s, axis=-1, keepdims=True), sink)
        p = jnp.exp(s - m)
        denoms.append(jnp.sum(p, axis=-1, keepdims=True) + jnp.exp(sink - m))
        probs.append(p.astype(BF16))
    outs = [_dot(probs[hq], vband[hq // grp]) / denoms[hq] for hq in heads]
    out_ref[...] = jnp.concatenate(outs, axis=-1).astype(out_ref.dtype)


def _swa(main3d, sinks):
    b, t, _ = main3d.shape
    blk = SWA_BLOCK
    qb = COL_SQ // SWA_WIDTH
    kb = COL_SK // SWA_KV_WIDTH
    vb = COL_SV // SWA_KV_WIDTH
    grid_spec = pltpu.PrefetchScalarGridSpec(
        num_scalar_prefetch=1,
        grid=(b, t // blk),
        in_specs=[
            pl.BlockSpec((None, blk, SWA_WIDTH), lambda bi, ni, s: (bi, ni, qb)),
            pl.BlockSpec((None, blk, SWA_KV_WIDTH), lambda bi, ni, s: (bi, jnp.maximum(ni - 1, 0), kb)),
            pl.BlockSpec((None, blk, SWA_KV_WIDTH), lambda bi, ni, s: (bi, ni, kb)),
            pl.BlockSpec((None, blk, SWA_KV_WIDTH), lambda bi, ni, s: (bi, jnp.maximum(ni - 1, 0), vb)),
            pl.BlockSpec((None, blk, SWA_KV_WIDTH), lambda bi, ni, s: (bi, ni, vb)),
        ],
        out_specs=pl.BlockSpec((None, blk, SWA_WIDTH), lambda bi, ni, s: (bi, ni, 0)),
        scratch_shapes=[pltpu.VMEM((SWA_Q_HEADS, blk, 2 * blk), F32)],
    )
    return pl.pallas_call(
        _swa_kernel,
        grid_spec=grid_spec,
        out_shape=jax.ShapeDtypeStruct((b, t, SWA_WIDTH), BF16),
        compiler_params=pltpu.CompilerParams(
            dimension_semantics=("arbitrary", "arbitrary"), vmem_limit_bytes=VMEM_LIMIT),
        name="swa",
    )(sinks, main3d, main3d, main3d, main3d, main3d)


def _layer_norm(y, g, b):
    mu = jnp.mean(y, axis=-1, keepdims=True)
    yc = y - mu
    var = jnp.mean(yc * yc, axis=-1, keepdims=True)
    return yc * lax.rsqrt(var + LN_EPS) * g + b


def _pack_bf16_pair(lo, hi):
    lo_bits = lax.bitcast_convert_type(lo.astype(BF16).astype(F32), U32)
    hi_bits = lax.bitcast_convert_type(hi.astype(BF16).astype(F32), U32)
    return (hi_bits & jnp.uint32(0xFFFF0000)) | (lo_bits >> 16)


def _unpack_bf16_pair(packed):
    lo = lax.bitcast_convert_type(packed << 16, F32)
    hi = lax.bitcast_convert_type(packed & jnp.uint32(0xFFFF0000), F32)
    return lo, hi


def _post_mix_kernel(x_ref, dn_ref, swa_ref, wo_dn_ref, wo_swa_ref, g_ref, b_ref,
                     rwh_ref, rwl_ref, sg_ref, su_ref, sd_ref,
                     base_ref, xpk_ref, logit_ref):
    mix = _dot(dn_ref[...], wo_dn_ref[...]) + _dot(swa_ref[...], wo_swa_ref[...])
    x1 = _layer_norm(DEEPNORM_ALPHA * x_ref[...] + mix, g_ref[...], b_ref[...])
    half = D_MODEL // 2
    xpk_ref[...] = _pack_bf16_pair(x1[:, :half], x1[:, half:])
    xh, xl = _split2(x1)
    nt = (((1,), (1,)), ((), ()))
    logit_ref[...] = (lax.dot_general(rwh_ref[...], xh, nt, preferred_element_type=F32)
                      + lax.dot_general(rwh_ref[...], xl, nt, preferred_element_type=F32)
                      + lax.dot_general(rwl_ref[...], xh, nt, preferred_element_type=F32))
    hmid = _silu(_dot(xh, sg_ref[...])) * _dot(xh, su_ref[...])
    base_ref[...] = DEEPNORM_ALPHA * x1 + _dot(hmid.astype(BF16), sd_ref[...])


def _post_mix(x2d, dn2d, swa2d, wo_dn, wo_swa, ln_g, ln_b, rwh, rwl, sg, su, sd):
    n = x2d.shape[0]
    tm = TM_POST
    full = lambda shape: pl.BlockSpec(shape, lambda i: (0, 0))
    return pl.pallas_call(
        _post_mix_kernel,
        grid=(n // tm,),
        in_specs=[
            pl.BlockSpec((tm, D_MODEL), lambda i: (i, 0)),
            pl.BlockSpec((tm, DN_WIDTH), lambda i: (i, 0)),
            pl.BlockSpec((tm, SWA_WIDTH), lambda i: (i, 0)),
            full((DN_WIDTH, D_MODEL)), full((SWA_WIDTH, D_MODEL)),
            full((1, D_MODEL)), full((1, D_MODEL)),
            full((N_EXPERTS, D_MODEL)), full((N_EXPERTS, D_MODEL)),
            full((D_MODEL, SHARED_FF)), full((D_MODEL, SHARED_FF)), full((SHARED_FF, D_MODEL)),
        ],
        out_specs=[
            pl.BlockSpec((tm, D_MODEL), lambda i: (i, 0)),
            pl.BlockSpec((tm, D_MODEL // 2), lambda i: (i, 0)),
            pl.BlockSpec((N_EXPERTS, tm), lambda i: (0, i)),
        ],
        out_shape=[
            jax.ShapeDtypeStruct((n, D_MODEL), F32),
            jax.ShapeDtypeStruct((n, D_MODEL // 2), U32),
            jax.ShapeDtypeStruct((N_EXPERTS, n), F32),
        ],
        compiler_params=pltpu.CompilerParams(
            dimension_semantics=("parallel",), vmem_limit_bytes=VMEM_LIMIT),
        name="post_mix",
    )(x2d, dn2d, swa2d, wo_dn, wo_swa, ln_g, ln_b, rwh, rwl, sg, su, sd)


def _route_kernel(lg_ref, bias_ref, eidx_ref, gate_ref, rank_ref, cnt_ref, carry_ref):
    @pl.when(pl.program_id(0) == 0)
    def _():
        carry_ref[...] = jnp.zeros_like(carry_ref)

    tt = lg_ref.shape[1]
    scores = _sigmoid(lg_ref[...])
    sel = scores + bias_ref[...]

    iog = lax.broadcasted_iota(I32, (GROUP_SIZE, tt), 0)
    grp_rows = []
    for g in range(N_GROUPS):
        blk = sel[g * GROUP_SIZE:(g + 1) * GROUP_SIZE, :]
        m1 = jnp.max(blk, axis=0, keepdims=True)
        i1 = jnp.min(jnp.where(blk == m1, iog, GROUP_SIZE), axis=0, keepdims=True)
        m2 = jnp.max(jnp.where(iog == i1, NEG_INF, blk), axis=0, keepdims=True)
        grp_rows.append(m1 + m2)
    gs = jnp.concatenate(grp_rows, axis=0)

    io8 = lax.broadcasted_iota(I32, (N_GROUPS, tt), 0)
    gsel = jnp.zeros((N_GROUPS, tt), F32)
    for _ in range(TOPK_GROUPS):
        mg = jnp.max(gs, axis=0, keepdims=True)
        ig = jnp.min(jnp.where(gs == mg, io8, N_GROUPS), axis=0, keepdims=True)
        hit = io8 == ig
        gsel = jnp.where(hit, 1.0, gsel)
        gs = jnp.where(hit, NEG_INF, gs)

    val = jnp.concatenate(
        [jnp.where(gsel[g:g + 1, :] > 0.0, sel[g * GROUP_SIZE:(g + 1) * GROUP_SIZE, :], NEG_INF)
         for g in range(N_GROUPS)], axis=0)

    ioe = lax.broadcasted_iota(I32, (N_EXPERTS, tt), 0)
    onehot = jnp.zeros((N_EXPERTS, tt), F32)
    idx_rows, gate_rows = [], []
    for _ in range(TOP_K):
        m = jnp.max(val, axis=0, keepdims=True)
        ik = jnp.min(jnp.where(val == m, ioe, N_EXPERTS), axis=0, keepdims=True)
        hit = ioe == ik
        gate_rows.append(jnp.sum(jnp.where(hit, scores, 0.0), axis=0, keepdims=True))
        idx_rows.append(ik)
        val = jnp.where(hit, NEG_INF, val)
        onehot = jnp.where(hit, 1.0, onehot)
    gsum = gate_rows[0]
    for r in gate_rows[1:]:
        gsum = gsum + r
    gate_ref[...] = jnp.concatenate(gate_rows, axis=0) / gsum * ROUTED_SCALE
    eidx_ref[...] = jnp.concatenate(idx_rows, axis=0)

    ti = lax.broadcasted_iota(I32, (tt, tt), 0)
    tj = lax.broadcasted_iota(I32, (tt, tt), 1)
    upper = jnp.where(ti < tj, 1.0, 0.0).astype(BF16)
    cum = _dot(onehot.astype(BF16), upper) + jnp.broadcast_to(carry_ref[:, 0:1], (N_EXPERTS, tt))
    rank_rows = [jnp.sum(jnp.where(ioe == ik, cum, 0.0), axis=0, keepdims=True) for ik in idx_rows]
    rank_ref[...] = jnp.concatenate(rank_rows, axis=0).astype(I32)
    carry_ref[...] = carry_ref[...] + jnp.broadcast_to(
        jnp.sum(onehot, axis=1, keepdims=True), carry_ref.shape)
    cnt_ref[...] = carry_ref[...].astype(I32)


def _route(logits_t, bias_col):
    n = logits_t.shape[1]
    tt = TT_ROUTE
    row_spec = pl.BlockSpec((TOP_K, tt), lambda i: (0, i))
    return pl.pallas_call(
        _route_kernel,
        grid=(n // tt,),
        in_specs=[
            pl.BlockSpec((N_EXPERTS, tt), lambda i: (0, i)),
            pl.BlockSpec((N_EXPERTS, 1), lambda i: (0, 0)),
        ],
        out_specs=[row_spec, row_spec, row_spec,
                   pl.BlockSpec((N_EXPERTS, 128), lambda i: (0, 0))],
        out_shape=[
            jax.ShapeDtypeStruct((TOP_K, n), I32),
            jax.ShapeDtypeStruct((TOP_K, n), F32),
            jax.ShapeDtypeStruct((TOP_K, n), I32),
            jax.ShapeDtypeStruct((N_EXPERTS, 128), I32),
        ],
        scratch_shapes=[pltpu.VMEM((N_EXPERTS, 128), F32)],
        compiler_params=pltpu.CompilerParams(
            dimension_semantics=("arbitrary",), vmem_limit_bytes=VMEM_LIMIT),
        name="route",
    )(logits_t, bias_col)


def _place_kernel(eidx_ref, rank_ref, pstart_ref, dest_ref):
    tt = eidx_ref.shape[1]
    ioe = lax.broadcasted_iota(I32, (N_EXPERTS, tt), 0)
    pstart = pstart_ref[...]
    rows = [jnp.sum(jnp.where(ioe == eidx_ref[k:k + 1, :], pstart, 0.0), axis=0, keepdims=True)
            for k in range(TOP_K)]
    dest_ref[...] = jnp.concatenate(rows, axis=0).astype(I32) + rank_ref[...]


def _place(eidx, rank, pstart_col):
    n = eidx.shape[1]
    tt = TT_ROUTE
    row_spec = pl.BlockSpec((TOP_K, tt), lambda i: (0, i))
    return pl.pallas_call(
        _place_kernel,
        grid=(n // tt,),
        in_specs=[row_spec, row_spec, pl.BlockSpec((N_EXPERTS, 1), lambda i: (0, 0))],
        out_specs=row_spec,
        out_shape=jax.ShapeDtypeStruct((TOP_K, n), I32),
        compiler_params=pltpu.CompilerParams(
            dimension_semantics=("parallel",), vmem_limit_bytes=VMEM_LIMIT),
        name="place",
    )(eidx, rank, pstart_col)


def _sc_mesh():
    return plsc.VectorSubcoreMesh(core_axis_name="c", subcore_axis_name="s",
                                  num_cores=SC_NC, num_subcores=SC_NS)


def _sc_scatter_rows(rows, idx, nrows_out):
    n, d = rows.shape
    nk = idx.shape[0]
    per_w = n // SC_NW
    nwin = per_w // SC_WIN
    assert per_w * SC_NW == n and nwin * SC_WIN == per_w and nwin % 2 == 0

    @functools.partial(
        pl.kernel, mesh=_sc_mesh(),
        out_type=jax.ShapeDtypeStruct((nrows_out, d), rows.dtype),
        scratch_types=[
            pltpu.VMEM((nwin, nk, SC_WIN), I32),
            pltpu.VMEM((2, SC_WIN, d), rows.dtype),
            pltpu.SemaphoreType.DMA((2,)),
            pltpu.SemaphoreType.DMA((2,)),
        ],
        compiler_params=pltpu.CompilerParams(use_tc_tiling_on_sc=True),
        name="sc_scatter_rows",
    )
    def scatter_kernel(rows_hbm, idx_hbm, out_hbm, idx_v, rows_v, lsem, ssem):
        wid = lax.axis_index("s") * SC_NC + lax.axis_index("c")
        base = wid * per_w
        pltpu.sync_copy(idx_hbm.at[wid], idx_v)

        def load(w, slot):
            return pltpu.make_async_copy(
                rows_hbm.at[pl.ds(base + w * SC_WIN, SC_WIN)], rows_v.at[slot], lsem.at[slot])

        def scat(w, k, slot):
            return pltpu.make_async_copy(rows_v.at[slot], out_hbm.at[idx_v.at[w, k]], ssem.at[slot])

        load(0, 0).start()

        @pl.loop(0, nwin, step=2)
        def _(w0):
            for slot in range(2):
                w = w0 + slot
                load(w, slot).wait()

                @pl.when(w + 1 < nwin)
                def _():
                    @pl.when(w >= 1)
                    def _():
                        for k in range(nk):
                            scat(w - 1, k, 1 - slot).wait()
                    load(w + 1, 1 - slot).start()

                for k in range(nk):
                    scat(w, k, slot).start()

        for k in range(nk):
            scat(nwin - 2, k, 0).wait()
        for k in range(nk):
            scat(nwin - 1, k, 1).wait()

    idx4 = idx.reshape(nk, SC_NW, nwin, SC_WIN).transpose(1, 2, 0, 3)
    return scatter_kernel(rows, idx4)


def _expert_kernel(gstart_ref, cnt_ref, xs_hbm, wg_ref, wu_ref, wd_ref, y_hbm,
                   wgb_ref, wub_ref, wdb_ref, xbuf_ref, ybuf_ref, xsem, ysem):
    e = pl.program_id(0)
    ne = pl.num_programs(0)
    bm = xbuf_ref.shape[1]
    nblk = y_hbm.shape[0] // bm
    half = D_MODEL // 2
    g_lo = gstart_ref[e]
    g_hi = gstart_ref[e + 1]
    g_end = gstart_ref[ne]

    def x_copy(g, slot):
        return pltpu.make_async_copy(xs_hbm.at[pl.ds(g * bm, bm), :], xbuf_ref.at[slot], xsem.at[slot])

    def y_copy(g, slot):
        return pltpu.make_async_copy(ybuf_ref.at[slot], y_hbm.at[pl.ds(g * bm, bm), :], ysem.at[slot])

    nslot = xbuf_ref.shape[0]

    ahead = nslot - 2

    @pl.when(e == 0)
    def _():
        for g0 in range(ahead):
            @pl.when(g0 < g_end)
            def _():
                x_copy(g0, g0).start()

    @pl.when(g_hi > g_lo)
    def _():
        wgb_ref[...] = wg_ref[...].astype(BF16)
        wub_ref[...] = wu_ref[...].astype(BF16)
        wdb_ref[...] = wd_ref[...].astype(BF16)

    row = lax.broadcasted_iota(I32, (bm, half), 0)

    def acquire(g):
        x_copy(g, g % nslot).wait()

        @pl.when(g + ahead < g_end)
        def _():
            x_copy(g + ahead, (g + ahead) % nslot).start()

        @pl.when(g >= nslot)
        def _():
            y_copy(g - nslot, g % nslot).wait()

    def load(g):
        n_valid = cnt_ref[e] - (g - g_lo) * bm
        x_lo, x_hi = _unpack_bf16_pair(jnp.where(row < n_valid, xbuf_ref[g % nslot], jnp.uint32(0)))
        return x_lo.astype(BF16), x_hi.astype(BF16)

    def gate_up(x):
        x_lo, x_hi = x
        gate = _dot(x_lo, wgb_ref[:half, :]) + _dot(x_hi, wgb_ref[half:, :])
        up = _dot(x_lo, wub_ref[:half, :]) + _dot(x_hi, wub_ref[half:, :])
        return gate, up

    def down(gu):
        gate, up = gu
        return _dot((_silu(gate) * up).astype(BF16), wdb_ref[...])

    def store(g, y):
        ybuf_ref[g % nslot] = _pack_bf16_pair(y[:, :half], y[:, half:])
        y_copy(g, g % nslot).start()

    def pair(p, carry):
        g = g_lo + 2 * p
        acquire(g)
        acquire(g + 1)
        gu_a = gate_up(load(g))
        gu_b = gate_up(load(g + 1))
        y_a = down(gu_a)
        y_b = down(gu_b)
        store(g, y_a)
        store(g + 1, y_b)
        return carry

    n_own = g_hi - g_lo
    lax.fori_loop(0, n_own // 2, pair, 0)

    @pl.when(n_own % 2 == 1)
    def _():
        acquire(g_hi - 1)
        store(g_hi - 1, down(gate_up(load(g_hi - 1))))

    @pl.when(e == ne - 1)
    def _():
        for back in range(nslot, 0, -1):
            @pl.when(g_end >= back)
            def _():
                y_copy(g_end - back, (g_end - back) % nslot).wait()

        ybuf_ref[0] = jnp.zeros((bm, half), U32)

        def fill(g, carry):
            y_copy(g, 0).start()
            return carry

        def drain(g, carry):
            y_copy(g, 0).wait()
            return carry

        lax.fori_loop(g_end, nblk, fill, 0)
        lax.fori_loop(g_end, nblk, drain, 0)


def _experts(gstart, counts, xs, w_gate, w_up, w_down):
    bm = BM_EXP
    nblk = xs.shape[0] // bm
    half = D_MODEL // 2
    grid_spec = pltpu.PrefetchScalarGridSpec(
        num_scalar_prefetch=2,
        grid=(N_EXPERTS,),
        in_specs=[
            pl.BlockSpec(memory_space=pl.ANY),
            pl.BlockSpec((None, D_MODEL, EXPERT_FF), lambda e, gs, cn: (e, 0, 0)),
            pl.BlockSpec((None, D_MODEL, EXPERT_FF), lambda e, gs, cn: (e, 0, 0)),
            pl.BlockSpec((None, EXPERT_FF, D_MODEL), lambda e, gs, cn: (e, 0, 0)),
        ],
        out_specs=pl.BlockSpec(memory_space=pl.ANY),
        scratch_shapes=[
            pltpu.VMEM((D_MODEL, EXPERT_FF), BF16),
            pltpu.VMEM((D_MODEL, EXPERT_FF), BF16),
            pltpu.VMEM((EXPERT_FF, D_MODEL), BF16),
            pltpu.VMEM((EXP_SLOTS, bm, half), U32),
            pltpu.VMEM((EXP_SLOTS, bm, half), U32),
            pltpu.SemaphoreType.DMA((EXP_SLOTS,)),
            pltpu.SemaphoreType.DMA((EXP_SLOTS,)),
        ],
    )
    return pl.pallas_call(
        _expert_kernel,
        grid_spec=grid_spec,
        out_shape=jax.ShapeDtypeStruct((nblk * bm, half), U32),
        compiler_params=pltpu.CompilerParams(
            dimension_semantics=("arbitrary",), vmem_limit_bytes=VMEM_LIMIT),
        name="experts",
    )(gstart, counts, xs, w_gate, w_up, w_down)


def _sc_gather_rows(table, idx):
    nrows = idx.shape[0]
    d = table.shape[1]
    per_w = nrows // SC_NW
    nwin = per_w // SC_WIN
    assert per_w * SC_NW == nrows and nwin * SC_WIN == per_w and nwin % 2 == 0
    @functools.partial(
        pl.kernel, mesh=_sc_mesh(),
        out_type=jax.ShapeDtypeStruct((nrows, d), table.dtype),
        scratch_types=[
            pltpu.VMEM((nwin, SC_WIN), I32),
            pltpu.VMEM((2, SC_WIN, d), table.dtype),
            pltpu.SemaphoreType.DMA((2,)),
            pltpu.SemaphoreType.DMA((2,)),
        ],
        compiler_params=pltpu.CompilerParams(use_tc_tiling_on_sc=True),
        name="sc_gather_rows",
    )
    def gather_kernel(table_hbm, idx_hbm, out_hbm, idx_v, rows_v, gsem, wsem):
        wid = lax.axis_index("s") * SC_NC + lax.axis_index("c")
        base = wid * per_w
        pltpu.sync_copy(idx_hbm.at[wid], idx_v)

        def gather(w, slot):
            return pltpu.make_async_copy(table_hbm.at[idx_v.at[w]], rows_v.at[slot], gsem.at[slot])

        def put(w, slot):
            return pltpu.make_async_copy(
                rows_v.at[slot], out_hbm.at[pl.ds(base + w * SC_WIN, SC_WIN)], wsem.at[slot])

        gather(0, 0).start()

        @pl.loop(0, nwin, step=2)
        def _(w0):
            for slot in range(2):
                w = w0 + slot
                gather(w, slot).wait()

                @pl.when(w + 1 < nwin)
                def _():
                    @pl.when(w >= 1)
                    def _():
                        put(w - 1, 1 - slot).wait()
                    gather(w + 1, 1 - slot).start()

                put(w, slot).start()

        put(nwin - 2, 0).wait()
        put(nwin - 1, 1).wait()

    return gather_kernel(table, idx.reshape(SC_NW, nwin, SC_WIN))


def _combine_kernel(y_ref, base_ref, gate_ref, g_ref, b_ref, out_ref):
    half = D_MODEL // 2
    gates = gate_ref[...]
    acc_lo = base_ref[:, :half]
    acc_hi = base_ref[:, half:]
    for k in range(TOP_K):
        y_lo, y_hi = _unpack_bf16_pair(y_ref[k])
        gk = gates[:, k:k + 1]
        acc_lo = acc_lo + gk * y_lo
        acc_hi = acc_hi + gk * y_hi
    mu = (jnp.sum(acc_lo, axis=-1, keepdims=True) + jnp.sum(acc_hi, axis=-1, keepdims=True)) / D_MODEL
    c_lo = acc_lo - mu
    c_hi = acc_hi - mu
    var = (jnp.sum(c_lo * c_lo, axis=-1, keepdims=True)
           + jnp.sum(c_hi * c_hi, axis=-1, keepdims=True)) / D_MODEL
    inv = lax.rsqrt(var + LN_EPS)
    out_ref[:, :half] = c_lo * inv * g_ref[:, :half] + b_ref[:, :half]
    out_ref[:, half:] = c_hi * inv * g_ref[:, half:] + b_ref[:, half:]


def _combine(ybuf_part, acc, gate_tok, ln_g, ln_b, part):
    n = acc.shape[0]
    n_part = ybuf_part.shape[1]
    tt = TT_COMB
    half = D_MODEL // 2
    steps = n_part // tt
    first = part * steps
    return pl.pallas_call(
        _combine_kernel,
        grid=(steps,),
        in_specs=[
            pl.BlockSpec((TOP_K, tt, half), lambda i: (0, i, 0)),
            pl.BlockSpec((tt, D_MODEL), lambda i: (first + i, 0)),
            pl.BlockSpec((tt, TOP_K), lambda i: (first + i, 0)),
            pl.BlockSpec((1, D_MODEL), lambda i: (0, 0)),
            pl.BlockSpec((1, D_MODEL), lambda i: (0, 0)),
        ],
        out_specs=pl.BlockSpec((tt, D_MODEL), lambda i: (first + i, 0)),
        out_shape=jax.ShapeDtypeStruct((n, D_MODEL), F32),
        input_output_aliases={1: 0},
        compiler_params=pltpu.CompilerParams(
            dimension_semantics=("parallel",), vmem_limit_bytes=VMEM_LIMIT),
        name="combine",
    )(ybuf_part, acc, gate_tok, ln_g, ln_b)


def _regroup_w_in(w_in):
    o = 0
    cols = {}
    for name, width in (("dnq", DN_WIDTH), ("dnk", DN_WIDTH), ("dnv", DN_WIDTH), ("sq", SWA_WIDTH),
                        ("sk", SWA_KV_WIDTH), ("sv", SWA_KV_WIDTH), ("z", DN_WIDTH),
                        ("b", DN_HEADS), ("a", DN_HEADS)):
        cols[name] = w_in[:, o:o + width]
        o += width
    w_main = jnp.concatenate([cols[k] for k in ("dnq", "dnk", "dnv", "z", "sq", "sk", "sv")], axis=1)
    w_gates = jnp.concatenate(
        [cols["b"], cols["a"], jnp.zeros((D_MODEL, GATE_COLS - 2 * DN_HEADS), w_in.dtype)], axis=1)
    return w_main.astype(BF16), w_gates.astype(BF16)


def _layer(x, w_in, conv_w, a_log, dt_bias, dn_norm_g, sinks, w_out, ln1_g, ln1_b,
           router_w, router_bias, w_gate, w_up, w_down, sh_gate, sh_up, sh_down, ln2_g, ln2_b):
    b, t, d = x.shape
    n = b * t
    x2d = x.reshape(n, d)

    w_main, w_gates = _regroup_w_in(w_in)
    main, gates = _in_proj(x2d, w_main, w_gates)
    main3d = main.reshape(b, t, MAIN_COLS)

    pad = jnp.zeros((GATE_COLS - 2 * DN_HEADS,), F32)
    gpar = jnp.stack([jnp.concatenate([jnp.zeros((DN_HEADS,), F32), a_log.astype(F32), pad]),
                      jnp.concatenate([jnp.zeros((DN_HEADS,), F32), dt_bias.astype(F32), pad])])
    dn_out = _deltanet(main3d, gates.reshape(b, t, GATE_COLS), conv_w.astype(F32), gpar,
                       dn_norm_g.astype(F32).reshape(1, DN_HEAD_DIM))
    swa_out = _swa(main3d, sinks.astype(F32))

    rw_t = router_w.T.astype(F32)
    rwh = rw_t.astype(BF16)
    rwl = (rw_t - rwh.astype(F32)).astype(BF16)
    base, xpk, logits_t = _post_mix(
        x2d, dn_out.reshape(n, DN_WIDTH), swa_out.reshape(n, SWA_WIDTH),
        w_out[:DN_WIDTH].astype(BF16), w_out[DN_WIDTH:].astype(BF16),
        ln1_g.reshape(1, d).astype(F32), ln1_b.reshape(1, d).astype(F32), rwh, rwl,
        sh_gate.astype(BF16), sh_up.astype(BF16), sh_down.astype(BF16))

    eidx, gate, rank, cnt = _route(logits_t, router_bias.astype(F32).reshape(N_EXPERTS, 1))

    bm = BM_EXP
    counts = cnt[:, 0]
    padded = (counts + bm - 1) // bm * bm
    pend = jnp.cumsum(padded)
    pstart = pend - padded
    nblk = -(-(n * TOP_K) // bm) + N_EXPERTS
    gstart = (jnp.concatenate([pstart, pend[-1:]]) // bm).astype(I32)

    dest = _place(eidx, rank, pstart.astype(F32).reshape(N_EXPERTS, 1))
    xs = _sc_scatter_rows(xpk, dest, nblk * bm)
    ypk = _experts(gstart, counts, xs, w_gate, w_up, w_down)
    n_part = n // COMB_PARTS
    gate_tok = gate.T
    ln_g, ln_b = ln2_g.reshape(1, d).astype(F32), ln2_b.reshape(1, d).astype(F32)
    out = base
    for p in range(COMB_PARTS):
        idx = dest[:, p * n_part:(p + 1) * n_part].reshape(-1)
        ybuf = _sc_gather_rows(ypk, idx).reshape(TOP_K, n_part, d // 2)
        out = _combine(ybuf, out, gate_tok, ln_g, ln_b, p)
    return out.reshape(b, t, d)


def kernel(x, w_in, conv_w, a_log, dt_bias, dn_norm_g, sinks, w_out, ln1_g, ln1_b, router_w, router_bias,
           w_gate, w_up, w_down, shared_w_gate, shared_w_up, shared_w_down, ln2_g, ln2_b):
    depth = w_in.shape[0]
    for l in range(depth):
        x = _layer(x, w_in[l], conv_w[l], a_log[l], dt_bias[l], dn_norm_g[l], sinks[l], w_out[l],
                   ln1_g[l], ln1_b[l], router_w[l], router_bias[l], w_gate[l], w_up[l], w_down[l],
                   shared_w_gate[l], shared_w_up[l], shared_w_down[l], ln2_g[l], ln2_b[l])
    return x
```

```python
import functools

import jax
import jax.numpy as jnp
from jax import lax
from jax.experimental import pallas as pl
from jax.experimental.pallas import tpu as pltpu
from jax.experimental.pallas import tpu_sc as plsc

F32 = jnp.float32
BF16 = jnp.bfloat16
I32 = jnp.int32
U32 = jnp.uint32

D_MODEL = 1024
DN_HEADS = 4
DN_HEAD_DIM = 128
DN_WIDTH = DN_HEADS * DN_HEAD_DIM
CONV_WIDTH = 4
DN_CHUNK = 64
SWA_Q_HEADS = 8
SWA_KV_HEADS = 2
SWA_HEAD_DIM = 64
SWA_WIDTH = SWA_Q_HEADS * SWA_HEAD_DIM
SWA_KV_WIDTH = SWA_KV_HEADS * SWA_HEAD_DIM
SWA_WINDOW = 128
SWA_BLOCK = 128
N_EXPERTS = 256
N_GROUPS = 8
GROUP_SIZE = N_EXPERTS // N_GROUPS
TOPK_GROUPS = 4
TOP_K = 8
EXPERT_FF = 256
SHARED_FF = 256
ROUTED_SCALE = 2.5
DEEPNORM_ALPHA = 2.0 ** 0.25
LN_EPS = 1e-5
RMS_EPS = 1e-6
L2_EPS = 1e-6

COL_DNQ = 0
COL_DNK = DN_WIDTH
COL_DNV = 2 * DN_WIDTH
COL_Z = 3 * DN_WIDTH
COL_SQ = 4 * DN_WIDTH
COL_SK = COL_SQ + SWA_WIDTH
COL_SV = COL_SK + SWA_KV_WIDTH
MAIN_COLS = COL_SV + SWA_KV_WIDTH
GATE_COLS = 128

TM_PROJ = 512
TS_DN = 512
DN_SEQS = 2
DN_A_UNROLL = 4
TM_POST = 512
TT_ROUTE = 512
BM_EXP = 256
EXP_SLOTS = 6
TT_COMB = 256
SC_NC = 2
SC_NS = 16
SC_NW = SC_NC * SC_NS
SC_WIN = 64
VMEM_LIMIT = 56 * 1024 * 1024
NEG_INF = float("-inf")


def _dot(a, b):
    return jnp.dot(a, b, preferred_element_type=F32)


def _mm(a, b):
    return _dot(a.astype(BF16), b.astype(BF16))


def _mm_nt(a, b):
    return lax.dot_general(a.astype(BF16), b.astype(BF16), (((1,), (1,)), ((), ())),
                           preferred_element_type=F32)


def _mm_tn(a, b):
    return lax.dot_general(a.astype(BF16), b.astype(BF16), (((0,), (0,)), ((), ())),
                           preferred_element_type=F32)


def _split2(a):
    hi = a.astype(BF16)
    lo = (a - hi.astype(F32)).astype(BF16)
    return hi, lo


def _mm3(a, b):
    ah, al = _split2(a)
    bh, bl = _split2(b)
    return _dot(ah, bh) + _dot(ah, bl) + _dot(al, bh)


def _mm_exact_lhs(l_bf16, g):
    g1 = g.astype(BF16)
    r1 = g - g1.astype(F32)
    g2 = r1.astype(BF16)
    g3 = (r1 - g2.astype(F32)).astype(BF16)
    return _dot(l_bf16, g1) + _dot(l_bf16, g2) + _dot(l_bf16, g3)


def _sigmoid(x):
    return 1.0 / (1.0 + jnp.exp(-x))


def _silu(x):
    return x * _sigmoid(x)


def _in_proj_kernel(x_ref, w_ref, wg_ref, main_ref, gates_ref):
    xb = x_ref[...].astype(BF16)
    main_ref[...] = _dot(xb, w_ref[...]).astype(BF16)
    gates_ref[...] = _dot(xb, wg_ref[...])


def _in_proj(x2d, w_main, w_gates):
    n = x2d.shape[0]
    return pl.pallas_call(
        _in_proj_kernel,
        grid=(n // TM_PROJ,),
        in_specs=[
            pl.BlockSpec((TM_PROJ, D_MODEL), lambda i: (i, 0)),
            pl.BlockSpec((D_MODEL, MAIN_COLS), lambda i: (0, 0)),
            pl.BlockSpec((D_MODEL, GATE_COLS), lambda i: (0, 0)),
        ],
        out_specs=[
            pl.BlockSpec((TM_PROJ, MAIN_COLS), lambda i: (i, 0)),
            pl.BlockSpec((TM_PROJ, GATE_COLS), lambda i: (i, 0)),
        ],
        out_shape=[
            jax.ShapeDtypeStruct((n, MAIN_COLS), BF16),
            jax.ShapeDtypeStruct((n, GATE_COLS), F32),
        ],
        compiler_params=pltpu.CompilerParams(
            dimension_semantics=("parallel",), vmem_limit_bytes=VMEM_LIMIT),
        name="in_proj",
    )(x2d, w_main, w_gates)


def _dn_kernel(x_ref, gates_ref, convw_ref, gpar_ref, normg_ref, out_ref,
               xc_ref, gl_ref, gc_ref, wq_ref, u_ref, kt_ref, attn_ref, egl_ref, s_ref, hist_ref):
    nseq = x_ref.shape[0]
    ts = x_ref.shape[1]
    c = DN_CHUNK
    hd = DN_HEAD_DIM
    qkv_w = 3 * DN_WIDTH
    nch = ts // c

    @pl.when(pl.program_id(1) == 0)
    def _():
        s_ref[...] = jnp.zeros_like(s_ref)
        hist_ref[...] = jnp.zeros_like(hist_ref)

    def stage_inputs(bi, carry):
        xc_ref[bi, 0:8, :] = hist_ref[bi]
        xc_ref[bi, 8:ts + 8, :] = x_ref[bi, :, 0:qkv_w].astype(F32)
        hist_ref[bi] = xc_ref[bi, ts:ts + 8, :]
        gsl = gates_ref[bi]
        sp_in = gsl + gpar_ref[1:2, :]
        softplus = jnp.maximum(sp_in, 0.0) + jnp.log(1.0 + jnp.exp(-jnp.abs(sp_in)))
        lane = lax.broadcasted_iota(I32, gsl.shape, 1)
        gl = jnp.where(lane < DN_HEADS, _sigmoid(gsl), -jnp.exp(gpar_ref[0:1, :]) * softplus)
        gl_ref[bi] = gl
        row_in_chunk = lax.broadcasted_iota(I32, gsl.shape, 0) % c
        gc = gl
        shift = 1
        while shift < c:
            gc = gc + jnp.where(row_in_chunk >= shift, pltpu.roll(gc, shift, 0), 0.0)
            shift *= 2
        gc_ref[bi] = gc
        return carry

    lax.fori_loop(0, nseq, stage_inputs, 0)

    ii = lax.broadcasted_iota(I32, (c, c), 0)
    jj = lax.broadcasted_iota(I32, (c, c), 1)
    tri_incl = ii >= jj
    tri_strict = ii > jj
    eye = jnp.where(ii == jj, 1.0, 0.0).astype(F32)
    heads = range(DN_HEADS)

    def conv_silu(bi, r0, col):
        w = convw_ref[:, col:col + hd]
        y = w[CONV_WIDTH - 1:CONV_WIDTH, :] * xc_ref[bi, pl.ds(r0 + 8, c), col:col + hd]
        for j in range(CONV_WIDTH - 1):
            off = 8 - (CONV_WIDTH - 1) + j
            y = y + w[j:j + 1, :] * xc_ref[bi, pl.ds(r0 + off, c), col:col + hd]
        return _silu(y)

    def l2n(t, scale):
        return t * (lax.rsqrt(jnp.sum(t * t, axis=-1, keepdims=True) + L2_EPS) * scale)

    def phase_a(i, bi):
        chains = []
        for sub in range(DN_A_UNROLL):
            ci = i * DN_A_UNROLL + sub
            r0 = ci * c
            glc = gl_ref[bi, pl.ds(r0, c), :]
            gcc = gc_ref[bi, pl.ds(r0, c), :]
            gct = jnp.concatenate([gcc, gcc], axis=0).T
            egl_ref[bi, ci] = jnp.exp(gcc[c - 8:c, :])
            for h in heads:
                chains.append((ci, r0, h, glc, gcc, gct))
        nchain = len(chains)
        q = [l2n(conv_silu(bi, r0, COL_DNQ + h * hd), hd ** -0.5) for (ci, r0, h, _, _, _) in chains]
        k = [l2n(conv_silu(bi, r0, COL_DNK + h * hd), 1.0) for (ci, r0, h, _, _, _) in chains]
        v = [conv_silu(bi, r0, COL_DNV + h * hd) for (ci, r0, h, _, _, _) in chains]
        kb, vb, decay, egc = [], [], [], []
        for n_, (ci, r0, h, glc, gcc, gct) in enumerate(chains):
            beta = glc[:, h:h + 1]
            gc_col = gcc[:, DN_HEADS + h:DN_HEADS + h + 1]
            gc_row = gct[DN_HEADS + h:DN_HEADS + h + 1, 0:c]
            decay.append(jnp.where(tri_incl, jnp.exp(jnp.minimum(gc_col - gc_row, 0.0)), 0.0))
            egc.append(jnp.exp(gc_col))
            e_tail = jnp.exp(gcc[c - 1:c, DN_HEADS + h:DN_HEADS + h + 1] - gc_col)
            kb.append(k[n_] * beta)
            vb.append(v[n_] * beta)
            kt_ref[bi, ci, h] = (k[n_] * e_tail).astype(BF16)
        kq = [_mm_nt(jnp.concatenate([kb[n_], q[n_]], axis=0), k[n_]) for n_ in range(nchain)]
        a_mat = [jnp.where(tri_strict, kq[n_][0:c] * decay[n_], 0.0) for n_ in range(nchain)]
        for n_, (ci, r0, h, _, _, _) in enumerate(chains):
            attn_ref[bi, ci, h] = (kq[n_][c:2 * c] * decay[n_]).astype(BF16)
        t_inv = [eye - a for a in a_mat]
        p = a_mat
        for _ in range(5):
            p = [_mm(x, x) for x in p]
            t_inv = [t + _mm(t, x) for t, x in zip(t_inv, p)]
        for n_, (ci, r0, h, _, _, _) in enumerate(chains):
            uw = _mm3(t_inv[n_], jnp.concatenate([vb[n_], kb[n_] * egc[n_]], axis=1))
            u_ref[bi, ci, h] = uw[:, 0:hd]
            wq_ref[bi, ci, h, 0:c, :] = uw[:, hd:2 * hd].astype(BF16)
            wq_ref[bi, ci, h, c:2 * c, :] = (q[n_] * egc[n_]).astype(BF16)

    for bi in range(nseq):
        for i in range(nch // DN_A_UNROLL):
            phase_a(i, bi)

    normg = normg_ref[...]

    def phase_b(ci, carry):
        r0 = pl.multiple_of(ci * c, c)
        rows = pl.ds(r0, c)
        chains = [(bi, h) for bi in range(nseq) for h in heads]
        egl = [egl_ref[bi, ci] for bi in range(nseq)]
        s_old = [s_ref[bi, h] for bi, h in chains]
        ws = [_dot(wq_ref[bi, ci, h], s.astype(BF16)) for (bi, h), s in zip(chains, s_old)]
        v_new = [(u_ref[bi, ci, h] - w[0:c]).astype(BF16) for (bi, h), w in zip(chains, ws)]
        for n_, (bi, h) in enumerate(chains):
            s_ref[bi, h] = (s_old[n_] * egl[bi][7:8, DN_HEADS + h:DN_HEADS + h + 1]
                            + lax.dot_general(kt_ref[bi, ci, h], v_new[n_], (((0,), (0,)), ((), ())),
                                              preferred_element_type=F32))
        for n_, (bi, h) in enumerate(chains):
            o = ws[n_][c:2 * c] + _dot(attn_ref[bi, ci, h], v_new[n_])
            o = o * lax.rsqrt(jnp.mean(o * o, axis=-1, keepdims=True) + RMS_EPS) * normg
            z = x_ref[bi, rows, COL_Z + h * hd:COL_Z + (h + 1) * hd].astype(F32)
            out_ref[bi, rows, h * hd:(h + 1) * hd] = (o * _silu(z)).astype(out_ref.dtype)
        return carry

    lax.fori_loop(0, nch, phase_b, 0)


def _deltanet(main3d, gates3d, conv_w, gpar, normg):
    b, t, _ = main3d.shape
    ts = TS_DN
    nseq = DN_SEQS
    nch = ts // DN_CHUNK
    dn_in = COL_Z + DN_WIDTH
    return pl.pallas_call(
        _dn_kernel,
        grid=(b // nseq, t // ts),
        in_specs=[
            pl.BlockSpec((nseq, ts, dn_in), lambda bi, si: (bi, si, 0)),
            pl.BlockSpec((nseq, ts, GATE_COLS), lambda bi, si: (bi, si, 0)),
            pl.BlockSpec((CONV_WIDTH, 3 * DN_WIDTH), lambda bi, si: (0, 0)),
            pl.BlockSpec((2, GATE_COLS), lambda bi, si: (0, 0)),
            pl.BlockSpec((1, DN_HEAD_DIM), lambda bi, si: (0, 0)),
        ],
        out_specs=pl.BlockSpec((nseq, ts, DN_WIDTH), lambda bi, si: (bi, si, 0)),
        out_shape=jax.ShapeDtypeStruct((b, t, DN_WIDTH), BF16),
        scratch_shapes=[
            pltpu.VMEM((nseq, ts + 8, 3 * DN_WIDTH), F32),
            pltpu.VMEM((nseq, ts, GATE_COLS), F32),
            pltpu.VMEM((nseq, ts, GATE_COLS), F32),
            pltpu.VMEM((nseq, nch, DN_HEADS, 2 * DN_CHUNK, DN_HEAD_DIM), BF16),
            pltpu.VMEM((nseq, nch, DN_HEADS, DN_CHUNK, DN_HEAD_DIM), F32),
            pltpu.VMEM((nseq, nch, DN_HEADS, DN_CHUNK, DN_HEAD_DIM), BF16),
            pltpu.VMEM((nseq, nch, DN_HEADS, DN_CHUNK, DN_CHUNK), BF16),
            pltpu.VMEM((nseq, nch, 8, GATE_COLS), F32),
            pltpu.VMEM((nseq, DN_HEADS, DN_HEAD_DIM, DN_HEAD_DIM), F32),
            pltpu.VMEM((nseq, 8, 3 * DN_WIDTH), F32),
        ],
        compiler_params=pltpu.CompilerParams(
            dimension_semantics=("parallel", "arbitrary"), vmem_limit_bytes=VMEM_LIMIT),
        name="deltanet",
    )(main3d, gates3d, conv_w, gpar, normg)


def _swa_kernel(sinks_ref, q_ref, kp_ref, kc_ref, vp_ref, vc_ref, out_ref, bias_ref):
    n = pl.program_id(1)
    blk = SWA_BLOCK
    d = SWA_HEAD_DIM
    grp = SWA_Q_HEADS // SWA_KV_HEADS

    @pl.when((pl.program_id(0) == 0) & (n == 0))
    def _():
        qi = lax.broadcasted_iota(I32, (blk, 2 * blk), 0)
        kj = lax.broadcasted_iota(I32, (blk, 2 * blk), 1)
        dist = qi + blk - kj
        valid = (dist >= 0) & (dist < SWA_WINDOW)
        dist_f = dist.astype(F32)
        for hq in range(SWA_Q_HEADS):
            slope = 2.0 ** (-8.0 * (hq + 1.0) / SWA_Q_HEADS)
            bias_ref[hq] = jnp.where(valid, -slope * dist_f, NEG_INF)

    kcol = lax.broadcasted_iota(I32, (1, 2 * blk), 1)
    colmask = jnp.where((kcol >= blk) | (n > 0), 0.0, NEG_INF)
    q_all = q_ref[...] * (d ** -0.5)
    kband = [jnp.concatenate([kp_ref[:, hk * d:(hk + 1) * d], kc_ref[:, hk * d:(hk + 1) * d]], axis=0)
             for hk in range(SWA_KV_HEADS)]
    vband = [jnp.concatenate([vp_ref[:, hk * d:(hk + 1) * d], vc_ref[:, hk * d:(hk + 1) * d]], axis=0)
             for hk in range(SWA_KV_HEADS)]
    heads = range(SWA_Q_HEADS)
    scores = [_mm_nt(q_all[:, hq * d:(hq + 1) * d], kband[hq // grp]) for hq in heads]
    probs, denoms = [], []
    for hq in heads:
        s = scores[hq] + bias_ref[hq] + colmask
        sink = sinks_ref[hq]
        m = jnp.maximum(jnp.max(s, axis=-1, keepdims=True), sink)
        p = jnp.exp(s - m)
        denoms.append(jnp.sum(p, axis=-1, keepdims=True) + jnp.exp(sink - m))
        probs.append(p.astype(BF16))
    outs = [_dot(probs[hq], vband[hq // grp]) / denoms[hq] for hq in heads]
    out_ref[...] = jnp.concatenate(outs, axis=-1).astype(out_ref.dtype)


def _swa(main3d, sinks):
    b, t, _ = main3d.shape
    blk = SWA_BLOCK
    qb = COL_SQ // SWA_WIDTH
    kb = COL_SK // SWA_KV_WIDTH
    vb = COL_SV // SWA_KV_WIDTH
    grid_spec = pltpu.PrefetchScalarGridSpec(
        num_scalar_prefetch=1,
        grid=(b, t // blk),
        in_specs=[
            pl.BlockSpec((None, blk, SWA_WIDTH), lambda bi, ni, s: (bi, ni, qb)),
            pl.BlockSpec((None, blk, SWA_KV_WIDTH), lambda bi, ni, s: (bi, jnp.maximum(ni - 1, 0), kb)),
            pl.BlockSpec((None, blk, SWA_KV_WIDTH), lambda bi, ni, s: (bi, ni, kb)),
            pl.BlockSpec((None, blk, SWA_KV_WIDTH), lambda bi, ni, s: (bi, jnp.maximum(ni - 1, 0), vb)),
            pl.BlockSpec((None, blk, SWA_KV_WIDTH), lambda bi, ni, s: (bi, ni, vb)),
        ],
        out_specs=pl.BlockSpec((None, blk, SWA_WIDTH), lambda bi, ni, s: (bi, ni, 0)),
        scratch_shapes=[pltpu.VMEM((SWA_Q_HEADS, blk, 2 * blk), F32)],
    )
    return pl.pallas_call(
        _swa_kernel,
        grid_spec=grid_spec,
        out_shape=jax.ShapeDtypeStruct((b, t, SWA_WIDTH), BF16),
        compiler_params=pltpu.CompilerParams(
            dimension_semantics=("arbitrary", "arbitrary"), vmem_limit_bytes=VMEM_LIMIT),
        name="swa",
    )(sinks, main3d, main3d, main3d, main3d, main3d)


def _layer_norm(y, g, b):
    mu = jnp.mean(y, axis=-1, keepdims=True)
    yc = y - mu
    var = jnp.mean(yc * yc, axis=-1, keepdims=True)
    return yc * lax.rsqrt(var + LN_EPS) * g + b


def _pack_bf16_pair(lo, hi):
    lo_bits = lax.bitcast_convert_type(lo.astype(BF16).astype(F32), U32)
    hi_bits = lax.bitcast_convert_type(hi.astype(BF16).astype(F32), U32)
    return (hi_bits & jnp.uint32(0xFFFF0000)) | (lo_bits >> 16)


def _unpack_bf16_pair(packed):
    lo = lax.bitcast_convert_type(packed << 16, F32)
    hi = lax.bitcast_convert_type(packed & jnp.uint32(0xFFFF0000), F32)
    return lo, hi


def _post_mix_kernel(x_ref, dn_ref, swa_ref, wo_dn_ref, wo_swa_ref, g_ref, b_ref,
                     rwh_ref, rwl_ref, sg_ref, su_ref, sd_ref,
                     base_ref, xpk_ref, logit_ref):
    mix = _dot(dn_ref[...], wo_dn_ref[...]) + _dot(swa_ref[...], wo_swa_ref[...])
    x1 = _layer_norm(DEEPNORM_ALPHA * x_ref[...] + mix, g_ref[...], b_ref[...])
    half = D_MODEL // 2
    xpk_ref[...] = _pack_bf16_pair(x1[:, :half], x1[:, half:])
    xh, xl = _split2(x1)
    nt = (((1,), (1,)), ((), ()))
    logit_ref[...] = (lax.dot_general(rwh_ref[...], xh, nt, preferred_element_type=F32)
                      + lax.dot_general(rwh_ref[...], xl, nt, preferred_element_type=F32)
                      + lax.dot_general(rwl_ref[...], xh, nt, preferred_element_type=F32))
    hmid = _silu(_dot(xh, sg_ref[...])) * _dot(xh, su_ref[...])
    base_ref[...] = DEEPNORM_ALPHA * x1 + _dot(hmid.astype(BF16), sd_ref[...])


def _post_mix(x2d, dn2d, swa2d, wo_dn, wo_swa, ln_g, ln_b, rwh, rwl, sg, su, sd):
    n = x2d.shape[0]
    tm = TM_POST
    full = lambda shape: pl.BlockSpec(shape, lambda i: (0, 0))
    return pl.pallas_call(
        _post_mix_kernel,
        grid=(n // tm,),
        in_specs=[
            pl.BlockSpec((tm, D_MODEL), lambda i: (i, 0)),
            pl.BlockSpec((tm, DN_WIDTH), lambda i: (i, 0)),
            pl.BlockSpec((tm, SWA_WIDTH), lambda i: (i, 0)),
            full((DN_WIDTH, D_MODEL)), full((SWA_WIDTH, D_MODEL)),
            full((1, D_MODEL)), full((1, D_MODEL)),
            full((N_EXPERTS, D_MODEL)), full((N_EXPERTS, D_MODEL)),
            full((D_MODEL, SHARED_FF)), full((D_MODEL, SHARED_FF)), full((SHARED_FF, D_MODEL)),
        ],
        out_specs=[
            pl.BlockSpec((tm, D_MODEL), lambda i: (i, 0)),
            pl.BlockSpec((tm, D_MODEL // 2), lambda i: (i, 0)),
            pl.BlockSpec((N_EXPERTS, tm), lambda i: (0, i)),
        ],
        out_shape=[
            jax.ShapeDtypeStruct((n, D_MODEL), F32),
            jax.ShapeDtypeStruct((n, D_MODEL // 2), U32),
            jax.ShapeDtypeStruct((N_EXPERTS, n), F32),
        ],
        compiler_params=pltpu.CompilerParams(
            dimension_semantics=("parallel",), vmem_limit_bytes=VMEM_LIMIT),
        name="post_mix",
    )(x2d, dn2d, swa2d, wo_dn, wo_swa, ln_g, ln_b, rwh, rwl, sg, su, sd)


def _route_kernel(lg_ref, bias_ref, eidx_ref, gate_ref, rank_ref, cnt_ref, carry_ref):
    @pl.when(pl.program_id(0) == 0)
    def _():
        carry_ref[...] = jnp.zeros_like(carry_ref)

    tt = lg_ref.shape[1]
    scores = _sigmoid(lg_ref[...])
    sel = scores + bias_ref[...]

    iog = lax.broadcasted_iota(I32, (GROUP_SIZE, tt), 0)
    grp_rows = []
    for g in range(N_GROUPS):
        blk = sel[g * GROUP_SIZE:(g + 1) * GROUP_SIZE, :]
        m1 = jnp.max(blk, axis=0, keepdims=True)
        i1 = jnp.min(jnp.where(blk == m1, iog, GROUP_SIZE), axis=0, keepdims=True)
        m2 = jnp.max(jnp.where(iog == i1, NEG_INF, blk), axis=0, keepdims=True)
        grp_rows.append(m1 + m2)
    gs = jnp.concatenate(grp_rows, axis=0)

    io8 = lax.broadcasted_iota(I32, (N_GROUPS, tt), 0)
    gsel = jnp.zeros((N_GROUPS, tt), F32)
    for _ in range(TOPK_GROUPS):
        mg = jnp.max(gs, axis=0, keepdims=True)
        ig = jnp.min(jnp.where(gs == mg, io8, N_GROUPS), axis=0, keepdims=True)
        hit = io8 == ig
        gsel = jnp.where(hit, 1.0, gsel)
        gs = jnp.where(hit, NEG_INF, gs)

    val = jnp.concatenate(
        [jnp.where(gsel[g:g + 1, :] > 0.0, sel[g * GROUP_SIZE:(g + 1) * GROUP_SIZE, :], NEG_INF)
         for g in range(N_GROUPS)], axis=0)

    ioe = lax.broadcasted_iota(I32, (N_EXPERTS, tt), 0)
    onehot = jnp.zeros((N_EXPERTS, tt), F32)
    idx_rows, gate_rows = [], []
    for _ in range(TOP_K):
        m = jnp.max(val, axis=0, keepdims=True)
        ik = jnp.min(jnp.where(val == m, ioe, N_EXPERTS), axis=0, keepdims=True)
        hit = ioe == ik
        gate_rows.append(jnp.sum(jnp.where(hit, scores, 0.0), axis=0, keepdims=True))
        idx_rows.append(ik)
        val = jnp.where(hit, NEG_INF, val)
        onehot = jnp.where(hit, 1.0, onehot)
    gsum = gate_rows[0]
    for r in gate_rows[1:]:
        gsum = gsum + r
    gate_ref[...] = jnp.concatenate(gate_rows, axis=0) / gsum * ROUTED_SCALE
    eidx_ref[...] = jnp.concatenate(idx_rows, axis=0)

    ti = lax.broadcasted_iota(I32, (tt, tt), 0)
    tj = lax.broadcasted_iota(I32, (tt, tt), 1)
    upper = jnp.where(ti < tj, 1.0, 0.0).astype(BF16)
    cum = _dot(onehot.astype(BF16), upper) + jnp.broadcast_to(carry_ref[:, 0:1], (N_EXPERTS, tt))
    rank_rows = [jnp.sum(jnp.where(ioe == ik, cum, 0.0), axis=0, keepdims=True) for ik in idx_rows]
    rank_ref[...] = jnp.concatenate(rank_rows, axis=0).astype(I32)
    carry_ref[...] = carry_ref[...] + jnp.broadcast_to(
        jnp.sum(onehot, axis=1, keepdims=True), carry_ref.shape)
    cnt_ref[...] = carry_ref[...].astype(I32)


def _route(logits_t, bias_col):
    n = logits_t.shape[1]
    tt = TT_ROUTE
    row_spec = pl.BlockSpec((TOP_K, tt), lambda i: (0, i))
    return pl.pallas_call(
        _route_kernel,
        grid=(n // tt,),
        in_specs=[
            pl.BlockSpec((N_EXPERTS, tt), lambda i: (0, i)),
            pl.BlockSpec((N_EXPERTS, 1), lambda i: (0, 0)),
        ],
        out_specs=[row_spec, row_spec, row_spec,
                   pl.BlockSpec((N_EXPERTS, 128), lambda i: (0, 0))],
        out_shape=[
            jax.ShapeDtypeStruct((TOP_K, n), I32),
            jax.ShapeDtypeStruct((TOP_K, n), F32),
            jax.ShapeDtypeStruct((TOP_K, n), I32),
            jax.ShapeDtypeStruct((N_EXPERTS, 128), I32),
        ],
        scratch_shapes=[pltpu.VMEM((N_EXPERTS, 128), F32)],
        compiler_params=pltpu.CompilerParams(
            dimension_semantics=("arbitrary",), vmem_limit_bytes=VMEM_LIMIT),
        name="route",
    )(logits_t, bias_col)


def _place_kernel(eidx_ref, rank_ref, pstart_ref, dest_ref):
    tt = eidx_ref.shape[1]
    ioe = lax.broadcasted_iota(I32, (N_EXPERTS, tt), 0)
    pstart = pstart_ref[...]
    rows = [jnp.sum(jnp.where(ioe == eidx_ref[k:k + 1, :], pstart, 0.0), axis=0, keepdims=True)
            for k in range(TOP_K)]
    dest_ref[...] = jnp.concatenate(rows, axis=0).astype(I32) + rank_ref[...]


def _place(eidx, rank, pstart_col):
    n = eidx.shape[1]
    tt = TT_ROUTE
    row_spec = pl.BlockSpec((TOP_K, tt), lambda i: (0, i))
    return pl.pallas_call(
        _place_kernel,
        grid=(n // tt,),
        in_specs=[row_spec, row_spec, pl.BlockSpec((N_EXPERTS, 1), lambda i: (0, 0))],
        out_specs=row_spec,
        out_shape=jax.ShapeDtypeStruct((TOP_K, n), I32),
        compiler_params=pltpu.CompilerParams(
            dimension_semantics=("parallel",), vmem_limit_bytes=VMEM_LIMIT),
        name="place",
    )(eidx, rank, pstart_col)


def _sc_mesh():
    return plsc.VectorSubcoreMesh(core_axis_name="c", subcore_axis_name="s",
                                  num_cores=SC_NC, num_subcores=SC_NS)


def _sc_scatter_rows(rows, idx, nrows_out):
    n, d = rows.shape
    nk = idx.shape[0]
    per_w = n // SC_NW
    nwin = per_w // SC_WIN
    assert per_w * SC_NW == n and nwin * SC_WIN == per_w and nwin % 2 == 0

    @functools.partial(
        pl.kernel, mesh=_sc_mesh(),
        out_type=jax.ShapeDtypeStruct((nrows_out, d), rows.dtype),
        scratch_types=[
            pltpu.VMEM((nwin, nk, SC_WIN), I32),
            pltpu.VMEM((2, SC_WIN, d), rows.dtype),
            pltpu.SemaphoreType.DMA((2,)),
            pltpu.SemaphoreType.DMA((2,)),
        ],
        compiler_params=pltpu.CompilerParams(use_tc_tiling_on_sc=True),
        name="sc_scatter_rows",
    )
    def scatter_kernel(rows_hbm, idx_hbm, out_hbm, idx_v, rows_v, lsem, ssem):
        wid = lax.axis_index("s") * SC_NC + lax.axis_index("c")
        base = wid * per_w
        pltpu.sync_copy(idx_hbm.at[wid], idx_v)

        def load(w, slot):
            return pltpu.make_async_copy(
                rows_hbm.at[pl.ds(base + w * SC_WIN, SC_WIN)], rows_v.at[slot], lsem.at[slot])

        def scat(w, k, slot):
            return pltpu.make_async_copy(rows_v.at[slot], out_hbm.at[idx_v.at[w, k]], ssem.at[slot])

        load(0, 0).start()

        @pl.loop(0, nwin, step=2)
        def _(w0):
            for slot in range(2):
                w = w0 + slot
                load(w, slot).wait()

                @pl.when(w + 1 < nwin)
                def _():
                    @pl.when(w >= 1)
                    def _():
                        for k in range(nk):
                            scat(w - 1, k, 1 - slot).wait()
                    load(w + 1, 1 - slot).start()

                for k in range(nk):
                    scat(w, k, slot).start()

        for k in range(nk):
            scat(nwin - 2, k, 0).wait()
        for k in range(nk):
            scat(nwin - 1, k, 1).wait()

    idx4 = idx.reshape(nk, SC_NW, nwin, SC_WIN).transpose(1, 2, 0, 3)
    return scatter_kernel(rows, idx4)


def _expert_kernel(gstart_ref, cnt_ref, xs_hbm, wg_ref, wu_ref, wd_ref, y_hbm,
                   wgb_ref, wub_ref, wdb_ref, xbuf_ref, ybuf_ref, xsem, ysem):
    e = pl.program_id(0)
    ne = pl.num_programs(0)
    bm = xbuf_ref.shape[1]
    nblk = y_hbm.shape[0] // bm
    half = D_MODEL // 2
    g_lo = gstart_ref[e]
    g_hi = gstart_ref[e + 1]
    g_end = gstart_ref[ne]

    def x_copy(g, slot):
        return pltpu.make_async_copy(xs_hbm.at[pl.ds(g * bm, bm), :], xbuf_ref.at[slot], xsem.at[slot])

    def y_copy(g, slot):
        return pltpu.make_async_copy(ybuf_ref.at[slot], y_hbm.at[pl.ds(g * bm, bm), :], ysem.at[slot])

    nslot = xbuf_ref.shape[0]

    ahead = nslot - 2

    @pl.when(e == 0)
    def _():
        for g0 in range(ahead):
            @pl.when(g0 < g_end)
            def _():
                x_copy(g0, g0).start()

    @pl.when(g_hi > g_lo)
    def _():
        wgb_ref[...] = wg_ref[...].astype(BF16)
        wub_ref[...] = wu_ref[...].astype(BF16)
        wdb_ref[...] = wd_ref[...].astype(BF16)

    row = lax.broadcasted_iota(I32, (bm, half), 0)

    def acquire(g):
        x_copy(g, g % nslot).wait()

        @pl.when(g + ahead < g_end)
        def _():
            x_copy(g + ahead, (g + ahead) % nslot).start()

        @pl.when(g >= nslot)
        def _():
            y_copy(g - nslot, g % nslot).wait()

    def load(g):
        n_valid = cnt_ref[e] - (g - g_lo) * bm
        x_lo, x_hi = _unpack_bf16_pair(jnp.where(row < n_valid, xbuf_ref[g % nslot], jnp.uint32(0)))
        return x_lo.astype(BF16), x_hi.astype(BF16)

    def gate_up(x):
        x_lo, x_hi = x
        gate = _dot(x_lo, wgb_ref[:half, :]) + _dot(x_hi, wgb_ref[half:, :])
        up = _dot(x_lo, wub_ref[:half, :]) + _dot(x_hi, wub_ref[half:, :])
        return gate, up

    def down(gu):
        gate, up = gu
        return _dot((_silu(gate) * up).astype(BF16), wdb_ref[...])

    def store(g, y):
        ybuf_ref[g % nslot] = _pack_bf16_pair(y[:, :half], y[:, half:])
        y_copy(g, g % nslot).start()

    def pair(p, carry):
        g = g_lo + 2 * p
        acquire(g)
        acquire(g + 1)
        gu_a = gate_up(load(g))
        gu_b = gate_up(load(g + 1))
        y_a = down(gu_a)
        y_b = down(gu_b)
        store(g, y_a)
        store(g + 1, y_b)
        return carry

    n_own = g_hi - g_lo
    lax.fori_loop(0, n_own // 2, pair, 0)

    @pl.when(n_own % 2 == 1)
    def _():
        acquire(g_hi - 1)
        store(g_hi - 1, down(gate_up(load(g_hi - 1))))

    @pl.when(e == ne - 1)
    def _():
        for back in range(nslot, 0, -1):
            @pl.when(g_end >= back)
            def _():
                y_copy(g_end - back, (g_end - back) % nslot).wait()

        ybuf_ref[0] = jnp.zeros((bm, half), U32)

        def fill(g, carry):
            y_copy(g, 0).start()
            return carry

        def drain(g, carry):
            y_copy(g, 0).wait()
            return carry

        lax.fori_loop(g_end, nblk, fill, 0)
        lax.fori_loop(g_end, nblk, drain, 0)


def _experts(gstart, counts, xs, w_gate, w_up, w_down):
    bm = BM_EXP
    nblk = xs.shape[0] // bm
    half = D_MODEL // 2
    grid_spec = pltpu.PrefetchScalarGridSpec(
        num_scalar_prefetch=2,
        grid=(N_EXPERTS,),
        in_specs=[
            pl.BlockSpec(memory_space=pl.ANY),
            pl.BlockSpec((None, D_MODEL, EXPERT_FF), lambda e, gs, cn: (e, 0, 0)),
            pl.BlockSpec((None, D_MODEL, EXPERT_FF), lambda e, gs, cn: (e, 0, 0)),
            pl.BlockSpec((None, EXPERT_FF, D_MODEL), lambda e, gs, cn: (e, 0, 0)),
        ],
        out_specs=pl.BlockSpec(memory_space=pl.ANY),
        scratch_shapes=[
            pltpu.VMEM((D_MODEL, EXPERT_FF), BF16),
            pltpu.VMEM((D_MODEL, EXPERT_FF), BF16),
            pltpu.VMEM((EXPERT_FF, D_MODEL), BF16),
            pltpu.VMEM((EXP_SLOTS, bm, half), U32),
            pltpu.VMEM((EXP_SLOTS, bm, half), U32),
            pltpu.SemaphoreType.DMA((EXP_SLOTS,)),
            pltpu.SemaphoreType.DMA((EXP_SLOTS,)),
        ],
    )
    return pl.pallas_call(
        _expert_kernel,
        grid_spec=grid_spec,
        out_shape=jax.ShapeDtypeStruct((nblk * bm, half), U32),
        compiler_params=pltpu.CompilerParams(
            dimension_semantics=("arbitrary",), vmem_limit_bytes=VMEM_LIMIT),
        name="experts",
    )(gstart, counts, xs, w_gate, w_up, w_down)


def _sc_gather_rows(table, idx):
    nrows = idx.shape[0]
    d = table.shape[1]
    per_w = nrows // SC_NW
    nwin = per_w // SC_WIN
    assert per_w * SC_NW == nrows and nwin * SC_WIN == per_w and nwin % 2 == 0
    @functools.partial(
        pl.kernel, mesh=_sc_mesh(),
        out_type=jax.ShapeDtypeStruct((nrows, d), table.dtype),
        scratch_types=[
            pltpu.VMEM((nwin, SC_WIN), I32),
            pltpu.VMEM((2, SC_WIN, d), table.dtype),
            pltpu.SemaphoreType.DMA((2,)),
            pltpu.SemaphoreType.DMA((2,)),
        ],
        compiler_params=pltpu.CompilerParams(use_tc_tiling_on_sc=True),
        name="sc_gather_rows",
    )
    def gather_kernel(table_hbm, idx_hbm, out_hbm, idx_v, rows_v, gsem, wsem):
        wid = lax.axis_index("s") * SC_NC + lax.axis_index("c")
        base = wid * per_w
        pltpu.sync_copy(idx_hbm.at[wid], idx_v)

        def gather(w, slot):
            return pltpu.make_async_copy(table_hbm.at[idx_v.at[w]], rows_v.at[slot], gsem.at[slot])

        def put(w, slot):
            return pltpu.make_async_copy(
                rows_v.at[slot], out_hbm.at[pl.ds(base + w * SC_WIN, SC_WIN)], wsem.at[slot])

        gather(0, 0).start()

        @pl.loop(0, nwin, step=2)
        def _(w0):
            for slot in range(2):
                w = w0 + slot
                gather(w, slot).wait()

                @pl.when(w + 1 < nwin)
                def _():
                    @pl.when(w >= 1)
                    def _():
                        put(w - 1, 1 - slot).wait()
                    gather(w + 1, 1 - slot).start()

                put(w, slot).start()

        put(nwin - 2, 0).wait()
        put(nwin - 1, 1).wait()

    return gather_kernel(table, idx.reshape(SC_NW, nwin, SC_WIN))


def _combine_kernel(y_ref, base_ref, gate_ref, g_ref, b_ref, out_ref):
    half = D_MODEL // 2
    gates = gate_ref[...]
    acc_lo = base_ref[:, :half]
    acc_hi = base_ref[:, half:]
    for k in range(TOP_K):
        y_lo, y_hi = _unpack_bf16_pair(y_ref[k])
        gk = gates[:, k:k + 1]
        acc_lo = acc_lo + gk * y_lo
        acc_hi = acc_hi + gk * y_hi
    mu = (jnp.sum(acc_lo, axis=-1, keepdims=True) + jnp.sum(acc_hi, axis=-1, keepdims=True)) / D_MODEL
    c_lo = acc_lo - mu
    c_hi = acc_hi - mu
    var = (jnp.sum(c_lo * c_lo, axis=-1, keepdims=True)
           + jnp.sum(c_hi * c_hi, axis=-1, keepdims=True)) / D_MODEL
    inv = lax.rsqrt(var + LN_EPS)
    out_ref[:, :half] = c_lo * inv * g_ref[:, :half] + b_ref[:, :half]
    out_ref[:, half:] = c_hi * inv * g_ref[:, half:] + b_ref[:, half:]


def _combine(ybuf, base, gate_tok, ln_g, ln_b):
    n = base.shape[0]
    tt = TT_COMB
    half = D_MODEL // 2
    return pl.pallas_call(
        _combine_kernel,
        grid=(n // tt,),
        in_specs=[
            pl.BlockSpec((TOP_K, tt, half), lambda i: (0, i, 0)),
            pl.BlockSpec((tt, D_MODEL), lambda i: (i, 0)),
            pl.BlockSpec((tt, TOP_K), lambda i: (i, 0)),
            pl.BlockSpec((1, D_MODEL), lambda i: (0, 0)),
            pl.BlockSpec((1, D_MODEL), lambda i: (0, 0)),
        ],
        out_specs=pl.BlockSpec((tt, D_MODEL), lambda i: (i, 0)),
        out_shape=jax.ShapeDtypeStruct((n, D_MODEL), F32),
        compiler_params=pltpu.CompilerParams(
            dimension_semantics=("parallel",), vmem_limit_bytes=VMEM_LIMIT),
        name="combine",
    )(ybuf, base, gate_tok, ln_g, ln_b)


def _regroup_w_in(w_in):
    o = 0
    cols = {}
    for name, width in (("dnq", DN_WIDTH), ("dnk", DN_WIDTH), ("dnv", DN_WIDTH), ("sq", SWA_WIDTH),
                        ("sk", SWA_KV_WIDTH), ("sv", SWA_KV_WIDTH), ("z", DN_WIDTH),
                        ("b", DN_HEADS), ("a", DN_HEADS)):
        cols[name] = w_in[:, o:o + width]
        o += width
    w_main = jnp.concatenate([cols[k] for k in ("dnq", "dnk", "dnv", "z", "sq", "sk", "sv")], axis=1)
    w_gates = jnp.concatenate(
        [cols["b"], cols["a"], jnp.zeros((D_MODEL, GATE_COLS - 2 * DN_HEADS), w_in.dtype)], axis=1)
    return w_main.astype(BF16), w_gates.astype(BF16)


def _layer(x, w_in, conv_w, a_log, dt_bias, dn_norm_g, sinks, w_out, ln1_g, ln1_b,
           router_w, router_bias, w_gate, w_up, w_down, sh_gate, sh_up, sh_down, ln2_g, ln2_b):
    b, t, d = x.shape
    n = b * t
    x2d = x.reshape(n, d)

    w_main, w_gates = _regroup_w_in(w_in)
    main, gates = _in_proj(x2d, w_main, w_gates)
    main3d = main.reshape(b, t, MAIN_COLS)

    pad = jnp.zeros((GATE_COLS - 2 * DN_HEADS,), F32)
    gpar = jnp.stack([jnp.concatenate([jnp.zeros((DN_HEADS,), F32), a_log.astype(F32), pad]),
                      jnp.concatenate([jnp.zeros((DN_HEADS,), F32), dt_bias.astype(F32), pad])])
    dn_out = _deltanet(main3d, gates.reshape(b, t, GATE_COLS), conv_w.astype(F32), gpar,
                       dn_norm_g.astype(F32).reshape(1, DN_HEAD_DIM))
    swa_out = _swa(main3d, sinks.astype(F32))

    rw_t = router_w.T.astype(F32)
    rwh = rw_t.astype(BF16)
    rwl = (rw_t - rwh.astype(F32)).astype(BF16)
    base, xpk, logits_t = _post_mix(
        x2d, dn_out.reshape(n, DN_WIDTH), swa_out.reshape(n, SWA_WIDTH),
        w_out[:DN_WIDTH].astype(BF16), w_out[DN_WIDTH:].astype(BF16),
        ln1_g.reshape(1, d).astype(F32), ln1_b.reshape(1, d).astype(F32), rwh, rwl,
        sh_gate.astype(BF16), sh_up.astype(BF16), sh_down.astype(BF16))

    eidx, gate, rank, cnt = _route(logits_t, router_bias.astype(F32).reshape(N_EXPERTS, 1))

    bm = BM_EXP
    counts = cnt[:, 0]
    padded = (counts + bm - 1) // bm * bm
    pend = jnp.cumsum(padded)
    pstart = pend - padded
    nblk = -(-(n * TOP_K) // bm) + N_EXPERTS
    gstart = (jnp.concatenate([pstart, pend[-1:]]) // bm).astype(I32)

    dest = _place(eidx, rank, pstart.astype(F32).reshape(N_EXPERTS, 1))
    xs = _sc_scatter_rows(xpk, dest, nblk * bm)
    ypk = _experts(gstart, counts, xs, w_gate, w_up, w_down)
    ybuf = _sc_gather_rows(ypk, dest.reshape(-1)).reshape(TOP_K, n, d // 2)
    out = _combine(ybuf, base, gate.T, ln2_g.reshape(1, d).astype(F32), ln2_b.reshape(1, d).astype(F32))
    return out.reshape(b, t, d)


def kernel(x, w_in, conv_w, a_log, dt_bias, dn_norm_g, sinks, w_out, ln1_g, ln1_b, router_w, router_bias,
           w_gate, w_up, w_down, shared_w_gate, shared_w_up, shared_w_down, ln2_g, ln2_b):
    depth = w_in.shape[0]
    for l in range(depth):
        x = _layer(x, w_in[l], conv_w[l], a_log[l], dt_bias[l], dn_norm_g[l], sinks[l], w_out[l],
                   ln1_g[l], ln1_b[l], router_w[l], router_bias[l], w_gate[l], w_up[l], w_down[l],
                   shared_w_gate[l], shared_w_up[l], shared_w_down[l], ln2_g[l], ln2_b[l])
    return x
```

```python
import functools

import jax
import jax.numpy as jnp
from jax import lax
from jax.experimental import pallas as pl
from jax.experimental.pallas import tpu as pltpu
from jax.experimental.pallas import tpu_sc as plsc

F32 = jnp.float32
BF16 = jnp.bfloat16
I32 = jnp.int32
U32 = jnp.uint32

D_MODEL = 1024
DN_HEADS = 4
DN_HEAD_DIM = 128
DN_WIDTH = DN_HEADS * DN_HEAD_DIM
CONV_WIDTH = 4
DN_CHUNK = 64
SWA_Q_HEADS = 8
SWA_KV_HEADS = 2
SWA_HEAD_DIM = 64
SWA_WIDTH = SWA_Q_HEADS * SWA_HEAD_DIM
SWA_KV_WIDTH = SWA_KV_HEADS * SWA_HEAD_DIM
SWA_WINDOW = 128
SWA_BLOCK = 128
N_EXPERTS = 256
N_GROUPS = 8
GROUP_SIZE = N_EXPERTS // N_GROUPS
TOPK_GROUPS = 4
TOP_K = 8
EXPERT_FF = 256
SHARED_FF = 256
ROUTED_SCALE = 2.5
DEEPNORM_ALPHA = 2.0 ** 0.25
LN_EPS = 1e-5
RMS_EPS = 1e-6
L2_EPS = 1e-6

COL_DNQ = 0
COL_DNK = DN_WIDTH
COL_DNV = 2 * DN_WIDTH
COL_Z = 3 * DN_WIDTH
COL_SQ = 4 * DN_WIDTH
COL_SK = COL_SQ + SWA_WIDTH
COL_SV = COL_SK + SWA_KV_WIDTH
MAIN_COLS = COL_SV + SWA_KV_WIDTH
GATE_COLS = 128

TM_PROJ = 512
TS_DN = 256
DN_SEQS = 4
DN_A_UNROLL = 4
TM_POST = 512
TT_ROUTE = 512
BM_EXP = 256
EXP_SLOTS = 6
TT_COMB = 512
SC_NC = 2
SC_NS = 16
SC_NW = SC_NC * SC_NS
SC_WIN = 64
VMEM_LIMIT = 56 * 1024 * 1024
NEG_INF = float("-inf")


def _dot(a, b):
    return jnp.dot(a, b, preferred_element_type=F32)


def _mm(a, b):
    return _dot(a.astype(BF16), b.astype(BF16))


def _mm_nt(a, b):
    return lax.dot_general(a.astype(BF16), b.astype(BF16), (((1,), (1,)), ((), ())),
                           preferred_element_type=F32)


def _mm_tn(a, b):
    return lax.dot_general(a.astype(BF16), b.astype(BF16), (((0,), (0,)), ((), ())),
                           preferred_element_type=F32)


def _split2(a):
    hi = a.astype(BF16)
    lo = (a - hi.astype(F32)).astype(BF16)
    return hi, lo


def _mm3(a, b):
    ah, al = _split2(a)
    bh, bl = _split2(b)
    return _dot(ah, bh) + _dot(ah, bl) + _dot(al, bh)


def _mm_exact_lhs(l_bf16, g):
    g1 = g.astype(BF16)
    r1 = g - g1.astype(F32)
    g2 = r1.astype(BF16)
    g3 = (r1 - g2.astype(F32)).astype(BF16)
    return _dot(l_bf16, g1) + _dot(l_bf16, g2) + _dot(l_bf16, g3)


def _sigmoid(x):
    return 1.0 / (1.0 + jnp.exp(-x))


def _silu(x):
    return x * _sigmoid(x)


def _in_proj_kernel(x_ref, w_ref, wg_ref, main_ref, gates_ref):
    xb = x_ref[...].astype(BF16)
    main_ref[...] = _dot(xb, w_ref[...]).astype(BF16)
    gates_ref[...] = _dot(xb, wg_ref[...])


def _in_proj(x2d, w_main, w_gates):
    n = x2d.shape[0]
    return pl.pallas_call(
        _in_proj_kernel,
        grid=(n // TM_PROJ,),
        in_specs=[
            pl.BlockSpec((TM_PROJ, D_MODEL), lambda i: (i, 0)),
            pl.BlockSpec((D_MODEL, MAIN_COLS), lambda i: (0, 0)),
            pl.BlockSpec((D_MODEL, GATE_COLS), lambda i: (0, 0)),
        ],
        out_specs=[
            pl.BlockSpec((TM_PROJ, MAIN_COLS), lambda i: (i, 0)),
            pl.BlockSpec((TM_PROJ, GATE_COLS), lambda i: (i, 0)),
        ],
        out_shape=[
            jax.ShapeDtypeStruct((n, MAIN_COLS), BF16),
            jax.ShapeDtypeStruct((n, GATE_COLS), F32),
        ],
        compiler_params=pltpu.CompilerParams(
            dimension_semantics=("parallel",), vmem_limit_bytes=VMEM_LIMIT),
        name="in_proj",
    )(x2d, w_main, w_gates)


def _dn_kernel(x_ref, gates_ref, convw_ref, gpar_ref, normg_ref, out_ref,
               xc_ref, gl_ref, gc_ref, wq_ref, u_ref, kt_ref, attn_ref, egl_ref, s_ref, hist_ref):
    nseq = x_ref.shape[0]
    ts = x_ref.shape[1]
    c = DN_CHUNK
    hd = DN_HEAD_DIM
    qkv_w = 3 * DN_WIDTH
    nch = ts // c

    @pl.when(pl.program_id(1) == 0)
    def _():
        s_ref[...] = jnp.zeros_like(s_ref)
        hist_ref[...] = jnp.zeros_like(hist_ref)

    def stage_inputs(bi, carry):
        xc_ref[bi, 0:8, :] = hist_ref[bi]
        xc_ref[bi, 8:ts + 8, :] = x_ref[bi, :, 0:qkv_w].astype(F32)
        hist_ref[bi] = xc_ref[bi, ts:ts + 8, :]
        gsl = gates_ref[bi]
        sp_in = gsl + gpar_ref[1:2, :]
        softplus = jnp.maximum(sp_in, 0.0) + jnp.log(1.0 + jnp.exp(-jnp.abs(sp_in)))
        lane = lax.broadcasted_iota(I32, gsl.shape, 1)
        gl = jnp.where(lane < DN_HEADS, _sigmoid(gsl), -jnp.exp(gpar_ref[0:1, :]) * softplus)
        gl_ref[bi] = gl
        row_in_chunk = lax.broadcasted_iota(I32, gsl.shape, 0) % c
        gc = gl
        shift = 1
        while shift < c:
            gc = gc + jnp.where(row_in_chunk >= shift, pltpu.roll(gc, shift, 0), 0.0)
            shift *= 2
        gc_ref[bi] = gc
        return carry

    lax.fori_loop(0, nseq, stage_inputs, 0)

    ii = lax.broadcasted_iota(I32, (c, c), 0)
    jj = lax.broadcasted_iota(I32, (c, c), 1)
    tri_incl = ii >= jj
    tri_strict = ii > jj
    eye = jnp.where(ii == jj, 1.0, 0.0).astype(F32)
    heads = range(DN_HEADS)

    def conv_silu(bi, r0, col):
        w = convw_ref[:, col:col + hd]
        xt = xc_ref[bi, pl.ds(r0, c + 8), col:col + hd]
        y = w[CONV_WIDTH - 1:CONV_WIDTH, :] * xt[8:8 + c, :]
        for delay in range(1, CONV_WIDTH):
            tap = CONV_WIDTH - 1 - delay
            y = y + w[tap:tap + 1, :] * pltpu.roll(xt, delay, 0)[8:8 + c, :]
        return _silu(y)

    def l2n(t, scale):
        return t * (lax.rsqrt(jnp.sum(t * t, axis=-1, keepdims=True) + L2_EPS) * scale)

    def phase_a(i, bi):
        chains = []
        for sub in range(DN_A_UNROLL):
            ci = i * DN_A_UNROLL + sub
            r0 = ci * c
            glc = gl_ref[bi, pl.ds(r0, c), :]
            gcc = gc_ref[bi, pl.ds(r0, c), :]
            gct = jnp.concatenate([gcc, gcc], axis=0).T
            egl_ref[bi, ci] = jnp.exp(gcc[c - 8:c, :])
            for h in heads:
                chains.append((ci, r0, h, glc, gcc, gct))
        nchain = len(chains)
        q = [l2n(conv_silu(bi, r0, COL_DNQ + h * hd), hd ** -0.5) for (ci, r0, h, _, _, _) in chains]
        k = [l2n(conv_silu(bi, r0, COL_DNK + h * hd), 1.0) for (ci, r0, h, _, _, _) in chains]
        v = [conv_silu(bi, r0, COL_DNV + h * hd) for (ci, r0, h, _, _, _) in chains]
        kb, vb, decay, egc = [], [], [], []
        for n_, (ci, r0, h, glc, gcc, gct) in enumerate(chains):
            beta = glc[:, h:h + 1]
            gc_col = gcc[:, DN_HEADS + h:DN_HEADS + h + 1]
            gc_row = gct[DN_HEADS + h:DN_HEADS + h + 1, 0:c]
            decay.append(jnp.where(tri_incl, jnp.exp(jnp.minimum(gc_col - gc_row, 0.0)), 0.0))
            egc.append(jnp.exp(gc_col))
            e_tail = jnp.exp(gcc[c - 1:c, DN_HEADS + h:DN_HEADS + h + 1] - gc_col)
            kb.append(k[n_] * beta)
            vb.append(v[n_] * beta)
            kt_ref[bi, ci, h] = (k[n_] * e_tail).astype(BF16)
        kq = [_mm_nt(jnp.concatenate([kb[n_], q[n_]], axis=0), k[n_]) for n_ in range(nchain)]
        a_mat = [jnp.where(tri_strict, kq[n_][0:c] * decay[n_], 0.0) for n_ in range(nchain)]
        for n_, (ci, r0, h, _, _, _) in enumerate(chains):
            attn_ref[bi, ci, h] = (kq[n_][c:2 * c] * decay[n_]).astype(BF16)
        t_inv = [eye - a for a in a_mat]
        p = a_mat
        for _ in range(5):
            p = [_mm(x, x) for x in p]
            t_inv = [t + _mm(t, x) for t, x in zip(t_inv, p)]
        for n_, (ci, r0, h, _, _, _) in enumerate(chains):
            uw = _mm3(t_inv[n_], jnp.concatenate([vb[n_], kb[n_] * egc[n_]], axis=1))
            u_ref[bi, ci, h] = uw[:, 0:hd]
            wq_ref[bi, ci, h, 0:c, :] = uw[:, hd:2 * hd].astype(BF16)
            wq_ref[bi, ci, h, c:2 * c, :] = (q[n_] * egc[n_]).astype(BF16)

    for bi in range(nseq):
        for i in range(nch // DN_A_UNROLL):
            phase_a(i, bi)

    normg = normg_ref[...]

    def phase_b(ci, carry):
        r0 = pl.multiple_of(ci * c, c)
        rows = pl.ds(r0, c)
        chains = [(bi, h) for bi in range(nseq) for h in heads]
        egl = [egl_ref[bi, ci] for bi in range(nseq)]
        s_old = [s_ref[bi, h] for bi, h in chains]
        ws = [_dot(wq_ref[bi, ci, h], s.astype(BF16)) for (bi, h), s in zip(chains, s_old)]
        v_new = [(u_ref[bi, ci, h] - w[0:c]).astype(BF16) for (bi, h), w in zip(chains, ws)]
        for n_, (bi, h) in enumerate(chains):
            s_ref[bi, h] = (s_old[n_] * egl[bi][7:8, DN_HEADS + h:DN_HEADS + h + 1]
                            + lax.dot_general(kt_ref[bi, ci, h], v_new[n_], (((0,), (0,)), ((), ())),
                                              preferred_element_type=F32))
        for n_, (bi, h) in enumerate(chains):
            o = ws[n_][c:2 * c] + _dot(attn_ref[bi, ci, h], v_new[n_])
            o = o * lax.rsqrt(jnp.mean(o * o, axis=-1, keepdims=True) + RMS_EPS) * normg
            z = x_ref[bi, rows, COL_Z + h * hd:COL_Z + (h + 1) * hd].astype(F32)
            out_ref[bi, rows, h * hd:(h + 1) * hd] = (o * _silu(z)).astype(out_ref.dtype)
        return carry

    lax.fori_loop(0, nch, phase_b, 0)


def _deltanet(main3d, gates3d, conv_w, gpar, normg):
    b, t, _ = main3d.shape
    ts = TS_DN
    nseq = DN_SEQS
    nch = ts // DN_CHUNK
    dn_in = COL_Z + DN_WIDTH
    return pl.pallas_call(
        _dn_kernel,
        grid=(b // nseq, t // ts),
        in_specs=[
            pl.BlockSpec((nseq, ts, dn_in), lambda bi, si: (bi, si, 0)),
            pl.BlockSpec((nseq, ts, GATE_COLS), lambda bi, si: (bi, si, 0)),
            pl.BlockSpec((CONV_WIDTH, 3 * DN_WIDTH), lambda bi, si: (0, 0)),
            pl.BlockSpec((2, GATE_COLS), lambda bi, si: (0, 0)),
            pl.BlockSpec((1, DN_HEAD_DIM), lambda bi, si: (0, 0)),
        ],
        out_specs=pl.BlockSpec((nseq, ts, DN_WIDTH), lambda bi, si: (bi, si, 0)),
        out_shape=jax.ShapeDtypeStruct((b, t, DN_WIDTH), BF16),
        scratch_shapes=[
            pltpu.VMEM((nseq, ts + 8, 3 * DN_WIDTH), F32),
            pltpu.VMEM((nseq, ts, GATE_COLS), F32),
            pltpu.VMEM((nseq, ts, GATE_COLS), F32),
            pltpu.VMEM((nseq, nch, DN_HEADS, 2 * DN_CHUNK, DN_HEAD_DIM), BF16),
            pltpu.VMEM((nseq, nch, DN_HEADS, DN_CHUNK, DN_HEAD_DIM), F32),
            pltpu.VMEM((nseq, nch, DN_HEADS, DN_CHUNK, DN_HEAD_DIM), BF16),
            pltpu.VMEM((nseq, nch, DN_HEADS, DN_CHUNK, DN_CHUNK), BF16),
            pltpu.VMEM((nseq, nch, 8, GATE_COLS), F32),
            pltpu.VMEM((nseq, DN_HEADS, DN_HEAD_DIM, DN_HEAD_DIM), F32),
            pltpu.VMEM((nseq, 8, 3 * DN_WIDTH), F32),
        ],
        compiler_params=pltpu.CompilerParams(
            dimension_semantics=("parallel", "arbitrary"), vmem_limit_bytes=VMEM_LIMIT),
        name="deltanet",
    )(main3d, gates3d, conv_w, gpar, normg)


def _swa_kernel(sinks_ref, q_ref, kp_ref, kc_ref, vp_ref, vc_ref, out_ref, bias_ref):
    n = pl.program_id(1)
    blk = SWA_BLOCK
    d = SWA_HEAD_DIM
    grp = SWA_Q_HEADS // SWA_KV_HEADS

    @pl.when((pl.program_id(0) == 0) & (n == 0))
    def _():
        qi = lax.broadcasted_iota(I32, (blk, 2 * blk), 0)
        kj = lax.broadcasted_iota(I32, (blk, 2 * blk), 1)
        dist = qi + blk - kj
        valid = (dist >= 0) & (dist < SWA_WINDOW)
        dist_f = dist.astype(F32)
        for hq in range(SWA_Q_HEADS):
            slope = 2.0 ** (-8.0 * (hq + 1.0) / SWA_Q_HEADS)
            bias_ref[hq] = jnp.where(valid, -slope * dist_f, NEG_INF)

    kcol = lax.broadcasted_iota(I32, (1, 2 * blk), 1)
    colmask = jnp.where((kcol >= blk) | (n > 0), 0.0, NEG_INF)
    q_all = q_ref[...] * (d ** -0.5)
    kband = [jnp.concatenate([kp_ref[:, hk * d:(hk + 1) * d], kc_ref[:, hk * d:(hk + 1) * d]], axis=0)
             for hk in range(SWA_KV_HEADS)]
    vband = [jnp.concatenate([vp_ref[:, hk * d:(hk + 1) * d], vc_ref[:, hk * d:(hk + 1) * d]], axis=0)
             for hk in range(SWA_KV_HEADS)]
    heads = range(SWA_Q_HEADS)
    scores = [_mm_nt(q_all[:, hq * d:(hq + 1) * d], kband[hq // grp]) for hq in heads]
    probs, denoms = [], []
    for hq in heads:
        s = scores[hq] + bias_ref[hq] + colmask
        sink = sinks_ref[hq]
        m = jnp.maximum(jnp.max(s, axis=-1, keepdims=True), sink)
        p = jnp.exp(s - m)
        denoms.append(jnp.sum(p, axis=-1, keepdims=True) + jnp.exp(sink - m))
        probs.append(p.astype(BF16))
    outs = [_dot(probs[hq], vband[hq // grp]) / denoms[hq] for hq in heads]
    out_ref[...] = jnp.concatenate(outs, axis=-1).astype(out_ref.dtype)


def _swa(main3d, sinks):
    b, t, _ = main3d.shape
    blk = SWA_BLOCK
    qb = COL_SQ // SWA_WIDTH
    kb = COL_SK // SWA_KV_WIDTH
    vb = COL_SV // SWA_KV_WIDTH
    grid_spec = pltpu.PrefetchScalarGridSpec(
        num_scalar_prefetch=1,
        grid=(b, t // blk),
        in_specs=[
            pl.BlockSpec((None, blk, SWA_WIDTH), lambda bi, ni, s: (bi, ni, qb)),
            pl.BlockSpec((None, blk, SWA_KV_WIDTH), lambda bi, ni, s: (bi, jnp.maximum(ni - 1, 0), kb)),
            pl.BlockSpec((None, blk, SWA_KV_WIDTH), lambda bi, ni, s: (bi, ni, kb)),
            pl.BlockSpec((None, blk, SWA_KV_WIDTH), lambda bi, ni, s: (bi, jnp.maximum(ni - 1, 0), vb)),
            pl.BlockSpec((None, blk, SWA_KV_WIDTH), lambda bi, ni, s: (bi, ni, vb)),
        ],
        out_specs=pl.BlockSpec((None, blk, SWA_WIDTH), lambda bi, ni, s: (bi, ni, 0)),
        scratch_shapes=[pltpu.VMEM((SWA_Q_HEADS, blk, 2 * blk), F32)],
    )
    return pl.pallas_call(
        _swa_kernel,
        grid_spec=grid_spec,
        out_shape=jax.ShapeDtypeStruct((b, t, SWA_WIDTH), BF16),
        compiler_params=pltpu.CompilerParams(
            dimension_semantics=("arbitrary", "arbitrary"), vmem_limit_bytes=VMEM_LIMIT),
        name="swa",
    )(sinks, main3d, main3d, main3d, main3d, main3d)


def _layer_norm(y, g, b):
    mu = jnp.mean(y, axis=-1, keepdims=True)
    yc = y - mu
    var = jnp.mean(yc * yc, axis=-1, keepdims=True)
    return yc * lax.rsqrt(var + LN_EPS) * g + b


def _pack_bf16_pair(lo, hi):
    lo_bits = lax.bitcast_convert_type(lo.astype(BF16).astype(F32), U32)
    hi_bits = lax.bitcast_convert_type(hi.astype(BF16).astype(F32), U32)
    return (hi_bits & jnp.uint32(0xFFFF0000)) | (lo_bits >> 16)


def _unpack_bf16_pair(packed):
    lo = lax.bitcast_convert_type(packed << 16, F32)
    hi = lax.bitcast_convert_type(packed & jnp.uint32(0xFFFF0000), F32)
    return lo, hi


def _post_mix_kernel(x_ref, dn_ref, swa_ref, wo_dn_ref, wo_swa_ref, g_ref, b_ref,
                     rw_ref, sg_ref, su_ref, sd_ref,
                     base_ref, xpk_ref, logit_ref):
    mix = _dot(dn_ref[...], wo_dn_ref[...]) + _dot(swa_ref[...], wo_swa_ref[...])
    x1 = _layer_norm(DEEPNORM_ALPHA * x_ref[...] + mix, g_ref[...], b_ref[...])
    half = D_MODEL // 2
    xpk_ref[...] = _pack_bf16_pair(x1[:, :half], x1[:, half:])
    xh = x1.astype(BF16)
    logit_ref[...] = lax.dot_general(rw_ref[...], xh, (((1,), (1,)), ((), ())), preferred_element_type=F32)
    hmid = _silu(_dot(xh, sg_ref[...])) * _dot(xh, su_ref[...])
    base_ref[...] = DEEPNORM_ALPHA * x1 + _dot(hmid.astype(BF16), sd_ref[...])


def _post_mix(x2d, dn2d, swa2d, wo_dn, wo_swa, ln_g, ln_b, rw_t, sg, su, sd):
    n = x2d.shape[0]
    tm = TM_POST
    full = lambda shape: pl.BlockSpec(shape, lambda i: (0, 0))
    return pl.pallas_call(
        _post_mix_kernel,
        grid=(n // tm,),
        in_specs=[
            pl.BlockSpec((tm, D_MODEL), lambda i: (i, 0)),
            pl.BlockSpec((tm, DN_WIDTH), lambda i: (i, 0)),
            pl.BlockSpec((tm, SWA_WIDTH), lambda i: (i, 0)),
            full((DN_WIDTH, D_MODEL)), full((SWA_WIDTH, D_MODEL)),
            full((1, D_MODEL)), full((1, D_MODEL)),
            full((N_EXPERTS, D_MODEL)),
            full((D_MODEL, SHARED_FF)), full((D_MODEL, SHARED_FF)), full((SHARED_FF, D_MODEL)),
        ],
        out_specs=[
            pl.BlockSpec((tm, D_MODEL), lambda i: (i, 0)),
            pl.BlockSpec((tm, D_MODEL // 2), lambda i: (i, 0)),
            pl.BlockSpec((N_EXPERTS, tm), lambda i: (0, i)),
        ],
        out_shape=[
            jax.ShapeDtypeStruct((n, D_MODEL), F32),
            jax.ShapeDtypeStruct((n, D_MODEL // 2), U32),
            jax.ShapeDtypeStruct((N_EXPERTS, n), F32),
        ],
        compiler_params=pltpu.CompilerParams(
            dimension_semantics=("parallel",), vmem_limit_bytes=VMEM_LIMIT),
        name="post_mix",
    )(x2d, dn2d, swa2d, wo_dn, wo_swa, ln_g, ln_b, rw_t, sg, su, sd)


def _route_kernel(lg_ref, bias_ref, eidx_ref, gate_ref, rank_ref, cnt_ref, carry_ref):
    @pl.when(pl.program_id(0) == 0)
    def _():
        carry_ref[...] = jnp.zeros_like(carry_ref)

    tt = lg_ref.shape[1]
    scores = _sigmoid(lg_ref[...])
    sel = scores + bias_ref[...]

    iog = lax.broadcasted_iota(I32, (GROUP_SIZE, tt), 0)
    grp_rows = []
    for g in range(N_GROUPS):
        blk = sel[g * GROUP_SIZE:(g + 1) * GROUP_SIZE, :]
        m1 = jnp.max(blk, axis=0, keepdims=True)
        i1 = jnp.min(jnp.where(blk == m1, iog, GROUP_SIZE), axis=0, keepdims=True)
        m2 = jnp.max(jnp.where(iog == i1, NEG_INF, blk), axis=0, keepdims=True)
        grp_rows.append(m1 + m2)
    gs = jnp.concatenate(grp_rows, axis=0)

    io8 = lax.broadcasted_iota(I32, (N_GROUPS, tt), 0)
    gsel = jnp.zeros((N_GROUPS, tt), F32)
    for _ in range(TOPK_GROUPS):
        mg = jnp.max(gs, axis=0, keepdims=True)
        ig = jnp.min(jnp.where(gs == mg, io8, N_GROUPS), axis=0, keepdims=True)
        hit = io8 == ig
        gsel = jnp.where(hit, 1.0, gsel)
        gs = jnp.where(hit, NEG_INF, gs)

    val = jnp.concatenate(
        [jnp.where(gsel[g:g + 1, :] > 0.0, sel[g * GROUP_SIZE:(g + 1) * GROUP_SIZE, :], NEG_INF)
         for g in range(N_GROUPS)], axis=0)

    ioe = lax.broadcasted_iota(I32, (N_EXPERTS, tt), 0)
    onehot = jnp.zeros((N_EXPERTS, tt), F32)
    idx_rows, gate_rows = [], []
    for _ in range(TOP_K):
        m = jnp.max(val, axis=0, keepdims=True)
        ik = jnp.min(jnp.where(val == m, ioe, N_EXPERTS), axis=0, keepdims=True)
        hit = ioe == ik
        gate_rows.append(jnp.sum(jnp.where(hit, scores, 0.0), axis=0, keepdims=True))
        idx_rows.append(ik)
        val = jnp.where(hit, NEG_INF, val)
        onehot = jnp.where(hit, 1.0, onehot)
    gsum = gate_rows[0]
    for r in gate_rows[1:]:
        gsum = gsum + r
    gate_ref[...] = jnp.concatenate(gate_rows, axis=0) / gsum * ROUTED_SCALE
    eidx_ref[...] = jnp.concatenate(idx_rows, axis=0)

    ti = lax.broadcasted_iota(I32, (tt, tt), 0)
    tj = lax.broadcasted_iota(I32, (tt, tt), 1)
    upper = jnp.where(ti < tj, 1.0, 0.0).astype(BF16)
    cum = _dot(onehot.astype(BF16), upper) + jnp.broadcast_to(carry_ref[:, 0:1], (N_EXPERTS, tt))
    rank_rows = [jnp.sum(jnp.where(ioe == ik, cum, 0.0), axis=0, keepdims=True) for ik in idx_rows]
    rank_ref[...] = jnp.concatenate(rank_rows, axis=0).astype(I32)
    carry_ref[...] = carry_ref[...] + jnp.broadcast_to(
        jnp.sum(onehot, axis=1, keepdims=True), carry_ref.shape)
    cnt_ref[...] = carry_ref[...].astype(I32)


def _route(logits_t, bias_col):
    n = logits_t.shape[1]
    tt = TT_ROUTE
    row_spec = pl.BlockSpec((TOP_K, tt), lambda i: (0, i))
    return pl.pallas_call(
        _route_kernel,
        grid=(n // tt,),
        in_specs=[
            pl.BlockSpec((N_EXPERTS, tt), lambda i: (0, i)),
            pl.BlockSpec((N_EXPERTS, 1), lambda i: (0, 0)),
        ],
        out_specs=[row_spec, row_spec, row_spec,
                   pl.BlockSpec((N_EXPERTS, 128), lambda i: (0, 0))],
        out_shape=[
            jax.ShapeDtypeStruct((TOP_K, n), I32),
            jax.ShapeDtypeStruct((TOP_K, n), F32),
            jax.ShapeDtypeStruct((TOP_K, n), I32),
            jax.ShapeDtypeStruct((N_EXPERTS, 128), I32),
        ],
        scratch_shapes=[pltpu.VMEM((N_EXPERTS, 128), F32)],
        compiler_params=pltpu.CompilerParams(
            dimension_semantics=("arbitrary",), vmem_limit_bytes=VMEM_LIMIT),
        name="route",
    )(logits_t, bias_col)


def _place_kernel(eidx_ref, rank_ref, pstart_ref, dest_ref):
    tt = eidx_ref.shape[1]
    ioe = lax.broadcasted_iota(I32, (N_EXPERTS, tt), 0)
    pstart = pstart_ref[...]
    rows = [jnp.sum(jnp.where(ioe == eidx_ref[k:k + 1, :], pstart, 0.0), axis=0, keepdims=True)
            for k in range(TOP_K)]
    dest_ref[...] = jnp.concatenate(rows, axis=0).astype(I32) + rank_ref[...]


def _place(eidx, rank, pstart_col):
    n = eidx.shape[1]
    tt = TT_ROUTE
    row_spec = pl.BlockSpec((TOP_K, tt), lambda i: (0, i))
    return pl.pallas_call(
        _place_kernel,
        grid=(n // tt,),
        in_specs=[row_spec, row_spec, pl.BlockSpec((N_EXPERTS, 1), lambda i: (0, 0))],
        out_specs=row_spec,
        out_shape=jax.ShapeDtypeStruct((TOP_K, n), I32),
        compiler_params=pltpu.CompilerParams(
            dimension_semantics=("parallel",), vmem_limit_bytes=VMEM_LIMIT),
        name="place",
    )(eidx, rank, pstart_col)


def _sc_mesh():
    return plsc.VectorSubcoreMesh(core_axis_name="c", subcore_axis_name="s",
                                  num_cores=SC_NC, num_subcores=SC_NS)


def _sc_scatter_rows(rows, idx, nrows_out):
    n, d = rows.shape
    nk = idx.shape[0]
    per_w = n // SC_NW
    nwin = per_w // SC_WIN
    assert per_w * SC_NW == n and nwin * SC_WIN == per_w and nwin % 2 == 0

    @functools.partial(
        pl.kernel, mesh=_sc_mesh(),
        out_type=jax.ShapeDtypeStruct((nrows_out, d), rows.dtype),
        scratch_types=[
            pltpu.VMEM((nwin, nk, SC_WIN), I32),
            pltpu.VMEM((2, SC_WIN, d), rows.dtype),
            pltpu.SemaphoreType.DMA((2,)),
            pltpu.SemaphoreType.DMA((2,)),
        ],
        compiler_params=pltpu.CompilerParams(use_tc_tiling_on_sc=True),
        name="sc_scatter_rows",
    )
    def scatter_kernel(rows_hbm, idx_hbm, out_hbm, idx_v, rows_v, lsem, ssem):
        wid = lax.axis_index("s") * SC_NC + lax.axis_index("c")
        base = wid * per_w
        pltpu.sync_copy(idx_hbm.at[wid], idx_v)

        def load(w, slot):
            return pltpu.make_async_copy(
                rows_hbm.at[pl.ds(base + w * SC_WIN, SC_WIN)], rows_v.at[slot], lsem.at[slot])

        def scat(w, k, slot):
            return pltpu.make_async_copy(rows_v.at[slot], out_hbm.at[idx_v.at[w, k]], ssem.at[slot])

        load(0, 0).start()

        @pl.loop(0, nwin, step=2)
        def _(w0):
            for slot in range(2):
                w = w0 + slot
                load(w, slot).wait()

                @pl.when(w + 1 < nwin)
                def _():
                    @pl.when(w >= 1)
                    def _():
                        for k in range(nk):
                            scat(w - 1, k, 1 - slot).wait()
                    load(w + 1, 1 - slot).start()

                for k in range(nk):
                    scat(w, k, slot).start()

        for k in range(nk):
            scat(nwin - 2, k, 0).wait()
        for k in range(nk):
            scat(nwin - 1, k, 1).wait()

    idx4 = idx.reshape(nk, SC_NW, nwin, SC_WIN).transpose(1, 2, 0, 3)
    return scatter_kernel(rows, idx4)


def _expert_kernel(gstart_ref, cnt_ref, xs_hbm, wg_ref, wu_ref, wd_ref, y_hbm,
                   wgb_ref, wub_ref, wdb_ref, xbuf_ref, ybuf_ref, xsem, ysem):
    e = pl.program_id(0)
    ne = pl.num_programs(0)
    bm = xbuf_ref.shape[1]
    nblk = y_hbm.shape[0] // bm
    half = D_MODEL // 2
    g_lo = gstart_ref[e]
    g_hi = gstart_ref[e + 1]
    g_end = gstart_ref[ne]

    def x_copy(g, slot):
        return pltpu.make_async_copy(xs_hbm.at[pl.ds(g * bm, bm), :], xbuf_ref.at[slot], xsem.at[slot])

    def y_copy(g, slot):
        return pltpu.make_async_copy(ybuf_ref.at[slot], y_hbm.at[pl.ds(g * bm, bm), :], ysem.at[slot])

    nslot = xbuf_ref.shape[0]

    ahead = nslot - 2

    @pl.when(e == 0)
    def _():
        for g0 in range(ahead):
            @pl.when(g0 < g_end)
            def _():
                x_copy(g0, g0).start()

    @pl.when(g_hi > g_lo)
    def _():
        wgb_ref[...] = wg_ref[...].astype(BF16)
        wub_ref[...] = wu_ref[...].astype(BF16)
        wdb_ref[...] = wd_ref[...].astype(BF16)

    row = lax.broadcasted_iota(I32, (bm, half), 0)

    def acquire(g):
        x_copy(g, g % nslot).wait()

        @pl.when(g + ahead < g_end)
        def _():
            x_copy(g + ahead, (g + ahead) % nslot).start()

        @pl.when(g >= nslot)
        def _():
            y_copy(g - nslot, g % nslot).wait()

    def load(g):
        n_valid = cnt_ref[e] - (g - g_lo) * bm
        x_lo, x_hi = _unpack_bf16_pair(jnp.where(row < n_valid, xbuf_ref[g % nslot], jnp.uint32(0)))
        return x_lo.astype(BF16), x_hi.astype(BF16)

    def gate_up(x):
        x_lo, x_hi = x
        gate = _dot(x_lo, wgb_ref[:half, :]) + _dot(x_hi, wgb_ref[half:, :])
        up = _dot(x_lo, wub_ref[:half, :]) + _dot(x_hi, wub_ref[half:, :])
        return gate, up

    def down(gu):
        gate, up = gu
        return _dot((_silu(gate) * up).astype(BF16), wdb_ref[...])

    def store(g, y):
        ybuf_ref[g % nslot] = _pack_bf16_pair(y[:, :half], y[:, half:])
        y_copy(g, g % nslot).start()

    def pair(p, carry):
        g = g_lo + 2 * p
        acquire(g)
        acquire(g + 1)
        gu_a = gate_up(load(g))
        gu_b = gate_up(load(g + 1))
        y_a = down(gu_a)
        y_b = down(gu_b)
        store(g, y_a)
        store(g + 1, y_b)
        return carry

    n_own = g_hi - g_lo
    lax.fori_loop(0, n_own // 2, pair, 0)

    @pl.when(n_own % 2 == 1)
    def _():
        acquire(g_hi - 1)
        store(g_hi - 1, down(gate_up(load(g_hi - 1))))

    @pl.when(e == ne - 1)
    def _():
        for back in range(nslot, 0, -1):
            @pl.when(g_end >= back)
            def _():
                y_copy(g_end - back, (g_end - back) % nslot).wait()

        ybuf_ref[0] = jnp.zeros((bm, half), U32)

        def fill(g, carry):
            y_copy(g, 0).start()
            return carry

        def drain(g, carry):
            y_copy(g, 0).wait()
            return carry

        lax.fori_loop(g_end, nblk, fill, 0)
        lax.fori_loop(g_end, nblk, drain, 0)


def _experts(gstart, counts, xs, w_gate, w_up, w_down):
    bm = BM_EXP
    nblk = xs.shape[0] // bm
    half = D_MODEL // 2
    grid_spec = pltpu.PrefetchScalarGridSpec(
        num_scalar_prefetch=2,
        grid=(N_EXPERTS,),
        in_specs=[
            pl.BlockSpec(memory_space=pl.ANY),
            pl.BlockSpec((None, D_MODEL, EXPERT_FF), lambda e, gs, cn: (e, 0, 0)),
            pl.BlockSpec((None, D_MODEL, EXPERT_FF), lambda e, gs, cn: (e, 0, 0)),
            pl.BlockSpec((None, EXPERT_FF, D_MODEL), lambda e, gs, cn: (e, 0, 0)),
        ],
        out_specs=pl.BlockSpec(memory_space=pl.ANY),
        scratch_shapes=[
            pltpu.VMEM((D_MODEL, EXPERT_FF), BF16),
            pltpu.VMEM((D_MODEL, EXPERT_FF), BF16),
            pltpu.VMEM((EXPERT_FF, D_MODEL), BF16),
            pltpu.VMEM((EXP_SLOTS, bm, half), U32),
            pltpu.VMEM((EXP_SLOTS, bm, half), U32),
            pltpu.SemaphoreType.DMA((EXP_SLOTS,)),
            pltpu.SemaphoreType.DMA((EXP_SLOTS,)),
        ],
    )
    return pl.pallas_call(
        _expert_kernel,
        grid_spec=grid_spec,
        out_shape=jax.ShapeDtypeStruct((nblk * bm, half), U32),
        compiler_params=pltpu.CompilerParams(
            dimension_semantics=("arbitrary",), vmem_limit_bytes=VMEM_LIMIT),
        name="experts",
    )(gstart, counts, xs, w_gate, w_up, w_down)


def _sc_gather_rows(table, idx):
    nrows = idx.shape[0]
    d = table.shape[1]
    per_w = nrows // SC_NW
    nwin = per_w // SC_WIN
    assert per_w * SC_NW == nrows and nwin * SC_WIN == per_w and nwin % 2 == 0
    @functools.partial(
        pl.kernel, mesh=_sc_mesh(),
        out_type=jax.ShapeDtypeStruct((nrows, d), table.dtype),
        scratch_types=[
            pltpu.VMEM((nwin, SC_WIN), I32),
            pltpu.VMEM((2, SC_WIN, d), table.dtype),
            pltpu.SemaphoreType.DMA((2,)),
            pltpu.SemaphoreType.DMA((2,)),
        ],
        compiler_params=pltpu.CompilerParams(use_tc_tiling_on_sc=True),
        name="sc_gather_rows",
    )
    def gather_kernel(table_hbm, idx_hbm, out_hbm, idx_v, rows_v, gsem, wsem):
        wid = lax.axis_index("s") * SC_NC + lax.axis_index("c")
        base = wid * per_w
        pltpu.sync_copy(idx_hbm.at[wid], idx_v)

        def gather(w, slot):
            return pltpu.make_async_copy(table_hbm.at[idx_v.at[w]], rows_v.at[slot], gsem.at[slot])

        def put(w, slot):
            return pltpu.make_async_copy(
                rows_v.at[slot], out_hbm.at[pl.ds(base + w * SC_WIN, SC_WIN)], wsem.at[slot])

        gather(0, 0).start()

        @pl.loop(0, nwin, step=2)
        def _(w0):
            for slot in range(2):
                w = w0 + slot
                gather(w, slot).wait()

                @pl.when(w + 1 < nwin)
                def _():
                    @pl.when(w >= 1)
                    def _():
                        put(w - 1, 1 - slot).wait()
                    gather(w + 1, 1 - slot).start()

                put(w, slot).start()

        put(nwin - 2, 0).wait()
        put(nwin - 1, 1).wait()

    return gather_kernel(table, idx.reshape(SC_NW, nwin, SC_WIN))


def _combine_kernel(y_ref, base_ref, gate_ref, g_ref, b_ref, out_ref):
    half = D_MODEL // 2
    gates = gate_ref[...]
    acc_lo = base_ref[:, :half]
    acc_hi = base_ref[:, half:]
    for k in range(TOP_K):
        y_lo, y_hi = _unpack_bf16_pair(y_ref[k])
        gk = gates[:, k:k + 1]
        acc_lo = acc_lo + gk * y_lo
        acc_hi = acc_hi + gk * y_hi
    mu = (jnp.sum(acc_lo, axis=-1, keepdims=True) + jnp.sum(acc_hi, axis=-1, keepdims=True)) / D_MODEL
    c_lo = acc_lo - mu
    c_hi = acc_hi - mu
    var = (jnp.sum(c_lo * c_lo, axis=-1, keepdims=True)
           + jnp.sum(c_hi * c_hi, axis=-1, keepdims=True)) / D_MODEL
    inv = lax.rsqrt(var + LN_EPS)
    out_ref[:, :half] = c_lo * inv * g_ref[:, :half] + b_ref[:, :half]
    out_ref[:, half:] = c_hi * inv * g_ref[:, half:] + b_ref[:, half:]


def _combine(ybuf, base, gate_tok, ln_g, ln_b):
    n = base.shape[0]
    tt = TT_COMB
    half = D_MODEL // 2
    return pl.pallas_call(
        _combine_kernel,
        grid=(n // tt,),
        in_specs=[
            pl.BlockSpec((TOP_K, tt, half), lambda i: (0, i, 0)),
            pl.BlockSpec((tt, D_MODEL), lambda i: (i, 0)),
            pl.BlockSpec((tt, TOP_K), lambda i: (i, 0)),
            pl.BlockSpec((1, D_MODEL), lambda i: (0, 0)),
            pl.BlockSpec((1, D_MODEL), lambda i: (0, 0)),
        ],
        out_specs=pl.BlockSpec((tt, D_MODEL), lambda i: (i, 0)),
        out_shape=jax.ShapeDtypeStruct((n, D_MODEL), F32),
        compiler_params=pltpu.CompilerParams(
            dimension_semantics=("parallel",), vmem_limit_bytes=VMEM_LIMIT),
        name="combine",
    )(ybuf, base, gate_tok, ln_g, ln_b)


def _regroup_w_in(w_in):
    o = 0
    cols = {}
    for name, width in (("dnq", DN_WIDTH), ("dnk", DN_WIDTH), ("dnv", DN_WIDTH), ("sq", SWA_WIDTH),
                        ("sk", SWA_KV_WIDTH), ("sv", SWA_KV_WIDTH), ("z", DN_WIDTH),
                        ("b", DN_HEADS), ("a", DN_HEADS)):
        cols[name] = w_in[:, o:o + width]
        o += width
    w_main = jnp.concatenate([cols[k] for k in ("dnq", "dnk", "dnv", "z", "sq", "sk", "sv")], axis=1)
    w_gates = jnp.concatenate(
        [cols["b"], cols["a"], jnp.zeros((D_MODEL, GATE_COLS - 2 * DN_HEADS), w_in.dtype)], axis=1)
    return w_main.astype(BF16), w_gates.astype(BF16)


def _layer(x, w_in, conv_w, a_log, dt_bias, dn_norm_g, sinks, w_out, ln1_g, ln1_b,
           router_w, router_bias, w_gate, w_up, w_down, sh_gate, sh_up, sh_down, ln2_g, ln2_b):
    b, t, d = x.shape
    n = b * t
    x2d = x.reshape(n, d)

    w_main, w_gates = _regroup_w_in(w_in)
    main, gates = _in_proj(x2d, w_main, w_gates)
    main3d = main.reshape(b, t, MAIN_COLS)

    pad = jnp.zeros((GATE_COLS - 2 * DN_HEADS,), F32)
    gpar = jnp.stack([jnp.concatenate([jnp.zeros((DN_HEADS,), F32), a_log.astype(F32), pad]),
                      jnp.concatenate([jnp.zeros((DN_HEADS,), F32), dt_bias.astype(F32), pad])])
    dn_out = _deltanet(main3d, gates.reshape(b, t, GATE_COLS), conv_w.astype(F32), gpar,
                       dn_norm_g.astype(F32).reshape(1, DN_HEAD_DIM))
    swa_out = _swa(main3d, sinks.astype(F32))

    base, xpk, logits_t = _post_mix(
        x2d, dn_out.reshape(n, DN_WIDTH), swa_out.reshape(n, SWA_WIDTH),
        w_out[:DN_WIDTH].astype(BF16), w_out[DN_WIDTH:].astype(BF16),
        ln1_g.reshape(1, d).astype(F32), ln1_b.reshape(1, d).astype(F32),
        router_w.T.astype(BF16),
        sh_gate.astype(BF16), sh_up.astype(BF16), sh_down.astype(BF16))

    eidx, gate, rank, cnt = _route(logits_t, router_bias.astype(F32).reshape(N_EXPERTS, 1))

    bm = BM_EXP
    counts = cnt[:, 0]
    padded = (counts + bm - 1) // bm * bm
    pend = jnp.cumsum(padded)
    pstart = pend - padded
    nblk = -(-(n * TOP_K) // bm) + N_EXPERTS
    gstart = (jnp.concatenate([pstart, pend[-1:]]) // bm).astype(I32)

    dest = _place(eidx, rank, pstart.astype(F32).reshape(N_EXPERTS, 1))
    xs = _sc_scatter_rows(xpk, dest, nblk * bm)
    ypk = _experts(gstart, counts, xs, w_gate, w_up, w_down)
    ybuf = _sc_gather_rows(ypk, dest.reshape(-1)).reshape(TOP_K, n, d // 2)
    out = _combine(ybuf, base, gate.T, ln2_g.reshape(1, d).astype(F32), ln2_b.reshape(1, d).astype(F32))
    return out.reshape(b, t, d)


def kernel(x, w_in, conv_w, a_log, dt_bias, dn_norm_g, sinks, w_out, ln1_g, ln1_b, router_w, router_bias,
           w_gate, w_up, w_down, shared_w_gate, shared_w_up, shared_w_down, ln2_g, ln2_b):
    depth = w_in.shape[0]
    for l in range(depth):
        x = _layer(x, w_in[l], conv_w[l], a_log[l], dt_bias[l], dn_norm_g[l], sinks[l], w_out[l],
                   ln1_g[l], ln1_b[l], router_w[l], router_bias[l], w_gate[l], w_up[l], w_down[l],
                   shared_w_gate[l], shared_w_up[l], shared_w_down[l], ln2_g[l], ln2_b[l])
    return x
```

```python
import functools

import jax
import jax.numpy as jnp
from jax import lax
from jax.experimental import pallas as pl
from jax.experimental.pallas import tpu as pltpu
from jax.experimental.pallas import tpu_sc as plsc

F32 = jnp.float32
BF16 = jnp.bfloat16
I32 = jnp.int32
U32 = jnp.uint32

D_MODEL = 1024
DN_HEADS = 4
DN_HEAD_DIM = 128
DN_WIDTH = DN_HEADS * DN_HEAD_DIM
CONV_WIDTH = 4
DN_CHUNK = 64
SWA_Q_HEADS = 8
SWA_KV_HEADS = 2
SWA_HEAD_DIM = 64
SWA_WIDTH = SWA_Q_HEADS * SWA_HEAD_DIM
SWA_KV_WIDTH = SWA_KV_HEADS * SWA_HEAD_DIM
SWA_WINDOW = 128
SWA_BLOCK = 128
N_EXPERTS = 256
N_GROUPS = 8
GROUP_SIZE = N_EXPERTS // N_GROUPS
TOPK_GROUPS = 4
TOP_K = 8
EXPERT_FF = 256
SHARED_FF = 256
ROUTED_SCALE = 2.5
DEEPNORM_ALPHA = 2.0 ** 0.25
LN_EPS = 1e-5
RMS_EPS = 1e-6
L2_EPS = 1e-6

COL_DNQ = 0
COL_DNK = DN_WIDTH
COL_DNV = 2 * DN_WIDTH
COL_Z = 3 * DN_WIDTH
COL_SQ = 4 * DN_WIDTH
COL_SK = COL_SQ + SWA_WIDTH
COL_SV = COL_SK + SWA_KV_WIDTH
MAIN_COLS = COL_SV + SWA_KV_WIDTH
GATE_COLS = 128

TM_PROJ = 512
TS_DN = 256
DN_SEQS = 4
DN_A_UNROLL = 4
TM_POST = 512
TT_ROUTE = 512
BM_EXP = 256
EXP_GROUP = 4
EXP_SLOTS = 8
TT_COMB = 512
SC_NC = 2
SC_NS = 16
SC_NW = SC_NC * SC_NS
SC_WIN = 64
VMEM_LIMIT = 56 * 1024 * 1024
NEG_INF = float("-inf")


def _dot(a, b):
    return jnp.dot(a, b, preferred_element_type=F32)


def _mm(a, b):
    return _dot(a.astype(BF16), b.astype(BF16))


def _mm_nt(a, b):
    return lax.dot_general(a.astype(BF16), b.astype(BF16), (((1,), (1,)), ((), ())),
                           preferred_element_type=F32)


def _mm_tn(a, b):
    return lax.dot_general(a.astype(BF16), b.astype(BF16), (((0,), (0,)), ((), ())),
                           preferred_element_type=F32)


def _split2(a):
    hi = a.astype(BF16)
    lo = (a - hi.astype(F32)).astype(BF16)
    return hi, lo


def _mm3(a, b):
    ah, al = _split2(a)
    bh, bl = _split2(b)
    return _dot(ah, bh) + _dot(ah, bl) + _dot(al, bh)


def _mm_exact_lhs(l_bf16, g):
    g1 = g.astype(BF16)
    r1 = g - g1.astype(F32)
    g2 = r1.astype(BF16)
    g3 = (r1 - g2.astype(F32)).astype(BF16)
    return _dot(l_bf16, g1) + _dot(l_bf16, g2) + _dot(l_bf16, g3)


def _sigmoid(x):
    return 1.0 / (1.0 + jnp.exp(-x))


def _silu(x):
    return x * _sigmoid(x)


def _in_proj_kernel(x_ref, w_ref, wg_ref, main_ref, gates_ref):
    xb = x_ref[...].astype(BF16)
    main_ref[...] = _dot(xb, w_ref[...]).astype(BF16)
    gates_ref[...] = _dot(xb, wg_ref[...])


def _in_proj(x2d, w_main, w_gates):
    n = x2d.shape[0]
    return pl.pallas_call(
        _in_proj_kernel,
        grid=(n // TM_PROJ,),
        in_specs=[
            pl.BlockSpec((TM_PROJ, D_MODEL), lambda i: (i, 0)),
            pl.BlockSpec((D_MODEL, MAIN_COLS), lambda i: (0, 0)),
            pl.BlockSpec((D_MODEL, GATE_COLS), lambda i: (0, 0)),
        ],
        out_specs=[
            pl.BlockSpec((TM_PROJ, MAIN_COLS), lambda i: (i, 0)),
            pl.BlockSpec((TM_PROJ, GATE_COLS), lambda i: (i, 0)),
        ],
        out_shape=[
            jax.ShapeDtypeStruct((n, MAIN_COLS), BF16),
            jax.ShapeDtypeStruct((n, GATE_COLS), F32),
        ],
        compiler_params=pltpu.CompilerParams(
            dimension_semantics=("parallel",), vmem_limit_bytes=VMEM_LIMIT),
        name="in_proj",
    )(x2d, w_main, w_gates)


def _dn_kernel(x_ref, gates_ref, convw_ref, gpar_ref, normg_ref, out_ref,
               xc_ref, gl_ref, gc_ref, wq_ref, u_ref, kt_ref, attn_ref, egl_ref, s_ref, hist_ref):
    nseq = x_ref.shape[0]
    ts = x_ref.shape[1]
    c = DN_CHUNK
    hd = DN_HEAD_DIM
    qkv_w = 3 * DN_WIDTH
    nch = ts // c

    @pl.when(pl.program_id(1) == 0)
    def _():
        s_ref[...] = jnp.zeros_like(s_ref)
        hist_ref[...] = jnp.zeros_like(hist_ref)

    def stage_inputs(bi, carry):
        xc_ref[bi, 0:8, :] = hist_ref[bi]
        xc_ref[bi, 8:ts + 8, :] = x_ref[bi, :, 0:qkv_w].astype(F32)
        hist_ref[bi] = xc_ref[bi, ts:ts + 8, :]
        gsl = gates_ref[bi]
        sp_in = gsl + gpar_ref[1:2, :]
        softplus = jnp.maximum(sp_in, 0.0) + jnp.log(1.0 + jnp.exp(-jnp.abs(sp_in)))
        lane = lax.broadcasted_iota(I32, gsl.shape, 1)
        gl = jnp.where(lane < DN_HEADS, _sigmoid(gsl), -jnp.exp(gpar_ref[0:1, :]) * softplus)
        gl_ref[bi] = gl
        row_in_chunk = lax.broadcasted_iota(I32, gsl.shape, 0) % c
        gc = gl
        shift = 1
        while shift < c:
            gc = gc + jnp.where(row_in_chunk >= shift, pltpu.roll(gc, shift, 0), 0.0)
            shift *= 2
        gc_ref[bi] = gc
        return carry

    lax.fori_loop(0, nseq, stage_inputs, 0)

    ii = lax.broadcasted_iota(I32, (c, c), 0)
    jj = lax.broadcasted_iota(I32, (c, c), 1)
    tri_incl = ii >= jj
    tri_strict = ii > jj
    eye = jnp.where(ii == jj, 1.0, 0.0).astype(F32)
    heads = range(DN_HEADS)

    def conv_silu(bi, r0, col):
        w = convw_ref[:, col:col + hd]
        xt = xc_ref[bi, pl.ds(r0, c + 8), col:col + hd]
        y = w[CONV_WIDTH - 1:CONV_WIDTH, :] * xt[8:8 + c, :]
        for delay in range(1, CONV_WIDTH):
            tap = CONV_WIDTH - 1 - delay
            y = y + w[tap:tap + 1, :] * pltpu.roll(xt, delay, 0)[8:8 + c, :]
        return _silu(y)

    def l2n(t, scale):
        return t * (lax.rsqrt(jnp.sum(t * t, axis=-1, keepdims=True) + L2_EPS) * scale)

    def phase_a(i, bi):
        chains = []
        for sub in range(DN_A_UNROLL):
            ci = i * DN_A_UNROLL + sub
            r0 = ci * c
            glc = gl_ref[bi, pl.ds(r0, c), :]
            gcc = gc_ref[bi, pl.ds(r0, c), :]
            gct = jnp.concatenate([gcc, gcc], axis=0).T
            egl_ref[bi, ci] = jnp.exp(gcc[c - 8:c, :])
            for h in heads:
                chains.append((ci, r0, h, glc, gcc, gct))
        nchain = len(chains)
        q = [l2n(conv_silu(bi, r0, COL_DNQ + h * hd), hd ** -0.5) for (ci, r0, h, _, _, _) in chains]
        k = [l2n(conv_silu(bi, r0, COL_DNK + h * hd), 1.0) for (ci, r0, h, _, _, _) in chains]
        v = [conv_silu(bi, r0, COL_DNV + h * hd) for (ci, r0, h, _, _, _) in chains]
        kb, vb, decay, egc = [], [], [], []
        for n_, (ci, r0, h, glc, gcc, gct) in enumerate(chains):
            beta = glc[:, h:h + 1]
            gc_col = gcc[:, DN_HEADS + h:DN_HEADS + h + 1]
            gc_row = gct[DN_HEADS + h:DN_HEADS + h + 1, 0:c]
            decay.append(jnp.where(tri_incl, jnp.exp(jnp.minimum(gc_col - gc_row, 0.0)), 0.0))
            egc.append(jnp.exp(gc_col))
            e_tail = jnp.exp(gcc[c - 1:c, DN_HEADS + h:DN_HEADS + h + 1] - gc_col)
            kb.append(k[n_] * beta)
            vb.append(v[n_] * beta)
            kt_ref[bi, ci, h] = (k[n_] * e_tail).astype(BF16)
        kq = [_mm_nt(jnp.concatenate([kb[n_], q[n_]], axis=0), k[n_]) for n_ in range(nchain)]
        a_mat = [jnp.where(tri_strict, kq[n_][0:c] * decay[n_], 0.0) for n_ in range(nchain)]
        for n_, (ci, r0, h, _, _, _) in enumerate(chains):
            attn_ref[bi, ci, h] = (kq[n_][c:2 * c] * decay[n_]).astype(BF16)
        t_inv = [eye - a for a in a_mat]
        p = a_mat
        for _ in range(5):
            p = [_mm(x, x) for x in p]
            t_inv = [t + _mm(t, x) for t, x in zip(t_inv, p)]
        for n_, (ci, r0, h, _, _, _) in enumerate(chains):
            uw = _mm3(t_inv[n_], jnp.concatenate([vb[n_], kb[n_] * egc[n_]], axis=1))
            u_ref[bi, ci, h] = uw[:, 0:hd]
            wq_ref[bi, ci, h, 0:c, :] = uw[:, hd:2 * hd].astype(BF16)
            wq_ref[bi, ci, h, c:2 * c, :] = (q[n_] * egc[n_]).astype(BF16)

    for bi in range(nseq):
        for i in range(nch // DN_A_UNROLL):
            phase_a(i, bi)

    normg = normg_ref[...]

    def phase_b(ci, carry):
        r0 = pl.multiple_of(ci * c, c)
        rows = pl.ds(r0, c)
        chains = [(bi, h) for bi in range(nseq) for h in heads]
        egl = [egl_ref[bi, ci] for bi in range(nseq)]
        s_old = [s_ref[bi, h] for bi, h in chains]
        ws = [_dot(wq_ref[bi, ci, h], s.astype(BF16)) for (bi, h), s in zip(chains, s_old)]
        v_new = [(u_ref[bi, ci, h] - w[0:c]).astype(BF16) for (bi, h), w in zip(chains, ws)]
        for n_, (bi, h) in enumerate(chains):
            s_ref[bi, h] = (s_old[n_] * egl[bi][7:8, DN_HEADS + h:DN_HEADS + h + 1]
                            + lax.dot_general(kt_ref[bi, ci, h], v_new[n_], (((0,), (0,)), ((), ())),
                                              preferred_element_type=F32))
        for n_, (bi, h) in enumerate(chains):
            o = ws[n_][c:2 * c] + _dot(attn_ref[bi, ci, h], v_new[n_])
            o = o * lax.rsqrt(jnp.mean(o * o, axis=-1, keepdims=True) + RMS_EPS) * normg
            z = x_ref[bi, rows, COL_Z + h * hd:COL_Z + (h + 1) * hd].astype(F32)
            out_ref[bi, rows, h * hd:(h + 1) * hd] = (o * _silu(z)).astype(out_ref.dtype)
        return carry

    lax.fori_loop(0, nch, phase_b, 0)


def _deltanet(main3d, gates3d, conv_w, gpar, normg):
    b, t, _ = main3d.shape
    ts = TS_DN
    nseq = DN_SEQS
    nch = ts // DN_CHUNK
    dn_in = COL_Z + DN_WIDTH
    return pl.pallas_call(
        _dn_kernel,
        grid=(b // nseq, t // ts),
        in_specs=[
            pl.BlockSpec((nseq, ts, dn_in), lambda bi, si: (bi, si, 0)),
            pl.BlockSpec((nseq, ts, GATE_COLS), lambda bi, si: (bi, si, 0)),
            pl.BlockSpec((CONV_WIDTH, 3 * DN_WIDTH), lambda bi, si: (0, 0)),
            pl.BlockSpec((2, GATE_COLS), lambda bi, si: (0, 0)),
            pl.BlockSpec((1, DN_HEAD_DIM), lambda bi, si: (0, 0)),
        ],
        out_specs=pl.BlockSpec((nseq, ts, DN_WIDTH), lambda bi, si: (bi, si, 0)),
        out_shape=jax.ShapeDtypeStruct((b, t, DN_WIDTH), BF16),
        scratch_shapes=[
            pltpu.VMEM((nseq, ts + 8, 3 * DN_WIDTH), F32),
            pltpu.VMEM((nseq, ts, GATE_COLS), F32),
            pltpu.VMEM((nseq, ts, GATE_COLS), F32),
            pltpu.VMEM((nseq, nch, DN_HEADS, 2 * DN_CHUNK, DN_HEAD_DIM), BF16),
            pltpu.VMEM((nseq, nch, DN_HEADS, DN_CHUNK, DN_HEAD_DIM), F32),
            pltpu.VMEM((nseq, nch, DN_HEADS, DN_CHUNK, DN_HEAD_DIM), BF16),
            pltpu.VMEM((nseq, nch, DN_HEADS, DN_CHUNK, DN_CHUNK), BF16),
            pltpu.VMEM((nseq, nch, 8, GATE_COLS), F32),
            pltpu.VMEM((nseq, DN_HEADS, DN_HEAD_DIM, DN_HEAD_DIM), F32),
            pltpu.VMEM((nseq, 8, 3 * DN_WIDTH), F32),
        ],
        compiler_params=pltpu.CompilerParams(
            dimension_semantics=("parallel", "arbitrary"), vmem_limit_bytes=VMEM_LIMIT),
        name="deltanet",
    )(main3d, gates3d, conv_w, gpar, normg)


def _swa_kernel(sinks_ref, q_ref, kp_ref, kc_ref, vp_ref, vc_ref, out_ref, bias_ref):
    n = pl.program_id(1)
    blk = SWA_BLOCK
    d = SWA_HEAD_DIM
    grp = SWA_Q_HEADS // SWA_KV_HEADS

    @pl.when((pl.program_id(0) == 0) & (n == 0))
    def _():
        qi = lax.broadcasted_iota(I32, (blk, 2 * blk), 0)
        kj = lax.broadcasted_iota(I32, (blk, 2 * blk), 1)
        dist = qi + blk - kj
        valid = (dist >= 0) & (dist < SWA_WINDOW)
        dist_f = dist.astype(F32)
        for hq in range(SWA_Q_HEADS):
            slope = 2.0 ** (-8.0 * (hq + 1.0) / SWA_Q_HEADS)
            bias_ref[hq] = jnp.where(valid, -slope * dist_f, NEG_INF)

    kcol = lax.broadcasted_iota(I32, (1, 2 * blk), 1)
    colmask = jnp.where((kcol >= blk) | (n > 0), 0.0, NEG_INF)
    q_all = q_ref[...] * (d ** -0.5)
    kband = [jnp.concatenate([kp_ref[:, hk * d:(hk + 1) * d], kc_ref[:, hk * d:(hk + 1) * d]], axis=0)
             for hk in range(SWA_KV_HEADS)]
    vband = [jnp.concatenate([vp_ref[:, hk * d:(hk + 1) * d], vc_ref[:, hk * d:(hk + 1) * d]], axis=0)
             for hk in range(SWA_KV_HEADS)]
    heads = range(SWA_Q_HEADS)
    scores = [_mm_nt(q_all[:, hq * d:(hq + 1) * d], kband[hq // grp]) for hq in heads]
    probs, denoms = [], []
    for hq in heads:
        s = scores[hq] + bias_ref[hq] + colmask
        sink = sinks_ref[hq]
        m = jnp.maximum(jnp.max(s, axis=-1, keepdims=True), sink)
        p = jnp.exp(s - m)
        denoms.append(jnp.sum(p, axis=-1, keepdims=True) + jnp.exp(sink - m))
        probs.append(p.astype(BF16))
    outs = [_dot(probs[hq], vband[hq // grp]) / denoms[hq] for hq in heads]
    out_ref[...] = jnp.concatenate(outs, axis=-1).astype(out_ref.dtype)


def _swa(main3d, sinks):
    b, t, _ = main3d.shape
    blk = SWA_BLOCK
    qb = COL_SQ // SWA_WIDTH
    kb = COL_SK // SWA_KV_WIDTH
    vb = COL_SV // SWA_KV_WIDTH
    grid_spec = pltpu.PrefetchScalarGridSpec(
        num_scalar_prefetch=1,
        grid=(b, t // blk),
        in_specs=[
            pl.BlockSpec((None, blk, SWA_WIDTH), lambda bi, ni, s: (bi, ni, qb)),
            pl.BlockSpec((None, blk, SWA_KV_WIDTH), lambda bi, ni, s: (bi, jnp.maximum(ni - 1, 0), kb)),
            pl.BlockSpec((None, blk, SWA_KV_WIDTH), lambda bi, ni, s: (bi, ni, kb)),
            pl.BlockSpec((None, blk, SWA_KV_WIDTH), lambda bi, ni, s: (bi, jnp.maximum(ni - 1, 0), vb)),
            pl.BlockSpec((None, blk, SWA_KV_WIDTH), lambda bi, ni, s: (bi, ni, vb)),
        ],
        out_specs=pl.BlockSpec((None, blk, SWA_WIDTH), lambda bi, ni, s: (bi, ni, 0)),
        scratch_shapes=[pltpu.VMEM((SWA_Q_HEADS, blk, 2 * blk), F32)],
    )
    return pl.pallas_call(
        _swa_kernel,
        grid_spec=grid_spec,
        out_shape=jax.ShapeDtypeStruct((b, t, SWA_WIDTH), BF16),
        compiler_params=pltpu.CompilerParams(
            dimension_semantics=("arbitrary", "arbitrary"), vmem_limit_bytes=VMEM_LIMIT),
        name="swa",
    )(sinks, main3d, main3d, main3d, main3d, main3d)


def _layer_norm(y, g, b):
    mu = jnp.mean(y, axis=-1, keepdims=True)
    yc = y - mu
    var = jnp.mean(yc * yc, axis=-1, keepdims=True)
    return yc * lax.rsqrt(var + LN_EPS) * g + b


def _pack_bf16_pair(lo, hi):
    lo_bits = lax.bitcast_convert_type(lo.astype(BF16).astype(F32), U32)
    hi_bits = lax.bitcast_convert_type(hi.astype(BF16).astype(F32), U32)
    return (hi_bits & jnp.uint32(0xFFFF0000)) | (lo_bits >> 16)


def _unpack_bf16_pair(packed):
    lo = lax.bitcast_convert_type(packed << 16, F32)
    hi = lax.bitcast_convert_type(packed & jnp.uint32(0xFFFF0000), F32)
    return lo, hi


def _post_mix_kernel(x_ref, dn_ref, swa_ref, wo_dn_ref, wo_swa_ref, g_ref, b_ref,
                     rw_ref, sg_ref, su_ref, sd_ref,
                     base_ref, xpk_ref, logit_ref):
    mix = _dot(dn_ref[...], wo_dn_ref[...]) + _dot(swa_ref[...], wo_swa_ref[...])
    x1 = _layer_norm(DEEPNORM_ALPHA * x_ref[...] + mix, g_ref[...], b_ref[...])
    half = D_MODEL // 2
    xpk_ref[...] = _pack_bf16_pair(x1[:, :half], x1[:, half:])
    xh = x1.astype(BF16)
    logit_ref[...] = lax.dot_general(rw_ref[...], xh, (((1,), (1,)), ((), ())), preferred_element_type=F32)
    hmid = _silu(_dot(xh, sg_ref[...])) * _dot(xh, su_ref[...])
    base_ref[...] = DEEPNORM_ALPHA * x1 + _dot(hmid.astype(BF16), sd_ref[...])


def _post_mix(x2d, dn2d, swa2d, wo_dn, wo_swa, ln_g, ln_b, rw_t, sg, su, sd):
    n = x2d.shape[0]
    tm = TM_POST
    full = lambda shape: pl.BlockSpec(shape, lambda i: (0, 0))
    return pl.pallas_call(
        _post_mix_kernel,
        grid=(n // tm,),
        in_specs=[
            pl.BlockSpec((tm, D_MODEL), lambda i: (i, 0)),
            pl.BlockSpec((tm, DN_WIDTH), lambda i: (i, 0)),
            pl.BlockSpec((tm, SWA_WIDTH), lambda i: (i, 0)),
            full((DN_WIDTH, D_MODEL)), full((SWA_WIDTH, D_MODEL)),
            full((1, D_MODEL)), full((1, D_MODEL)),
            full((N_EXPERTS, D_MODEL)),
            full((D_MODEL, SHARED_FF)), full((D_MODEL, SHARED_FF)), full((SHARED_FF, D_MODEL)),
        ],
        out_specs=[
            pl.BlockSpec((tm, D_MODEL), lambda i: (i, 0)),
            pl.BlockSpec((tm, D_MODEL // 2), lambda i: (i, 0)),
            pl.BlockSpec((N_EXPERTS, tm), lambda i: (0, i)),
        ],
        out_shape=[
            jax.ShapeDtypeStruct((n, D_MODEL), F32),
            jax.ShapeDtypeStruct((n, D_MODEL // 2), U32),
            jax.ShapeDtypeStruct((N_EXPERTS, n), F32),
        ],
        compiler_params=pltpu.CompilerParams(
            dimension_semantics=("parallel",), vmem_limit_bytes=VMEM_LIMIT),
        name="post_mix",
    )(x2d, dn2d, swa2d, wo_dn, wo_swa, ln_g, ln_b, rw_t, sg, su, sd)


def _route_kernel(lg_ref, bias_ref, eidx_ref, gate_ref, rank_ref, cnt_ref, carry_ref, pick_ref):
    @pl.when(pl.program_id(0) == 0)
    def _():
        carry_ref[...] = jnp.zeros_like(carry_ref)

    tt = lg_ref.shape[1]
    scores = _sigmoid(lg_ref[...])
    sel = scores + bias_ref[...]

    iog = lax.broadcasted_iota(I32, (GROUP_SIZE, tt), 0)
    grp_rows = []
    for g in range(N_GROUPS):
        blk = sel[g * GROUP_SIZE:(g + 1) * GROUP_SIZE, :]
        m1 = jnp.max(blk, axis=0, keepdims=True)
        i1 = jnp.min(jnp.where(blk == m1, iog, GROUP_SIZE), axis=0, keepdims=True)
        m2 = jnp.max(jnp.where(iog == i1, NEG_INF, blk), axis=0, keepdims=True)
        grp_rows.append(m1 + m2)
    gs = jnp.concatenate(grp_rows, axis=0)

    io8 = lax.broadcasted_iota(I32, (N_GROUPS, tt), 0)
    gsel = jnp.zeros((N_GROUPS, tt), F32)
    for _ in range(TOPK_GROUPS):
        mg = jnp.max(gs, axis=0, keepdims=True)
        ig = jnp.min(jnp.where(gs == mg, io8, N_GROUPS), axis=0, keepdims=True)
        hit = io8 == ig
        gsel = jnp.where(hit, 1.0, gsel)
        gs = jnp.where(hit, NEG_INF, gs)

    val = jnp.concatenate(
        [jnp.where(gsel[g:g + 1, :] > 0.0, sel[g * GROUP_SIZE:(g + 1) * GROUP_SIZE, :], NEG_INF)
         for g in range(N_GROUPS)], axis=0)

    ioe = lax.broadcasted_iota(I32, (N_EXPERTS, tt), 0)
    v = val
    for _ in range(TOP_K):
        m = jnp.max(v, axis=0, keepdims=True)
        v = jnp.where(v >= m, NEG_INF, v)
    picked = jnp.where(val >= m, 1.0, 0.0)
    pick_ref[...] = picked
    n_off = jnp.max(jnp.abs(jnp.sum(picked, axis=0, keepdims=True) - TOP_K))

    @pl.when(n_off > 0.0)
    def _():
        v = val
        onehot = jnp.zeros((N_EXPERTS, tt), F32)
        for _ in range(TOP_K):
            m = jnp.max(v, axis=0, keepdims=True)
            ik = jnp.min(jnp.where(v == m, ioe, N_EXPERTS), axis=0, keepdims=True)
            hit = ioe == ik
            v = jnp.where(hit, NEG_INF, v)
            onehot = jnp.where(hit, 1.0, onehot)
        pick_ref[...] = onehot

    onehot = pick_ref[...]
    oh16 = onehot.astype(BF16)
    ei = lax.broadcasted_iota(I32, (N_EXPERTS, N_EXPERTS), 0)
    ej = lax.broadcasted_iota(I32, (N_EXPERTS, N_EXPERTS), 1)
    slot = _dot(jnp.where(ej < ei, 1.0, 0.0).astype(BF16), oh16)
    ti = lax.broadcasted_iota(I32, (tt, tt), 0)
    tj = lax.broadcasted_iota(I32, (tt, tt), 1)
    cum = _dot(oh16, jnp.where(ti < tj, 1.0, 0.0).astype(BF16)) + jnp.broadcast_to(
        carry_ref[:, 0:1], (N_EXPERTS, tt))
    key = jnp.where(onehot > 0.0, slot, float(TOP_K))
    packed = ioe.astype(F32) + float(N_EXPERTS) * cum
    packed_rows, gate_rows = [], []
    for k in range(TOP_K):
        at_k = key == float(k)
        packed_rows.append(jnp.sum(jnp.where(at_k, packed, 0.0), axis=0, keepdims=True))
        gate_rows.append(jnp.sum(jnp.where(at_k, scores, 0.0), axis=0, keepdims=True))
    gsum = gate_rows[0]
    for r in gate_rows[1:]:
        gsum = gsum + r
    gate_ref[...] = jnp.concatenate(gate_rows, axis=0) / gsum * ROUTED_SCALE
    packed_i = jnp.concatenate(packed_rows, axis=0).astype(I32)
    eidx_ref[...] = packed_i & (N_EXPERTS - 1)
    rank_ref[...] = packed_i >> (N_EXPERTS.bit_length() - 1)
    carry_ref[...] = carry_ref[...] + jnp.broadcast_to(
        jnp.sum(onehot, axis=1, keepdims=True), carry_ref.shape)
    cnt_ref[...] = carry_ref[...].astype(I32)


def _route(logits_t, bias_col):
    n = logits_t.shape[1]
    tt = TT_ROUTE
    row_spec = pl.BlockSpec((TOP_K, tt), lambda i: (0, i))
    return pl.pallas_call(
        _route_kernel,
        grid=(n // tt,),
        in_specs=[
            pl.BlockSpec((N_EXPERTS, tt), lambda i: (0, i)),
            pl.BlockSpec((N_EXPERTS, 1), lambda i: (0, 0)),
        ],
        out_specs=[row_spec, row_spec, row_spec,
                   pl.BlockSpec((N_EXPERTS, 128), lambda i: (0, 0))],
        out_shape=[
            jax.ShapeDtypeStruct((TOP_K, n), I32),
            jax.ShapeDtypeStruct((TOP_K, n), F32),
            jax.ShapeDtypeStruct((TOP_K, n), I32),
            jax.ShapeDtypeStruct((N_EXPERTS, 128), I32),
        ],
        scratch_shapes=[pltpu.VMEM((N_EXPERTS, 128), F32), pltpu.VMEM((N_EXPERTS, tt), F32)],
        compiler_params=pltpu.CompilerParams(
            dimension_semantics=("arbitrary",), vmem_limit_bytes=VMEM_LIMIT),
        name="route",
    )(logits_t, bias_col)


def _place_kernel(eidx_ref, rank_ref, pstart_ref, dest_ref):
    tt = eidx_ref.shape[1]
    ioe = lax.broadcasted_iota(I32, (N_EXPERTS, tt), 0)
    pstart = pstart_ref[...]
    rows = [jnp.sum(jnp.where(ioe == eidx_ref[k:k + 1, :], pstart, 0.0), axis=0, keepdims=True)
            for k in range(TOP_K)]
    dest_ref[...] = jnp.concatenate(rows, axis=0).astype(I32) + rank_ref[...]


def _place(eidx, rank, pstart_col):
    n = eidx.shape[1]
    tt = TT_ROUTE
    row_spec = pl.BlockSpec((TOP_K, tt), lambda i: (0, i))
    return pl.pallas_call(
        _place_kernel,
        grid=(n // tt,),
        in_specs=[row_spec, row_spec, pl.BlockSpec((N_EXPERTS, 1), lambda i: (0, 0))],
        out_specs=row_spec,
        out_shape=jax.ShapeDtypeStruct((TOP_K, n), I32),
        compiler_params=pltpu.CompilerParams(
            dimension_semantics=("parallel",), vmem_limit_bytes=VMEM_LIMIT),
        name="place",
    )(eidx, rank, pstart_col)


def _sc_mesh():
    return plsc.VectorSubcoreMesh(core_axis_name="c", subcore_axis_name="s",
                                  num_cores=SC_NC, num_subcores=SC_NS)


def _sc_scatter_rows(rows, idx, nrows_out):
    n, d = rows.shape
    nk = idx.shape[0]
    per_w = n // SC_NW
    nwin = per_w // SC_WIN
    assert per_w * SC_NW == n and nwin * SC_WIN == per_w and nwin % 2 == 0

    @functools.partial(
        pl.kernel, mesh=_sc_mesh(),
        out_type=jax.ShapeDtypeStruct((nrows_out, d), rows.dtype),
        scratch_types=[
            pltpu.VMEM((nwin, nk, SC_WIN), I32),
            pltpu.VMEM((2, SC_WIN, d), rows.dtype),
            pltpu.SemaphoreType.DMA((2,)),
            pltpu.SemaphoreType.DMA((2,)),
        ],
        compiler_params=pltpu.CompilerParams(use_tc_tiling_on_sc=True),
        name="sc_scatter_rows",
    )
    def scatter_kernel(rows_hbm, idx_hbm, out_hbm, idx_v, rows_v, lsem, ssem):
        wid = lax.axis_index("s") * SC_NC + lax.axis_index("c")
        base = wid * per_w
        pltpu.sync_copy(idx_hbm.at[wid], idx_v)

        def load(w, slot):
            return pltpu.make_async_copy(
                rows_hbm.at[pl.ds(base + w * SC_WIN, SC_WIN)], rows_v.at[slot], lsem.at[slot])

        def scat(w, k, slot):
            return pltpu.make_async_copy(rows_v.at[slot], out_hbm.at[idx_v.at[w, k]], ssem.at[slot])

        load(0, 0).start()

        @pl.loop(0, nwin, step=2)
        def _(w0):
            for slot in range(2):
                w = w0 + slot
                load(w, slot).wait()

                @pl.when(w + 1 < nwin)
                def _():
                    @pl.when(w >= 1)
                    def _():
                        for k in range(nk):
                            scat(w - 1, k, 1 - slot).wait()
                    load(w + 1, 1 - slot).start()

                for k in range(nk):
                    scat(w, k, slot).start()

        for k in range(nk):
            scat(nwin - 2, k, 0).wait()
        for k in range(nk):
            scat(nwin - 1, k, 1).wait()

    idx4 = idx.reshape(nk, SC_NW, nwin, SC_WIN).transpose(1, 2, 0, 3)
    return scatter_kernel(rows, idx4)


def _expert_kernel(gstart_ref, cnt_ref, xs_hbm, wg_ref, wu_ref, wd_ref, y_hbm,
                   wgb_ref, wub_ref, wdb_ref, xbuf_ref, ybuf_ref, xsem, ysem):
    e = pl.program_id(0)
    ne = pl.num_programs(0)
    bm = xbuf_ref.shape[1]
    nblk = y_hbm.shape[0] // bm
    half = D_MODEL // 2
    g_lo = gstart_ref[e]
    g_hi = gstart_ref[e + 1]
    g_end = gstart_ref[ne]

    def x_copy(g, slot):
        return pltpu.make_async_copy(xs_hbm.at[pl.ds(g * bm, bm), :], xbuf_ref.at[slot], xsem.at[slot])

    def y_copy(g, slot):
        return pltpu.make_async_copy(ybuf_ref.at[slot], y_hbm.at[pl.ds(g * bm, bm), :], ysem.at[slot])

    nslot = xbuf_ref.shape[0]

    ahead = nslot - EXP_GROUP

    @pl.when(e == 0)
    def _():
        for g0 in range(ahead):
            @pl.when(g0 < g_end)
            def _():
                x_copy(g0, g0).start()

    @pl.when(g_hi > g_lo)
    def _():
        wgb_ref[...] = wg_ref[...].astype(BF16)
        wub_ref[...] = wu_ref[...].astype(BF16)
        wdb_ref[...] = wd_ref[...].astype(BF16)

    row = lax.broadcasted_iota(I32, (bm, half), 0)

    def acquire(g):
        x_copy(g, g % nslot).wait()

        @pl.when(g + ahead < g_end)
        def _():
            x_copy(g + ahead, (g + ahead) % nslot).start()

        @pl.when(g >= nslot)
        def _():
            y_copy(g - nslot, g % nslot).wait()

    def load(g):
        n_valid = cnt_ref[e] - (g - g_lo) * bm
        x_lo, x_hi = _unpack_bf16_pair(jnp.where(row < n_valid, xbuf_ref[g % nslot], jnp.uint32(0)))
        return x_lo.astype(BF16), x_hi.astype(BF16)

    def gate_up(x):
        x_lo, x_hi = x
        gate = _dot(x_lo, wgb_ref[:half, :]) + _dot(x_hi, wgb_ref[half:, :])
        up = _dot(x_lo, wub_ref[:half, :]) + _dot(x_hi, wub_ref[half:, :])
        return gate, up

    def down(gu):
        gate, up = gu
        return _dot((_silu(gate) * up).astype(BF16), wdb_ref[...])

    def store(g, y):
        ybuf_ref[g % nslot] = _pack_bf16_pair(y[:, :half], y[:, half:])
        y_copy(g, g % nslot).start()

    def run_blocks(g, count):
        for j in range(count):
            acquire(g + j)
        gus = [gate_up(load(g + j)) for j in range(count)]
        ys = [down(gu) for gu in gus]
        for j in range(count):
            store(g + j, ys[j])

    def full_group(p, carry):
        run_blocks(g_lo + EXP_GROUP * p, EXP_GROUP)
        return carry

    n_own = g_hi - g_lo
    lax.fori_loop(0, n_own // EXP_GROUP, full_group, 0)
    size = EXP_GROUP // 2
    while size >= 1:
        @pl.when(n_own % (2 * size) >= size)
        def _(size=size):
            run_blocks(g_lo + n_own // (2 * size) * (2 * size), size)
        size //= 2

    @pl.when(e == ne - 1)
    def _():
        for back in range(nslot, 0, -1):
            @pl.when(g_end >= back)
            def _():
                y_copy(g_end - back, (g_end - back) % nslot).wait()

        ybuf_ref[0] = jnp.zeros((bm, half), U32)

        def fill(g, carry):
            y_copy(g, 0).start()
            return carry

        def drain(g, carry):
            y_copy(g, 0).wait()
            return carry

        lax.fori_loop(g_end, nblk, fill, 0)
        lax.fori_loop(g_end, nblk, drain, 0)


def _experts(gstart, counts, xs, w_gate, w_up, w_down):
    bm = BM_EXP
    nblk = xs.shape[0] // bm
    half = D_MODEL // 2
    grid_spec = pltpu.PrefetchScalarGridSpec(
        num_scalar_prefetch=2,
        grid=(N_EXPERTS,),
        in_specs=[
            pl.BlockSpec(memory_space=pl.ANY),
            pl.BlockSpec((None, D_MODEL, EXPERT_FF), lambda e, gs, cn: (e, 0, 0)),
            pl.BlockSpec((None, D_MODEL, EXPERT_FF), lambda e, gs, cn: (e, 0, 0)),
            pl.BlockSpec((None, EXPERT_FF, D_MODEL), lambda e, gs, cn: (e, 0, 0)),
        ],
        out_specs=pl.BlockSpec(memory_space=pl.ANY),
        scratch_shapes=[
            pltpu.VMEM((D_MODEL, EXPERT_FF), BF16),
            pltpu.VMEM((D_MODEL, EXPERT_FF), BF16),
            pltpu.VMEM((EXPERT_FF, D_MODEL), BF16),
            pltpu.VMEM((EXP_SLOTS, bm, half), U32),
            pltpu.VMEM((EXP_SLOTS, bm, half), U32),
            pltpu.SemaphoreType.DMA((EXP_SLOTS,)),
            pltpu.SemaphoreType.DMA((EXP_SLOTS,)),
        ],
    )
    return pl.pallas_call(
        _expert_kernel,
        grid_spec=grid_spec,
        out_shape=jax.ShapeDtypeStruct((nblk * bm, half), U32),
        compiler_params=pltpu.CompilerParams(
            dimension_semantics=("arbitrary",), vmem_limit_bytes=VMEM_LIMIT),
        name="experts",
    )(gstart, counts, xs, w_gate, w_up, w_down)


def _sc_gather_rows(table, idx):
    nrows = idx.shape[0]
    d = table.shape[1]
    per_w = nrows // SC_NW
    nwin = per_w // SC_WIN
    assert per_w * SC_NW == nrows and nwin * SC_WIN == per_w and nwin % 2 == 0
    @functools.partial(
        pl.kernel, mesh=_sc_mesh(),
        out_type=jax.ShapeDtypeStruct((nrows, d), table.dtype),
        scratch_types=[
            pltpu.VMEM((nwin, SC_WIN), I32),
            pltpu.VMEM((2, SC_WIN, d), table.dtype),
            pltpu.SemaphoreType.DMA((2,)),
            pltpu.SemaphoreType.DMA((2,)),
        ],
        compiler_params=pltpu.CompilerParams(use_tc_tiling_on_sc=True),
        name="sc_gather_rows",
    )
    def gather_kernel(table_hbm, idx_hbm, out_hbm, idx_v, rows_v, gsem, wsem):
        wid = lax.axis_index("s") * SC_NC + lax.axis_index("c")
        base = wid * per_w
        pltpu.sync_copy(idx_hbm.at[wid], idx_v)

        def gather(w, slot):
            return pltpu.make_async_copy(table_hbm.at[idx_v.at[w]], rows_v.at[slot], gsem.at[slot])

        def put(w, slot):
            return pltpu.make_async_copy(
                rows_v.at[slot], out_hbm.at[pl.ds(base + w * SC_WIN, SC_WIN)], wsem.at[slot])

        gather(0, 0).start()

        @pl.loop(0, nwin, step=2)
        def _(w0):
            for slot in range(2):
                w = w0 + slot
                gather(w, slot).wait()

                @pl.when(w + 1 < nwin)
                def _():
                    @pl.when(w >= 1)
                    def _():
                        put(w - 1, 1 - slot).wait()
                    gather(w + 1, 1 - slot).start()

                put(w, slot).start()

        put(nwin - 2, 0).wait()
        put(nwin - 1, 1).wait()

    return gather_kernel(table, idx.reshape(SC_NW, nwin, SC_WIN))


def _combine_kernel(y_ref, base_ref, gate_ref, g_ref, b_ref, out_ref):
    half = D_MODEL // 2
    gates = gate_ref[...]
    acc_lo = base_ref[:, :half]
    acc_hi = base_ref[:, half:]
    for k in range(TOP_K):
        y_lo, y_hi = _unpack_bf16_pair(y_ref[k])
        gk = gates[:, k:k + 1]
        acc_lo = acc_lo + gk * y_lo
        acc_hi = acc_hi + gk * y_hi
    mu = (jnp.sum(acc_lo, axis=-1, keepdims=True) + jnp.sum(acc_hi, axis=-1, keepdims=True)) / D_MODEL
    c_lo = acc_lo - mu
    c_hi = acc_hi - mu
    var = (jnp.sum(c_lo * c_lo, axis=-1, keepdims=True)
           + jnp.sum(c_hi * c_hi, axis=-1, keepdims=True)) / D_MODEL
    inv = lax.rsqrt(var + LN_EPS)
    out_ref[:, :half] = c_lo * inv * g_ref[:, :half] + b_ref[:, :half]
    out_ref[:, half:] = c_hi * inv * g_ref[:, half:] + b_ref[:, half:]


def _combine(ybuf, base, gate_tok, ln_g, ln_b):
    n = base.shape[0]
    tt = TT_COMB
    half = D_MODEL // 2
    return pl.pallas_call(
        _combine_kernel,
        grid=(n // tt,),
        in_specs=[
            pl.BlockSpec((TOP_K, tt, half), lambda i: (0, i, 0)),
            pl.BlockSpec((tt, D_MODEL), lambda i: (i, 0)),
            pl.BlockSpec((tt, TOP_K), lambda i: (i, 0)),
            pl.BlockSpec((1, D_MODEL), lambda i: (0, 0)),
            pl.BlockSpec((1, D_MODEL), lambda i: (0, 0)),
        ],
        out_specs=pl.BlockSpec((tt, D_MODEL), lambda i: (i, 0)),
        out_shape=jax.ShapeDtypeStruct((n, D_MODEL), F32),
        compiler_params=pltpu.CompilerParams(
            dimension_semantics=("parallel",), vmem_limit_bytes=VMEM_LIMIT),
        name="combine",
    )(ybuf, base, gate_tok, ln_g, ln_b)


def _regroup_w_in(w_in):
    o = 0
    cols = {}
    for name, width in (("dnq", DN_WIDTH), ("dnk", DN_WIDTH), ("dnv", DN_WIDTH), ("sq", SWA_WIDTH),
                        ("sk", SWA_KV_WIDTH), ("sv", SWA_KV_WIDTH), ("z", DN_WIDTH),
                        ("b", DN_HEADS), ("a", DN_HEADS)):
        cols[name] = w_in[:, o:o + width]
        o += width
    w_main = jnp.concatenate([cols[k] for k in ("dnq", "dnk", "dnv", "z", "sq", "sk", "sv")], axis=1)
    w_gates = jnp.concatenate(
        [cols["b"], cols["a"], jnp.zeros((D_MODEL, GATE_COLS - 2 * DN_HEADS), w_in.dtype)], axis=1)
    return w_main.astype(BF16), w_gates.astype(BF16)


def _layer(x, w_in, conv_w, a_log, dt_bias, dn_norm_g, sinks, w_out, ln1_g, ln1_b,
           router_w, router_bias, w_gate, w_up, w_down, sh_gate, sh_up, sh_down, ln2_g, ln2_b):
    b, t, d = x.shape
    n = b * t
    x2d = x.reshape(n, d)

    w_main, w_gates = _regroup_w_in(w_in)
    main, gates = _in_proj(x2d, w_main, w_gates)
    main3d = main.reshape(b, t, MAIN_COLS)

    pad = jnp.zeros((GATE_COLS - 2 * DN_HEADS,), F32)
    gpar = jnp.stack([jnp.concatenate([jnp.zeros((DN_HEADS,), F32), a_log.astype(F32), pad]),
                      jnp.concatenate([jnp.zeros((DN_HEADS,), F32), dt_bias.astype(F32), pad])])
    dn_out = _deltanet(main3d, gates.reshape(b, t, GATE_COLS), conv_w.astype(F32), gpar,
                       dn_norm_g.astype(F32).reshape(1, DN_HEAD_DIM))
    swa_out = _swa(main3d, sinks.astype(F32))

    base, xpk, logits_t = _post_mix(
        x2d, dn_out.reshape(n, DN_WIDTH), swa_out.reshape(n, SWA_WIDTH),
        w_out[:DN_WIDTH].astype(BF16), w_out[DN_WIDTH:].astype(BF16),
        ln1_g.reshape(1, d).astype(F32), ln1_b.reshape(1, d).astype(F32),
        router_w.T.astype(BF16),
        sh_gate.astype(BF16), sh_up.astype(BF16), sh_down.astype(BF16))

    eidx, gate, rank, cnt = _route(logits_t, router_bias.astype(F32).reshape(N_EXPERTS, 1))

    bm = BM_EXP
    counts = cnt[:, 0]
    padded = (counts + bm - 1) // bm * bm
    pend = jnp.cumsum(padded)
    pstart = pend - padded
    nblk = -(-(n * TOP_K) // bm) + N_EXPERTS
    gstart = (jnp.concatenate([pstart, pend[-1:]]) // bm).astype(I32)

    dest = _place(eidx, rank, pstart.astype(F32).reshape(N_EXPERTS, 1))
    xs = _sc_scatter_rows(xpk, dest, nblk * bm)
    ypk = _experts(gstart, counts, xs, w_gate, w_up, w_down)
    ybuf = _sc_gather_rows(ypk, dest.reshape(-1)).reshape(TOP_K, n, d // 2)
    out = _combine(ybuf, base, gate.T, ln2_g.reshape(1, d).astype(F32), ln2_b.reshape(1, d).astype(F32))
    return out.reshape(b, t, d)


def kernel(x, w_in, conv_w, a_log, dt_bias, dn_norm_g, sinks, w_out, ln1_g, ln1_b, router_w, router_bias,
           w_gate, w_up, w_down, shared_w_gate, shared_w_up, shared_w_down, ln2_g, ln2_b):
    depth = w_in.shape[0]
    for l in range(depth):
        x = _layer(x, w_in[l], conv_w[l], a_log[l], dt_bias[l], dn_norm_g[l], sinks[l], w_out[l],
                   ln1_g[l], ln1_b[l], router_w[l], router_bias[l], w_gate[l], w_up[l], w_down[l],
                   shared_w_gate[l], shared_w_up[l], shared_w_down[l], ln2_g[l], ln2_b[l])
    return x
```

```python
import functools

import jax
import jax.numpy as jnp
from jax import lax
from jax.experimental import pallas as pl
from jax.experimental.pallas import tpu as pltpu
from jax.experimental.pallas import tpu_sc as plsc

F32 = jnp.float32
BF16 = jnp.bfloat16
I32 = jnp.int32
U32 = jnp.uint32

D_MODEL = 1024
DN_HEADS = 4
DN_HEAD_DIM = 128
DN_WIDTH = DN_HEADS * DN_HEAD_DIM
CONV_WIDTH = 4
DN_CHUNK = 64
SWA_Q_HEADS = 8
SWA_KV_HEADS = 2
SWA_HEAD_DIM = 64
SWA_WIDTH = SWA_Q_HEADS * SWA_HEAD_DIM
SWA_KV_WIDTH = SWA_KV_HEADS * SWA_HEAD_DIM
SWA_WINDOW = 128
SWA_BLOCK = 128
N_EXPERTS = 256
N_GROUPS = 8
GROUP_SIZE = N_EXPERTS // N_GROUPS
TOPK_GROUPS = 4
TOP_K = 8
EXPERT_FF = 256
SHARED_FF = 256
ROUTED_SCALE = 2.5
DEEPNORM_ALPHA = 2.0 ** 0.25
LN_EPS = 1e-5
RMS_EPS = 1e-6
L2_EPS = 1e-6

COL_DNQ = 0
COL_DNK = DN_WIDTH
COL_DNV = 2 * DN_WIDTH
COL_Z = 3 * DN_WIDTH
COL_SQ = 4 * DN_WIDTH
COL_SK = COL_SQ + SWA_WIDTH
COL_SV = COL_SK + SWA_KV_WIDTH
MAIN_COLS = COL_SV + SWA_KV_WIDTH
GATE_COLS = 128

TM_PROJ = 1024
TS_DN = 256
DN_SEQS = 4
DN_A_UNROLL = 4
TM_POST = 1024
TT_ROUTE = 512
BM_EXP = 256
EXP_GROUP = 2
EXP_SLOTS = 6
TT_COMB = 512
SC_NC = 2
SC_NS = 16
SC_NW = SC_NC * SC_NS
SC_WIN = 64
VMEM_LIMIT = 56 * 1024 * 1024
NEG_INF = float("-inf")


def _dot(a, b):
    return jnp.dot(a, b, preferred_element_type=F32)


def _mm(a, b):
    return _dot(a.astype(BF16), b.astype(BF16))


def _mm_nt(a, b):
    return lax.dot_general(a.astype(BF16), b.astype(BF16), (((1,), (1,)), ((), ())),
                           preferred_element_type=F32)


def _mm_tn(a, b):
    return lax.dot_general(a.astype(BF16), b.astype(BF16), (((0,), (0,)), ((), ())),
                           preferred_element_type=F32)


def _split2(a):
    hi = a.astype(BF16)
    lo = (a - hi.astype(F32)).astype(BF16)
    return hi, lo


def _mm3(a, b):
    ah, al = _split2(a)
    bh, bl = _split2(b)
    return _dot(ah, bh) + _dot(ah, bl) + _dot(al, bh)


def _mm_exact_lhs(l_bf16, g):
    g1 = g.astype(BF16)
    r1 = g - g1.astype(F32)
    g2 = r1.astype(BF16)
    g3 = (r1 - g2.astype(F32)).astype(BF16)
    return _dot(l_bf16, g1) + _dot(l_bf16, g2) + _dot(l_bf16, g3)


def _sigmoid(x):
    return 1.0 / (1.0 + jnp.exp(-x))


def _silu(x):
    return x * _sigmoid(x)


def _in_proj_kernel(x_ref, w_ref, wg_ref, main_ref, gates_ref):
    xb = x_ref[...].astype(BF16)
    main_ref[...] = _dot(xb, w_ref[...]).astype(BF16)
    gates_ref[...] = _dot(xb, wg_ref[...])


def _in_proj(x2d, w_main, w_gates):
    n = x2d.shape[0]
    return pl.pallas_call(
        _in_proj_kernel,
        grid=(n // TM_PROJ,),
        in_specs=[
            pl.BlockSpec((TM_PROJ, D_MODEL), lambda i: (i, 0)),
            pl.BlockSpec((D_MODEL, MAIN_COLS), lambda i: (0, 0)),
            pl.BlockSpec((D_MODEL, GATE_COLS), lambda i: (0, 0)),
        ],
        out_specs=[
            pl.BlockSpec((TM_PROJ, MAIN_COLS), lambda i: (i, 0)),
            pl.BlockSpec((TM_PROJ, GATE_COLS), lambda i: (i, 0)),
        ],
        out_shape=[
            jax.ShapeDtypeStruct((n, MAIN_COLS), BF16),
            jax.ShapeDtypeStruct((n, GATE_COLS), F32),
        ],
        compiler_params=pltpu.CompilerParams(
            dimension_semantics=("parallel",), vmem_limit_bytes=VMEM_LIMIT),
        name="in_proj",
    )(x2d, w_main, w_gates)


def _dn_kernel(x_ref, gates_ref, convw_ref, gpar_ref, normg_ref, out_ref,
               xc_ref, gl_ref, gc_ref, wq_ref, u_ref, kt_ref, attn_ref, egl_ref, s_ref, hist_ref):
    nseq = x_ref.shape[0]
    ts = x_ref.shape[1]
    c = DN_CHUNK
    hd = DN_HEAD_DIM
    qkv_w = 3 * DN_WIDTH
    nch = ts // c

    @pl.when(pl.program_id(1) == 0)
    def _():
        s_ref[...] = jnp.zeros_like(s_ref)
        hist_ref[...] = jnp.zeros_like(hist_ref)

    def stage_inputs(bi, carry):
        xc_ref[bi, 0:8, :] = hist_ref[bi]
        xc_ref[bi, 8:ts + 8, :] = x_ref[bi, :, 0:qkv_w].astype(F32)
        hist_ref[bi] = xc_ref[bi, ts:ts + 8, :]
        gsl = gates_ref[bi]
        sp_in = gsl + gpar_ref[1:2, :]
        softplus = jnp.maximum(sp_in, 0.0) + jnp.log(1.0 + jnp.exp(-jnp.abs(sp_in)))
        lane = lax.broadcasted_iota(I32, gsl.shape, 1)
        gl = jnp.where(lane < DN_HEADS, _sigmoid(gsl), -jnp.exp(gpar_ref[0:1, :]) * softplus)
        gl_ref[bi] = gl
        row_in_chunk = lax.broadcasted_iota(I32, gsl.shape, 0) % c
        gc = gl
        shift = 1
        while shift < c:
            gc = gc + jnp.where(row_in_chunk >= shift, pltpu.roll(gc, shift, 0), 0.0)
            shift *= 2
        gc_ref[bi] = gc
        return carry

    lax.fori_loop(0, nseq, stage_inputs, 0)

    ii = lax.broadcasted_iota(I32, (c, c), 0)
    jj = lax.broadcasted_iota(I32, (c, c), 1)
    tri_incl = ii >= jj
    tri_strict = ii > jj
    eye = jnp.where(ii == jj, 1.0, 0.0).astype(F32)
    heads = range(DN_HEADS)

    def conv_silu(bi, r0, col):
        w = convw_ref[:, col:col + hd]
        xt = xc_ref[bi, pl.ds(r0, c + 8), col:col + hd]
        y = w[CONV_WIDTH - 1:CONV_WIDTH, :] * xt[8:8 + c, :]
        for delay in range(1, CONV_WIDTH):
            tap = CONV_WIDTH - 1 - delay
            y = y + w[tap:tap + 1, :] * pltpu.roll(xt, delay, 0)[8:8 + c, :]
        return _silu(y)

    def l2n(t, scale):
        return t * (lax.rsqrt(jnp.sum(t * t, axis=-1, keepdims=True) + L2_EPS) * scale)

    def phase_a(i, bi):
        chains = []
        for sub in range(DN_A_UNROLL):
            ci = i * DN_A_UNROLL + sub
            r0 = ci * c
            glc = gl_ref[bi, pl.ds(r0, c), :]
            gcc = gc_ref[bi, pl.ds(r0, c), :]
            gct = jnp.concatenate([gcc, gcc], axis=0).T
            egl_ref[bi, ci] = jnp.exp(gcc[c - 8:c, :])
            for h in heads:
                chains.append((ci, r0, h, glc, gcc, gct))
        nchain = len(chains)
        q = [l2n(conv_silu(bi, r0, COL_DNQ + h * hd), hd ** -0.5) for (ci, r0, h, _, _, _) in chains]
        k = [l2n(conv_silu(bi, r0, COL_DNK + h * hd), 1.0) for (ci, r0, h, _, _, _) in chains]
        v = [conv_silu(bi, r0, COL_DNV + h * hd) for (ci, r0, h, _, _, _) in chains]
        kb, vb, decay, egc = [], [], [], []
        for n_, (ci, r0, h, glc, gcc, gct) in enumerate(chains):
            beta = glc[:, h:h + 1]
            gc_col = gcc[:, DN_HEADS + h:DN_HEADS + h + 1]
            gc_row = gct[DN_HEADS + h:DN_HEADS + h + 1, 0:c]
            decay.append(jnp.where(tri_incl, jnp.exp(jnp.minimum(gc_col - gc_row, 0.0)), 0.0))
            egc.append(jnp.exp(gc_col))
            e_tail = jnp.exp(gcc[c - 1:c, DN_HEADS + h:DN_HEADS + h + 1] - gc_col)
            kb.append(k[n_] * beta)
            vb.append(v[n_] * beta)
            kt_ref[bi, ci, h] = (k[n_] * e_tail).astype(BF16)
        kq = [_mm_nt(jnp.concatenate([kb[n_], q[n_]], axis=0), k[n_]) for n_ in range(nchain)]
        a_mat = [jnp.where(tri_strict, kq[n_][0:c] * decay[n_], 0.0) for n_ in range(nchain)]
        for n_, (ci, r0, h, _, _, _) in enumerate(chains):
            attn_ref[bi, ci, h] = (kq[n_][c:2 * c] * decay[n_]).astype(BF16)
        t_inv = [eye - a for a in a_mat]
        p = a_mat
        for _ in range(5):
            p = [_mm(x, x) for x in p]
            t_inv = [t + _mm(t, x) for t, x in zip(t_inv, p)]
        for n_, (ci, r0, h, _, _, _) in enumerate(chains):
            uw = _mm3(t_inv[n_], jnp.concatenate([vb[n_], kb[n_] * egc[n_]], axis=1))
            u_ref[bi, ci, h] = uw[:, 0:hd]
            wq_ref[bi, ci, h, 0:c, :] = uw[:, hd:2 * hd].astype(BF16)
            wq_ref[bi, ci, h, c:2 * c, :] = (q[n_] * egc[n_]).astype(BF16)

    for bi in range(nseq):
        for i in range(nch // DN_A_UNROLL):
            phase_a(i, bi)

    normg = normg_ref[...]

    def phase_b(ci, carry):
        r0 = pl.multiple_of(ci * c, c)
        rows = pl.ds(r0, c)
        chains = [(bi, h) for bi in range(nseq) for h in heads]
        egl = [egl_ref[bi, ci] for bi in range(nseq)]
        s_old = [s_ref[bi, h] for bi, h in chains]
        ws = [_dot(wq_ref[bi, ci, h], s.astype(BF16)) for (bi, h), s in zip(chains, s_old)]
        v_new = [(u_ref[bi, ci, h] - w[0:c]).astype(BF16) for (bi, h), w in zip(chains, ws)]
        for n_, (bi, h) in enumerate(chains):
            s_ref[bi, h] = (s_old[n_] * egl[bi][7:8, DN_HEADS + h:DN_HEADS + h + 1]
                            + lax.dot_general(kt_ref[bi, ci, h], v_new[n_], (((0,), (0,)), ((), ())),
                                              preferred_element_type=F32))
        for n_, (bi, h) in enumerate(chains):
            o = ws[n_][c:2 * c] + _dot(attn_ref[bi, ci, h], v_new[n_])
            o = o * lax.rsqrt(jnp.mean(o * o, axis=-1, keepdims=True) + RMS_EPS) * normg
            z = x_ref[bi, rows, COL_Z + h * hd:COL_Z + (h + 1) * hd].astype(F32)
            out_ref[bi, rows, h * hd:(h + 1) * hd] = (o * _silu(z)).astype(out_ref.dtype)
        return carry

    lax.fori_loop(0, nch, phase_b, 0)


def _deltanet(main3d, gates3d, conv_w, gpar, normg):
    b, t, _ = main3d.shape
    ts = TS_DN
    nseq = DN_SEQS
    nch = ts // DN_CHUNK
    dn_in = COL_Z + DN_WIDTH
    return pl.pallas_call(
        _dn_kernel,
        grid=(b // nseq, t // ts),
        in_specs=[
            pl.BlockSpec((nseq, ts, dn_in), lambda bi, si: (bi, si, 0)),
            pl.BlockSpec((nseq, ts, GATE_COLS), lambda bi, si: (bi, si, 0)),
            pl.BlockSpec((CONV_WIDTH, 3 * DN_WIDTH), lambda bi, si: (0, 0)),
            pl.BlockSpec((2, GATE_COLS), lambda bi, si: (0, 0)),
            pl.BlockSpec((1, DN_HEAD_DIM), lambda bi, si: (0, 0)),
        ],
        out_specs=pl.BlockSpec((nseq, ts, DN_WIDTH), lambda bi, si: (bi, si, 0)),
        out_shape=jax.ShapeDtypeStruct((b, t, DN_WIDTH), BF16),
        scratch_shapes=[
            pltpu.VMEM((nseq, ts + 8, 3 * DN_WIDTH), F32),
            pltpu.VMEM((nseq, ts, GATE_COLS), F32),
            pltpu.VMEM((nseq, ts, GATE_COLS), F32),
            pltpu.VMEM((nseq, nch, DN_HEADS, 2 * DN_CHUNK, DN_HEAD_DIM), BF16),
            pltpu.VMEM((nseq, nch, DN_HEADS, DN_CHUNK, DN_HEAD_DIM), F32),
            pltpu.VMEM((nseq, nch, DN_HEADS, DN_CHUNK, DN_HEAD_DIM), BF16),
            pltpu.VMEM((nseq, nch, DN_HEADS, DN_CHUNK, DN_CHUNK), BF16),
            pltpu.VMEM((nseq, nch, 8, GATE_COLS), F32),
            pltpu.VMEM((nseq, DN_HEADS, DN_HEAD_DIM, DN_HEAD_DIM), F32),
            pltpu.VMEM((nseq, 8, 3 * DN_WIDTH), F32),
        ],
        compiler_params=pltpu.CompilerParams(
            dimension_semantics=("parallel", "arbitrary"), vmem_limit_bytes=VMEM_LIMIT),
        name="deltanet",
    )(main3d, gates3d, conv_w, gpar, normg)


def _swa_kernel(sinks_ref, q_ref, kp_ref, kc_ref, vp_ref, vc_ref, out_ref, bias_ref):
    n = pl.program_id(1)
    blk = SWA_BLOCK
    d = SWA_HEAD_DIM
    grp = SWA_Q_HEADS // SWA_KV_HEADS

    @pl.when((pl.program_id(0) == 0) & (n == 0))
    def _():
        qi = lax.broadcasted_iota(I32, (blk, 2 * blk), 0)
        kj = lax.broadcasted_iota(I32, (blk, 2 * blk), 1)
        dist = qi + blk - kj
        valid = (dist >= 0) & (dist < SWA_WINDOW)
        dist_f = dist.astype(F32)
        for hq in range(SWA_Q_HEADS):
            slope = 2.0 ** (-8.0 * (hq + 1.0) / SWA_Q_HEADS)
            bias_ref[hq] = jnp.where(valid, -slope * dist_f, NEG_INF)

    kcol = lax.broadcasted_iota(I32, (1, 2 * blk), 1)
    colmask = jnp.where((kcol >= blk) | (n > 0), 0.0, NEG_INF)
    q_all = q_ref[...] * (d ** -0.5)
    kband = [jnp.concatenate([kp_ref[:, hk * d:(hk + 1) * d], kc_ref[:, hk * d:(hk + 1) * d]], axis=0)
             for hk in range(SWA_KV_HEADS)]
    vband = [jnp.concatenate([vp_ref[:, hk * d:(hk + 1) * d], vc_ref[:, hk * d:(hk + 1) * d]], axis=0)
             for hk in range(SWA_KV_HEADS)]
    heads = range(SWA_Q_HEADS)
    scores = [_mm_nt(q_all[:, hq * d:(hq + 1) * d], kband[hq // grp]) for hq in heads]
    probs, denoms = [], []
    for hq in heads:
        s = scores[hq] + bias_ref[hq] + colmask
        sink = sinks_ref[hq]
        m = jnp.maximum(jnp.max(s, axis=-1, keepdims=True), sink)
        p = jnp.exp(s - m)
        denoms.append(jnp.sum(p, axis=-1, keepdims=True) + jnp.exp(sink - m))
        probs.append(p.astype(BF16))
    outs = [_dot(probs[hq], vband[hq // grp]) / denoms[hq] for hq in heads]
    out_ref[...] = jnp.concatenate(outs, axis=-1).astype(out_ref.dtype)


def _swa(main3d, sinks):
    b, t, _ = main3d.shape
    blk = SWA_BLOCK
    qb = COL_SQ // SWA_WIDTH
    kb = COL_SK // SWA_KV_WIDTH
    vb = COL_SV // SWA_KV_WIDTH
    grid_spec = pltpu.PrefetchScalarGridSpec(
        num_scalar_prefetch=1,
        grid=(b, t // blk),
        in_specs=[
            pl.BlockSpec((None, blk, SWA_WIDTH), lambda bi, ni, s: (bi, ni, qb)),
            pl.BlockSpec((None, blk, SWA_KV_WIDTH), lambda bi, ni, s: (bi, jnp.maximum(ni - 1, 0), kb)),
            pl.BlockSpec((None, blk, SWA_KV_WIDTH), lambda bi, ni, s: (bi, ni, kb)),
            pl.BlockSpec((None, blk, SWA_KV_WIDTH), lambda bi, ni, s: (bi, jnp.maximum(ni - 1, 0), vb)),
            pl.BlockSpec((None, blk, SWA_KV_WIDTH), lambda bi, ni, s: (bi, ni, vb)),
        ],
        out_specs=pl.BlockSpec((None, blk, SWA_WIDTH), lambda bi, ni, s: (bi, ni, 0)),
        scratch_shapes=[pltpu.VMEM((SWA_Q_HEADS, blk, 2 * blk), F32)],
    )
    return pl.pallas_call(
        _swa_kernel,
        grid_spec=grid_spec,
        out_shape=jax.ShapeDtypeStruct((b, t, SWA_WIDTH), BF16),
        compiler_params=pltpu.CompilerParams(
            dimension_semantics=("arbitrary", "arbitrary"), vmem_limit_bytes=VMEM_LIMIT),
        name="swa",
    )(sinks, main3d, main3d, main3d, main3d, main3d)


def _layer_norm(y, g, b):
    mu = jnp.mean(y, axis=-1, keepdims=True)
    yc = y - mu
    var = jnp.mean(yc * yc, axis=-1, keepdims=True)
    return yc * lax.rsqrt(var + LN_EPS) * g + b


def _pack_bf16_pair(lo, hi):
    lo_bits = lax.bitcast_convert_type(lo.astype(BF16).astype(F32), U32)
    hi_bits = lax.bitcast_convert_type(hi.astype(BF16).astype(F32), U32)
    return (hi_bits & jnp.uint32(0xFFFF0000)) | (lo_bits >> 16)


def _unpack_bf16_pair(packed):
    lo = lax.bitcast_convert_type(packed << 16, F32)
    hi = lax.bitcast_convert_type(packed & jnp.uint32(0xFFFF0000), F32)
    return lo, hi


def _post_mix_kernel(x_ref, dn_ref, swa_ref, wo_dn_ref, wo_swa_ref, g_ref, b_ref,
                     rw_ref, sg_ref, su_ref, sd_ref,
                     base_ref, xpk_ref, logit_ref):
    mix = _dot(dn_ref[...], wo_dn_ref[...]) + _dot(swa_ref[...], wo_swa_ref[...])
    x1 = _layer_norm(DEEPNORM_ALPHA * x_ref[...] + mix, g_ref[...], b_ref[...])
    half = D_MODEL // 2
    xpk_ref[...] = _pack_bf16_pair(x1[:, :half], x1[:, half:])
    xh = x1.astype(BF16)
    logit_ref[...] = lax.dot_general(rw_ref[...], xh, (((1,), (1,)), ((), ())), preferred_element_type=F32)
    hmid = _silu(_dot(xh, sg_ref[...])) * _dot(xh, su_ref[...])
    base_ref[...] = DEEPNORM_ALPHA * x1 + _dot(hmid.astype(BF16), sd_ref[...])


def _post_mix(x2d, dn2d, swa2d, wo_dn, wo_swa, ln_g, ln_b, rw_t, sg, su, sd):
    n = x2d.shape[0]
    tm = TM_POST
    full = lambda shape: pl.BlockSpec(shape, lambda i: (0, 0))
    return pl.pallas_call(
        _post_mix_kernel,
        grid=(n // tm,),
        in_specs=[
            pl.BlockSpec((tm, D_MODEL), lambda i: (i, 0)),
            pl.BlockSpec((tm, DN_WIDTH), lambda i: (i, 0)),
            pl.BlockSpec((tm, SWA_WIDTH), lambda i: (i, 0)),
            full((DN_WIDTH, D_MODEL)), full((SWA_WIDTH, D_MODEL)),
            full((1, D_MODEL)), full((1, D_MODEL)),
            full((N_EXPERTS, D_MODEL)),
            full((D_MODEL, SHARED_FF)), full((D_MODEL, SHARED_FF)), full((SHARED_FF, D_MODEL)),
        ],
        out_specs=[
            pl.BlockSpec((tm, D_MODEL), lambda i: (i, 0)),
            pl.BlockSpec((tm, D_MODEL // 2), lambda i: (i, 0)),
            pl.BlockSpec((N_EXPERTS, tm), lambda i: (0, i)),
        ],
        out_shape=[
            jax.ShapeDtypeStruct((n, D_MODEL), F32),
            jax.ShapeDtypeStruct((n, D_MODEL // 2), U32),
            jax.ShapeDtypeStruct((N_EXPERTS, n), F32),
        ],
        compiler_params=pltpu.CompilerParams(
            dimension_semantics=("parallel",), vmem_limit_bytes=VMEM_LIMIT),
        name="post_mix",
    )(x2d, dn2d, swa2d, wo_dn, wo_swa, ln_g, ln_b, rw_t, sg, su, sd)


def _route_kernel(lg_ref, bias_ref, eidx_ref, gate_ref, rank_ref, cnt_ref, carry_ref, pick_ref):
    @pl.when(pl.program_id(0) == 0)
    def _():
        carry_ref[...] = jnp.zeros_like(carry_ref)

    tt = lg_ref.shape[1]
    scores = _sigmoid(lg_ref[...])
    sel = scores + bias_ref[...]

    iog = lax.broadcasted_iota(I32, (GROUP_SIZE, tt), 0)
    grp_rows = []
    for g in range(N_GROUPS):
        blk = sel[g * GROUP_SIZE:(g + 1) * GROUP_SIZE, :]
        m1 = jnp.max(blk, axis=0, keepdims=True)
        i1 = jnp.min(jnp.where(blk == m1, iog, GROUP_SIZE), axis=0, keepdims=True)
        m2 = jnp.max(jnp.where(iog == i1, NEG_INF, blk), axis=0, keepdims=True)
        grp_rows.append(m1 + m2)
    gs = jnp.concatenate(grp_rows, axis=0)

    io8 = lax.broadcasted_iota(I32, (N_GROUPS, tt), 0)
    gsel = jnp.zeros((N_GROUPS, tt), F32)
    for _ in range(TOPK_GROUPS):
        mg = jnp.max(gs, axis=0, keepdims=True)
        ig = jnp.min(jnp.where(gs == mg, io8, N_GROUPS), axis=0, keepdims=True)
        hit = io8 == ig
        gsel = jnp.where(hit, 1.0, gsel)
        gs = jnp.where(hit, NEG_INF, gs)

    val = jnp.concatenate(
        [jnp.where(gsel[g:g + 1, :] > 0.0, sel[g * GROUP_SIZE:(g + 1) * GROUP_SIZE, :], NEG_INF)
         for g in range(N_GROUPS)], axis=0)

    ioe = lax.broadcasted_iota(I32, (N_EXPERTS, tt), 0)
    v = val
    for _ in range(TOP_K):
        m = jnp.max(v, axis=0, keepdims=True)
        v = jnp.where(v >= m, NEG_INF, v)
    picked = jnp.where(val >= m, 1.0, 0.0)
    pick_ref[...] = picked
    n_off = jnp.max(jnp.abs(jnp.sum(picked, axis=0, keepdims=True) - TOP_K))

    @pl.when(n_off > 0.0)
    def _():
        v = val
        onehot = jnp.zeros((N_EXPERTS, tt), F32)
        for _ in range(TOP_K):
            m = jnp.max(v, axis=0, keepdims=True)
            ik = jnp.min(jnp.where(v == m, ioe, N_EXPERTS), axis=0, keepdims=True)
            hit = ioe == ik
            v = jnp.where(hit, NEG_INF, v)
            onehot = jnp.where(hit, 1.0, onehot)
        pick_ref[...] = onehot

    onehot = pick_ref[...]
    oh16 = onehot.astype(BF16)
    ei = lax.broadcasted_iota(I32, (N_EXPERTS, N_EXPERTS), 0)
    ej = lax.broadcasted_iota(I32, (N_EXPERTS, N_EXPERTS), 1)
    slot = _dot(jnp.where(ej < ei, 1.0, 0.0).astype(BF16), oh16)
    ti = lax.broadcasted_iota(I32, (tt, tt), 0)
    tj = lax.broadcasted_iota(I32, (tt, tt), 1)
    cum = _dot(oh16, jnp.where(ti < tj, 1.0, 0.0).astype(BF16)) + jnp.broadcast_to(
        carry_ref[:, 0:1], (N_EXPERTS, tt))
    key = jnp.where(onehot > 0.0, slot, float(TOP_K))
    packed = ioe.astype(F32) + float(N_EXPERTS) * cum
    packed_rows, gate_rows = [], []
    for k in range(TOP_K):
        at_k = key == float(k)
        packed_rows.append(jnp.sum(jnp.where(at_k, packed, 0.0), axis=0, keepdims=True))
        gate_rows.append(jnp.sum(jnp.where(at_k, scores, 0.0), axis=0, keepdims=True))
    gsum = gate_rows[0]
    for r in gate_rows[1:]:
        gsum = gsum + r
    gate_ref[...] = jnp.concatenate(gate_rows, axis=0) / gsum * ROUTED_SCALE
    packed_i = jnp.concatenate(packed_rows, axis=0).astype(I32)
    eidx_ref[...] = packed_i & (N_EXPERTS - 1)
    rank_ref[...] = packed_i >> (N_EXPERTS.bit_length() - 1)
    carry_ref[...] = carry_ref[...] + jnp.broadcast_to(
        jnp.sum(onehot, axis=1, keepdims=True), carry_ref.shape)
    cnt_ref[...] = carry_ref[...].astype(I32)


def _route(logits_t, bias_col):
    n = logits_t.shape[1]
    tt = TT_ROUTE
    row_spec = pl.BlockSpec((TOP_K, tt), lambda i: (0, i))
    return pl.pallas_call(
        _route_kernel,
        grid=(n // tt,),
        in_specs=[
            pl.BlockSpec((N_EXPERTS, tt), lambda i: (0, i)),
            pl.BlockSpec((N_EXPERTS, 1), lambda i: (0, 0)),
        ],
        out_specs=[row_spec, row_spec, row_spec,
                   pl.BlockSpec((N_EXPERTS, 128), lambda i: (0, 0))],
        out_shape=[
            jax.ShapeDtypeStruct((TOP_K, n), I32),
            jax.ShapeDtypeStruct((TOP_K, n), F32),
            jax.ShapeDtypeStruct((TOP_K, n), I32),
            jax.ShapeDtypeStruct((N_EXPERTS, 128), I32),
        ],
        scratch_shapes=[pltpu.VMEM((N_EXPERTS, 128), F32), pltpu.VMEM((N_EXPERTS, tt), F32)],
        compiler_params=pltpu.CompilerParams(
            dimension_semantics=("arbitrary",), vmem_limit_bytes=VMEM_LIMIT),
        name="route",
    )(logits_t, bias_col)


def _place_kernel(eidx_ref, rank_ref, pstart_ref, dest_ref):
    tt = eidx_ref.shape[1]
    ioe = lax.broadcasted_iota(I32, (N_EXPERTS, tt), 0)
    pstart = pstart_ref[...]
    rows = [jnp.sum(jnp.where(ioe == eidx_ref[k:k + 1, :], pstart, 0.0), axis=0, keepdims=True)
            for k in range(TOP_K)]
    dest_ref[...] = jnp.concatenate(rows, axis=0).astype(I32) + rank_ref[...]


def _place(eidx, rank, pstart_col):
    n = eidx.shape[1]
    tt = TT_ROUTE
    row_spec = pl.BlockSpec((TOP_K, tt), lambda i: (0, i))
    return pl.pallas_call(
        _place_kernel,
        grid=(n // tt,),
        in_specs=[row_spec, row_spec, pl.BlockSpec((N_EXPERTS, 1), lambda i: (0, 0))],
        out_specs=row_spec,
        out_shape=jax.ShapeDtypeStruct((TOP_K, n), I32),
        compiler_params=pltpu.CompilerParams(
            dimension_semantics=("parallel",), vmem_limit_bytes=VMEM_LIMIT),
        name="place",
    )(eidx, rank, pstart_col)


def _sc_mesh():
    return plsc.VectorSubcoreMesh(core_axis_name="c", subcore_axis_name="s",
                                  num_cores=SC_NC, num_subcores=SC_NS)


def _sc_scatter_rows(rows, idx, nrows_out):
    n, d = rows.shape
    nk = idx.shape[0]
    per_w = n // SC_NW
    nwin = per_w // SC_WIN
    assert per_w * SC_NW == n and nwin * SC_WIN == per_w and nwin % 2 == 0

    @functools.partial(
        pl.kernel, mesh=_sc_mesh(),
        out_type=jax.ShapeDtypeStruct((nrows_out, d), rows.dtype),
        scratch_types=[
            pltpu.VMEM((nwin, nk, SC_WIN), I32),
            pltpu.VMEM((2, SC_WIN, d), rows.dtype),
            pltpu.SemaphoreType.DMA((2,)),
            pltpu.SemaphoreType.DMA((2,)),
        ],
        compiler_params=pltpu.CompilerParams(use_tc_tiling_on_sc=True),
        name="sc_scatter_rows",
    )
    def scatter_kernel(rows_hbm, idx_hbm, out_hbm, idx_v, rows_v, lsem, ssem):
        wid = lax.axis_index("s") * SC_NC + lax.axis_index("c")
        base = wid * per_w
        pltpu.sync_copy(idx_hbm.at[wid], idx_v)

        def load(w, slot):
            return pltpu.make_async_copy(
                rows_hbm.at[pl.ds(base + w * SC_WIN, SC_WIN)], rows_v.at[slot], lsem.at[slot])

        def scat(w, k, slot):
            return pltpu.make_async_copy(rows_v.at[slot], out_hbm.at[idx_v.at[w, k]], ssem.at[slot])

        load(0, 0).start()

        @pl.loop(0, nwin, step=2)
        def _(w0):
            for slot in range(2):
                w = w0 + slot
                load(w, slot).wait()

                @pl.when(w + 1 < nwin)
                def _():
                    @pl.when(w >= 1)
                    def _():
                        for k in range(nk):
                            scat(w - 1, k, 1 - slot).wait()
                    load(w + 1, 1 - slot).start()

                for k in range(nk):
                    scat(w, k, slot).start()

        for k in range(nk):
            scat(nwin - 2, k, 0).wait()
        for k in range(nk):
            scat(nwin - 1, k, 1).wait()

    idx4 = idx.reshape(nk, SC_NW, nwin, SC_WIN).transpose(1, 2, 0, 3)
    return scatter_kernel(rows, idx4)


def _expert_kernel(gstart_ref, cnt_ref, xs_hbm, wg_ref, wu_ref, wd_ref, y_hbm,
                   wgb_ref, wub_ref, wdb_ref, xbuf_ref, ybuf_ref, xsem, ysem):
    e = pl.program_id(0)
    ne = pl.num_programs(0)
    bm = xbuf_ref.shape[1]
    nblk = y_hbm.shape[0] // bm
    half = D_MODEL // 2
    g_lo = gstart_ref[e]
    g_hi = gstart_ref[e + 1]
    g_end = gstart_ref[ne]

    def x_copy(g, slot):
        return pltpu.make_async_copy(xs_hbm.at[pl.ds(g * bm, bm), :], xbuf_ref.at[slot], xsem.at[slot])

    def y_copy(g, slot):
        return pltpu.make_async_copy(ybuf_ref.at[slot], y_hbm.at[pl.ds(g * bm, bm), :], ysem.at[slot])

    nslot = xbuf_ref.shape[0]

    ahead = nslot - EXP_GROUP

    @pl.when(e == 0)
    def _():
        for g0 in range(ahead):
            @pl.when(g0 < g_end)
            def _():
                x_copy(g0, g0).start()

    @pl.when(g_hi > g_lo)
    def _():
        wgb_ref[...] = wg_ref[...].astype(BF16)
        wub_ref[...] = wu_ref[...].astype(BF16)
        wdb_ref[...] = wd_ref[...].astype(BF16)

    row = lax.broadcasted_iota(I32, (bm, half), 0)

    def acquire(g):
        x_copy(g, g % nslot).wait()

        @pl.when(g + ahead < g_end)
        def _():
            x_copy(g + ahead, (g + ahead) % nslot).start()

        @pl.when(g >= nslot)
        def _():
            y_copy(g - nslot, g % nslot).wait()

    def load(g):
        n_valid = cnt_ref[e] - (g - g_lo) * bm
        x_lo, x_hi = _unpack_bf16_pair(jnp.where(row < n_valid, xbuf_ref[g % nslot], jnp.uint32(0)))
        return x_lo.astype(BF16), x_hi.astype(BF16)

    def gate_up(x):
        x_lo, x_hi = x
        gate = _dot(x_lo, wgb_ref[:half, :]) + _dot(x_hi, wgb_ref[half:, :])
        up = _dot(x_lo, wub_ref[:half, :]) + _dot(x_hi, wub_ref[half:, :])
        return gate, up

    def down(gu):
        gate, up = gu
        return _dot((_silu(gate) * up).astype(BF16), wdb_ref[...])

    def store(g, y):
        ybuf_ref[g % nslot] = _pack_bf16_pair(y[:, :half], y[:, half:])
        y_copy(g, g % nslot).start()

    def run_blocks(g, count):
        for j in range(count):
            acquire(g + j)
        gus = [gate_up(load(g + j)) for j in range(count)]
        ys = [down(gu) for gu in gus]
        for j in range(count):
            store(g + j, ys[j])

    def full_group(p, carry):
        run_blocks(g_lo + EXP_GROUP * p, EXP_GROUP)
        return carry

    n_own = g_hi - g_lo
    lax.fori_loop(0, n_own // EXP_GROUP, full_group, 0)
    size = EXP_GROUP // 2
    while size >= 1:
        @pl.when(n_own % (2 * size) >= size)
        def _(size=size):
            run_blocks(g_lo + n_own // (2 * size) * (2 * size), size)
        size //= 2

    @pl.when(e == ne - 1)
    def _():
        for back in range(nslot, 0, -1):
            @pl.when(g_end >= back)
            def _():
                y_copy(g_end - back, (g_end - back) % nslot).wait()

        ybuf_ref[0] = jnp.zeros((bm, half), U32)

        def fill(g, carry):
            y_copy(g, 0).start()
            return carry

        def drain(g, carry):
            y_copy(g, 0).wait()
            return carry

        lax.fori_loop(g_end, nblk, fill, 0)
        lax.fori_loop(g_end, nblk, drain, 0)


def _experts(gstart, counts, xs, w_gate, w_up, w_down):
    bm = BM_EXP
    nblk = xs.shape[0] // bm
    half = D_MODEL // 2
    grid_spec = pltpu.PrefetchScalarGridSpec(
        num_scalar_prefetch=2,
        grid=(N_EXPERTS,),
        in_specs=[
            pl.BlockSpec(memory_space=pl.ANY),
            pl.BlockSpec((None, D_MODEL, EXPERT_FF), lambda e, gs, cn: (e, 0, 0)),
            pl.BlockSpec((None, D_MODEL, EXPERT_FF), lambda e, gs, cn: (e, 0, 0)),
            pl.BlockSpec((None, EXPERT_FF, D_MODEL), lambda e, gs, cn: (e, 0, 0)),
        ],
        out_specs=pl.BlockSpec(memory_space=pl.ANY),
        scratch_shapes=[
            pltpu.VMEM((D_MODEL, EXPERT_FF), BF16),
            pltpu.VMEM((D_MODEL, EXPERT_FF), BF16),
            pltpu.VMEM((EXPERT_FF, D_MODEL), BF16),
            pltpu.VMEM((EXP_SLOTS, bm, half), U32),
            pltpu.VMEM((EXP_SLOTS, bm, half), U32),
            pltpu.SemaphoreType.DMA((EXP_SLOTS,)),
            pltpu.SemaphoreType.DMA((EXP_SLOTS,)),
        ],
    )
    return pl.pallas_call(
        _expert_kernel,
        grid_spec=grid_spec,
        out_shape=jax.ShapeDtypeStruct((nblk * bm, half), U32),
        compiler_params=pltpu.CompilerParams(
            dimension_semantics=("arbitrary",), vmem_limit_bytes=VMEM_LIMIT),
        name="experts",
    )(gstart, counts, xs, w_gate, w_up, w_down)


def _sc_gather_rows(table, idx):
    nrows = idx.shape[0]
    d = table.shape[1]
    per_w = nrows // SC_NW
    nwin = per_w // SC_WIN
    assert per_w * SC_NW == nrows and nwin * SC_WIN == per_w and nwin % 2 == 0
    @functools.partial(
        pl.kernel, mesh=_sc_mesh(),
        out_type=jax.ShapeDtypeStruct((nrows, d), table.dtype),
        scratch_types=[
            pltpu.VMEM((nwin, SC_WIN), I32),
            pltpu.VMEM((2, SC_WIN, d), table.dtype),
            pltpu.SemaphoreType.DMA((2,)),
            pltpu.SemaphoreType.DMA((2,)),
        ],
        compiler_params=pltpu.CompilerParams(use_tc_tiling_on_sc=True),
        name="sc_gather_rows",
    )
    def gather_kernel(table_hbm, idx_hbm, out_hbm, idx_v, rows_v, gsem, wsem):
        wid = lax.axis_index("s") * SC_NC + lax.axis_index("c")
        base = wid * per_w
        pltpu.sync_copy(idx_hbm.at[wid], idx_v)

        def gather(w, slot):
            return pltpu.make_async_copy(table_hbm.at[idx_v.at[w]], rows_v.at[slot], gsem.at[slot])

        def put(w, slot):
            return pltpu.make_async_copy(
                rows_v.at[slot], out_hbm.at[pl.ds(base + w * SC_WIN, SC_WIN)], wsem.at[slot])

        gather(0, 0).start()

        @pl.loop(0, nwin, step=2)
        def _(w0):
            for slot in range(2):
                w = w0 + slot
                gather(w, slot).wait()

                @pl.when(w + 1 < nwin)
                def _():
                    @pl.when(w >= 1)
                    def _():
                        put(w - 1, 1 - slot).wait()
                    gather(w + 1, 1 - slot).start()

                put(w, slot).start()

        put(nwin - 2, 0).wait()
        put(nwin - 1, 1).wait()

    return gather_kernel(table, idx.reshape(SC_NW, nwin, SC_WIN))


def _combine_kernel(y_ref, base_ref, gate_ref, g_ref, b_ref, out_ref):
    half = D_MODEL // 2
    gates = gate_ref[...]
    acc_lo = base_ref[:, :half]
    acc_hi = base_ref[:, half:]
    for k in range(TOP_K):
        y_lo, y_hi = _unpack_bf16_pair(y_ref[k])
        gk = gates[:, k:k + 1]
        acc_lo = acc_lo + gk * y_lo
        acc_hi = acc_hi + gk * y_hi
    mu = (jnp.sum(acc_lo, axis=-1, keepdims=True) + jnp.sum(acc_hi, axis=-1, keepdims=True)) / D_MODEL
    c_lo = acc_lo - mu
    c_hi = acc_hi - mu
    var = (jnp.sum(c_lo * c_lo, axis=-1, keepdims=True)
           + jnp.sum(c_hi * c_hi, axis=-1, keepdims=True)) / D_MODEL
    inv = lax.rsqrt(var + LN_EPS)
    out_ref[:, :half] = c_lo * inv * g_ref[:, :half] + b_ref[:, :half]
    out_ref[:, half:] = c_hi * inv * g_ref[:, half:] + b_ref[:, half:]


def _combine(ybuf, base, gate_tok, ln_g, ln_b):
    n = base.shape[0]
    tt = TT_COMB
    half = D_MODEL // 2
    return pl.pallas_call(
        _combine_kernel,
        grid=(n // tt,),
        in_specs=[
            pl.BlockSpec((TOP_K, tt, half), lambda i: (0, i, 0)),
            pl.BlockSpec((tt, D_MODEL), lambda i: (i, 0)),
            pl.BlockSpec((tt, TOP_K), lambda i: (i, 0)),
            pl.BlockSpec((1, D_MODEL), lambda i: (0, 0)),
            pl.BlockSpec((1, D_MODEL), lambda i: (0, 0)),
        ],
        out_specs=pl.BlockSpec((tt, D_MODEL), lambda i: (i, 0)),
        out_shape=jax.ShapeDtypeStruct((n, D_MODEL), F32),
        compiler_params=pltpu.CompilerParams(
            dimension_semantics=("parallel",), vmem_limit_bytes=VMEM_LIMIT),
        name="combine",
    )(ybuf, base, gate_tok, ln_g, ln_b)


def _regroup_w_in(w_in):
    o = 0
    cols = {}
    for name, width in (("dnq", DN_WIDTH), ("dnk", DN_WIDTH), ("dnv", DN_WIDTH), ("sq", SWA_WIDTH),
                        ("sk", SWA_KV_WIDTH), ("sv", SWA_KV_WIDTH), ("z", DN_WIDTH),
                        ("b", DN_HEADS), ("a", DN_HEADS)):
        cols[name] = w_in[:, o:o + width]
        o += width
    w_main = jnp.concatenate([cols[k] for k in ("dnq", "dnk", "dnv", "z", "sq", "sk", "sv")], axis=1)
    w_gates = jnp.concatenate(
        [cols["b"], cols["a"], jnp.zeros((D_MODEL, GATE_COLS - 2 * DN_HEADS), w_in.dtype)], axis=1)
    return w_main.astype(BF16), w_gates.astype(BF16)


def _layer(x, w_in, conv_w, a_log, dt_bias, dn_norm_g, sinks, w_out, ln1_g, ln1_b,
           router_w, router_bias, w_gate, w_up, w_down, sh_gate, sh_up, sh_down, ln2_g, ln2_b):
    b, t, d = x.shape
    n = b * t
    x2d = x.reshape(n, d)

    w_main, w_gates = _regroup_w_in(w_in)
    main, gates = _in_proj(x2d, w_main, w_gates)
    main3d = main.reshape(b, t, MAIN_COLS)

    pad = jnp.zeros((GATE_COLS - 2 * DN_HEADS,), F32)
    gpar = jnp.stack([jnp.concatenate([jnp.zeros((DN_HEADS,), F32), a_log.astype(F32), pad]),
                      jnp.concatenate([jnp.zeros((DN_HEADS,), F32), dt_bias.astype(F32), pad])])
    dn_out = _deltanet(main3d, gates.reshape(b, t, GATE_COLS), conv_w.astype(F32), gpar,
                       dn_norm_g.astype(F32).reshape(1, DN_HEAD_DIM))
    swa_out = _swa(main3d, sinks.astype(F32))

    base, xpk, logits_t = _post_mix(
        x2d, dn_out.reshape(n, DN_WIDTH), swa_out.reshape(n, SWA_WIDTH),
        w_out[:DN_WIDTH].astype(BF16), w_out[DN_WIDTH:].astype(BF16),
        ln1_g.reshape(1, d).astype(F32), ln1_b.reshape(1, d).astype(F32),
        router_w.T.astype(BF16),
        sh_gate.astype(BF16), sh_up.astype(BF16), sh_down.astype(BF16))

    eidx, gate, rank, cnt = _route(logits_t, router_bias.astype(F32).reshape(N_EXPERTS, 1))

    bm = BM_EXP
    counts = cnt[:, 0]
    padded = (counts + bm - 1) // bm * bm
    pend = jnp.cumsum(padded)
    pstart = pend - padded
    nblk = -(-(n * TOP_K) // bm) + N_EXPERTS
    gstart = (jnp.concatenate([pstart, pend[-1:]]) // bm).astype(I32)

    dest = _place(eidx, rank, pstart.astype(F32).reshape(N_EXPERTS, 1))
    xs = _sc_scatter_rows(xpk, dest, nblk * bm)
    ypk = _experts(gstart, counts, xs, w_gate, w_up, w_down)
    ybuf = _sc_gather_rows(ypk, dest.reshape(-1)).reshape(TOP_K, n, d // 2)
    out = _combine(ybuf, base, gate.T, ln2_g.reshape(1, d).astype(F32), ln2_b.reshape(1, d).astype(F32))
    return out.reshape(b, t, d)


def kernel(x, w_in, conv_w, a_log, dt_bias, dn_norm_g, sinks, w_out, ln1_g, ln1_b, router_w, router_bias,
           w_gate, w_up, w_down, shared_w_gate, shared_w_up, shared_w_down, ln2_g, ln2_b):
    depth = w_in.shape[0]
    for l in range(depth):
        x = _layer(x, w_in[l], conv_w[l], a_log[l], dt_bias[l], dn_norm_g[l], sinks[l], w_out[l],
                   ln1_g[l], ln1_b[l], router_w[l], router_bias[l], w_gate[l], w_up[l], w_down[l],
                   shared_w_gate[l], shared_w_up[l], shared_w_down[l], ln2_g[l], ln2_b[l])
    return x
```

```python
import functools

import jax
import jax.numpy as jnp
from jax import lax
from jax.experimental import pallas as pl
from jax.experimental.pallas import tpu as pltpu
from jax.experimental.pallas import tpu_sc as plsc

F32 = jnp.float32
BF16 = jnp.bfloat16
I32 = jnp.int32
U32 = jnp.uint32

D_MODEL = 1024
DN_HEADS = 4
DN_HEAD_DIM = 128
DN_WIDTH = DN_HEADS * DN_HEAD_DIM
CONV_WIDTH = 4
DN_CHUNK = 64
SWA_Q_HEADS = 8
SWA_KV_HEADS = 2
SWA_HEAD_DIM = 64
SWA_WIDTH = SWA_Q_HEADS * SWA_HEAD_DIM
SWA_KV_WIDTH = SWA_KV_HEADS * SWA_HEAD_DIM
SWA_WINDOW = 128
SWA_BLOCK = 128
N_EXPERTS = 256
N_GROUPS = 8
GROUP_SIZE = N_EXPERTS // N_GROUPS
TOPK_GROUPS = 4
TOP_K = 8
EXPERT_FF = 256
SHARED_FF = 256
ROUTED_SCALE = 2.5
DEEPNORM_ALPHA = 2.0 ** 0.25
LN_EPS = 1e-5
RMS_EPS = 1e-6
L2_EPS = 1e-6

COL_DNQ = 0
COL_DNK = DN_WIDTH
COL_DNV = 2 * DN_WIDTH
COL_Z = 3 * DN_WIDTH
COL_SQ = 4 * DN_WIDTH
COL_SK = COL_SQ + SWA_WIDTH
COL_SV = COL_SK + SWA_KV_WIDTH
MAIN_COLS = COL_SV + SWA_KV_WIDTH
GATE_COLS = 128

TM_PROJ = 1024
TS_DN = 256
DN_SEQS = 4
DN_A_UNROLL = 4
TM_POST = 1024
POST_SPLIT = 2
TT_ROUTE = 512
BM_EXP = 256
EXP_GROUP = 2
EXP_SLOTS = 6
TT_COMB = 512
SC_NC = 2
SC_NS = 16
SC_NW = SC_NC * SC_NS
SC_WIN = 64
VMEM_LIMIT = 56 * 1024 * 1024
NEG_INF = float("-inf")


def _dot(a, b):
    return jnp.dot(a, b, preferred_element_type=F32)


def _mm(a, b):
    return _dot(a.astype(BF16), b.astype(BF16))


def _mm_nt(a, b):
    return lax.dot_general(a.astype(BF16), b.astype(BF16), (((1,), (1,)), ((), ())),
                           preferred_element_type=F32)


def _sigmoid(x):
    return 1.0 / (1.0 + jnp.exp(-x))


def _silu(x):
    return x * _sigmoid(x)


def _in_proj_kernel(x_ref, w_ref, wg_ref, main_ref, gates_ref):
    xb = x_ref[...].astype(BF16)
    main_ref[...] = _dot(xb, w_ref[...]).astype(BF16)
    gates_ref[...] = _dot(xb, wg_ref[...])


def _in_proj(x2d, w_main, w_gates):
    n = x2d.shape[0]
    return pl.pallas_call(
        _in_proj_kernel,
        grid=(n // TM_PROJ,),
        in_specs=[
            pl.BlockSpec((TM_PROJ, D_MODEL), lambda i: (i, 0)),
            pl.BlockSpec((D_MODEL, MAIN_COLS), lambda i: (0, 0)),
            pl.BlockSpec((D_MODEL, GATE_COLS), lambda i: (0, 0)),
        ],
        out_specs=[
            pl.BlockSpec((TM_PROJ, MAIN_COLS), lambda i: (i, 0)),
            pl.BlockSpec((TM_PROJ, GATE_COLS), lambda i: (i, 0)),
        ],
        out_shape=[
            jax.ShapeDtypeStruct((n, MAIN_COLS), BF16),
            jax.ShapeDtypeStruct((n, GATE_COLS), F32),
        ],
        compiler_params=pltpu.CompilerParams(
            dimension_semantics=("parallel",), vmem_limit_bytes=VMEM_LIMIT),
        name="in_proj",
    )(x2d, w_main, w_gates)


def _dn_kernel(x_ref, gates_ref, convw_ref, gpar_ref, normg_ref, out_ref,
               xc_ref, gl_ref, gc_ref, wq_ref, u_ref, kt_ref, attn_ref, egl_ref, s_ref, hist_ref):
    nseq = x_ref.shape[0]
    ts = x_ref.shape[1]
    c = DN_CHUNK
    hd = DN_HEAD_DIM
    qkv_w = 3 * DN_WIDTH
    nch = ts // c

    @pl.when(pl.program_id(1) == 0)
    def _():
        s_ref[...] = jnp.zeros_like(s_ref)
        hist_ref[...] = jnp.zeros_like(hist_ref)

    def stage_inputs(bi, carry):
        xc_ref[bi, 0:8, :] = hist_ref[bi]
        xc_ref[bi, 8:ts + 8, :] = x_ref[bi, :, 0:qkv_w].astype(F32)
        hist_ref[bi] = xc_ref[bi, ts:ts + 8, :]
        gsl = gates_ref[bi]
        sp_in = gsl + gpar_ref[1:2, :]
        softplus = jnp.maximum(sp_in, 0.0) + jnp.log(1.0 + jnp.exp(-jnp.abs(sp_in)))
        lane = lax.broadcasted_iota(I32, gsl.shape, 1)
        gl = jnp.where(lane < DN_HEADS, _sigmoid(gsl), -jnp.exp(gpar_ref[0:1, :]) * softplus)
        gl_ref[bi] = gl
        row_in_chunk = lax.broadcasted_iota(I32, gsl.shape, 0) % c
        gc = gl
        shift = 1
        while shift < c:
            gc = gc + jnp.where(row_in_chunk >= shift, pltpu.roll(gc, shift, 0), 0.0)
            shift *= 2
        gc_ref[bi] = gc
        return carry

    lax.fori_loop(0, nseq, stage_inputs, 0)

    ii = lax.broadcasted_iota(I32, (c, c), 0)
    jj = lax.broadcasted_iota(I32, (c, c), 1)
    tri_incl = ii >= jj
    tri_strict = ii > jj
    eye = jnp.where(ii == jj, 1.0, 0.0).astype(F32)
    heads = range(DN_HEADS)

    def conv_silu(bi, r0, col):
        w = convw_ref[:, col:col + hd]
        xt = xc_ref[bi, pl.ds(r0, c + 8), col:col + hd]
        y = w[CONV_WIDTH - 1:CONV_WIDTH, :] * xt[8:8 + c, :]
        for delay in range(1, CONV_WIDTH):
            tap = CONV_WIDTH - 1 - delay
            y = y + w[tap:tap + 1, :] * pltpu.roll(xt, delay, 0)[8:8 + c, :]
        return _silu(y)

    def l2n(t, scale):
        return t * (lax.rsqrt(jnp.sum(t * t, axis=-1, keepdims=True) + L2_EPS) * scale)

    def phase_a(i, bi):
        chains = []
        for sub in range(DN_A_UNROLL):
            ci = i * DN_A_UNROLL + sub
            r0 = ci * c
            glc = gl_ref[bi, pl.ds(r0, c), :]
            gcc = gc_ref[bi, pl.ds(r0, c), :]
            gct = jnp.concatenate([gcc, gcc], axis=0).T
            egl_ref[bi, ci] = jnp.exp(gcc[c - 8:c, :])
            for h in heads:
                chains.append((ci, r0, h, glc, gcc, gct))
        nchain = len(chains)
        q = [l2n(conv_silu(bi, r0, COL_DNQ + h * hd), hd ** -0.5) for (ci, r0, h, _, _, _) in chains]
        k = [l2n(conv_silu(bi, r0, COL_DNK + h * hd), 1.0) for (ci, r0, h, _, _, _) in chains]
        v = [conv_silu(bi, r0, COL_DNV + h * hd) for (ci, r0, h, _, _, _) in chains]
        kb, vb, decay, egc = [], [], [], []
        for n_, (ci, r0, h, glc, gcc, gct) in enumerate(chains):
            beta = glc[:, h:h + 1]
            gc_col = gcc[:, DN_HEADS + h:DN_HEADS + h + 1]
            gc_row = gct[DN_HEADS + h:DN_HEADS + h + 1, 0:c]
            decay.append(jnp.where(tri_incl, jnp.exp(jnp.minimum(gc_col - gc_row, 0.0)), 0.0))
            egc.append(jnp.exp(gc_col))
            e_tail = jnp.exp(gcc[c - 1:c, DN_HEADS + h:DN_HEADS + h + 1] - gc_col)
            kb.append(k[n_] * beta)
            vb.append(v[n_] * beta)
            kt_ref[bi, ci, h] = (k[n_] * e_tail).astype(BF16)
        kq = [_mm_nt(jnp.concatenate([kb[n_], q[n_]], axis=0), k[n_]) for n_ in range(nchain)]
        a_mat = [jnp.where(tri_strict, kq[n_][0:c] * decay[n_], 0.0) for n_ in range(nchain)]
        for n_, (ci, r0, h, _, _, _) in enumerate(chains):
            attn_ref[bi, ci, h] = (kq[n_][c:2 * c] * decay[n_]).astype(BF16)
        t_inv = [eye - a for a in a_mat]
        p = a_mat
        for _ in range(5):
            p = [_mm(x, x) for x in p]
            t_inv = [t + _mm(t, x) for t, x in zip(t_inv, p)]
        for n_, (ci, r0, h, _, _, _) in enumerate(chains):
            uw = _mm(t_inv[n_], jnp.concatenate([vb[n_], kb[n_] * egc[n_]], axis=1))
            u_ref[bi, ci, h] = uw[:, 0:hd]
            wq_ref[bi, ci, h, 0:c, :] = uw[:, hd:2 * hd].astype(BF16)
            wq_ref[bi, ci, h, c:2 * c, :] = (q[n_] * egc[n_]).astype(BF16)

    for bi in range(nseq):
        for i in range(nch // DN_A_UNROLL):
            phase_a(i, bi)

    normg = normg_ref[...]

    def phase_b(ci, carry):
        r0 = pl.multiple_of(ci * c, c)
        rows = pl.ds(r0, c)
        chains = [(bi, h) for bi in range(nseq) for h in heads]
        egl = [egl_ref[bi, ci] for bi in range(nseq)]
        s_old = [s_ref[bi, h] for bi, h in chains]
        ws = [_dot(wq_ref[bi, ci, h], s.astype(BF16)) for (bi, h), s in zip(chains, s_old)]
        v_new = [(u_ref[bi, ci, h] - w[0:c]).astype(BF16) for (bi, h), w in zip(chains, ws)]
        for n_, (bi, h) in enumerate(chains):
            s_ref[bi, h] = (s_old[n_] * egl[bi][7:8, DN_HEADS + h:DN_HEADS + h + 1]
                            + lax.dot_general(kt_ref[bi, ci, h], v_new[n_], (((0,), (0,)), ((), ())),
                                              preferred_element_type=F32))
        for n_, (bi, h) in enumerate(chains):
            o = ws[n_][c:2 * c] + _dot(attn_ref[bi, ci, h], v_new[n_])
            o = o * lax.rsqrt(jnp.mean(o * o, axis=-1, keepdims=True) + RMS_EPS) * normg
            z = x_ref[bi, rows, COL_Z + h * hd:COL_Z + (h + 1) * hd].astype(F32)
            out_ref[bi, rows, h * hd:(h + 1) * hd] = (o * _silu(z)).astype(out_ref.dtype)
        return carry

    lax.fori_loop(0, nch, phase_b, 0)


def _deltanet(main3d, gates3d, conv_w, gpar, normg):
    b, t, _ = main3d.shape
    ts = TS_DN
    nseq = DN_SEQS
    nch = ts // DN_CHUNK
    dn_in = COL_Z + DN_WIDTH
    return pl.pallas_call(
        _dn_kernel,
        grid=(b // nseq, t // ts),
        in_specs=[
            pl.BlockSpec((nseq, ts, dn_in), lambda bi, si: (bi, si, 0)),
            pl.BlockSpec((nseq, ts, GATE_COLS), lambda bi, si: (bi, si, 0)),
            pl.BlockSpec((CONV_WIDTH, 3 * DN_WIDTH), lambda bi, si: (0, 0)),
            pl.BlockSpec((2, GATE_COLS), lambda bi, si: (0, 0)),
            pl.BlockSpec((1, DN_HEAD_DIM), lambda bi, si: (0, 0)),
        ],
        out_specs=pl.BlockSpec((nseq, ts, DN_WIDTH), lambda bi, si: (bi, si, 0)),
        out_shape=jax.ShapeDtypeStruct((b, t, DN_WIDTH), BF16),
        scratch_shapes=[
            pltpu.VMEM((nseq, ts + 8, 3 * DN_WIDTH), F32),
            pltpu.VMEM((nseq, ts, GATE_COLS), F32),
            pltpu.VMEM((nseq, ts, GATE_COLS), F32),
            pltpu.VMEM((nseq, nch, DN_HEADS, 2 * DN_CHUNK, DN_HEAD_DIM), BF16),
            pltpu.VMEM((nseq, nch, DN_HEADS, DN_CHUNK, DN_HEAD_DIM), F32),
            pltpu.VMEM((nseq, nch, DN_HEADS, DN_CHUNK, DN_HEAD_DIM), BF16),
            pltpu.VMEM((nseq, nch, DN_HEADS, DN_CHUNK, DN_CHUNK), BF16),
            pltpu.VMEM((nseq, nch, 8, GATE_COLS), F32),
            pltpu.VMEM((nseq, DN_HEADS, DN_HEAD_DIM, DN_HEAD_DIM), F32),
            pltpu.VMEM((nseq, 8, 3 * DN_WIDTH), F32),
        ],
        compiler_params=pltpu.CompilerParams(
            dimension_semantics=("parallel", "arbitrary"), vmem_limit_bytes=VMEM_LIMIT),
        name="deltanet",
    )(main3d, gates3d, conv_w, gpar, normg)


def _swa_kernel(sinks_ref, q_ref, kp_ref, kc_ref, vp_ref, vc_ref, out_ref, bias_ref):
    n = pl.program_id(1)
    blk = SWA_BLOCK
    d = SWA_HEAD_DIM
    grp = SWA_Q_HEADS // SWA_KV_HEADS

    @pl.when((pl.program_id(0) == 0) & (n == 0))
    def _():
        qi = lax.broadcasted_iota(I32, (blk, 2 * blk), 0)
        kj = lax.broadcasted_iota(I32, (blk, 2 * blk), 1)
        dist = qi + blk - kj
        valid = (dist >= 0) & (dist < SWA_WINDOW)
        dist_f = dist.astype(F32)
        for hq in range(SWA_Q_HEADS):
            slope = 2.0 ** (-8.0 * (hq + 1.0) / SWA_Q_HEADS)
            bias_ref[hq] = jnp.where(valid, -slope * dist_f, NEG_INF)

    kcol = lax.broadcasted_iota(I32, (1, 2 * blk), 1)
    colmask = jnp.where((kcol >= blk) | (n > 0), 0.0, NEG_INF)
    q_all = q_ref[...] * (d ** -0.5)
    kband = [jnp.concatenate([kp_ref[:, hk * d:(hk + 1) * d], kc_ref[:, hk * d:(hk + 1) * d]], axis=0)
             for hk in range(SWA_KV_HEADS)]
    vband = [jnp.concatenate([vp_ref[:, hk * d:(hk + 1) * d], vc_ref[:, hk * d:(hk + 1) * d]], axis=0)
             for hk in range(SWA_KV_HEADS)]
    heads = range(SWA_Q_HEADS)
    scores = [_mm_nt(q_all[:, hq * d:(hq + 1) * d], kband[hq // grp]) for hq in heads]
    probs, denoms = [], []
    for hq in heads:
        s = scores[hq] + bias_ref[hq] + colmask
        sink = sinks_ref[hq]
        m = jnp.maximum(jnp.max(s, axis=-1, keepdims=True), sink)
        p = jnp.exp(s - m)
        denoms.append(jnp.sum(p, axis=-1, keepdims=True) + jnp.exp(sink - m))
        probs.append(p.astype(BF16))
    outs = [_dot(probs[hq], vband[hq // grp]) / denoms[hq] for hq in heads]
    out_ref[...] = jnp.concatenate(outs, axis=-1).astype(out_ref.dtype)


def _swa(main3d, sinks):
    b, t, _ = main3d.shape
    blk = SWA_BLOCK
    qb = COL_SQ // SWA_WIDTH
    kb = COL_SK // SWA_KV_WIDTH
    vb = COL_SV // SWA_KV_WIDTH
    grid_spec = pltpu.PrefetchScalarGridSpec(
        num_scalar_prefetch=1,
        grid=(b, t // blk),
        in_specs=[
            pl.BlockSpec((None, blk, SWA_WIDTH), lambda bi, ni, s: (bi, ni, qb)),
            pl.BlockSpec((None, blk, SWA_KV_WIDTH), lambda bi, ni, s: (bi, jnp.maximum(ni - 1, 0), kb)),
            pl.BlockSpec((None, blk, SWA_KV_WIDTH), lambda bi, ni, s: (bi, ni, kb)),
            pl.BlockSpec((None, blk, SWA_KV_WIDTH), lambda bi, ni, s: (bi, jnp.maximum(ni - 1, 0), vb)),
            pl.BlockSpec((None, blk, SWA_KV_WIDTH), lambda bi, ni, s: (bi, ni, vb)),
        ],
        out_specs=pl.BlockSpec((None, blk, SWA_WIDTH), lambda bi, ni, s: (bi, ni, 0)),
        scratch_shapes=[pltpu.VMEM((SWA_Q_HEADS, blk, 2 * blk), F32)],
    )
    return pl.pallas_call(
        _swa_kernel,
        grid_spec=grid_spec,
        out_shape=jax.ShapeDtypeStruct((b, t, SWA_WIDTH), BF16),
        compiler_params=pltpu.CompilerParams(
            dimension_semantics=("arbitrary", "arbitrary"), vmem_limit_bytes=VMEM_LIMIT),
        name="swa",
    )(sinks, main3d, main3d, main3d, main3d, main3d)


def _layer_norm(y, g, b):
    mu = jnp.mean(y, axis=-1, keepdims=True)
    yc = y - mu
    var = jnp.mean(yc * yc, axis=-1, keepdims=True)
    return yc * lax.rsqrt(var + LN_EPS) * g + b


def _pack_bf16_pair(lo, hi):
    lo_bits = lax.bitcast_convert_type(lo.astype(BF16).astype(F32), U32)
    hi_bits = lax.bitcast_convert_type(hi.astype(BF16).astype(F32), U32)
    return (hi_bits & jnp.uint32(0xFFFF0000)) | (lo_bits >> 16)


def _unpack_bf16_pair(packed):
    lo = lax.bitcast_convert_type(packed << 16, F32)
    hi = lax.bitcast_convert_type(packed & jnp.uint32(0xFFFF0000), F32)
    return lo, hi


def _post_mix_kernel(x_ref, dn_ref, swa_ref, wo_dn_ref, wo_swa_ref, g_ref, b_ref,
                     rw_ref, sg_ref, su_ref, sd_ref,
                     base_ref, xpk_ref, logit_ref):
    half = D_MODEL // 2
    tm = x_ref.shape[0]
    sub = tm // POST_SPLIT
    parts = [pl.ds(i * sub, sub) for i in range(POST_SPLIT)]
    mix = [_dot(dn_ref[r, :], wo_dn_ref[...]) + _dot(swa_ref[r, :], wo_swa_ref[...]) for r in parts]
    x1 = [_layer_norm(DEEPNORM_ALPHA * x_ref[r, :] + m, g_ref[...], b_ref[...]) for r, m in zip(parts, mix)]
    xh = [x.astype(BF16) for x in x1]
    for r, x in zip(parts, x1):
        xpk_ref[r, :] = _pack_bf16_pair(x[:, :half], x[:, half:])
    for r, x in zip(parts, xh):
        logit_ref[:, r] = lax.dot_general(rw_ref[...], x, (((1,), (1,)), ((), ())),
                                          preferred_element_type=F32)
    gate_up = [(_dot(x, sg_ref[...]), _dot(x, su_ref[...])) for x in xh]
    hmid = [(_silu(g) * u).astype(BF16) for g, u in gate_up]
    for r, x, h in zip(parts, x1, hmid):
        base_ref[r, :] = DEEPNORM_ALPHA * x + _dot(h, sd_ref[...])


def _post_mix(x2d, dn2d, swa2d, wo_dn, wo_swa, ln_g, ln_b, rw_t, sg, su, sd):
    n = x2d.shape[0]
    tm = TM_POST
    full = lambda shape: pl.BlockSpec(shape, lambda i: (0, 0))
    return pl.pallas_call(
        _post_mix_kernel,
        grid=(n // tm,),
        in_specs=[
            pl.BlockSpec((tm, D_MODEL), lambda i: (i, 0)),
            pl.BlockSpec((tm, DN_WIDTH), lambda i: (i, 0)),
            pl.BlockSpec((tm, SWA_WIDTH), lambda i: (i, 0)),
            full((DN_WIDTH, D_MODEL)), full((SWA_WIDTH, D_MODEL)),
            full((1, D_MODEL)), full((1, D_MODEL)),
            full((N_EXPERTS, D_MODEL)),
            full((D_MODEL, SHARED_FF)), full((D_MODEL, SHARED_FF)), full((SHARED_FF, D_MODEL)),
        ],
        out_specs=[
            pl.BlockSpec((tm, D_MODEL), lambda i: (i, 0)),
            pl.BlockSpec((tm, D_MODEL // 2), lambda i: (i, 0)),
            pl.BlockSpec((N_EXPERTS, tm), lambda i: (0, i)),
        ],
        out_shape=[
            jax.ShapeDtypeStruct((n, D_MODEL), F32),
            jax.ShapeDtypeStruct((n, D_MODEL // 2), U32),
            jax.ShapeDtypeStruct((N_EXPERTS, n), F32),
        ],
        compiler_params=pltpu.CompilerParams(
            dimension_semantics=("parallel",), vmem_limit_bytes=VMEM_LIMIT),
        name="post_mix",
    )(x2d, dn2d, swa2d, wo_dn, wo_swa, ln_g, ln_b, rw_t, sg, su, sd)


def _route_kernel(lg_ref, bias_ref, eidx_ref, gate_ref, rank_ref, cnt_ref, carry_ref, pick_ref):
    @pl.when(pl.program_id(0) == 0)
    def _():
        carry_ref[...] = jnp.zeros_like(carry_ref)

    tt = lg_ref.shape[1]
    scores = _sigmoid(lg_ref[...])
    sel = scores + bias_ref[...]

    iog = lax.broadcasted_iota(I32, (GROUP_SIZE, tt), 0)
    grp_rows = []
    for g in range(N_GROUPS):
        blk = sel[g * GROUP_SIZE:(g + 1) * GROUP_SIZE, :]
        m1 = jnp.max(blk, axis=0, keepdims=True)
        i1 = jnp.min(jnp.where(blk == m1, iog, GROUP_SIZE), axis=0, keepdims=True)
        m2 = jnp.max(jnp.where(iog == i1, NEG_INF, blk), axis=0, keepdims=True)
        grp_rows.append(m1 + m2)
    gs = jnp.concatenate(grp_rows, axis=0)

    io8 = lax.broadcasted_iota(I32, (N_GROUPS, tt), 0)
    gsel = jnp.zeros((N_GROUPS, tt), F32)
    for _ in range(TOPK_GROUPS):
        mg = jnp.max(gs, axis=0, keepdims=True)
        ig = jnp.min(jnp.where(gs == mg, io8, N_GROUPS), axis=0, keepdims=True)
        hit = io8 == ig
        gsel = jnp.where(hit, 1.0, gsel)
        gs = jnp.where(hit, NEG_INF, gs)

    val = jnp.concatenate(
        [jnp.where(gsel[g:g + 1, :] > 0.0, sel[g * GROUP_SIZE:(g + 1) * GROUP_SIZE, :], NEG_INF)
         for g in range(N_GROUPS)], axis=0)

    ioe = lax.broadcasted_iota(I32, (N_EXPERTS, tt), 0)
    v = val
    for _ in range(TOP_K):
        m = jnp.max(v, axis=0, keepdims=True)
        v = jnp.where(v >= m, NEG_INF, v)
    picked = jnp.where(val >= m, 1.0, 0.0)
    pick_ref[...] = picked
    n_off = jnp.max(jnp.abs(jnp.sum(picked, axis=0, keepdims=True) - TOP_K))

    @pl.when(n_off > 0.0)
    def _():
        v = val
        onehot = jnp.zeros((N_EXPERTS, tt), F32)
        for _ in range(TOP_K):
            m = jnp.max(v, axis=0, keepdims=True)
            ik = jnp.min(jnp.where(v == m, ioe, N_EXPERTS), axis=0, keepdims=True)
            hit = ioe == ik
            v = jnp.where(hit, NEG_INF, v)
            onehot = jnp.where(hit, 1.0, onehot)
        pick_ref[...] = onehot

    onehot = pick_ref[...]
    oh16 = onehot.astype(BF16)
    ei = lax.broadcasted_iota(I32, (N_EXPERTS, N_EXPERTS), 0)
    ej = lax.broadcasted_iota(I32, (N_EXPERTS, N_EXPERTS), 1)
    slot = _dot(jnp.where(ej < ei, 1.0, 0.0).astype(BF16), oh16)
    ti = lax.broadcasted_iota(I32, (tt, tt), 0)
    tj = lax.broadcasted_iota(I32, (tt, tt), 1)
    cum = _dot(oh16, jnp.where(ti < tj, 1.0, 0.0).astype(BF16)) + jnp.broadcast_to(
        carry_ref[:, 0:1], (N_EXPERTS, tt))
    key = jnp.where(onehot > 0.0, slot, float(TOP_K))
    packed = ioe.astype(F32) + float(N_EXPERTS) * cum
    packed_rows, gate_rows = [], []
    for k in range(TOP_K):
        at_k = key == float(k)
        packed_rows.append(jnp.sum(jnp.where(at_k, packed, 0.0), axis=0, keepdims=True))
        gate_rows.append(jnp.sum(jnp.where(at_k, scores, 0.0), axis=0, keepdims=True))
    gsum = gate_rows[0]
    for r in gate_rows[1:]:
        gsum = gsum + r
    gate_ref[...] = jnp.concatenate(gate_rows, axis=0) / gsum * ROUTED_SCALE
    packed_i = jnp.concatenate(packed_rows, axis=0).astype(I32)
    eidx_ref[...] = packed_i & (N_EXPERTS - 1)
    rank_ref[...] = packed_i >> (N_EXPERTS.bit_length() - 1)
    carry_ref[...] = carry_ref[...] + jnp.broadcast_to(
        jnp.sum(onehot, axis=1, keepdims=True), carry_ref.shape)
    cnt_ref[...] = carry_ref[...].astype(I32)


def _route(logits_t, bias_col):
    n = logits_t.shape[1]
    tt = TT_ROUTE
    row_spec = pl.BlockSpec((TOP_K, tt), lambda i: (0, i))
    return pl.pallas_call(
        _route_kernel,
        grid=(n // tt,),
        in_specs=[
            pl.BlockSpec((N_EXPERTS, tt), lambda i: (0, i)),
            pl.BlockSpec((N_EXPERTS, 1), lambda i: (0, 0)),
        ],
        out_specs=[row_spec, row_spec, row_spec,
                   pl.BlockSpec((N_EXPERTS, 128), lambda i: (0, 0))],
        out_shape=[
            jax.ShapeDtypeStruct((TOP_K, n), I32),
            jax.ShapeDtypeStruct((TOP_K, n), F32),
            jax.ShapeDtypeStruct((TOP_K, n), I32),
            jax.ShapeDtypeStruct((N_EXPERTS, 128), I32),
        ],
        scratch_shapes=[pltpu.VMEM((N_EXPERTS, 128), F32), pltpu.VMEM((N_EXPERTS, tt), F32)],
        compiler_params=pltpu.CompilerParams(
            dimension_semantics=("arbitrary",), vmem_limit_bytes=VMEM_LIMIT),
        name="route",
    )(logits_t, bias_col)


def _place_kernel(eidx_ref, rank_ref, pstart_ref, dest_ref):
    tt = eidx_ref.shape[1]
    ioe = lax.broadcasted_iota(I32, (N_EXPERTS, tt), 0)
    pstart = pstart_ref[...]
    rows = [jnp.sum(jnp.where(ioe == eidx_ref[k:k + 1, :], pstart, 0.0), axis=0, keepdims=True)
            for k in range(TOP_K)]
    dest_ref[...] = jnp.concatenate(rows, axis=0).astype(I32) + rank_ref[...]


def _place(eidx, rank, pstart_col):
    n = eidx.shape[1]
    tt = TT_ROUTE
    row_spec = pl.BlockSpec((TOP_K, tt), lambda i: (0, i))
    return pl.pallas_call(
        _place_kernel,
        grid=(n // tt,),
        in_specs=[row_spec, row_spec, pl.BlockSpec((N_EXPERTS, 1), lambda i: (0, 0))],
        out_specs=row_spec,
        out_shape=jax.ShapeDtypeStruct((TOP_K, n), I32),
        compiler_params=pltpu.CompilerParams(
            dimension_semantics=("parallel",), vmem_limit_bytes=VMEM_LIMIT),
        name="place",
    )(eidx, rank, pstart_col)


def _sc_mesh():
    return plsc.VectorSubcoreMesh(core_axis_name="c", subcore_axis_name="s",
                                  num_cores=SC_NC, num_subcores=SC_NS)


def _sc_scatter_rows(rows, idx, nrows_out):
    n, d = rows.shape
    nk = idx.shape[0]
    per_w = n // SC_NW
    nwin = per_w // SC_WIN
    assert per_w * SC_NW == n and nwin * SC_WIN == per_w and nwin % 2 == 0

    @functools.partial(
        pl.kernel, mesh=_sc_mesh(),
        out_type=jax.ShapeDtypeStruct((nrows_out, d), rows.dtype),
        scratch_types=[
            pltpu.VMEM((nwin, nk, SC_WIN), I32),
            pltpu.VMEM((2, SC_WIN, d), rows.dtype),
            pltpu.SemaphoreType.DMA((2,)),
            pltpu.SemaphoreType.DMA((2,)),
        ],
        compiler_params=pltpu.CompilerParams(use_tc_tiling_on_sc=True),
        name="sc_scatter_rows",
    )
    def scatter_kernel(rows_hbm, idx_hbm, out_hbm, idx_v, rows_v, lsem, ssem):
        wid = lax.axis_index("s") * SC_NC + lax.axis_index("c")
        base = wid * per_w
        pltpu.sync_copy(idx_hbm.at[wid], idx_v)

        def load(w, slot):
            return pltpu.make_async_copy(
                rows_hbm.at[pl.ds(base + w * SC_WIN, SC_WIN)], rows_v.at[slot], lsem.at[slot])

        def scat(w, k, slot):
            return pltpu.make_async_copy(rows_v.at[slot], out_hbm.at[idx_v.at[w, k]], ssem.at[slot])

        load(0, 0).start()

        @pl.loop(0, nwin, step=2)
        def _(w0):
            for slot in range(2):
                w = w0 + slot
                load(w, slot).wait()

                @pl.when(w + 1 < nwin)
                def _():
                    @pl.when(w >= 1)
                    def _():
                        for k in range(nk):
                            scat(w - 1, k, 1 - slot).wait()
                    load(w + 1, 1 - slot).start()

                for k in range(nk):
                    scat(w, k, slot).start()

        for k in range(nk):
            scat(nwin - 2, k, 0).wait()
        for k in range(nk):
            scat(nwin - 1, k, 1).wait()

    idx4 = idx.reshape(nk, SC_NW, nwin, SC_WIN).transpose(1, 2, 0, 3)
    return scatter_kernel(rows, idx4)


def _expert_kernel(gstart_ref, cnt_ref, xs_hbm, wg_ref, wu_ref, wd_ref, y_hbm,
                   wgb_ref, wub_ref, wdb_ref, xbuf_ref, ybuf_ref, xsem, ysem):
    e = pl.program_id(0)
    ne = pl.num_programs(0)
    bm = xbuf_ref.shape[1]
    nblk = y_hbm.shape[0] // bm
    half = D_MODEL // 2
    g_lo = gstart_ref[e]
    g_hi = gstart_ref[e + 1]
    g_end = gstart_ref[ne]

    def x_copy(g, slot):
        return pltpu.make_async_copy(xs_hbm.at[pl.ds(g * bm, bm), :], xbuf_ref.at[slot], xsem.at[slot])

    def y_copy(g, slot):
        return pltpu.make_async_copy(ybuf_ref.at[slot], y_hbm.at[pl.ds(g * bm, bm), :], ysem.at[slot])

    nslot = xbuf_ref.shape[0]

    ahead = nslot - EXP_GROUP

    @pl.when(e == 0)
    def _():
        for g0 in range(ahead):
            @pl.when(g0 < g_end)
            def _():
                x_copy(g0, g0).start()

    @pl.when(g_hi > g_lo)
    def _():
        wgb_ref[...] = wg_ref[...].astype(BF16)
        wub_ref[...] = wu_ref[...].astype(BF16)
        wdb_ref[...] = wd_ref[...].astype(BF16)

    row = lax.broadcasted_iota(I32, (bm, half), 0)

    def acquire(g):
        x_copy(g, g % nslot).wait()

        @pl.when(g + ahead < g_end)
        def _():
            x_copy(g + ahead, (g + ahead) % nslot).start()

        @pl.when(g >= nslot)
        def _():
            y_copy(g - nslot, g % nslot).wait()

    def load(g):
        n_valid = cnt_ref[e] - (g - g_lo) * bm
        x_lo, x_hi = _unpack_bf16_pair(jnp.where(row < n_valid, xbuf_ref[g % nslot], jnp.uint32(0)))
        return x_lo.astype(BF16), x_hi.astype(BF16)

    def gate_up(x):
        x_lo, x_hi = x
        gate = _dot(x_lo, wgb_ref[:half, :]) + _dot(x_hi, wgb_ref[half:, :])
        up = _dot(x_lo, wub_ref[:half, :]) + _dot(x_hi, wub_ref[half:, :])
        return gate, up

    def down(gu):
        gate, up = gu
        return _dot((_silu(gate) * up).astype(BF16), wdb_ref[...])

    def store(g, y):
        ybuf_ref[g % nslot] = _pack_bf16_pair(y[:, :half], y[:, half:])
        y_copy(g, g % nslot).start()

    def run_blocks(g, count):
        for j in range(count):
            acquire(g + j)
        gus = [gate_up(load(g + j)) for j in range(count)]
        ys = [down(gu) for gu in gus]
        for j in range(count):
            store(g + j, ys[j])

    def full_group(p, carry):
        run_blocks(g_lo + EXP_GROUP * p, EXP_GROUP)
        return carry

    n_own = g_hi - g_lo
    lax.fori_loop(0, n_own // EXP_GROUP, full_group, 0)
    size = EXP_GROUP // 2
    while size >= 1:
        @pl.when(n_own % (2 * size) >= size)
        def _(size=size):
            run_blocks(g_lo + n_own // (2 * size) * (2 * size), size)
        size //= 2

    @pl.when(e == ne - 1)
    def _():
        for back in range(nslot, 0, -1):
            @pl.when(g_end >= back)
            def _():
                y_copy(g_end - back, (g_end - back) % nslot).wait()

        ybuf_ref[0] = jnp.zeros((bm, half), U32)

        def fill(g, carry):
            y_copy(g, 0).start()
            return carry

        def drain(g, carry):
            y_copy(g, 0).wait()
            return carry

        lax.fori_loop(g_end, nblk, fill, 0)
        lax.fori_loop(g_end, nblk, drain, 0)


def _experts(gstart, counts, xs, w_gate, w_up, w_down):
    bm = BM_EXP
    nblk = xs.shape[0] // bm
    half = D_MODEL // 2
    grid_spec = pltpu.PrefetchScalarGridSpec(
        num_scalar_prefetch=2,
        grid=(N_EXPERTS,),
        in_specs=[
            pl.BlockSpec(memory_space=pl.ANY),
            pl.BlockSpec((None, D_MODEL, EXPERT_FF), lambda e, gs, cn: (e, 0, 0)),
            pl.BlockSpec((None, D_MODEL, EXPERT_FF), lambda e, gs, cn: (e, 0, 0)),
            pl.BlockSpec((None, EXPERT_FF, D_MODEL), lambda e, gs, cn: (e, 0, 0)),
        ],
        out_specs=pl.BlockSpec(memory_space=pl.ANY),
        scratch_shapes=[
            pltpu.VMEM((D_MODEL, EXPERT_FF), BF16),
            pltpu.VMEM((D_MODEL, EXPERT_FF), BF16),
            pltpu.VMEM((EXPERT_FF, D_MODEL), BF16),
            pltpu.VMEM((EXP_SLOTS, bm, half), U32),
            pltpu.VMEM((EXP_SLOTS, bm, half), U32),
            pltpu.SemaphoreType.DMA((EXP_SLOTS,)),
            pltpu.SemaphoreType.DMA((EXP_SLOTS,)),
        ],
    )
    return pl.pallas_call(
        _expert_kernel,
        grid_spec=grid_spec,
        out_shape=jax.ShapeDtypeStruct((nblk * bm, half), U32),
        compiler_params=pltpu.CompilerParams(
            dimension_semantics=("arbitrary",), vmem_limit_bytes=VMEM_LIMIT),
        name="experts",
    )(gstart, counts, xs, w_gate, w_up, w_down)


def _sc_gather_rows(table, idx):
    nrows = idx.shape[0]
    d = table.shape[1]
    per_w = nrows // SC_NW
    nwin = per_w // SC_WIN
    assert per_w * SC_NW == nrows and nwin * SC_WIN == per_w and nwin % 2 == 0
    @functools.partial(
        pl.kernel, mesh=_sc_mesh(),
        out_type=jax.ShapeDtypeStruct((nrows, d), table.dtype),
        scratch_types=[
            pltpu.VMEM((nwin, SC_WIN), I32),
            pltpu.VMEM((2, SC_WIN, d), table.dtype),
            pltpu.SemaphoreType.DMA((2,)),
            pltpu.SemaphoreType.DMA((2,)),
        ],
        compiler_params=pltpu.CompilerParams(use_tc_tiling_on_sc=True),
        name="sc_gather_rows",
    )
    def gather_kernel(table_hbm, idx_hbm, out_hbm, idx_v, rows_v, gsem, wsem):
        wid = lax.axis_index("s") * SC_NC + lax.axis_index("c")
        base = wid * per_w
        pltpu.sync_copy(idx_hbm.at[wid], idx_v)

        def gather(w, slot):
            return pltpu.make_async_copy(table_hbm.at[idx_v.at[w]], rows_v.at[slot], gsem.at[slot])

        def put(w, slot):
            return pltpu.make_async_copy(
                rows_v.at[slot], out_hbm.at[pl.ds(base + w * SC_WIN, SC_WIN)], wsem.at[slot])

        gather(0, 0).start()

        @pl.loop(0, nwin, step=2)
        def _(w0):
            for slot in range(2):
                w = w0 + slot
                gather(w, slot).wait()

                @pl.when(w + 1 < nwin)
                def _():
                    @pl.when(w >= 1)
                    def _():
                        put(w - 1, 1 - slot).wait()
                    gather(w + 1, 1 - slot).start()

                put(w, slot).start()

        put(nwin - 2, 0).wait()
        put(nwin - 1, 1).wait()

    return gather_kernel(table, idx.reshape(SC_NW, nwin, SC_WIN))


def _combine_kernel(y_ref, base_ref, gate_ref, g_ref, b_ref, out_ref):
    half = D_MODEL // 2
    gates = gate_ref[...]
    acc_lo = base_ref[:, :half]
    acc_hi = base_ref[:, half:]
    for k in range(TOP_K):
        y_lo, y_hi = _unpack_bf16_pair(y_ref[k])
        gk = gates[:, k:k + 1]
        acc_lo = acc_lo + gk * y_lo
        acc_hi = acc_hi + gk * y_hi
    mu = (jnp.sum(acc_lo, axis=-1, keepdims=True) + jnp.sum(acc_hi, axis=-1, keepdims=True)) / D_MODEL
    c_lo = acc_lo - mu
    c_hi = acc_hi - mu
    var = (jnp.sum(c_lo * c_lo, axis=-1, keepdims=True)
           + jnp.sum(c_hi * c_hi, axis=-1, keepdims=True)) / D_MODEL
    inv = lax.rsqrt(var + LN_EPS)
    out_ref[:, :half] = c_lo * inv * g_ref[:, :half] + b_ref[:, :half]
    out_ref[:, half:] = c_hi * inv * g_ref[:, half:] + b_ref[:, half:]


def _combine(ybuf, base, gate_tok, ln_g, ln_b):
    n = base.shape[0]
    tt = TT_COMB
    half = D_MODEL // 2
    return pl.pallas_call(
        _combine_kernel,
        grid=(n // tt,),
        in_specs=[
            pl.BlockSpec((TOP_K, tt, half), lambda i: (0, i, 0)),
            pl.BlockSpec((tt, D_MODEL), lambda i: (i, 0)),
            pl.BlockSpec((tt, TOP_K), lambda i: (i, 0)),
            pl.BlockSpec((1, D_MODEL), lambda i: (0, 0)),
            pl.BlockSpec((1, D_MODEL), lambda i: (0, 0)),
        ],
        out_specs=pl.BlockSpec((tt, D_MODEL), lambda i: (i, 0)),
        out_shape=jax.ShapeDtypeStruct((n, D_MODEL), F32),
        compiler_params=pltpu.CompilerParams(
            dimension_semantics=("parallel",), vmem_limit_bytes=VMEM_LIMIT),
        name="combine",
    )(ybuf, base, gate_tok, ln_g, ln_b)


def _regroup_w_in(w_in):
    o = 0
    cols = {}
    for name, width in (("dnq", DN_WIDTH), ("dnk", DN_WIDTH), ("dnv", DN_WIDTH), ("sq", SWA_WIDTH),
                        ("sk", SWA_KV_WIDTH), ("sv", SWA_KV_WIDTH), ("z", DN_WIDTH),
                        ("b", DN_HEADS), ("a", DN_HEADS)):
        cols[name] = w_in[:, o:o + width]
        o += width
    w_main = jnp.concatenate([cols[k] for k in ("dnq", "dnk", "dnv", "z", "sq", "sk", "sv")], axis=1)
    w_gates = jnp.concatenate(
        [cols["b"], cols["a"], jnp.zeros((D_MODEL, GATE_COLS - 2 * DN_HEADS), w_in.dtype)], axis=1)
    return w_main.astype(BF16), w_gates.astype(BF16)


def _layer(x, w_in, conv_w, a_log, dt_bias, dn_norm_g, sinks, w_out, ln1_g, ln1_b,
           router_w, router_bias, w_gate, w_up, w_down, sh_gate, sh_up, sh_down, ln2_g, ln2_b):
    b, t, d = x.shape
    n = b * t
    x2d = x.reshape(n, d)

    w_main, w_gates = _regroup_w_in(w_in)
    main, gates = _in_proj(x2d, w_main, w_gates)
    main3d = main.reshape(b, t, MAIN_COLS)

    pad = jnp.zeros((GATE_COLS - 2 * DN_HEADS,), F32)
    gpar = jnp.stack([jnp.concatenate([jnp.zeros((DN_HEADS,), F32), a_log.astype(F32), pad]),
                      jnp.concatenate([jnp.zeros((DN_HEADS,), F32), dt_bias.astype(F32), pad])])
    dn_out = _deltanet(main3d, gates.reshape(b, t, GATE_COLS), conv_w.astype(F32), gpar,
                       dn_norm_g.astype(F32).reshape(1, DN_HEAD_DIM))
    swa_out = _swa(main3d, sinks.astype(F32))

    base, xpk, logits_t = _post_mix(
        x2d, dn_out.reshape(n, DN_WIDTH), swa_out.reshape(n, SWA_WIDTH),
        w_out[:DN_WIDTH].astype(BF16), w_out[DN_WIDTH:].astype(BF16),
        ln1_g.reshape(1, d).astype(F32), ln1_b.reshape(1, d).astype(F32),
        router_w.T.astype(BF16),
        sh_gate.astype(BF16), sh_up.astype(BF16), sh_down.astype(BF16))

    eidx, gate, rank, cnt = _route(logits_t, router_bias.astype(F32).reshape(N_EXPERTS, 1))

    bm = BM_EXP
    counts = cnt[:, 0]
    padded = (counts + bm - 1) // bm * bm
    pend = jnp.cumsum(padded)
    pstart = pend - padded
    nblk = -(-(n * TOP_K) // bm) + N_EXPERTS
    gstart = (jnp.concatenate([pstart, pend[-1:]]) // bm).astype(I32)

    dest = _place(eidx, rank, pstart.astype(F32).reshape(N_EXPERTS, 1))
    xs = _sc_scatter_rows(xpk, dest, nblk * bm)
    ypk = _experts(gstart, counts, xs, w_gate, w_up, w_down)
    ybuf = _sc_gather_rows(ypk, dest.reshape(-1)).reshape(TOP_K, n, d // 2)
    out = _combine(ybuf, base, gate.T, ln2_g.reshape(1, d).astype(F32), ln2_b.reshape(1, d).astype(F32))
    return out.reshape(b, t, d)


def kernel(x, w_in, conv_w, a_log, dt_bias, dn_norm_g, sinks, w_out, ln1_g, ln1_b, router_w, router_bias,
           w_gate, w_up, w_down, shared_w_gate, shared_w_up, shared_w_down, ln2_g, ln2_b):
    depth = w_in.shape[0]
    for l in range(depth):
        x = _layer(x, w_in[l], conv_w[l], a_log[l], dt_bias[l], dn_norm_g[l], sinks[l], w_out[l],
                   ln1_g[l], ln1_b[l], router_w[l], router_bias[l], w_gate[l], w_up[l], w_down[l],
                   shared_w_gate[l], shared_w_up[l], shared_w_down[l], ln2_g[l], ln2_b[l])
    return x
```

```python
import functools

import jax
import jax.numpy as jnp
from jax import lax
from jax.experimental import pallas as pl
from jax.experimental.pallas import tpu as pltpu
from jax.experimental.pallas import tpu_sc as plsc

F32 = jnp.float32
BF16 = jnp.bfloat16
I32 = jnp.int32
U32 = jnp.uint32

D_MODEL = 1024
DN_HEADS = 4
DN_HEAD_DIM = 128
DN_WIDTH = DN_HEADS * DN_HEAD_DIM
CONV_WIDTH = 4
DN_CHUNK = 64
SWA_Q_HEADS = 8
SWA_KV_HEADS = 2
SWA_HEAD_DIM = 64
SWA_WIDTH = SWA_Q_HEADS * SWA_HEAD_DIM
SWA_KV_WIDTH = SWA_KV_HEADS * SWA_HEAD_DIM
SWA_WINDOW = 128
SWA_BLOCK = 128
N_EXPERTS = 256
N_GROUPS = 8
GROUP_SIZE = N_EXPERTS // N_GROUPS
TOPK_GROUPS = 4
TOP_K = 8
EXPERT_FF = 256
SHARED_FF = 256
ROUTED_SCALE = 2.5
DEEPNORM_ALPHA = 2.0 ** 0.25
LN_EPS = 1e-5
RMS_EPS = 1e-6
L2_EPS = 1e-6

COL_DNQ = 0
COL_DNK = DN_WIDTH
COL_DNV = 2 * DN_WIDTH
COL_Z = 3 * DN_WIDTH
COL_SQ = 4 * DN_WIDTH
COL_SK = COL_SQ + SWA_WIDTH
COL_SV = COL_SK + SWA_KV_WIDTH
MAIN_COLS = COL_SV + SWA_KV_WIDTH
GATE_COLS = 128

TM_PROJ = 1024
TS_DN = 256
DN_SEQS = 4
DN_A_UNROLL = 4
TM_POST = 1024
POST_SPLIT = 2
TT_ROUTE = 512
TT_PLACE = 2048
BM_EXP = 256
EXP_GROUP = 2
EXP_SLOTS = 6
TT_COMB = 512
SC_NC = 2
SC_NS = 16
SC_NW = SC_NC * SC_NS
SC_WIN = 64
VMEM_LIMIT = 56 * 1024 * 1024
NEG_INF = float("-inf")


def _dot(a, b):
    return jnp.dot(a, b, preferred_element_type=F32)


def _mm(a, b):
    return _dot(a.astype(BF16), b.astype(BF16))


def _mm_nt(a, b):
    return lax.dot_general(a.astype(BF16), b.astype(BF16), (((1,), (1,)), ((), ())),
                           preferred_element_type=F32)


def _sigmoid(x):
    return 1.0 / (1.0 + jnp.exp(-x))


def _silu(x):
    return x * _sigmoid(x)


def _in_proj_kernel(x_ref, w_ref, wg_ref, main_ref, gates_ref):
    xb = x_ref[...].astype(BF16)
    main_ref[...] = _dot(xb, w_ref[...]).astype(BF16)
    gates_ref[...] = _dot(xb, wg_ref[...])


def _in_proj(x2d, w_main, w_gates):
    n = x2d.shape[0]
    return pl.pallas_call(
        _in_proj_kernel,
        grid=(n // TM_PROJ,),
        in_specs=[
            pl.BlockSpec((TM_PROJ, D_MODEL), lambda i: (i, 0)),
            pl.BlockSpec((D_MODEL, MAIN_COLS), lambda i: (0, 0)),
            pl.BlockSpec((D_MODEL, GATE_COLS), lambda i: (0, 0)),
        ],
        out_specs=[
            pl.BlockSpec((TM_PROJ, MAIN_COLS), lambda i: (i, 0)),
            pl.BlockSpec((TM_PROJ, GATE_COLS), lambda i: (i, 0)),
        ],
        out_shape=[
            jax.ShapeDtypeStruct((n, MAIN_COLS), BF16),
            jax.ShapeDtypeStruct((n, GATE_COLS), F32),
        ],
        compiler_params=pltpu.CompilerParams(
            dimension_semantics=("parallel",), vmem_limit_bytes=VMEM_LIMIT),
        name="in_proj",
    )(x2d, w_main, w_gates)


def _dn_kernel(x_ref, gates_ref, convw_ref, gpar_ref, normg_ref, out_ref,
               xc_ref, gl_ref, gc_ref, wq_ref, u_ref, kt_ref, attn_ref, egl_ref, s_ref, hist_ref):
    nseq = x_ref.shape[0]
    ts = x_ref.shape[1]
    c = DN_CHUNK
    hd = DN_HEAD_DIM
    qkv_w = 3 * DN_WIDTH
    nch = ts // c

    @pl.when(pl.program_id(1) == 0)
    def _():
        s_ref[...] = jnp.zeros_like(s_ref)
        hist_ref[...] = jnp.zeros_like(hist_ref)

    def stage_inputs(bi, carry):
        xc_ref[bi, 0:8, :] = hist_ref[bi]
        xc_ref[bi, 8:ts + 8, :] = x_ref[bi, :, 0:qkv_w].astype(F32)
        hist_ref[bi] = xc_ref[bi, ts:ts + 8, :]
        gsl = gates_ref[bi]
        sp_in = gsl + gpar_ref[1:2, :]
        softplus = jnp.maximum(sp_in, 0.0) + jnp.log(1.0 + jnp.exp(-jnp.abs(sp_in)))
        lane = lax.broadcasted_iota(I32, gsl.shape, 1)
        gl = jnp.where(lane < DN_HEADS, _sigmoid(gsl), -jnp.exp(gpar_ref[0:1, :]) * softplus)
        gl_ref[bi] = gl
        row_in_chunk = lax.broadcasted_iota(I32, gsl.shape, 0) % c
        gc = gl
        shift = 1
        while shift < c:
            gc = gc + jnp.where(row_in_chunk >= shift, pltpu.roll(gc, shift, 0), 0.0)
            shift *= 2
        gc_ref[bi] = gc
        return carry

    lax.fori_loop(0, nseq, stage_inputs, 0)

    ii = lax.broadcasted_iota(I32, (c, c), 0)
    jj = lax.broadcasted_iota(I32, (c, c), 1)
    tri_incl = ii >= jj
    tri_strict = ii > jj
    eye = jnp.where(ii == jj, 1.0, 0.0).astype(F32)
    heads = range(DN_HEADS)

    def conv_silu(bi, r0, col):
        w = convw_ref[:, col:col + hd]
        xt = xc_ref[bi, pl.ds(r0, c + 8), col:col + hd]
        y = w[CONV_WIDTH - 1:CONV_WIDTH, :] * xt[8:8 + c, :]
        for delay in range(1, CONV_WIDTH):
            tap = CONV_WIDTH - 1 - delay
            y = y + w[tap:tap + 1, :] * pltpu.roll(xt, delay, 0)[8:8 + c, :]
        return _silu(y)

    def l2n(t, scale):
        return t * (lax.rsqrt(jnp.sum(t * t, axis=-1, keepdims=True) + L2_EPS) * scale)

    def phase_a(ci):
        r0 = ci * c
        chains = []
        for bi in range(nseq):
            glc = gl_ref[bi, pl.ds(r0, c), :]
            gcc = gc_ref[bi, pl.ds(r0, c), :]
            gct = jnp.concatenate([gcc, gcc], axis=0).T
            egl_ref[bi, ci] = jnp.exp(gcc[c - 8:c, :])
            for h in heads:
                chains.append((bi, h, glc, gcc, gct))
        nchain = len(chains)
        q = [l2n(conv_silu(bi, r0, COL_DNQ + h * hd), hd ** -0.5) for (bi, h, _, _, _) in chains]
        k = [l2n(conv_silu(bi, r0, COL_DNK + h * hd), 1.0) for (bi, h, _, _, _) in chains]
        v = [conv_silu(bi, r0, COL_DNV + h * hd) for (bi, h, _, _, _) in chains]
        kb, vb, decay, egc = [], [], [], []
        for n_, (bi, h, glc, gcc, gct) in enumerate(chains):
            beta = glc[:, h:h + 1]
            gc_col = gcc[:, DN_HEADS + h:DN_HEADS + h + 1]
            gc_row = gct[DN_HEADS + h:DN_HEADS + h + 1, 0:c]
            decay.append(jnp.where(tri_incl, jnp.exp(jnp.minimum(gc_col - gc_row, 0.0)), 0.0))
            egc.append(jnp.exp(gc_col))
            e_tail = jnp.exp(gcc[c - 1:c, DN_HEADS + h:DN_HEADS + h + 1] - gc_col)
            kb.append(k[n_] * beta)
            vb.append(v[n_] * beta)
            kt_ref[bi, ci, h] = (k[n_] * e_tail).astype(BF16)
        kq = [_mm_nt(jnp.concatenate([kb[n_], q[n_]], axis=0), k[n_]) for n_ in range(nchain)]
        a_mat = [jnp.where(tri_strict, kq[n_][0:c] * decay[n_], 0.0) for n_ in range(nchain)]
        for n_, (bi, h, _, _, _) in enumerate(chains):
            attn_ref[bi, ci, h] = (kq[n_][c:2 * c] * decay[n_]).astype(BF16)
        t_inv = [eye - a for a in a_mat]
        p = a_mat
        for _ in range(5):
            p = [_mm(x, x) for x in p]
            t_inv = [t + _mm(t, x) for t, x in zip(t_inv, p)]
        for n_, (bi, h, _, _, _) in enumerate(chains):
            uw = _mm(t_inv[n_], jnp.concatenate([vb[n_], kb[n_] * egc[n_]], axis=1))
            u_ref[bi, ci, h] = uw[:, 0:hd]
            wq_ref[bi, ci, h, 0:c, :] = uw[:, hd:2 * hd].astype(BF16)
            wq_ref[bi, ci, h, c:2 * c, :] = (q[n_] * egc[n_]).astype(BF16)

    normg = normg_ref[...]

    def phase_b(ci):
        rows = pl.ds(ci * c, c)
        chains = [(bi, h) for bi in range(nseq) for h in heads]
        egl = [egl_ref[bi, ci] for bi in range(nseq)]
        s_old = [s_ref[bi, h] for bi, h in chains]
        ws = [_dot(wq_ref[bi, ci, h], s.astype(BF16)) for (bi, h), s in zip(chains, s_old)]
        v_new = [(u_ref[bi, ci, h] - w[0:c]).astype(BF16) for (bi, h), w in zip(chains, ws)]
        for n_, (bi, h) in enumerate(chains):
            s_ref[bi, h] = (s_old[n_] * egl[bi][7:8, DN_HEADS + h:DN_HEADS + h + 1]
                            + lax.dot_general(kt_ref[bi, ci, h], v_new[n_], (((0,), (0,)), ((), ())),
                                              preferred_element_type=F32))
        for n_, (bi, h) in enumerate(chains):
            o = ws[n_][c:2 * c] + _dot(attn_ref[bi, ci, h], v_new[n_])
            o = o * lax.rsqrt(jnp.mean(o * o, axis=-1, keepdims=True) + RMS_EPS) * normg
            z = x_ref[bi, rows, COL_Z + h * hd:COL_Z + (h + 1) * hd].astype(F32)
            out_ref[bi, rows, h * hd:(h + 1) * hd] = (o * _silu(z)).astype(out_ref.dtype)

    phase_a(0)
    for ci in range(nch):
        if ci + 1 < nch:
            phase_a(ci + 1)
        phase_b(ci)


def _deltanet(main3d, gates3d, conv_w, gpar, normg):
    b, t, _ = main3d.shape
    ts = TS_DN
    nseq = DN_SEQS
    nch = ts // DN_CHUNK
    dn_in = COL_Z + DN_WIDTH
    return pl.pallas_call(
        _dn_kernel,
        grid=(b // nseq, t // ts),
        in_specs=[
            pl.BlockSpec((nseq, ts, dn_in), lambda bi, si: (bi, si, 0)),
            pl.BlockSpec((nseq, ts, GATE_COLS), lambda bi, si: (bi, si, 0)),
            pl.BlockSpec((CONV_WIDTH, 3 * DN_WIDTH), lambda bi, si: (0, 0)),
            pl.BlockSpec((2, GATE_COLS), lambda bi, si: (0, 0)),
            pl.BlockSpec((1, DN_HEAD_DIM), lambda bi, si: (0, 0)),
        ],
        out_specs=pl.BlockSpec((nseq, ts, DN_WIDTH), lambda bi, si: (bi, si, 0)),
        out_shape=jax.ShapeDtypeStruct((b, t, DN_WIDTH), BF16),
        scratch_shapes=[
            pltpu.VMEM((nseq, ts + 8, 3 * DN_WIDTH), F32),
            pltpu.VMEM((nseq, ts, GATE_COLS), F32),
            pltpu.VMEM((nseq, ts, GATE_COLS), F32),
            pltpu.VMEM((nseq, nch, DN_HEADS, 2 * DN_CHUNK, DN_HEAD_DIM), BF16),
            pltpu.VMEM((nseq, nch, DN_HEADS, DN_CHUNK, DN_HEAD_DIM), F32),
            pltpu.VMEM((nseq, nch, DN_HEADS, DN_CHUNK, DN_HEAD_DIM), BF16),
            pltpu.VMEM((nseq, nch, DN_HEADS, DN_CHUNK, DN_CHUNK), BF16),
            pltpu.VMEM((nseq, nch, 8, GATE_COLS), F32),
            pltpu.VMEM((nseq, DN_HEADS, DN_HEAD_DIM, DN_HEAD_DIM), F32),
            pltpu.VMEM((nseq, 8, 3 * DN_WIDTH), F32),
        ],
        compiler_params=pltpu.CompilerParams(
            dimension_semantics=("parallel", "arbitrary"), vmem_limit_bytes=VMEM_LIMIT),
        name="deltanet",
    )(main3d, gates3d, conv_w, gpar, normg)


def _swa_kernel(sinks_ref, q_ref, kp_ref, kc_ref, vp_ref, vc_ref, out_ref, bias_ref):
    n = pl.program_id(1)
    blk = SWA_BLOCK
    d = SWA_HEAD_DIM
    grp = SWA_Q_HEADS // SWA_KV_HEADS

    @pl.when((pl.program_id(0) == 0) & (n == 0))
    def _():
        qi = lax.broadcasted_iota(I32, (blk, 2 * blk), 0)
        kj = lax.broadcasted_iota(I32, (blk, 2 * blk), 1)
        dist = qi + blk - kj
        valid = (dist >= 0) & (dist < SWA_WINDOW)
        dist_f = dist.astype(F32)
        for hq in range(SWA_Q_HEADS):
            slope = 2.0 ** (-8.0 * (hq + 1.0) / SWA_Q_HEADS)
            bias_ref[hq] = jnp.where(valid, -slope * dist_f, NEG_INF)

    kcol = lax.broadcasted_iota(I32, (1, 2 * blk), 1)
    colmask = jnp.where((kcol >= blk) | (n > 0), 0.0, NEG_INF)
    q_all = q_ref[...] * (d ** -0.5)
    kband = [jnp.concatenate([kp_ref[:, hk * d:(hk + 1) * d], kc_ref[:, hk * d:(hk + 1) * d]], axis=0)
             for hk in range(SWA_KV_HEADS)]
    vband = [jnp.concatenate([vp_ref[:, hk * d:(hk + 1) * d], vc_ref[:, hk * d:(hk + 1) * d]], axis=0)
             for hk in range(SWA_KV_HEADS)]
    heads = range(SWA_Q_HEADS)
    scores = [_mm_nt(q_all[:, hq * d:(hq + 1) * d], kband[hq // grp]) for hq in heads]
    probs, denoms = [], []
    for hq in heads:
        s = scores[hq] + bias_ref[hq] + colmask
        sink = sinks_ref[hq]
        m = jnp.maximum(jnp.max(s, axis=-1, keepdims=True), sink)
        p = jnp.exp(s - m)
        denoms.append(jnp.sum(p, axis=-1, keepdims=True) + jnp.exp(sink - m))
        probs.append(p.astype(BF16))
    outs = [_dot(probs[hq], vband[hq // grp]) / denoms[hq] for hq in heads]
    out_ref[...] = jnp.concatenate(outs, axis=-1).astype(out_ref.dtype)


def _swa(main3d, sinks):
    b, t, _ = main3d.shape
    blk = SWA_BLOCK
    qb = COL_SQ // SWA_WIDTH
    kb = COL_SK // SWA_KV_WIDTH
    vb = COL_SV // SWA_KV_WIDTH
    grid_spec = pltpu.PrefetchScalarGridSpec(
        num_scalar_prefetch=1,
        grid=(b, t // blk),
        in_specs=[
            pl.BlockSpec((None, blk, SWA_WIDTH), lambda bi, ni, s: (bi, ni, qb)),
            pl.BlockSpec((None, blk, SWA_KV_WIDTH), lambda bi, ni, s: (bi, jnp.maximum(ni - 1, 0), kb)),
            pl.BlockSpec((None, blk, SWA_KV_WIDTH), lambda bi, ni, s: (bi, ni, kb)),
            pl.BlockSpec((None, blk, SWA_KV_WIDTH), lambda bi, ni, s: (bi, jnp.maximum(ni - 1, 0), vb)),
            pl.BlockSpec((None, blk, SWA_KV_WIDTH), lambda bi, ni, s: (bi, ni, vb)),
        ],
        out_specs=pl.BlockSpec((None, blk, SWA_WIDTH), lambda bi, ni, s: (bi, ni, 0)),
        scratch_shapes=[pltpu.VMEM((SWA_Q_HEADS, blk, 2 * blk), F32)],
    )
    return pl.pallas_call(
        _swa_kernel,
        grid_spec=grid_spec,
        out_shape=jax.ShapeDtypeStruct((b, t, SWA_WIDTH), BF16),
        compiler_params=pltpu.CompilerParams(
            dimension_semantics=("arbitrary", "arbitrary"), vmem_limit_bytes=VMEM_LIMIT),
        name="swa",
    )(sinks, main3d, main3d, main3d, main3d, main3d)


def _layer_norm(y, g, b):
    mu = jnp.mean(y, axis=-1, keepdims=True)
    yc = y - mu
    var = jnp.mean(yc * yc, axis=-1, keepdims=True)
    return yc * lax.rsqrt(var + LN_EPS) * g + b


def _pack_bf16_pair(lo, hi):
    lo_bits = lax.bitcast_convert_type(lo.astype(BF16).astype(F32), U32)
    hi_bits = lax.bitcast_convert_type(hi.astype(BF16).astype(F32), U32)
    return (hi_bits & jnp.uint32(0xFFFF0000)) | (lo_bits >> 16)


def _unpack_bf16_pair(packed):
    lo = lax.bitcast_convert_type(packed << 16, F32)
    hi = lax.bitcast_convert_type(packed & jnp.uint32(0xFFFF0000), F32)
    return lo, hi


def _post_mix_kernel(x_ref, dn_ref, swa_ref, wo_dn_ref, wo_swa_ref, g_ref, b_ref,
                     rw_ref, sg_ref, su_ref, sd_ref,
                     base_ref, xpk_ref, logit_ref):
    half = D_MODEL // 2
    tm = x_ref.shape[0]
    sub = tm // POST_SPLIT
    parts = [pl.ds(i * sub, sub) for i in range(POST_SPLIT)]
    mix = [_dot(dn_ref[r, :], wo_dn_ref[...]) + _dot(swa_ref[r, :], wo_swa_ref[...]) for r in parts]
    x1 = [_layer_norm(DEEPNORM_ALPHA * x_ref[r, :] + m, g_ref[...], b_ref[...]) for r, m in zip(parts, mix)]
    xh = [x.astype(BF16) for x in x1]
    for r, x in zip(parts, x1):
        xpk_ref[r, :] = _pack_bf16_pair(x[:, :half], x[:, half:])
    for r, x in zip(parts, xh):
        logit_ref[:, r] = lax.dot_general(rw_ref[...], x, (((1,), (1,)), ((), ())),
                                          preferred_element_type=F32)
    gate_up = [(_dot(x, sg_ref[...]), _dot(x, su_ref[...])) for x in xh]
    hmid = [(_silu(g) * u).astype(BF16) for g, u in gate_up]
    for r, x, h in zip(parts, x1, hmid):
        base_ref[r, :] = DEEPNORM_ALPHA * x + _dot(h, sd_ref[...])


def _post_mix(x2d, dn2d, swa2d, wo_dn, wo_swa, ln_g, ln_b, rw_t, sg, su, sd):
    n = x2d.shape[0]
    tm = TM_POST
    full = lambda shape: pl.BlockSpec(shape, lambda i: (0, 0))
    return pl.pallas_call(
        _post_mix_kernel,
        grid=(n // tm,),
        in_specs=[
            pl.BlockSpec((tm, D_MODEL), lambda i: (i, 0)),
            pl.BlockSpec((tm, DN_WIDTH), lambda i: (i, 0)),
            pl.BlockSpec((tm, SWA_WIDTH), lambda i: (i, 0)),
            full((DN_WIDTH, D_MODEL)), full((SWA_WIDTH, D_MODEL)),
            full((1, D_MODEL)), full((1, D_MODEL)),
            full((N_EXPERTS, D_MODEL)),
            full((D_MODEL, SHARED_FF)), full((D_MODEL, SHARED_FF)), full((SHARED_FF, D_MODEL)),
        ],
        out_specs=[
            pl.BlockSpec((tm, D_MODEL), lambda i: (i, 0)),
            pl.BlockSpec((tm, D_MODEL // 2), lambda i: (i, 0)),
            pl.BlockSpec((N_EXPERTS, tm), lambda i: (0, i)),
        ],
        out_shape=[
            jax.ShapeDtypeStruct((n, D_MODEL), F32),
            jax.ShapeDtypeStruct((n, D_MODEL // 2), U32),
            jax.ShapeDtypeStruct((N_EXPERTS, n), F32),
        ],
        compiler_params=pltpu.CompilerParams(
            dimension_semantics=("parallel",), vmem_limit_bytes=VMEM_LIMIT),
        name="post_mix",
    )(x2d, dn2d, swa2d, wo_dn, wo_swa, ln_g, ln_b, rw_t, sg, su, sd)


def _route_kernel(lg_ref, bias_ref, eidx_ref, gate_ref, rank_ref, cnt_ref, carry_ref, pick_ref):
    @pl.when(pl.program_id(0) == 0)
    def _():
        carry_ref[...] = jnp.zeros_like(carry_ref)

    tt = lg_ref.shape[1]
    scores = _sigmoid(lg_ref[...])
    sel = scores + bias_ref[...]

    iog = lax.broadcasted_iota(I32, (GROUP_SIZE, tt), 0)
    grp_rows = []
    for g in range(N_GROUPS):
        blk = sel[g * GROUP_SIZE:(g + 1) * GROUP_SIZE, :]
        m1 = jnp.max(blk, axis=0, keepdims=True)
        i1 = jnp.min(jnp.where(blk == m1, iog, GROUP_SIZE), axis=0, keepdims=True)
        m2 = jnp.max(jnp.where(iog == i1, NEG_INF, blk), axis=0, keepdims=True)
        grp_rows.append(m1 + m2)
    gs = jnp.concatenate(grp_rows, axis=0)

    io8 = lax.broadcasted_iota(I32, (N_GROUPS, tt), 0)
    gsel = jnp.zeros((N_GROUPS, tt), F32)
    for _ in range(TOPK_GROUPS):
        mg = jnp.max(gs, axis=0, keepdims=True)
        ig = jnp.min(jnp.where(gs == mg, io8, N_GROUPS), axis=0, keepdims=True)
        hit = io8 == ig
        gsel = jnp.where(hit, 1.0, gsel)
        gs = jnp.where(hit, NEG_INF, gs)

    val = jnp.concatenate(
        [jnp.where(gsel[g:g + 1, :] > 0.0, sel[g * GROUP_SIZE:(g + 1) * GROUP_SIZE, :], NEG_INF)
         for g in range(N_GROUPS)], axis=0)

    ioe = lax.broadcasted_iota(I32, (N_EXPERTS, tt), 0)
    v = val
    for _ in range(TOP_K):
        m = jnp.max(v, axis=0, keepdims=True)
        v = jnp.where(v >= m, NEG_INF, v)
    picked = jnp.where(val >= m, 1.0, 0.0)
    pick_ref[...] = picked
    n_off = jnp.max(jnp.abs(jnp.sum(picked, axis=0, keepdims=True) - TOP_K))

    @pl.when(n_off > 0.0)
    def _():
        v = val
        onehot = jnp.zeros((N_EXPERTS, tt), F32)
        for _ in range(TOP_K):
            m = jnp.max(v, axis=0, keepdims=True)
            ik = jnp.min(jnp.where(v == m, ioe, N_EXPERTS), axis=0, keepdims=True)
            hit = ioe == ik
            v = jnp.where(hit, NEG_INF, v)
            onehot = jnp.where(hit, 1.0, onehot)
        pick_ref[...] = onehot

    onehot = pick_ref[...]
    oh16 = onehot.astype(BF16)
    ei = lax.broadcasted_iota(I32, (N_EXPERTS, N_EXPERTS), 0)
    ej = lax.broadcasted_iota(I32, (N_EXPERTS, N_EXPERTS), 1)
    slot = _dot(jnp.where(ej < ei, 1.0, 0.0).astype(BF16), oh16)
    ti = lax.broadcasted_iota(I32, (tt, tt), 0)
    tj = lax.broadcasted_iota(I32, (tt, tt), 1)
    cum = _dot(oh16, jnp.where(ti < tj, 1.0, 0.0).astype(BF16)) + jnp.broadcast_to(
        carry_ref[:, 0:1], (N_EXPERTS, tt))
    key = jnp.where(onehot > 0.0, slot, float(TOP_K))
    packed = ioe.astype(F32) + float(N_EXPERTS) * cum
    packed_rows, gate_rows = [], []
    for k in range(TOP_K):
        at_k = key == float(k)
        packed_rows.append(jnp.sum(jnp.where(at_k, packed, 0.0), axis=0, keepdims=True))
        gate_rows.append(jnp.sum(jnp.where(at_k, scores, 0.0), axis=0, keepdims=True))
    gsum = gate_rows[0]
    for r in gate_rows[1:]:
        gsum = gsum + r
    gate_ref[...] = jnp.concatenate(gate_rows, axis=0) / gsum * ROUTED_SCALE
    packed_i = jnp.concatenate(packed_rows, axis=0).astype(I32)
    eidx_ref[...] = packed_i & (N_EXPERTS - 1)
    rank_ref[...] = packed_i >> (N_EXPERTS.bit_length() - 1)
    carry_ref[...] = carry_ref[...] + jnp.broadcast_to(
        jnp.sum(onehot, axis=1, keepdims=True), carry_ref.shape)
    cnt_ref[...] = carry_ref[...].astype(I32)


def _route(logits_t, bias_col):
    n = logits_t.shape[1]
    tt = TT_ROUTE
    row_spec = pl.BlockSpec((TOP_K, tt), lambda i: (0, i))
    return pl.pallas_call(
        _route_kernel,
        grid=(n // tt,),
        in_specs=[
            pl.BlockSpec((N_EXPERTS, tt), lambda i: (0, i)),
            pl.BlockSpec((N_EXPERTS, 1), lambda i: (0, 0)),
        ],
        out_specs=[row_spec, row_spec, row_spec,
                   pl.BlockSpec((N_EXPERTS, 128), lambda i: (0, 0))],
        out_shape=[
            jax.ShapeDtypeStruct((TOP_K, n), I32),
            jax.ShapeDtypeStruct((TOP_K, n), F32),
            jax.ShapeDtypeStruct((TOP_K, n), I32),
            jax.ShapeDtypeStruct((N_EXPERTS, 128), I32),
        ],
        scratch_shapes=[pltpu.VMEM((N_EXPERTS, 128), F32), pltpu.VMEM((N_EXPERTS, tt), F32)],
        compiler_params=pltpu.CompilerParams(
            dimension_semantics=("arbitrary",), vmem_limit_bytes=VMEM_LIMIT),
        name="route",
    )(logits_t, bias_col)


def _place_kernel(eidx_ref, rank_ref, pstart_ref, dest_ref):
    tt = eidx_ref.shape[1]
    ioe = lax.broadcasted_iota(I32, (N_EXPERTS, tt), 0)
    pstart = pstart_ref[...]
    rows = [jnp.sum(jnp.where(ioe == eidx_ref[k:k + 1, :], pstart, 0.0), axis=0, keepdims=True)
            for k in range(TOP_K)]
    dest_ref[...] = jnp.concatenate(rows, axis=0).astype(I32) + rank_ref[...]


def _place(eidx, rank, pstart_col):
    n = eidx.shape[1]
    tt = TT_PLACE
    row_spec = pl.BlockSpec((TOP_K, tt), lambda i: (0, i))
    return pl.pallas_call(
        _place_kernel,
        grid=(n // tt,),
        in_specs=[row_spec, row_spec, pl.BlockSpec((N_EXPERTS, 1), lambda i: (0, 0))],
        out_specs=row_spec,
        out_shape=jax.ShapeDtypeStruct((TOP_K, n), I32),
        compiler_params=pltpu.CompilerParams(
            dimension_semantics=("parallel",), vmem_limit_bytes=VMEM_LIMIT),
        name="place",
    )(eidx, rank, pstart_col)


def _sc_mesh():
    return plsc.VectorSubcoreMesh(core_axis_name="c", subcore_axis_name="s",
                                  num_cores=SC_NC, num_subcores=SC_NS)


def _sc_scatter_rows(rows, idx, nrows_out):
    n, d = rows.shape
    nk = idx.shape[0]
    per_w = n // SC_NW
    nwin = per_w // SC_WIN
    assert per_w * SC_NW == n and nwin * SC_WIN == per_w and nwin % 2 == 0

    @functools.partial(
        pl.kernel, mesh=_sc_mesh(),
        out_type=jax.ShapeDtypeStruct((nrows_out, d), rows.dtype),
        scratch_types=[
            pltpu.VMEM((nwin, nk, SC_WIN), I32),
            pltpu.VMEM((2, SC_WIN, d), rows.dtype),
            pltpu.SemaphoreType.DMA((2,)),
            pltpu.SemaphoreType.DMA((2,)),
        ],
        compiler_params=pltpu.CompilerParams(use_tc_tiling_on_sc=True),
        name="sc_scatter_rows",
    )
    def scatter_kernel(rows_hbm, idx_hbm, out_hbm, idx_v, rows_v, lsem, ssem):
        wid = lax.axis_index("s") * SC_NC + lax.axis_index("c")
        base = wid * per_w
        pltpu.sync_copy(idx_hbm.at[wid], idx_v)

        def load(w, slot):
            return pltpu.make_async_copy(
                rows_hbm.at[pl.ds(base + w * SC_WIN, SC_WIN)], rows_v.at[slot], lsem.at[slot])

        def scat(w, k, slot):
            return pltpu.make_async_copy(rows_v.at[slot], out_hbm.at[idx_v.at[w, k]], ssem.at[slot])

        load(0, 0).start()

        @pl.loop(0, nwin, step=2)
        def _(w0):
            for slot in range(2):
                w = w0 + slot
                load(w, slot).wait()

                @pl.when(w + 1 < nwin)
                def _():
                    @pl.when(w >= 1)
                    def _():
                        for k in range(nk):
                            scat(w - 1, k, 1 - slot).wait()
                    load(w + 1, 1 - slot).start()

                for k in range(nk):
                    scat(w, k, slot).start()

        for k in range(nk):
            scat(nwin - 2, k, 0).wait()
        for k in range(nk):
            scat(nwin - 1, k, 1).wait()

    idx4 = idx.reshape(nk, SC_NW, nwin, SC_WIN).transpose(1, 2, 0, 3)
    return scatter_kernel(rows, idx4)


def _expert_kernel(gstart_ref, cnt_ref, xs_hbm, wg_ref, wu_ref, wd_ref, y_hbm,
                   wgb_ref, wub_ref, wdb_ref, xbuf_ref, ybuf_ref, xsem, ysem):
    e = pl.program_id(0)
    ne = pl.num_programs(0)
    bm = xbuf_ref.shape[1]
    nblk = y_hbm.shape[0] // bm
    half = D_MODEL // 2
    g_lo = gstart_ref[e]
    g_hi = gstart_ref[e + 1]
    g_end = gstart_ref[ne]

    def x_copy(g, slot):
        return pltpu.make_async_copy(xs_hbm.at[pl.ds(g * bm, bm), :], xbuf_ref.at[slot], xsem.at[slot])

    def y_copy(g, slot):
        return pltpu.make_async_copy(ybuf_ref.at[slot], y_hbm.at[pl.ds(g * bm, bm), :], ysem.at[slot])

    nslot = xbuf_ref.shape[0]

    ahead = nslot - EXP_GROUP

    @pl.when(e == 0)
    def _():
        for g0 in range(ahead):
            @pl.when(g0 < g_end)
            def _():
                x_copy(g0, g0).start()

    @pl.when(g_hi > g_lo)
    def _():
        wgb_ref[...] = wg_ref[...].astype(BF16)
        wub_ref[...] = wu_ref[...].astype(BF16)
        wdb_ref[...] = wd_ref[...].astype(BF16)

    row = lax.broadcasted_iota(I32, (bm, half), 0)

    def acquire(g):
        x_copy(g, g % nslot).wait()

        @pl.when(g + ahead < g_end)
        def _():
            x_copy(g + ahead, (g + ahead) % nslot).start()

        @pl.when(g >= nslot)
        def _():
            y_copy(g - nslot, g % nslot).wait()

    def load(g):
        n_valid = cnt_ref[e] - (g - g_lo) * bm
        x_lo, x_hi = _unpack_bf16_pair(jnp.where(row < n_valid, xbuf_ref[g % nslot], jnp.uint32(0)))
        return x_lo.astype(BF16), x_hi.astype(BF16)

    def gate_up(x):
        x_lo, x_hi = x
        gate = _dot(x_lo, wgb_ref[:half, :]) + _dot(x_hi, wgb_ref[half:, :])
        up = _dot(x_lo, wub_ref[:half, :]) + _dot(x_hi, wub_ref[half:, :])
        return gate, up

    def down(gu):
        gate, up = gu
        return _dot((_silu(gate) * up).astype(BF16), wdb_ref[...])

    def store(g, y):
        ybuf_ref[g % nslot] = _pack_bf16_pair(y[:, :half], y[:, half:])
        y_copy(g, g % nslot).start()

    def run_blocks(g, count):
        for j in range(count):
            acquire(g + j)
        gus = [gate_up(load(g + j)) for j in range(count)]
        ys = [down(gu) for gu in gus]
        for j in range(count):
            store(g + j, ys[j])

    def full_group(p, carry):
        run_blocks(g_lo + EXP_GROUP * p, EXP_GROUP)
        return carry

    n_own = g_hi - g_lo
    lax.fori_loop(0, n_own // EXP_GROUP, full_group, 0)
    size = EXP_GROUP // 2
    while size >= 1:
        @pl.when(n_own % (2 * size) >= size)
        def _(size=size):
            run_blocks(g_lo + n_own // (2 * size) * (2 * size), size)
        size //= 2

    @pl.when(e == ne - 1)
    def _():
        for back in range(nslot, 0, -1):
            @pl.when(g_end >= back)
            def _():
                y_copy(g_end - back, (g_end - back) % nslot).wait()

        ybuf_ref[0] = jnp.zeros((bm, half), U32)

        def fill(g, carry):
            y_copy(g, 0).start()
            return carry

        def drain(g, carry):
            y_copy(g, 0).wait()
            return carry

        lax.fori_loop(g_end, nblk, fill, 0)
        lax.fori_loop(g_end, nblk, drain, 0)


def _experts(gstart, counts, xs, w_gate, w_up, w_down):
    bm = BM_EXP
    nblk = xs.shape[0] // bm
    half = D_MODEL // 2
    grid_spec = pltpu.PrefetchScalarGridSpec(
        num_scalar_prefetch=2,
        grid=(N_EXPERTS,),
        in_specs=[
            pl.BlockSpec(memory_space=pl.ANY),
            pl.BlockSpec((None, D_MODEL, EXPERT_FF), lambda e, gs, cn: (e, 0, 0)),
            pl.BlockSpec((None, D_MODEL, EXPERT_FF), lambda e, gs, cn: (e, 0, 0)),
            pl.BlockSpec((None, EXPERT_FF, D_MODEL), lambda e, gs, cn: (e, 0, 0)),
        ],
        out_specs=pl.BlockSpec(memory_space=pl.ANY),
        scratch_shapes=[
            pltpu.VMEM((D_MODEL, EXPERT_FF), BF16),
            pltpu.VMEM((D_MODEL, EXPERT_FF), BF16),
            pltpu.VMEM((EXPERT_FF, D_MODEL), BF16),
            pltpu.VMEM((EXP_SLOTS, bm, half), U32),
            pltpu.VMEM((EXP_SLOTS, bm, half), U32),
            pltpu.SemaphoreType.DMA((EXP_SLOTS,)),
            pltpu.SemaphoreType.DMA((EXP_SLOTS,)),
        ],
    )
    return pl.pallas_call(
        _expert_kernel,
        grid_spec=grid_spec,
        out_shape=jax.ShapeDtypeStruct((nblk * bm, half), U32),
        compiler_params=pltpu.CompilerParams(
            dimension_semantics=("arbitrary",), vmem_limit_bytes=VMEM_LIMIT),
        name="experts",
    )(gstart, counts, xs, w_gate, w_up, w_down)


def _sc_gather_rows(table, idx):
    nrows = idx.shape[0]
    d = table.shape[1]
    per_w = nrows // SC_NW
    nwin = per_w // SC_WIN
    assert per_w * SC_NW == nrows and nwin * SC_WIN == per_w and nwin % 2 == 0
    @functools.partial(
        pl.kernel, mesh=_sc_mesh(),
        out_type=jax.ShapeDtypeStruct((nrows, d), table.dtype),
        scratch_types=[
            pltpu.VMEM((nwin, SC_WIN), I32),
            pltpu.VMEM((2, SC_WIN, d), table.dtype),
            pltpu.SemaphoreType.DMA((2,)),
            pltpu.SemaphoreType.DMA((2,)),
        ],
        compiler_params=pltpu.CompilerParams(use_tc_tiling_on_sc=True),
        name="sc_gather_rows",
    )
    def gather_kernel(table_hbm, idx_hbm, out_hbm, idx_v, rows_v, gsem, wsem):
        wid = lax.axis_index("s") * SC_NC + lax.axis_index("c")
        base = wid * per_w
        pltpu.sync_copy(idx_hbm.at[wid], idx_v)

        def gather(w, slot):
            return pltpu.make_async_copy(table_hbm.at[idx_v.at[w]], rows_v.at[slot], gsem.at[slot])

        def put(w, slot):
            return pltpu.make_async_copy(
                rows_v.at[slot], out_hbm.at[pl.ds(base + w * SC_WIN, SC_WIN)], wsem.at[slot])

        gather(0, 0).start()

        @pl.loop(0, nwin, step=2)
        def _(w0):
            for slot in range(2):
                w = w0 + slot
                gather(w, slot).wait()

                @pl.when(w + 1 < nwin)
                def _():
                    @pl.when(w >= 1)
                    def _():
                        put(w - 1, 1 - slot).wait()
                    gather(w + 1, 1 - slot).start()

                put(w, slot).start()

        put(nwin - 2, 0).wait()
        put(nwin - 1, 1).wait()

    return gather_kernel(table, idx.reshape(SC_NW, nwin, SC_WIN))


def _combine_kernel(y_ref, base_ref, gate_ref, g_ref, b_ref, out_ref):
    half = D_MODEL // 2
    gates = gate_ref[...]
    acc_lo = base_ref[:, :half]
    acc_hi = base_ref[:, half:]
    for k in range(TOP_K):
        y_lo, y_hi = _unpack_bf16_pair(y_ref[k])
        gk = gates[:, k:k + 1]
        acc_lo = acc_lo + gk * y_lo
        acc_hi = acc_hi + gk * y_hi
    mu = (jnp.sum(acc_lo, axis=-1, keepdims=True) + jnp.sum(acc_hi, axis=-1, keepdims=True)) / D_MODEL
    c_lo = acc_lo - mu
    c_hi = acc_hi - mu
    var = (jnp.sum(c_lo * c_lo, axis=-1, keepdims=True)
           + jnp.sum(c_hi * c_hi, axis=-1, keepdims=True)) / D_MODEL
    inv = lax.rsqrt(var + LN_EPS)
    out_ref[:, :half] = c_lo * inv * g_ref[:, :half] + b_ref[:, :half]
    out_ref[:, half:] = c_hi * inv * g_ref[:, half:] + b_ref[:, half:]


def _combine(ybuf, base, gate_tok, ln_g, ln_b):
    n = base.shape[0]
    tt = TT_COMB
    half = D_MODEL // 2
    return pl.pallas_call(
        _combine_kernel,
        grid=(n // tt,),
        in_specs=[
            pl.BlockSpec((TOP_K, tt, half), lambda i: (0, i, 0)),
            pl.BlockSpec((tt, D_MODEL), lambda i: (i, 0)),
            pl.BlockSpec((tt, TOP_K), lambda i: (i, 0)),
            pl.BlockSpec((1, D_MODEL), lambda i: (0, 0)),
            pl.BlockSpec((1, D_MODEL), lambda i: (0, 0)),
        ],
        out_specs=pl.BlockSpec((tt, D_MODEL), lambda i: (i, 0)),
        out_shape=jax.ShapeDtypeStruct((n, D_MODEL), F32),
        compiler_params=pltpu.CompilerParams(
            dimension_semantics=("parallel",), vmem_limit_bytes=VMEM_LIMIT),
        name="combine",
    )(ybuf, base, gate_tok, ln_g, ln_b)


def _regroup_w_in(w_in):
    o = 0
    cols = {}
    for name, width in (("dnq", DN_WIDTH), ("dnk", DN_WIDTH), ("dnv", DN_WIDTH), ("sq", SWA_WIDTH),
                        ("sk", SWA_KV_WIDTH), ("sv", SWA_KV_WIDTH), ("z", DN_WIDTH),
                        ("b", DN_HEADS), ("a", DN_HEADS)):
        cols[name] = w_in[:, o:o + width]
        o += width
    w_main = jnp.concatenate([cols[k] for k in ("dnq", "dnk", "dnv", "z", "sq", "sk", "sv")], axis=1)
    w_gates = jnp.concatenate(
        [cols["b"], cols["a"], jnp.zeros((D_MODEL, GATE_COLS - 2 * DN_HEADS), w_in.dtype)], axis=1)
    return w_main.astype(BF16), w_gates.astype(BF16)


def _layer(x, w_in, conv_w, a_log, dt_bias, dn_norm_g, sinks, w_out, ln1_g, ln1_b,
           router_w, router_bias, w_gate, w_up, w_down, sh_gate, sh_up, sh_down, ln2_g, ln2_b):
    b, t, d = x.shape
    n = b * t
    x2d = x.reshape(n, d)

    w_main, w_gates = _regroup_w_in(w_in)
    main, gates = _in_proj(x2d, w_main, w_gates)
    main3d = main.reshape(b, t, MAIN_COLS)

    pad = jnp.zeros((GATE_COLS - 2 * DN_HEADS,), F32)
    gpar = jnp.stack([jnp.concatenate([jnp.zeros((DN_HEADS,), F32), a_log.astype(F32), pad]),
                      jnp.concatenate([jnp.zeros((DN_HEADS,), F32), dt_bias.astype(F32), pad])])
    dn_out = _deltanet(main3d, gates.reshape(b, t, GATE_COLS), conv_w.astype(F32), gpar,
                       dn_norm_g.astype(F32).reshape(1, DN_HEAD_DIM))
    swa_out = _swa(main3d, sinks.astype(F32))

    base, xpk, logits_t = _post_mix(
        x2d, dn_out.reshape(n, DN_WIDTH), swa_out.reshape(n, SWA_WIDTH),
        w_out[:DN_WIDTH].astype(BF16), w_out[DN_WIDTH:].astype(BF16),
        ln1_g.reshape(1, d).astype(F32), ln1_b.reshape(1, d).astype(F32),
        router_w.T.astype(BF16),
        sh_gate.astype(BF16), sh_up.astype(BF16), sh_down.astype(BF16))

    eidx, gate, rank, cnt = _route(logits_t, router_bias.astype(F32).reshape(N_EXPERTS, 1))

    bm = BM_EXP
    counts = cnt[:, 0]
    padded = (counts + bm - 1) // bm * bm
    pend = jnp.cumsum(padded)
    pstart = pend - padded
    nblk = -(-(n * TOP_K) // bm) + N_EXPERTS
    gstart = (jnp.concatenate([pstart, pend[-1:]]) // bm).astype(I32)

    dest = _place(eidx, rank, pstart.astype(F32).reshape(N_EXPERTS, 1))
    xs = _sc_scatter_rows(xpk, dest, nblk * bm)
    ypk = _experts(gstart, counts, xs, w_gate, w_up, w_down)
    ybuf = _sc_gather_rows(ypk, dest.reshape(-1)).reshape(TOP_K, n, d // 2)
    out = _combine(ybuf, base, gate.T, ln2_g.reshape(1, d).astype(F32), ln2_b.reshape(1, d).astype(F32))
    return out.reshape(b, t, d)


def kernel(x, w_in, conv_w, a_log, dt_bias, dn_norm_g, sinks, w_out, ln1_g, ln1_b, router_w, router_bias,
           w_gate, w_up, w_down, shared_w_gate, shared_w_up, shared_w_down, ln2_g, ln2_b):
    depth = w_in.shape[0]
    for l in range(depth):
        x = _layer(x, w_in[l], conv_w[l], a_log[l], dt_bias[l], dn_norm_g[l], sinks[l], w_out[l],
                   ln1_g[l], ln1_b[l], router_w[l], router_bias[l], w_gate[l], w_up[l], w_down[l],
                   shared_w_gate[l], shared_w_up[l], shared_w_down[l], ln2_g[l], ln2_b[l])
    return x
```

```python
import functools

import jax
import jax.numpy as jnp
from jax import lax
from jax.experimental import pallas as pl
from jax.experimental.pallas import tpu as pltpu
from jax.experimental.pallas import tpu_sc as plsc

F32 = jnp.float32
BF16 = jnp.bfloat16
I32 = jnp.int32
U32 = jnp.uint32

D_MODEL = 1024
DN_HEADS = 4
DN_HEAD_DIM = 128
DN_WIDTH = DN_HEADS * DN_HEAD_DIM
CONV_WIDTH = 4
DN_CHUNK = 64
SWA_Q_HEADS = 8
SWA_KV_HEADS = 2
SWA_HEAD_DIM = 64
SWA_WIDTH = SWA_Q_HEADS * SWA_HEAD_DIM
SWA_KV_WIDTH = SWA_KV_HEADS * SWA_HEAD_DIM
SWA_WINDOW = 128
SWA_BLOCK = 128
N_EXPERTS = 256
N_GROUPS = 8
GROUP_SIZE = N_EXPERTS // N_GROUPS
TOPK_GROUPS = 4
TOP_K = 8
EXPERT_FF = 256
SHARED_FF = 256
ROUTED_SCALE = 2.5
DEEPNORM_ALPHA = 2.0 ** 0.25
LN_EPS = 1e-5
RMS_EPS = 1e-6
L2_EPS = 1e-6

COL_DNQ = 0
COL_DNK = DN_WIDTH
COL_DNV = 2 * DN_WIDTH
COL_Z = 3 * DN_WIDTH
COL_SQ = 4 * DN_WIDTH
COL_SK = COL_SQ + SWA_WIDTH
COL_SV = COL_SK + SWA_KV_WIDTH
MAIN_COLS = COL_SV + SWA_KV_WIDTH
GATE_COLS = 128

TM_PROJ = 1024
TS_DN = 256
DN_SEQS = 4
DN_A_UNROLL = 4
TM_POST = 1024
POST_SPLIT = 2
TT_ROUTE = 512
TT_PLACE = 2048
BM_EXP = 256
EXP_GROUP = 2
EXP_SLOTS = 6
TT_COMB = 512
SC_NC = 2
SC_NS = 16
SC_NW = SC_NC * SC_NS
SC_WIN = 64
SC_GATHER_SLOTS = 3
VMEM_LIMIT = 56 * 1024 * 1024
NEG_INF = float("-inf")


def _dot(a, b):
    return jnp.dot(a, b, preferred_element_type=F32)


def _mm(a, b):
    return _dot(a.astype(BF16), b.astype(BF16))


def _mm_nt(a, b):
    return lax.dot_general(a.astype(BF16), b.astype(BF16), (((1,), (1,)), ((), ())),
                           preferred_element_type=F32)


def _sigmoid(x):
    return 1.0 / (1.0 + jnp.exp(-x))


def _silu(x):
    return x * _sigmoid(x)


def _in_proj_kernel(x_ref, w_ref, wg_ref, main_ref, gates_ref):
    xb = x_ref[...].astype(BF16)
    main_ref[...] = _dot(xb, w_ref[...]).astype(BF16)
    gates_ref[...] = _dot(xb, wg_ref[...])


def _in_proj(x2d, w_main, w_gates):
    n = x2d.shape[0]
    return pl.pallas_call(
        _in_proj_kernel,
        grid=(n // TM_PROJ,),
        in_specs=[
            pl.BlockSpec((TM_PROJ, D_MODEL), lambda i: (i, 0)),
            pl.BlockSpec((D_MODEL, MAIN_COLS), lambda i: (0, 0)),
            pl.BlockSpec((D_MODEL, GATE_COLS), lambda i: (0, 0)),
        ],
        out_specs=[
            pl.BlockSpec((TM_PROJ, MAIN_COLS), lambda i: (i, 0)),
            pl.BlockSpec((TM_PROJ, GATE_COLS), lambda i: (i, 0)),
        ],
        out_shape=[
            jax.ShapeDtypeStruct((n, MAIN_COLS), BF16),
            jax.ShapeDtypeStruct((n, GATE_COLS), F32),
        ],
        compiler_params=pltpu.CompilerParams(
            dimension_semantics=("parallel",), vmem_limit_bytes=VMEM_LIMIT),
        name="in_proj",
    )(x2d, w_main, w_gates)


def _dn_kernel(x_ref, gates_ref, convw_ref, gpar_ref, normg_ref, out_ref,
               xc_ref, gl_ref, gc_ref, wq_ref, u_ref, kt_ref, attn_ref, egl_ref, s_ref, hist_ref):
    nseq = x_ref.shape[0]
    ts = x_ref.shape[1]
    c = DN_CHUNK
    hd = DN_HEAD_DIM
    qkv_w = 3 * DN_WIDTH
    nch = ts // c

    @pl.when(pl.program_id(1) == 0)
    def _():
        s_ref[...] = jnp.zeros_like(s_ref)
        hist_ref[...] = jnp.zeros_like(hist_ref)

    def stage_inputs(bi, carry):
        xc_ref[bi, 0:8, :] = hist_ref[bi]
        xc_ref[bi, 8:ts + 8, :] = x_ref[bi, :, 0:qkv_w].astype(F32)
        hist_ref[bi] = xc_ref[bi, ts:ts + 8, :]
        gsl = gates_ref[bi]
        sp_in = gsl + gpar_ref[1:2, :]
        softplus = jnp.maximum(sp_in, 0.0) + jnp.log(1.0 + jnp.exp(-jnp.abs(sp_in)))
        lane = lax.broadcasted_iota(I32, gsl.shape, 1)
        gl = jnp.where(lane < DN_HEADS, _sigmoid(gsl), -jnp.exp(gpar_ref[0:1, :]) * softplus)
        gl_ref[bi] = gl
        row_in_chunk = lax.broadcasted_iota(I32, gsl.shape, 0) % c
        gc = gl
        shift = 1
        while shift < c:
            gc = gc + jnp.where(row_in_chunk >= shift, pltpu.roll(gc, shift, 0), 0.0)
            shift *= 2
        gc_ref[bi] = gc
        return carry

    lax.fori_loop(0, nseq, stage_inputs, 0)

    ii = lax.broadcasted_iota(I32, (c, c), 0)
    jj = lax.broadcasted_iota(I32, (c, c), 1)
    tri_incl = ii >= jj
    tri_strict = ii > jj
    eye = jnp.where(ii == jj, 1.0, 0.0).astype(F32)
    heads = range(DN_HEADS)

    def conv_silu(bi, r0, col):
        w = convw_ref[:, col:col + hd]
        xt = xc_ref[bi, pl.ds(r0, c + 8), col:col + hd]
        y = w[CONV_WIDTH - 1:CONV_WIDTH, :] * xt[8:8 + c, :]
        for delay in range(1, CONV_WIDTH):
            tap = CONV_WIDTH - 1 - delay
            y = y + w[tap:tap + 1, :] * pltpu.roll(xt, delay, 0)[8:8 + c, :]
        return _silu(y)

    def l2n(t, scale):
        return t * (lax.rsqrt(jnp.sum(t * t, axis=-1, keepdims=True) + L2_EPS) * scale)

    def phase_a(ci):
        r0 = ci * c
        chains = []
        for bi in range(nseq):
            glc = gl_ref[bi, pl.ds(r0, c), :]
            gcc = gc_ref[bi, pl.ds(r0, c), :]
            gct = jnp.concatenate([gcc, gcc], axis=0).T
            egl_ref[bi, ci] = jnp.exp(gcc[c - 8:c, :])
            for h in heads:
                chains.append((bi, h, glc, gcc, gct))
        nchain = len(chains)
        q = [l2n(conv_silu(bi, r0, COL_DNQ + h * hd), hd ** -0.5) for (bi, h, _, _, _) in chains]
        k = [l2n(conv_silu(bi, r0, COL_DNK + h * hd), 1.0) for (bi, h, _, _, _) in chains]
        v = [conv_silu(bi, r0, COL_DNV + h * hd) for (bi, h, _, _, _) in chains]
        kb, vb, decay, egc = [], [], [], []
        for n_, (bi, h, glc, gcc, gct) in enumerate(chains):
            beta = glc[:, h:h + 1]
            gc_col = gcc[:, DN_HEADS + h:DN_HEADS + h + 1]
            gc_row = gct[DN_HEADS + h:DN_HEADS + h + 1, 0:c]
            decay.append(jnp.where(tri_incl, jnp.exp(jnp.minimum(gc_col - gc_row, 0.0)), 0.0))
            egc.append(jnp.exp(gc_col))
            e_tail = jnp.exp(gcc[c - 1:c, DN_HEADS + h:DN_HEADS + h + 1] - gc_col)
            kb.append(k[n_] * beta)
            vb.append(v[n_] * beta)
            kt_ref[bi, ci, h] = (k[n_] * e_tail).astype(BF16)
        kq = [_mm_nt(jnp.concatenate([kb[n_], q[n_]], axis=0), k[n_]) for n_ in range(nchain)]
        a_mat = [jnp.where(tri_strict, kq[n_][0:c] * decay[n_], 0.0) for n_ in range(nchain)]
        for n_, (bi, h, _, _, _) in enumerate(chains):
            attn_ref[bi, ci, h] = (kq[n_][c:2 * c] * decay[n_]).astype(BF16)
        t_inv = [eye - a for a in a_mat]
        p = a_mat
        for _ in range(5):
            p = [_mm(x, x) for x in p]
            t_inv = [t + _mm(t, x) for t, x in zip(t_inv, p)]
        for n_, (bi, h, _, _, _) in enumerate(chains):
            uw = _mm(t_inv[n_], jnp.concatenate([vb[n_], kb[n_] * egc[n_]], axis=1))
            u_ref[bi, ci, h] = uw[:, 0:hd]
            wq_ref[bi, ci, h, 0:c, :] = uw[:, hd:2 * hd].astype(BF16)
            wq_ref[bi, ci, h, c:2 * c, :] = (q[n_] * egc[n_]).astype(BF16)

    normg = normg_ref[...]

    def phase_b(ci):
        rows = pl.ds(ci * c, c)
        chains = [(bi, h) for bi in range(nseq) for h in heads]
        egl = [egl_ref[bi, ci] for bi in range(nseq)]
        s_old = [s_ref[bi, h] for bi, h in chains]
        ws = [_dot(wq_ref[bi, ci, h], s.astype(BF16)) for (bi, h), s in zip(chains, s_old)]
        v_new = [(u_ref[bi, ci, h] - w[0:c]).astype(BF16) for (bi, h), w in zip(chains, ws)]
        for n_, (bi, h) in enumerate(chains):
            s_ref[bi, h] = (s_old[n_] * egl[bi][7:8, DN_HEADS + h:DN_HEADS + h + 1]
                            + lax.dot_general(kt_ref[bi, ci, h], v_new[n_], (((0,), (0,)), ((), ())),
                                              preferred_element_type=F32))
        for n_, (bi, h) in enumerate(chains):
            o = ws[n_][c:2 * c] + _dot(attn_ref[bi, ci, h], v_new[n_])
            o = o * lax.rsqrt(jnp.mean(o * o, axis=-1, keepdims=True) + RMS_EPS) * normg
            z = x_ref[bi, rows, COL_Z + h * hd:COL_Z + (h + 1) * hd].astype(F32)
            out_ref[bi, rows, h * hd:(h + 1) * hd] = (o * _silu(z)).astype(out_ref.dtype)

    phase_a(0)
    for ci in range(nch):
        if ci + 1 < nch:
            phase_a(ci + 1)
        phase_b(ci)


def _deltanet(main3d, gates3d, conv_w, gpar, normg):
    b, t, _ = main3d.shape
    ts = TS_DN
    nseq = DN_SEQS
    nch = ts // DN_CHUNK
    dn_in = COL_Z + DN_WIDTH
    return pl.pallas_call(
        _dn_kernel,
        grid=(b // nseq, t // ts),
        in_specs=[
            pl.BlockSpec((nseq, ts, dn_in), lambda bi, si: (bi, si, 0)),
            pl.BlockSpec((nseq, ts, GATE_COLS), lambda bi, si: (bi, si, 0)),
            pl.BlockSpec((CONV_WIDTH, 3 * DN_WIDTH), lambda bi, si: (0, 0)),
            pl.BlockSpec((2, GATE_COLS), lambda bi, si: (0, 0)),
            pl.BlockSpec((1, DN_HEAD_DIM), lambda bi, si: (0, 0)),
        ],
        out_specs=pl.BlockSpec((nseq, ts, DN_WIDTH), lambda bi, si: (bi, si, 0)),
        out_shape=jax.ShapeDtypeStruct((b, t, DN_WIDTH), BF16),
        scratch_shapes=[
            pltpu.VMEM((nseq, ts + 8, 3 * DN_WIDTH), F32),
            pltpu.VMEM((nseq, ts, GATE_COLS), F32),
            pltpu.VMEM((nseq, ts, GATE_COLS), F32),
            pltpu.VMEM((nseq, nch, DN_HEADS, 2 * DN_CHUNK, DN_HEAD_DIM), BF16),
            pltpu.VMEM((nseq, nch, DN_HEADS, DN_CHUNK, DN_HEAD_DIM), F32),
            pltpu.VMEM((nseq, nch, DN_HEADS, DN_CHUNK, DN_HEAD_DIM), BF16),
            pltpu.VMEM((nseq, nch, DN_HEADS, DN_CHUNK, DN_CHUNK), BF16),
            pltpu.VMEM((nseq, nch, 8, GATE_COLS), F32),
            pltpu.VMEM((nseq, DN_HEADS, DN_HEAD_DIM, DN_HEAD_DIM), F32),
            pltpu.VMEM((nseq, 8, 3 * DN_WIDTH), F32),
        ],
        compiler_params=pltpu.CompilerParams(
            dimension_semantics=("parallel", "arbitrary"), vmem_limit_bytes=VMEM_LIMIT),
        name="deltanet",
    )(main3d, gates3d, conv_w, gpar, normg)


def _swa_kernel(sinks_ref, q_ref, kp_ref, kc_ref, vp_ref, vc_ref, out_ref, bias_ref):
    n = pl.program_id(1)
    blk = SWA_BLOCK
    d = SWA_HEAD_DIM
    grp = SWA_Q_HEADS // SWA_KV_HEADS

    @pl.when((pl.program_id(0) == 0) & (n == 0))
    def _():
        qi = lax.broadcasted_iota(I32, (blk, 2 * blk), 0)
        kj = lax.broadcasted_iota(I32, (blk, 2 * blk), 1)
        dist = qi + blk - kj
        valid = (dist >= 0) & (dist < SWA_WINDOW)
        dist_f = dist.astype(F32)
        for hq in range(SWA_Q_HEADS):
            slope = 2.0 ** (-8.0 * (hq + 1.0) / SWA_Q_HEADS)
            bias_ref[hq] = jnp.where(valid, -slope * dist_f, NEG_INF)

    kcol = lax.broadcasted_iota(I32, (1, 2 * blk), 1)
    colmask = jnp.where((kcol >= blk) | (n > 0), 0.0, NEG_INF)
    q_all = q_ref[...] * (d ** -0.5)
    kband = [jnp.concatenate([kp_ref[:, hk * d:(hk + 1) * d], kc_ref[:, hk * d:(hk + 1) * d]], axis=0)
             for hk in range(SWA_KV_HEADS)]
    vband = [jnp.concatenate([vp_ref[:, hk * d:(hk + 1) * d], vc_ref[:, hk * d:(hk + 1) * d]], axis=0)
             for hk in range(SWA_KV_HEADS)]
    heads = range(SWA_Q_HEADS)
    scores = [_mm_nt(q_all[:, hq * d:(hq + 1) * d], kband[hq // grp]) for hq in heads]
    probs, denoms = [], []
    for hq in heads:
        s = scores[hq] + bias_ref[hq] + colmask
        sink = sinks_ref[hq]
        m = jnp.maximum(jnp.max(s, axis=-1, keepdims=True), sink)
        p = jnp.exp(s - m)
        denoms.append(jnp.sum(p, axis=-1, keepdims=True) + jnp.exp(sink - m))
        probs.append(p.astype(BF16))
    outs = [_dot(probs[hq], vband[hq // grp]) / denoms[hq] for hq in heads]
    out_ref[...] = jnp.concatenate(outs, axis=-1).astype(out_ref.dtype)


def _swa(main3d, sinks):
    b, t, _ = main3d.shape
    blk = SWA_BLOCK
    qb = COL_SQ // SWA_WIDTH
    kb = COL_SK // SWA_KV_WIDTH
    vb = COL_SV // SWA_KV_WIDTH
    grid_spec = pltpu.PrefetchScalarGridSpec(
        num_scalar_prefetch=1,
        grid=(b, t // blk),
        in_specs=[
            pl.BlockSpec((None, blk, SWA_WIDTH), lambda bi, ni, s: (bi, ni, qb)),
            pl.BlockSpec((None, blk, SWA_KV_WIDTH), lambda bi, ni, s: (bi, jnp.maximum(ni - 1, 0), kb)),
            pl.BlockSpec((None, blk, SWA_KV_WIDTH), lambda bi, ni, s: (bi, ni, kb)),
            pl.BlockSpec((None, blk, SWA_KV_WIDTH), lambda bi, ni, s: (bi, jnp.maximum(ni - 1, 0), vb)),
            pl.BlockSpec((None, blk, SWA_KV_WIDTH), lambda bi, ni, s: (bi, ni, vb)),
        ],
        out_specs=pl.BlockSpec((None, blk, SWA_WIDTH), lambda bi, ni, s: (bi, ni, 0)),
        scratch_shapes=[pltpu.VMEM((SWA_Q_HEADS, blk, 2 * blk), F32)],
    )
    return pl.pallas_call(
        _swa_kernel,
        grid_spec=grid_spec,
        out_shape=jax.ShapeDtypeStruct((b, t, SWA_WIDTH), BF16),
        compiler_params=pltpu.CompilerParams(
            dimension_semantics=("arbitrary", "arbitrary"), vmem_limit_bytes=VMEM_LIMIT),
        name="swa",
    )(sinks, main3d, main3d, main3d, main3d, main3d)


def _layer_norm(y, g, b):
    mu = jnp.mean(y, axis=-1, keepdims=True)
    yc = y - mu
    var = jnp.mean(yc * yc, axis=-1, keepdims=True)
    return yc * lax.rsqrt(var + LN_EPS) * g + b


def _pack_bf16_pair(lo, hi):
    lo_bits = lax.bitcast_convert_type(lo.astype(BF16).astype(F32), U32)
    hi_bits = lax.bitcast_convert_type(hi.astype(BF16).astype(F32), U32)
    return (hi_bits & jnp.uint32(0xFFFF0000)) | (lo_bits >> 16)


def _unpack_bf16_pair(packed):
    lo = lax.bitcast_convert_type(packed << 16, F32)
    hi = lax.bitcast_convert_type(packed & jnp.uint32(0xFFFF0000), F32)
    return lo, hi


def _post_mix_kernel(x_ref, dn_ref, swa_ref, wo_dn_ref, wo_swa_ref, g_ref, b_ref,
                     rw_ref, sg_ref, su_ref, sd_ref,
                     base_ref, xpk_ref, logit_ref):
    half = D_MODEL // 2
    tm = x_ref.shape[0]
    sub = tm // POST_SPLIT
    parts = [pl.ds(i * sub, sub) for i in range(POST_SPLIT)]
    mix = [_dot(dn_ref[r, :], wo_dn_ref[...]) + _dot(swa_ref[r, :], wo_swa_ref[...]) for r in parts]
    x1 = [_layer_norm(DEEPNORM_ALPHA * x_ref[r, :] + m, g_ref[...], b_ref[...]) for r, m in zip(parts, mix)]
    xh = [x.astype(BF16) for x in x1]
    for r, x in zip(parts, x1):
        xpk_ref[r, :] = _pack_bf16_pair(x[:, :half], x[:, half:])
    for r, x in zip(parts, xh):
        logit_ref[:, r] = lax.dot_general(rw_ref[...], x, (((1,), (1,)), ((), ())),
                                          preferred_element_type=F32)
    gate_up = [(_dot(x, sg_ref[...]), _dot(x, su_ref[...])) for x in xh]
    hmid = [(_silu(g) * u).astype(BF16) for g, u in gate_up]
    for r, x, h in zip(parts, x1, hmid):
        base_ref[r, :] = DEEPNORM_ALPHA * x + _dot(h, sd_ref[...])


def _post_mix(x2d, dn2d, swa2d, wo_dn, wo_swa, ln_g, ln_b, rw_t, sg, su, sd):
    n = x2d.shape[0]
    tm = TM_POST
    full = lambda shape: pl.BlockSpec(shape, lambda i: (0, 0))
    return pl.pallas_call(
        _post_mix_kernel,
        grid=(n // tm,),
        in_specs=[
            pl.BlockSpec((tm, D_MODEL), lambda i: (i, 0)),
            pl.BlockSpec((tm, DN_WIDTH), lambda i: (i, 0)),
            pl.BlockSpec((tm, SWA_WIDTH), lambda i: (i, 0)),
            full((DN_WIDTH, D_MODEL)), full((SWA_WIDTH, D_MODEL)),
            full((1, D_MODEL)), full((1, D_MODEL)),
            full((N_EXPERTS, D_MODEL)),
            full((D_MODEL, SHARED_FF)), full((D_MODEL, SHARED_FF)), full((SHARED_FF, D_MODEL)),
        ],
        out_specs=[
            pl.BlockSpec((tm, D_MODEL), lambda i: (i, 0)),
            pl.BlockSpec((tm, D_MODEL // 2), lambda i: (i, 0)),
            pl.BlockSpec((N_EXPERTS, tm), lambda i: (0, i)),
        ],
        out_shape=[
            jax.ShapeDtypeStruct((n, D_MODEL), F32),
            jax.ShapeDtypeStruct((n, D_MODEL // 2), U32),
            jax.ShapeDtypeStruct((N_EXPERTS, n), F32),
        ],
        compiler_params=pltpu.CompilerParams(
            dimension_semantics=("parallel",), vmem_limit_bytes=VMEM_LIMIT),
        name="post_mix",
    )(x2d, dn2d, swa2d, wo_dn, wo_swa, ln_g, ln_b, rw_t, sg, su, sd)


def _route_kernel(lg_ref, bias_ref, eidx_ref, gate_ref, rank_ref, cnt_ref, carry_ref, pick_ref):
    @pl.when(pl.program_id(0) == 0)
    def _():
        carry_ref[...] = jnp.zeros_like(carry_ref)

    tt = lg_ref.shape[1]
    scores = _sigmoid(lg_ref[...])
    sel = scores + bias_ref[...]

    iog = lax.broadcasted_iota(I32, (GROUP_SIZE, tt), 0)
    grp_rows = []
    for g in range(N_GROUPS):
        blk = sel[g * GROUP_SIZE:(g + 1) * GROUP_SIZE, :]
        m1 = jnp.max(blk, axis=0, keepdims=True)
        i1 = jnp.min(jnp.where(blk == m1, iog, GROUP_SIZE), axis=0, keepdims=True)
        m2 = jnp.max(jnp.where(iog == i1, NEG_INF, blk), axis=0, keepdims=True)
        grp_rows.append(m1 + m2)
    gs = jnp.concatenate(grp_rows, axis=0)

    io8 = lax.broadcasted_iota(I32, (N_GROUPS, tt), 0)
    gsel = jnp.zeros((N_GROUPS, tt), F32)
    for _ in range(TOPK_GROUPS):
        mg = jnp.max(gs, axis=0, keepdims=True)
        ig = jnp.min(jnp.where(gs == mg, io8, N_GROUPS), axis=0, keepdims=True)
        hit = io8 == ig
        gsel = jnp.where(hit, 1.0, gsel)
        gs = jnp.where(hit, NEG_INF, gs)

    val = jnp.concatenate(
        [jnp.where(gsel[g:g + 1, :] > 0.0, sel[g * GROUP_SIZE:(g + 1) * GROUP_SIZE, :], NEG_INF)
         for g in range(N_GROUPS)], axis=0)

    ioe = lax.broadcasted_iota(I32, (N_EXPERTS, tt), 0)
    v = val
    for _ in range(TOP_K):
        m = jnp.max(v, axis=0, keepdims=True)
        v = jnp.where(v >= m, NEG_INF, v)
    picked = jnp.where(val >= m, 1.0, 0.0)
    pick_ref[...] = picked
    n_off = jnp.max(jnp.abs(jnp.sum(picked, axis=0, keepdims=True) - TOP_K))

    @pl.when(n_off > 0.0)
    def _():
        v = val
        onehot = jnp.zeros((N_EXPERTS, tt), F32)
        for _ in range(TOP_K):
            m = jnp.max(v, axis=0, keepdims=True)
            ik = jnp.min(jnp.where(v == m, ioe, N_EXPERTS), axis=0, keepdims=True)
            hit = ioe == ik
            v = jnp.where(hit, NEG_INF, v)
            onehot = jnp.where(hit, 1.0, onehot)
        pick_ref[...] = onehot

    onehot = pick_ref[...]
    oh16 = onehot.astype(BF16)
    ei = lax.broadcasted_iota(I32, (N_EXPERTS, N_EXPERTS), 0)
    ej = lax.broadcasted_iota(I32, (N_EXPERTS, N_EXPERTS), 1)
    slot = _dot(jnp.where(ej < ei, 1.0, 0.0).astype(BF16), oh16)
    ti = lax.broadcasted_iota(I32, (tt, tt), 0)
    tj = lax.broadcasted_iota(I32, (tt, tt), 1)
    cum = _dot(oh16, jnp.where(ti < tj, 1.0, 0.0).astype(BF16)) + jnp.broadcast_to(
        carry_ref[:, 0:1], (N_EXPERTS, tt))
    key = jnp.where(onehot > 0.0, slot, float(TOP_K))
    packed = ioe.astype(F32) + float(N_EXPERTS) * cum
    packed_rows, gate_rows = [], []
    for k in range(TOP_K):
        at_k = key == float(k)
        packed_rows.append(jnp.sum(jnp.where(at_k, packed, 0.0), axis=0, keepdims=True))
        gate_rows.append(jnp.sum(jnp.where(at_k, scores, 0.0), axis=0, keepdims=True))
    gsum = gate_rows[0]
    for r in gate_rows[1:]:
        gsum = gsum + r
    gate_ref[...] = jnp.concatenate(gate_rows, axis=0) / gsum * ROUTED_SCALE
    packed_i = jnp.concatenate(packed_rows, axis=0).astype(I32)
    eidx_ref[...] = packed_i & (N_EXPERTS - 1)
    rank_ref[...] = packed_i >> (N_EXPERTS.bit_length() - 1)
    carry_ref[...] = carry_ref[...] + jnp.broadcast_to(
        jnp.sum(onehot, axis=1, keepdims=True), carry_ref.shape)
    cnt_ref[...] = carry_ref[...].astype(I32)


def _route(logits_t, bias_col):
    n = logits_t.shape[1]
    tt = TT_ROUTE
    row_spec = pl.BlockSpec((TOP_K, tt), lambda i: (0, i))
    return pl.pallas_call(
        _route_kernel,
        grid=(n // tt,),
        in_specs=[
            pl.BlockSpec((N_EXPERTS, tt), lambda i: (0, i)),
            pl.BlockSpec((N_EXPERTS, 1), lambda i: (0, 0)),
        ],
        out_specs=[row_spec, row_spec, row_spec,
                   pl.BlockSpec((N_EXPERTS, 128), lambda i: (0, 0))],
        out_shape=[
            jax.ShapeDtypeStruct((TOP_K, n), I32),
            jax.ShapeDtypeStruct((TOP_K, n), F32),
            jax.ShapeDtypeStruct((TOP_K, n), I32),
            jax.ShapeDtypeStruct((N_EXPERTS, 128), I32),
        ],
        scratch_shapes=[pltpu.VMEM((N_EXPERTS, 128), F32), pltpu.VMEM((N_EXPERTS, tt), F32)],
        compiler_params=pltpu.CompilerParams(
            dimension_semantics=("arbitrary",), vmem_limit_bytes=VMEM_LIMIT),
        name="route",
    )(logits_t, bias_col)


def _place_kernel(eidx_ref, rank_ref, pstart_ref, dest_ref):
    tt = eidx_ref.shape[1]
    ioe = lax.broadcasted_iota(I32, (N_EXPERTS, tt), 0)
    pstart = pstart_ref[...]
    rows = [jnp.sum(jnp.where(ioe == eidx_ref[k:k + 1, :], pstart, 0.0), axis=0, keepdims=True)
            for k in range(TOP_K)]
    dest_ref[...] = jnp.concatenate(rows, axis=0).astype(I32) + rank_ref[...]


def _place(eidx, rank, pstart_col):
    n = eidx.shape[1]
    tt = TT_PLACE
    row_spec = pl.BlockSpec((TOP_K, tt), lambda i: (0, i))
    return pl.pallas_call(
        _place_kernel,
        grid=(n // tt,),
        in_specs=[row_spec, row_spec, pl.BlockSpec((N_EXPERTS, 1), lambda i: (0, 0))],
        out_specs=row_spec,
        out_shape=jax.ShapeDtypeStruct((TOP_K, n), I32),
        compiler_params=pltpu.CompilerParams(
            dimension_semantics=("parallel",), vmem_limit_bytes=VMEM_LIMIT),
        name="place",
    )(eidx, rank, pstart_col)


def _when(cond):
    if isinstance(cond, bool):
        return (lambda fn: fn()) if cond else (lambda fn: None)
    return pl.when(cond)


def _sc_mesh():
    return plsc.VectorSubcoreMesh(core_axis_name="c", subcore_axis_name="s",
                                  num_cores=SC_NC, num_subcores=SC_NS)


def _sc_scatter_rows(rows, idx, nrows_out):
    n, d = rows.shape
    nk = idx.shape[0]
    per_w = n // SC_NW
    nwin = per_w // SC_WIN
    assert per_w * SC_NW == n and nwin * SC_WIN == per_w and nwin % 2 == 0

    @functools.partial(
        pl.kernel, mesh=_sc_mesh(),
        out_type=jax.ShapeDtypeStruct((nrows_out, d), rows.dtype),
        scratch_types=[
            pltpu.VMEM((nwin, nk, SC_WIN), I32),
            pltpu.VMEM((2, SC_WIN, d), rows.dtype),
            pltpu.SemaphoreType.DMA((2,)),
            pltpu.SemaphoreType.DMA((2,)),
        ],
        compiler_params=pltpu.CompilerParams(use_tc_tiling_on_sc=True),
        name="sc_scatter_rows",
    )
    def scatter_kernel(rows_hbm, idx_hbm, out_hbm, idx_v, rows_v, lsem, ssem):
        wid = lax.axis_index("s") * SC_NC + lax.axis_index("c")
        base = wid * per_w
        pltpu.sync_copy(idx_hbm.at[wid], idx_v)

        def load(w, slot):
            return pltpu.make_async_copy(
                rows_hbm.at[pl.ds(base + w * SC_WIN, SC_WIN)], rows_v.at[slot], lsem.at[slot])

        def scat(w, k, slot):
            return pltpu.make_async_copy(rows_v.at[slot], out_hbm.at[idx_v.at[w, k]], ssem.at[slot])

        load(0, 0).start()

        @pl.loop(0, nwin, step=2)
        def _(w0):
            for slot in range(2):
                w = w0 + slot
                load(w, slot).wait()

                @pl.when(w + 1 < nwin)
                def _():
                    @pl.when(w >= 1)
                    def _():
                        for k in range(nk):
                            scat(w - 1, k, 1 - slot).wait()
                    load(w + 1, 1 - slot).start()

                for k in range(nk):
                    scat(w, k, slot).start()

        for k in range(nk):
            scat(nwin - 2, k, 0).wait()
        for k in range(nk):
            scat(nwin - 1, k, 1).wait()

    idx4 = idx.reshape(nk, SC_NW, nwin, SC_WIN).transpose(1, 2, 0, 3)
    return scatter_kernel(rows, idx4)


def _expert_kernel(gstart_ref, cnt_ref, xs_hbm, wg_ref, wu_ref, wd_ref, y_hbm,
                   wgb_ref, wub_ref, wdb_ref, xbuf_ref, ybuf_ref, xsem, ysem):
    e = pl.program_id(0)
    ne = pl.num_programs(0)
    bm = xbuf_ref.shape[1]
    nblk = y_hbm.shape[0] // bm
    half = D_MODEL // 2
    g_lo = gstart_ref[e]
    g_hi = gstart_ref[e + 1]
    g_end = gstart_ref[ne]

    def x_copy(g, slot):
        return pltpu.make_async_copy(xs_hbm.at[pl.ds(g * bm, bm), :], xbuf_ref.at[slot], xsem.at[slot])

    def y_copy(g, slot):
        return pltpu.make_async_copy(ybuf_ref.at[slot], y_hbm.at[pl.ds(g * bm, bm), :], ysem.at[slot])

    nslot = xbuf_ref.shape[0]

    ahead = nslot - EXP_GROUP

    @pl.when(e == 0)
    def _():
        for g0 in range(ahead):
            @pl.when(g0 < g_end)
            def _():
                x_copy(g0, g0).start()

    @pl.when(g_hi > g_lo)
    def _():
        wgb_ref[...] = wg_ref[...].astype(BF16)
        wub_ref[...] = wu_ref[...].astype(BF16)
        wdb_ref[...] = wd_ref[...].astype(BF16)

    row = lax.broadcasted_iota(I32, (bm, half), 0)

    def acquire(g):
        x_copy(g, g % nslot).wait()

        @pl.when(g + ahead < g_end)
        def _():
            x_copy(g + ahead, (g + ahead) % nslot).start()

        @pl.when(g >= nslot)
        def _():
            y_copy(g - nslot, g % nslot).wait()

    def load(g):
        n_valid = cnt_ref[e] - (g - g_lo) * bm
        x_lo, x_hi = _unpack_bf16_pair(jnp.where(row < n_valid, xbuf_ref[g % nslot], jnp.uint32(0)))
        return x_lo.astype(BF16), x_hi.astype(BF16)

    def gate_up(x):
        x_lo, x_hi = x
        gate = _dot(x_lo, wgb_ref[:half, :]) + _dot(x_hi, wgb_ref[half:, :])
        up = _dot(x_lo, wub_ref[:half, :]) + _dot(x_hi, wub_ref[half:, :])
        return gate, up

    def down(gu):
        gate, up = gu
        return _dot((_silu(gate) * up).astype(BF16), wdb_ref[...])

    def store(g, y):
        ybuf_ref[g % nslot] = _pack_bf16_pair(y[:, :half], y[:, half:])
        y_copy(g, g % nslot).start()

    def run_blocks(g, count):
        for j in range(count):
            acquire(g + j)
        gus = [gate_up(load(g + j)) for j in range(count)]
        ys = [down(gu) for gu in gus]
        for j in range(count):
            store(g + j, ys[j])

    def full_group(p, carry):
        run_blocks(g_lo + EXP_GROUP * p, EXP_GROUP)
        return carry

    n_own = g_hi - g_lo
    lax.fori_loop(0, n_own // EXP_GROUP, full_group, 0)
    size = EXP_GROUP // 2
    while size >= 1:
        @pl.when(n_own % (2 * size) >= size)
        def _(size=size):
            run_blocks(g_lo + n_own // (2 * size) * (2 * size), size)
        size //= 2

    @pl.when(e == ne - 1)
    def _():
        for back in range(nslot, 0, -1):
            @pl.when(g_end >= back)
            def _():
                y_copy(g_end - back, (g_end - back) % nslot).wait()

        ybuf_ref[0] = jnp.zeros((bm, half), U32)

        def fill(g, carry):
            y_copy(g, 0).start()
            return carry

        def drain(g, carry):
            y_copy(g, 0).wait()
            return carry

        lax.fori_loop(g_end, nblk, fill, 0)
        lax.fori_loop(g_end, nblk, drain, 0)


def _experts(gstart, counts, xs, w_gate, w_up, w_down):
    bm = BM_EXP
    nblk = xs.shape[0] // bm
    half = D_MODEL // 2
    grid_spec = pltpu.PrefetchScalarGridSpec(
        num_scalar_prefetch=2,
        grid=(N_EXPERTS,),
        in_specs=[
            pl.BlockSpec(memory_space=pl.ANY),
            pl.BlockSpec((None, D_MODEL, EXPERT_FF), lambda e, gs, cn: (e, 0, 0)),
            pl.BlockSpec((None, D_MODEL, EXPERT_FF), lambda e, gs, cn: (e, 0, 0)),
            pl.BlockSpec((None, EXPERT_FF, D_MODEL), lambda e, gs, cn: (e, 0, 0)),
        ],
        out_specs=pl.BlockSpec(memory_space=pl.ANY),
        scratch_shapes=[
            pltpu.VMEM((D_MODEL, EXPERT_FF), BF16),
            pltpu.VMEM((D_MODEL, EXPERT_FF), BF16),
            pltpu.VMEM((EXPERT_FF, D_MODEL), BF16),
            pltpu.VMEM((EXP_SLOTS, bm, half), U32),
            pltpu.VMEM((EXP_SLOTS, bm, half), U32),
            pltpu.SemaphoreType.DMA((EXP_SLOTS,)),
            pltpu.SemaphoreType.DMA((EXP_SLOTS,)),
        ],
    )
    return pl.pallas_call(
        _expert_kernel,
        grid_spec=grid_spec,
        out_shape=jax.ShapeDtypeStruct((nblk * bm, half), U32),
        compiler_params=pltpu.CompilerParams(
            dimension_semantics=("arbitrary",), vmem_limit_bytes=VMEM_LIMIT),
        name="experts",
    )(gstart, counts, xs, w_gate, w_up, w_down)


def _sc_gather_rows(table, idx):
    nrows = idx.shape[0]
    d = table.shape[1]
    per_w = nrows // SC_NW
    nwin = per_w // SC_WIN
    nslot = SC_GATHER_SLOTS
    ahead = nslot - 1
    assert per_w * SC_NW == nrows and nwin * SC_WIN == per_w and nwin >= nslot

    @functools.partial(
        pl.kernel, mesh=_sc_mesh(),
        out_type=jax.ShapeDtypeStruct((nrows, d), table.dtype),
        scratch_types=[
            pltpu.VMEM((nwin, SC_WIN), I32),
            pltpu.VMEM((nslot, SC_WIN, d), table.dtype),
            pltpu.SemaphoreType.DMA((nslot,)),
            pltpu.SemaphoreType.DMA((nslot,)),
        ],
        compiler_params=pltpu.CompilerParams(use_tc_tiling_on_sc=True),
        name="sc_gather_rows",
    )
    def gather_kernel(table_hbm, idx_hbm, out_hbm, idx_v, rows_v, gsem, wsem):
        wid = lax.axis_index("s") * SC_NC + lax.axis_index("c")
        base = wid * per_w
        pltpu.sync_copy(idx_hbm.at[wid], idx_v)

        def gather(w, slot):
            return pltpu.make_async_copy(table_hbm.at[idx_v.at[w]], rows_v.at[slot], gsem.at[slot])

        def put(w, slot):
            return pltpu.make_async_copy(
                rows_v.at[slot], out_hbm.at[pl.ds(base + w * SC_WIN, SC_WIN)], wsem.at[slot])

        def step(w, slot):
            gather(w, slot).wait()
            refill = (slot + ahead) % nslot

            @_when(w + ahead < nwin)
            def _():
                @_when(w >= 1)
                def _():
                    put(w - 1, refill).wait()
                gather(w + ahead, refill).start()

            put(w, slot).start()

        for w in range(ahead):
            gather(w, w).start()

        nfull = nwin // nslot * nslot

        @pl.loop(0, nfull, step=nslot)
        def _(w0):
            for slot in range(nslot):
                step(w0 + slot, slot)

        for w in range(nfull, nwin):
            step(w, w % nslot)
        for w in range(nwin - nslot, nwin):
            put(w, w % nslot).wait()

    return gather_kernel(table, idx.reshape(SC_NW, nwin, SC_WIN))


def _combine_kernel(y_ref, base_ref, gate_ref, g_ref, b_ref, out_ref):
    half = D_MODEL // 2
    gates = gate_ref[...]
    acc_lo = base_ref[:, :half]
    acc_hi = base_ref[:, half:]
    for k in range(TOP_K):
        y_lo, y_hi = _unpack_bf16_pair(y_ref[k])
        gk = gates[:, k:k + 1]
        acc_lo = acc_lo + gk * y_lo
        acc_hi = acc_hi + gk * y_hi
    mu = (jnp.sum(acc_lo, axis=-1, keepdims=True) + jnp.sum(acc_hi, axis=-1, keepdims=True)) / D_MODEL
    c_lo = acc_lo - mu
    c_hi = acc_hi - mu
    var = (jnp.sum(c_lo * c_lo, axis=-1, keepdims=True)
           + jnp.sum(c_hi * c_hi, axis=-1, keepdims=True)) / D_MODEL
    inv = lax.rsqrt(var + LN_EPS)
    out_ref[:, :half] = c_lo * inv * g_ref[:, :half] + b_ref[:, :half]
    out_ref[:, half:] = c_hi * inv * g_ref[:, half:] + b_ref[:, half:]


def _combine(ybuf, base, gate_tok, ln_g, ln_b):
    n = base.shape[0]
    tt = TT_COMB
    half = D_MODEL // 2
    return pl.pallas_call(
        _combine_kernel,
        grid=(n // tt,),
        in_specs=[
            pl.BlockSpec((TOP_K, tt, half), lambda i: (0, i, 0)),
            pl.BlockSpec((tt, D_MODEL), lambda i: (i, 0)),
            pl.BlockSpec((tt, TOP_K), lambda i: (i, 0)),
            pl.BlockSpec((1, D_MODEL), lambda i: (0, 0)),
            pl.BlockSpec((1, D_MODEL), lambda i: (0, 0)),
        ],
        out_specs=pl.BlockSpec((tt, D_MODEL), lambda i: (i, 0)),
        out_shape=jax.ShapeDtypeStruct((n, D_MODEL), F32),
        compiler_params=pltpu.CompilerParams(
            dimension_semantics=("parallel",), vmem_limit_bytes=VMEM_LIMIT),
        name="combine",
    )(ybuf, base, gate_tok, ln_g, ln_b)


def _regroup_w_in(w_in):
    o = 0
    cols = {}
    for name, width in (("dnq", DN_WIDTH), ("dnk", DN_WIDTH), ("dnv", DN_WIDTH), ("sq", SWA_WIDTH),
                        ("sk", SWA_KV_WIDTH), ("sv", SWA_KV_WIDTH), ("z", DN_WIDTH),
                        ("b", DN_HEADS), ("a", DN_HEADS)):
        cols[name] = w_in[:, o:o + width]
        o += width
    w_main = jnp.concatenate([cols[k] for k in ("dnq", "dnk", "dnv", "z", "sq", "sk", "sv")], axis=1)
    w_gates = jnp.concatenate(
        [cols["b"], cols["a"], jnp.zeros((D_MODEL, GATE_COLS - 2 * DN_HEADS), w_in.dtype)], axis=1)
    return w_main.astype(BF16), w_gates.astype(BF16)


def _layer(x, w_in, conv_w, a_log, dt_bias, dn_norm_g, sinks, w_out, ln1_g, ln1_b,
           router_w, router_bias, w_gate, w_up, w_down, sh_gate, sh_up, sh_down, ln2_g, ln2_b):
    b, t, d = x.shape
    n = b * t
    x2d = x.reshape(n, d)

    w_main, w_gates = _regroup_w_in(w_in)
    main, gates = _in_proj(x2d, w_main, w_gates)
    main3d = main.reshape(b, t, MAIN_COLS)

    pad = jnp.zeros((GATE_COLS - 2 * DN_HEADS,), F32)
    gpar = jnp.stack([jnp.concatenate([jnp.zeros((DN_HEADS,), F32), a_log.astype(F32), pad]),
                      jnp.concatenate([jnp.zeros((DN_HEADS,), F32), dt_bias.astype(F32), pad])])
    dn_out = _deltanet(main3d, gates.reshape(b, t, GATE_COLS), conv_w.astype(F32), gpar,
                       dn_norm_g.astype(F32).reshape(1, DN_HEAD_DIM))
    swa_out = _swa(main3d, sinks.astype(F32))

    base, xpk, logits_t = _post_mix(
        x2d, dn_out.reshape(n, DN_WIDTH), swa_out.reshape(n, SWA_WIDTH),
        w_out[:DN_WIDTH].astype(BF16), w_out[DN_WIDTH:].astype(BF16),
        ln1_g.reshape(1, d).astype(F32), ln1_b.reshape(1, d).astype(F32),
        router_w.T.astype(BF16),
        sh_gate.astype(BF16), sh_up.astype(BF16), sh_down.astype(BF16))

    eidx, gate, rank, cnt = _route(logits_t, router_bias.astype(F32).reshape(N_EXPERTS, 1))

    bm = BM_EXP
    counts = cnt[:, 0]
    padded = (counts + bm - 1) // bm * bm
    pend = jnp.cumsum(padded)
    pstart = pend - padded
    nblk = -(-(n * TOP_K) // bm) + N_EXPERTS
    gstart = (jnp.concatenate([pstart, pend[-1:]]) // bm).astype(I32)

    dest = _place(eidx, rank, pstart.astype(F32).reshape(N_EXPERTS, 1))
    xs = _sc_scatter_rows(xpk, dest, nblk * bm)
    ypk = _experts(gstart, counts, xs, w_gate, w_up, w_down)
    ybuf = _sc_gather_rows(ypk, dest.reshape(-1)).reshape(TOP_K, n, d // 2)
    out = _combine(ybuf, base, gate.T, ln2_g.reshape(1, d).astype(F32), ln2_b.reshape(1, d).astype(F32))
    return out.reshape(b, t, d)


def kernel(x, w_in, conv_w, a_log, dt_bias, dn_norm_g, sinks, w_out, ln1_g, ln1_b, router_w, router_bias,
           w_gate, w_up, w_down, shared_w_gate, shared_w_up, shared_w_down, ln2_g, ln2_b):
    depth = w_in.shape[0]
    for l in range(depth):
        x = _layer(x, w_in[l], conv_w[l], a_log[l], dt_bias[l], dn_norm_g[l], sinks[l], w_out[l],
                   ln1_g[l], ln1_b[l], router_w[l], router_bias[l], w_gate[l], w_up[l], w_down[l],
                   shared_w_gate[l], shared_w_up[l], shared_w_down[l], ln2_g[l], ln2_b[l])
    return x
```

```python
import functools

import jax
import jax.numpy as jnp
from jax import lax
from jax.experimental import pallas as pl
from jax.experimental.pallas import tpu as pltpu
from jax.experimental.pallas import tpu_sc as plsc

F32 = jnp.float32
BF16 = jnp.bfloat16
I32 = jnp.int32
U32 = jnp.uint32

D_MODEL = 1024
DN_HEADS = 4
DN_HEAD_DIM = 128
DN_WIDTH = DN_HEADS * DN_HEAD_DIM
CONV_WIDTH = 4
DN_CHUNK = 64
SWA_Q_HEADS = 8
SWA_KV_HEADS = 2
SWA_HEAD_DIM = 64
SWA_WIDTH = SWA_Q_HEADS * SWA_HEAD_DIM
SWA_KV_WIDTH = SWA_KV_HEADS * SWA_HEAD_DIM
SWA_WINDOW = 128
SWA_BLOCK = 128
N_EXPERTS = 256
N_GROUPS = 8
GROUP_SIZE = N_EXPERTS // N_GROUPS
TOPK_GROUPS = 4
TOP_K = 8
EXPERT_FF = 256
SHARED_FF = 256
ROUTED_SCALE = 2.5
DEEPNORM_ALPHA = 2.0 ** 0.25
LN_EPS = 1e-5
RMS_EPS = 1e-6
L2_EPS = 1e-6

COL_DNQ = 0
COL_DNK = DN_WIDTH
COL_DNV = 2 * DN_WIDTH
COL_Z = 3 * DN_WIDTH
COL_SQ = 4 * DN_WIDTH
COL_SK = COL_SQ + SWA_WIDTH
COL_SV = COL_SK + SWA_KV_WIDTH
MAIN_COLS = COL_SV + SWA_KV_WIDTH
GATE_COLS = 128

TM_PROJ = 1024
TS_DN = 256
DN_SEQS = 4
TM_POST = 1024
POST_SPLIT = 2
TT_ROUTE = 1024
TT_PLACE = 2048
BM_EXP = 256
EXP_GROUP = 2
EXP_SLOTS = 6
TT_COMB = 512
SC_NC = 2
SC_NS = 16
SC_NW = SC_NC * SC_NS
SC_WIN = 64
VMEM_LIMIT = 56 * 1024 * 1024
NEG_INF = float("-inf")


def _dot(a, b):
    return jnp.dot(a, b, preferred_element_type=F32)


def _mm(a, b):
    return _dot(a.astype(BF16), b.astype(BF16))


def _mm_nt(a, b):
    return lax.dot_general(a.astype(BF16), b.astype(BF16), (((1,), (1,)), ((), ())),
                           preferred_element_type=F32)


def _sigmoid(x):
    return 1.0 / (1.0 + jnp.exp(-x))


def _silu(x):
    return x * _sigmoid(x)


def _in_proj_kernel(x_ref, w_ref, wg_ref, main_ref, gates_ref):
    xb = x_ref[...].astype(BF16)
    main_ref[...] = _dot(xb, w_ref[...]).astype(BF16)
    gates_ref[...] = _dot(xb, wg_ref[...])


def _in_proj(x2d, w_main, w_gates):
    n = x2d.shape[0]
    return pl.pallas_call(
        _in_proj_kernel,
        grid=(n // TM_PROJ,),
        in_specs=[
            pl.BlockSpec((TM_PROJ, D_MODEL), lambda i: (i, 0)),
            pl.BlockSpec((D_MODEL, MAIN_COLS), lambda i: (0, 0)),
            pl.BlockSpec((D_MODEL, GATE_COLS), lambda i: (0, 0)),
        ],
        out_specs=[
            pl.BlockSpec((TM_PROJ, MAIN_COLS), lambda i: (i, 0)),
            pl.BlockSpec((TM_PROJ, GATE_COLS), lambda i: (i, 0)),
        ],
        out_shape=[
            jax.ShapeDtypeStruct((n, MAIN_COLS), BF16),
            jax.ShapeDtypeStruct((n, GATE_COLS), F32),
        ],
        compiler_params=pltpu.CompilerParams(
            dimension_semantics=("parallel",), vmem_limit_bytes=VMEM_LIMIT),
        name="in_proj",
    )(x2d, w_main, w_gates)


def _dn_kernel(x_ref, gates_ref, convw_ref, gpar_ref, normg_ref, out_ref,
               xc_ref, gl_ref, gc_ref, wq_ref, u_ref, kt_ref, attn_ref, egl_ref, s_ref, hist_ref):
    nseq = x_ref.shape[0]
    ts = x_ref.shape[1]
    c = DN_CHUNK
    hd = DN_HEAD_DIM
    qkv_w = 3 * DN_WIDTH
    nch = ts // c

    @pl.when(pl.program_id(1) == 0)
    def _():
        s_ref[...] = jnp.zeros_like(s_ref)
        hist_ref[...] = jnp.zeros_like(hist_ref)

    def stage_inputs(bi, carry):
        xc_ref[bi, 0:8, :] = hist_ref[bi]
        xc_ref[bi, 8:ts + 8, :] = x_ref[bi, :, 0:qkv_w].astype(F32)
        hist_ref[bi] = xc_ref[bi, ts:ts + 8, :]
        gsl = gates_ref[bi]
        sp_in = gsl + gpar_ref[1:2, :]
        softplus = jnp.maximum(sp_in, 0.0) + jnp.log(1.0 + jnp.exp(-jnp.abs(sp_in)))
        lane = lax.broadcasted_iota(I32, gsl.shape, 1)
        gl = jnp.where(lane < DN_HEADS, _sigmoid(gsl), -jnp.exp(gpar_ref[0:1, :]) * softplus)
        gl_ref[bi] = gl
        row_in_chunk = lax.broadcasted_iota(I32, gsl.shape, 0) % c
        gc = gl
        shift = 1
        while shift < c:
            gc = gc + jnp.where(row_in_chunk >= shift, pltpu.roll(gc, shift, 0), 0.0)
            shift *= 2
        gc_ref[bi] = gc
        return carry

    lax.fori_loop(0, nseq, stage_inputs, 0)

    ii = lax.broadcasted_iota(I32, (c, c), 0)
    jj = lax.broadcasted_iota(I32, (c, c), 1)
    tri_incl = ii >= jj
    tri_strict = ii > jj
    eye = jnp.where(ii == jj, 1.0, 0.0).astype(F32)
    heads = range(DN_HEADS)

    def conv_silu(bi, r0, col):
        w = convw_ref[:, col:col + hd]
        xt = xc_ref[bi, pl.ds(r0, c + 8), col:col + hd]
        y = w[CONV_WIDTH - 1:CONV_WIDTH, :] * xt[8:8 + c, :]
        for delay in range(1, CONV_WIDTH):
            tap = CONV_WIDTH - 1 - delay
            y = y + w[tap:tap + 1, :] * pltpu.roll(xt, delay, 0)[8:8 + c, :]
        return _silu(y)

    def l2n(t, scale):
        return t * (lax.rsqrt(jnp.sum(t * t, axis=-1, keepdims=True) + L2_EPS) * scale)

    def phase_a(ci):
        r0 = ci * c
        chains = []
        for bi in range(nseq):
            glc = gl_ref[bi, pl.ds(r0, c), :]
            gcc = gc_ref[bi, pl.ds(r0, c), :]
            gct = jnp.concatenate([gcc, gcc], axis=0).T
            egl_ref[bi, ci] = jnp.exp(gcc[c - 8:c, :])
            for h in heads:
                chains.append((bi, h, glc, gcc, gct))
        nchain = len(chains)
        q = [l2n(conv_silu(bi, r0, COL_DNQ + h * hd), hd ** -0.5) for (bi, h, _, _, _) in chains]
        k = [l2n(conv_silu(bi, r0, COL_DNK + h * hd), 1.0) for (bi, h, _, _, _) in chains]
        v = [conv_silu(bi, r0, COL_DNV + h * hd) for (bi, h, _, _, _) in chains]
        kb, vb, decay, egc = [], [], [], []
        for n_, (bi, h, glc, gcc, gct) in enumerate(chains):
            beta = glc[:, h:h + 1]
            gc_col = gcc[:, DN_HEADS + h:DN_HEADS + h + 1]
            gc_row = gct[DN_HEADS + h:DN_HEADS + h + 1, 0:c]
            decay.append(jnp.where(tri_incl, jnp.exp(jnp.minimum(gc_col - gc_row, 0.0)), 0.0))
            egc.append(jnp.exp(gc_col))
            e_tail = jnp.exp(gcc[c - 1:c, DN_HEADS + h:DN_HEADS + h + 1] - gc_col)
            kb.append(k[n_] * beta)
            vb.append(v[n_] * beta)
            kt_ref[bi, ci, h] = (k[n_] * e_tail).astype(BF16)
        kq = [_mm_nt(jnp.concatenate([kb[n_], q[n_]], axis=0), k[n_]) for n_ in range(nchain)]
        a_mat = [jnp.where(tri_strict, kq[n_][0:c] * decay[n_], 0.0) for n_ in range(nchain)]
        for n_, (bi, h, _, _, _) in enumerate(chains):
            attn_ref[bi, ci, h] = (kq[n_][c:2 * c] * decay[n_]).astype(BF16)
        t_inv = [eye - a for a in a_mat]
        p = a_mat
        for _ in range(5):
            p = [_mm(x, x) for x in p]
            t_inv = [t + _mm(t, x) for t, x in zip(t_inv, p)]
        for n_, (bi, h, _, _, _) in enumerate(chains):
            uw = _mm(t_inv[n_], jnp.concatenate([vb[n_], kb[n_] * egc[n_]], axis=1))
            u_ref[bi, ci, h] = uw[:, 0:hd]
            wq_ref[bi, ci, h, 0:c, :] = uw[:, hd:2 * hd].astype(BF16)
            wq_ref[bi, ci, h, c:2 * c, :] = (q[n_] * egc[n_]).astype(BF16)

    normg = normg_ref[...]

    def phase_b(ci):
        rows = pl.ds(ci * c, c)
        chains = [(bi, h) for bi in range(nseq) for h in heads]
        egl = [egl_ref[bi, ci] for bi in range(nseq)]
        s_old = [s_ref[bi, h] for bi, h in chains]
        ws = [_dot(wq_ref[bi, ci, h], s.astype(BF16)) for (bi, h), s in zip(chains, s_old)]
        v_new = [(u_ref[bi, ci, h] - w[0:c]).astype(BF16) for (bi, h), w in zip(chains, ws)]
        for n_, (bi, h) in enumerate(chains):
            s_ref[bi, h] = (s_old[n_] * egl[bi][7:8, DN_HEADS + h:DN_HEADS + h + 1]
                            + lax.dot_general(kt_ref[bi, ci, h], v_new[n_], (((0,), (0,)), ((), ())),
                                              preferred_element_type=F32))
        for n_, (bi, h) in enumerate(chains):
            o = ws[n_][c:2 * c] + _dot(attn_ref[bi, ci, h], v_new[n_])
            o = o * lax.rsqrt(jnp.mean(o * o, axis=-1, keepdims=True) + RMS_EPS) * normg
            z = x_ref[bi, rows, COL_Z + h * hd:COL_Z + (h + 1) * hd].astype(F32)
            out_ref[bi, rows, h * hd:(h + 1) * hd] = (o * _silu(z)).astype(out_ref.dtype)

    phase_a(0)
    for ci in range(nch):
        if ci + 1 < nch:
            phase_a(ci + 1)
        phase_b(ci)


def _deltanet(main3d, gates3d, conv_w, gpar, normg):
    b, t, _ = main3d.shape
    ts = TS_DN
    nseq = DN_SEQS
    nch = ts // DN_CHUNK
    dn_in = COL_Z + DN_WIDTH
    return pl.pallas_call(
        _dn_kernel,
        grid=(b // nseq, t // ts),
        in_specs=[
            pl.BlockSpec((nseq, ts, dn_in), lambda bi, si: (bi, si, 0)),
            pl.BlockSpec((nseq, ts, GATE_COLS), lambda bi, si: (bi, si, 0)),
            pl.BlockSpec((CONV_WIDTH, 3 * DN_WIDTH), lambda bi, si: (0, 0)),
            pl.BlockSpec((2, GATE_COLS), lambda bi, si: (0, 0)),
            pl.BlockSpec((1, DN_HEAD_DIM), lambda bi, si: (0, 0)),
        ],
        out_specs=pl.BlockSpec((nseq, ts, DN_WIDTH), lambda bi, si: (bi, si, 0)),
        out_shape=jax.ShapeDtypeStruct((b, t, DN_WIDTH), BF16),
        scratch_shapes=[
            pltpu.VMEM((nseq, ts + 8, 3 * DN_WIDTH), F32),
            pltpu.VMEM((nseq, ts, GATE_COLS), F32),
            pltpu.VMEM((nseq, ts, GATE_COLS), F32),
            pltpu.VMEM((nseq, nch, DN_HEADS, 2 * DN_CHUNK, DN_HEAD_DIM), BF16),
            pltpu.VMEM((nseq, nch, DN_HEADS, DN_CHUNK, DN_HEAD_DIM), F32),
            pltpu.VMEM((nseq, nch, DN_HEADS, DN_CHUNK, DN_HEAD_DIM), BF16),
            pltpu.VMEM((nseq, nch, DN_HEADS, DN_CHUNK, DN_CHUNK), BF16),
            pltpu.VMEM((nseq, nch, 8, GATE_COLS), F32),
            pltpu.VMEM((nseq, DN_HEADS, DN_HEAD_DIM, DN_HEAD_DIM), F32),
            pltpu.VMEM((nseq, 8, 3 * DN_WIDTH), F32),
        ],
        compiler_params=pltpu.CompilerParams(
            dimension_semantics=("parallel", "arbitrary"), vmem_limit_bytes=VMEM_LIMIT),
        name="deltanet",
    )(main3d, gates3d, conv_w, gpar, normg)


def _swa_kernel(sinks_ref, q_ref, kp_ref, kc_ref, vp_ref, vc_ref, out_ref, bias_ref):
    n = pl.program_id(1)
    blk = SWA_BLOCK
    d = SWA_HEAD_DIM
    grp = SWA_Q_HEADS // SWA_KV_HEADS

    @pl.when((pl.program_id(0) == 0) & (n == 0))
    def _():
        qi = lax.broadcasted_iota(I32, (blk, 2 * blk), 0)
        kj = lax.broadcasted_iota(I32, (blk, 2 * blk), 1)
        dist = qi + blk - kj
        valid = (dist >= 0) & (dist < SWA_WINDOW)
        dist_f = dist.astype(F32)
        for hq in range(SWA_Q_HEADS):
            slope = 2.0 ** (-8.0 * (hq + 1.0) / SWA_Q_HEADS)
            bias_ref[hq] = jnp.where(valid, -slope * dist_f, NEG_INF)

    kcol = lax.broadcasted_iota(I32, (1, 2 * blk), 1)
    colmask = jnp.where((kcol >= blk) | (n > 0), 0.0, NEG_INF)
    q_all = q_ref[...] * (d ** -0.5)
    kband = [jnp.concatenate([kp_ref[:, hk * d:(hk + 1) * d], kc_ref[:, hk * d:(hk + 1) * d]], axis=0)
             for hk in range(SWA_KV_HEADS)]
    vband = [jnp.concatenate([vp_ref[:, hk * d:(hk + 1) * d], vc_ref[:, hk * d:(hk + 1) * d]], axis=0)
             for hk in range(SWA_KV_HEADS)]
    heads = range(SWA_Q_HEADS)
    scores = [_mm_nt(q_all[:, hq * d:(hq + 1) * d], kband[hq // grp]) for hq in heads]
    probs, denoms = [], []
    for hq in heads:
        s = scores[hq] + bias_ref[hq] + colmask
        sink = sinks_ref[hq]
        m = jnp.maximum(jnp.max(s, axis=-1, keepdims=True), sink)
        p = jnp.exp(s - m)
        denoms.append(jnp.sum(p, axis=-1, keepdims=True) + jnp.exp(sink - m))
        probs.append(p.astype(BF16))
    outs = [_dot(probs[hq], vband[hq // grp]) / denoms[hq] for hq in heads]
    out_ref[...] = jnp.concatenate(outs, axis=-1).astype(out_ref.dtype)


def _swa(main3d, sinks):
    b, t, _ = main3d.shape
    blk = SWA_BLOCK
    qb = COL_SQ // SWA_WIDTH
    kb = COL_SK // SWA_KV_WIDTH
    vb = COL_SV // SWA_KV_WIDTH
    grid_spec = pltpu.PrefetchScalarGridSpec(
        num_scalar_prefetch=1,
        grid=(b, t // blk),
        in_specs=[
            pl.BlockSpec((None, blk, SWA_WIDTH), lambda bi, ni, s: (bi, ni, qb)),
            pl.BlockSpec((None, blk, SWA_KV_WIDTH), lambda bi, ni, s: (bi, jnp.maximum(ni - 1, 0), kb)),
            pl.BlockSpec((None, blk, SWA_KV_WIDTH), lambda bi, ni, s: (bi, ni, kb)),
            pl.BlockSpec((None, blk, SWA_KV_WIDTH), lambda bi, ni, s: (bi, jnp.maximum(ni - 1, 0), vb)),
            pl.BlockSpec((None, blk, SWA_KV_WIDTH), lambda bi, ni, s: (bi, ni, vb)),
        ],
        out_specs=pl.BlockSpec((None, blk, SWA_WIDTH), lambda bi, ni, s: (bi, ni, 0)),
        scratch_shapes=[pltpu.VMEM((SWA_Q_HEADS, blk, 2 * blk), F32)],
    )
    return pl.pallas_call(
        _swa_kernel,
        grid_spec=grid_spec,
        out_shape=jax.ShapeDtypeStruct((b, t, SWA_WIDTH), BF16),
        compiler_params=pltpu.CompilerParams(
            dimension_semantics=("arbitrary", "arbitrary"), vmem_limit_bytes=VMEM_LIMIT),
        name="swa",
    )(sinks, main3d, main3d, main3d, main3d, main3d)


def _layer_norm(y, g, b):
    mu = jnp.mean(y, axis=-1, keepdims=True)
    yc = y - mu
    var = jnp.mean(yc * yc, axis=-1, keepdims=True)
    return yc * lax.rsqrt(var + LN_EPS) * g + b


def _pack_bf16_pair(lo, hi):
    lo_bits = lax.bitcast_convert_type(lo.astype(BF16).astype(F32), U32)
    hi_bits = lax.bitcast_convert_type(hi.astype(BF16).astype(F32), U32)
    return (hi_bits & jnp.uint32(0xFFFF0000)) | (lo_bits >> 16)


def _unpack_bf16_pair(packed):
    lo = lax.bitcast_convert_type(packed << 16, F32)
    hi = lax.bitcast_convert_type(packed & jnp.uint32(0xFFFF0000), F32)
    return lo, hi


def _post_mix_kernel(x_ref, dn_ref, swa_ref, wo_dn_ref, wo_swa_ref, g_ref, b_ref,
                     rw_ref, sg_ref, su_ref, sd_ref,
                     base_ref, xpk_ref, logit_ref):
    half = D_MODEL // 2
    tm = x_ref.shape[0]
    sub = tm // POST_SPLIT
    parts = [pl.ds(i * sub, sub) for i in range(POST_SPLIT)]
    mix = [_dot(dn_ref[r, :], wo_dn_ref[...]) + _dot(swa_ref[r, :], wo_swa_ref[...]) for r in parts]
    x1 = [_layer_norm(DEEPNORM_ALPHA * x_ref[r, :] + m, g_ref[...], b_ref[...]) for r, m in zip(parts, mix)]
    xh = [x.astype(BF16) for x in x1]
    for r, x in zip(parts, x1):
        xpk_ref[r, :] = _pack_bf16_pair(x[:, :half], x[:, half:])
    for r, x in zip(parts, xh):
        logit_ref[:, r] = lax.dot_general(rw_ref[...], x, (((1,), (1,)), ((), ())),
                                          preferred_element_type=F32)
    gate_up = [(_dot(x, sg_ref[...]), _dot(x, su_ref[...])) for x in xh]
    hmid = [(_silu(g) * u).astype(BF16) for g, u in gate_up]
    for r, x, h in zip(parts, x1, hmid):
        base_ref[r, :] = DEEPNORM_ALPHA * x + _dot(h, sd_ref[...])


def _post_mix(x2d, dn2d, swa2d, wo_dn, wo_swa, ln_g, ln_b, rw_t, sg, su, sd):
    n = x2d.shape[0]
    tm = TM_POST
    full = lambda shape: pl.BlockSpec(shape, lambda i: (0, 0))
    return pl.pallas_call(
        _post_mix_kernel,
        grid=(n // tm,),
        in_specs=[
            pl.BlockSpec((tm, D_MODEL), lambda i: (i, 0)),
            pl.BlockSpec((tm, DN_WIDTH), lambda i: (i, 0)),
            pl.BlockSpec((tm, SWA_WIDTH), lambda i: (i, 0)),
            full((DN_WIDTH, D_MODEL)), full((SWA_WIDTH, D_MODEL)),
            full((1, D_MODEL)), full((1, D_MODEL)),
            full((N_EXPERTS, D_MODEL)),
            full((D_MODEL, SHARED_FF)), full((D_MODEL, SHARED_FF)), full((SHARED_FF, D_MODEL)),
        ],
        out_specs=[
            pl.BlockSpec((tm, D_MODEL), lambda i: (i, 0)),
            pl.BlockSpec((tm, D_MODEL // 2), lambda i: (i, 0)),
            pl.BlockSpec((N_EXPERTS, tm), lambda i: (0, i)),
        ],
        out_shape=[
            jax.ShapeDtypeStruct((n, D_MODEL), F32),
            jax.ShapeDtypeStruct((n, D_MODEL // 2), U32),
            jax.ShapeDtypeStruct((N_EXPERTS, n), F32),
        ],
        compiler_params=pltpu.CompilerParams(
            dimension_semantics=("parallel",), vmem_limit_bytes=VMEM_LIMIT),
        name="post_mix",
    )(x2d, dn2d, swa2d, wo_dn, wo_swa, ln_g, ln_b, rw_t, sg, su, sd)


def _route_kernel(lg_ref, bias_ref, eidx_ref, gate_ref, rank_ref, cnt_ref, carry_ref, pick_ref):
    @pl.when(pl.program_id(0) == 0)
    def _():
        carry_ref[...] = jnp.zeros_like(carry_ref)

    tt = lg_ref.shape[1]
    scores = _sigmoid(lg_ref[...])
    sel = scores + bias_ref[...]

    iog = lax.broadcasted_iota(I32, (GROUP_SIZE, tt), 0)
    grp_rows = []
    for g in range(N_GROUPS):
        blk = sel[g * GROUP_SIZE:(g + 1) * GROUP_SIZE, :]
        m1 = jnp.max(blk, axis=0, keepdims=True)
        i1 = jnp.min(jnp.where(blk == m1, iog, GROUP_SIZE), axis=0, keepdims=True)
        m2 = jnp.max(jnp.where(iog == i1, NEG_INF, blk), axis=0, keepdims=True)
        grp_rows.append(m1 + m2)
    gs = jnp.concatenate(grp_rows, axis=0)

    io8 = lax.broadcasted_iota(I32, (N_GROUPS, tt), 0)
    gsel = jnp.zeros((N_GROUPS, tt), F32)
    for _ in range(TOPK_GROUPS):
        mg = jnp.max(gs, axis=0, keepdims=True)
        ig = jnp.min(jnp.where(gs == mg, io8, N_GROUPS), axis=0, keepdims=True)
        hit = io8 == ig
        gsel = jnp.where(hit, 1.0, gsel)
        gs = jnp.where(hit, NEG_INF, gs)

    val = jnp.concatenate(
        [jnp.where(gsel[g:g + 1, :] > 0.0, sel[g * GROUP_SIZE:(g + 1) * GROUP_SIZE, :], NEG_INF)
         for g in range(N_GROUPS)], axis=0)

    ioe = lax.broadcasted_iota(I32, (N_EXPERTS, tt), 0)
    v = val
    for _ in range(TOP_K):
        m = jnp.max(v, axis=0, keepdims=True)
        v = jnp.where(v >= m, NEG_INF, v)
    picked = jnp.where(val >= m, 1.0, 0.0)
    pick_ref[...] = picked
    n_off = jnp.max(jnp.abs(jnp.sum(picked, axis=0, keepdims=True) - TOP_K))

    @pl.when(n_off > 0.0)
    def _():
        v = val
        onehot = jnp.zeros((N_EXPERTS, tt), F32)
        for _ in range(TOP_K):
            m = jnp.max(v, axis=0, keepdims=True)
            ik = jnp.min(jnp.where(v == m, ioe, N_EXPERTS), axis=0, keepdims=True)
            hit = ioe == ik
            v = jnp.where(hit, NEG_INF, v)
            onehot = jnp.where(hit, 1.0, onehot)
        pick_ref[...] = onehot

    onehot = pick_ref[...]
    oh16 = onehot.astype(BF16)
    ei = lax.broadcasted_iota(I32, (N_EXPERTS, N_EXPERTS), 0)
    ej = lax.broadcasted_iota(I32, (N_EXPERTS, N_EXPERTS), 1)
    slot = _dot(jnp.where(ej < ei, 1.0, 0.0).astype(BF16), oh16)
    ti = lax.broadcasted_iota(I32, (tt, tt), 0)
    tj = lax.broadcasted_iota(I32, (tt, tt), 1)
    cum = _dot(oh16, jnp.where(ti < tj, 1.0, 0.0).astype(BF16)) + jnp.broadcast_to(
        carry_ref[:, 0:1], (N_EXPERTS, tt))
    key = jnp.where(onehot > 0.0, slot, float(TOP_K))
    packed = ioe.astype(F32) + float(N_EXPERTS) * cum
    packed_rows, gate_rows = [], []
    for k in range(TOP_K):
        at_k = key == float(k)
        packed_rows.append(jnp.sum(jnp.where(at_k, packed, 0.0), axis=0, keepdims=True))
        gate_rows.append(jnp.sum(jnp.where(at_k, scores, 0.0), axis=0, keepdims=True))
    gsum = gate_rows[0]
    for r in gate_rows[1:]:
        gsum = gsum + r
    gate_ref[...] = jnp.concatenate(gate_rows, axis=0) / gsum * ROUTED_SCALE
    packed_i = jnp.concatenate(packed_rows, axis=0).astype(I32)
    eidx_ref[...] = packed_i & (N_EXPERTS - 1)
    rank_ref[...] = packed_i >> (N_EXPERTS.bit_length() - 1)
    carry_ref[...] = carry_ref[...] + jnp.broadcast_to(
        jnp.sum(onehot, axis=1, keepdims=True), carry_ref.shape)
    cnt_ref[...] = carry_ref[...].astype(I32)


def _route(logits_t, bias_col):
    n = logits_t.shape[1]
    tt = TT_ROUTE
    row_spec = pl.BlockSpec((TOP_K, tt), lambda i: (0, i))
    return pl.pallas_call(
        _route_kernel,
        grid=(n // tt,),
        in_specs=[
            pl.BlockSpec((N_EXPERTS, tt), lambda i: (0, i)),
            pl.BlockSpec((N_EXPERTS, 1), lambda i: (0, 0)),
        ],
        out_specs=[row_spec, row_spec, row_spec,
                   pl.BlockSpec((N_EXPERTS, 128), lambda i: (0, 0))],
        out_shape=[
            jax.ShapeDtypeStruct((TOP_K, n), I32),
            jax.ShapeDtypeStruct((TOP_K, n), F32),
            jax.ShapeDtypeStruct((TOP_K, n), I32),
            jax.ShapeDtypeStruct((N_EXPERTS, 128), I32),
        ],
        scratch_shapes=[pltpu.VMEM((N_EXPERTS, 128), F32), pltpu.VMEM((N_EXPERTS, tt), F32)],
        compiler_params=pltpu.CompilerParams(
            dimension_semantics=("arbitrary",), vmem_limit_bytes=VMEM_LIMIT),
        name="route",
    )(logits_t, bias_col)


def _place_kernel(eidx_ref, rank_ref, pstart_ref, dest_ref):
    tt = eidx_ref.shape[1]
    ioe = lax.broadcasted_iota(I32, (N_EXPERTS, tt), 0)
    pstart = pstart_ref[...]
    rows = [jnp.sum(jnp.where(ioe == eidx_ref[k:k + 1, :], pstart, 0.0), axis=0, keepdims=True)
            for k in range(TOP_K)]
    dest_ref[...] = jnp.concatenate(rows, axis=0).astype(I32) + rank_ref[...]


def _place(eidx, rank, pstart_col):
    n = eidx.shape[1]
    tt = TT_PLACE
    row_spec = pl.BlockSpec((TOP_K, tt), lambda i: (0, i))
    return pl.pallas_call(
        _place_kernel,
        grid=(n // tt,),
        in_specs=[row_spec, row_spec, pl.BlockSpec((N_EXPERTS, 1), lambda i: (0, 0))],
        out_specs=row_spec,
        out_shape=jax.ShapeDtypeStruct((TOP_K, n), I32),
        compiler_params=pltpu.CompilerParams(
            dimension_semantics=("parallel",), vmem_limit_bytes=VMEM_LIMIT),
        name="place",
    )(eidx, rank, pstart_col)


def _sc_mesh():
    return plsc.VectorSubcoreMesh(core_axis_name="c", subcore_axis_name="s",
                                  num_cores=SC_NC, num_subcores=SC_NS)


def _sc_scatter_rows(rows, idx, nrows_out):
    n, d = rows.shape
    nk = idx.shape[0]
    per_w = n // SC_NW
    nwin = per_w // SC_WIN
    assert per_w * SC_NW == n and nwin * SC_WIN == per_w and nwin % 2 == 0

    @functools.partial(
        pl.kernel, mesh=_sc_mesh(),
        out_type=jax.ShapeDtypeStruct((nrows_out, d), rows.dtype),
        scratch_types=[
            pltpu.VMEM((nwin, nk, SC_WIN), I32),
            pltpu.VMEM((2, SC_WIN, d), rows.dtype),
            pltpu.SemaphoreType.DMA((2,)),
            pltpu.SemaphoreType.DMA((2,)),
        ],
        compiler_params=pltpu.CompilerParams(use_tc_tiling_on_sc=True),
        name="sc_scatter_rows",
    )
    def scatter_kernel(rows_hbm, idx_hbm, out_hbm, idx_v, rows_v, lsem, ssem):
        wid = lax.axis_index("s") * SC_NC + lax.axis_index("c")
        base = wid * per_w
        pltpu.sync_copy(idx_hbm.at[wid], idx_v)

        def load(w, slot):
            return pltpu.make_async_copy(
                rows_hbm.at[pl.ds(base + w * SC_WIN, SC_WIN)], rows_v.at[slot], lsem.at[slot])

        def scat(w, k, slot):
            return pltpu.make_async_copy(rows_v.at[slot], out_hbm.at[idx_v.at[w, k]], ssem.at[slot])

        load(0, 0).start()

        @pl.loop(0, nwin, step=2)
        def _(w0):
            for slot in range(2):
                w = w0 + slot
                load(w, slot).wait()

                @pl.when(w + 1 < nwin)
                def _():
                    @pl.when(w >= 1)
                    def _():
                        for k in range(nk):
                            scat(w - 1, k, 1 - slot).wait()
                    load(w + 1, 1 - slot).start()

                for k in range(nk):
                    scat(w, k, slot).start()

        for k in range(nk):
            scat(nwin - 2, k, 0).wait()
        for k in range(nk):
            scat(nwin - 1, k, 1).wait()

    idx4 = idx.reshape(nk, SC_NW, nwin, SC_WIN).transpose(1, 2, 0, 3)
    return scatter_kernel(rows, idx4)


def _expert_kernel(gstart_ref, cnt_ref, xs_hbm, wg_ref, wu_ref, wd_ref, y_hbm,
                   wgb_ref, wub_ref, wdb_ref, xbuf_ref, ybuf_ref, xsem, ysem):
    e = pl.program_id(0)
    ne = pl.num_programs(0)
    bm = xbuf_ref.shape[1]
    nblk = y_hbm.shape[0] // bm
    half = D_MODEL // 2
    g_lo = gstart_ref[e]
    g_hi = gstart_ref[e + 1]
    g_end = gstart_ref[ne]

    def x_copy(g, slot):
        return pltpu.make_async_copy(xs_hbm.at[pl.ds(g * bm, bm), :], xbuf_ref.at[slot], xsem.at[slot])

    def y_copy(g, slot):
        return pltpu.make_async_copy(ybuf_ref.at[slot], y_hbm.at[pl.ds(g * bm, bm), :], ysem.at[slot])

    nslot = xbuf_ref.shape[0]

    ahead = nslot - EXP_GROUP

    @pl.when(e == 0)
    def _():
        for g0 in range(ahead):
            @pl.when(g0 < g_end)
            def _():
                x_copy(g0, g0).start()

    @pl.when(g_hi > g_lo)
    def _():
        wgb_ref[...] = wg_ref[...].astype(BF16)
        wub_ref[...] = wu_ref[...].astype(BF16)
        wdb_ref[...] = wd_ref[...].astype(BF16)

    row = lax.broadcasted_iota(I32, (bm, half), 0)

    def acquire(g):
        x_copy(g, g % nslot).wait()

        @pl.when(g + ahead < g_end)
        def _():
            x_copy(g + ahead, (g + ahead) % nslot).start()

        @pl.when(g >= nslot)
        def _():
            y_copy(g - nslot, g % nslot).wait()

    def load(g):
        n_valid = cnt_ref[e] - (g - g_lo) * bm
        x_lo, x_hi = _unpack_bf16_pair(jnp.where(row < n_valid, xbuf_ref[g % nslot], jnp.uint32(0)))
        return x_lo.astype(BF16), x_hi.astype(BF16)

    def gate_up(x):
        x_lo, x_hi = x
        gate = _dot(x_lo, wgb_ref[:half, :]) + _dot(x_hi, wgb_ref[half:, :])
        up = _dot(x_lo, wub_ref[:half, :]) + _dot(x_hi, wub_ref[half:, :])
        return gate, up

    def down(gu):
        gate, up = gu
        return _dot((_silu(gate) * up).astype(BF16), wdb_ref[...])

    def store(g, y):
        ybuf_ref[g % nslot] = _pack_bf16_pair(y[:, :half], y[:, half:])
        y_copy(g, g % nslot).start()

    def run_blocks(g, count):
        for j in range(count):
            acquire(g + j)
        gus = [gate_up(load(g + j)) for j in range(count)]
        ys = [down(gu) for gu in gus]
        for j in range(count):
            store(g + j, ys[j])

    def full_group(p, carry):
        run_blocks(g_lo + EXP_GROUP * p, EXP_GROUP)
        return carry

    n_own = g_hi - g_lo
    lax.fori_loop(0, n_own // EXP_GROUP, full_group, 0)
    size = EXP_GROUP // 2
    while size >= 1:
        @pl.when(n_own % (2 * size) >= size)
        def _(size=size):
            run_blocks(g_lo + n_own // (2 * size) * (2 * size), size)
        size //= 2

    @pl.when(e == ne - 1)
    def _():
        for back in range(nslot, 0, -1):
            @pl.when(g_end >= back)
            def _():
                y_copy(g_end - back, (g_end - back) % nslot).wait()

        ybuf_ref[0] = jnp.zeros((bm, half), U32)

        def fill(g, carry):
            y_copy(g, 0).start()
            return carry

        def drain(g, carry):
            y_copy(g, 0).wait()
            return carry

        lax.fori_loop(g_end, nblk, fill, 0)
        lax.fori_loop(g_end, nblk, drain, 0)


def _experts(gstart, counts, xs, w_gate, w_up, w_down):
    bm = BM_EXP
    nblk = xs.shape[0] // bm
    half = D_MODEL // 2
    grid_spec = pltpu.PrefetchScalarGridSpec(
        num_scalar_prefetch=2,
        grid=(N_EXPERTS,),
        in_specs=[
            pl.BlockSpec(memory_space=pl.ANY),
            pl.BlockSpec((None, D_MODEL, EXPERT_FF), lambda e, gs, cn: (e, 0, 0)),
            pl.BlockSpec((None, D_MODEL, EXPERT_FF), lambda e, gs, cn: (e, 0, 0)),
            pl.BlockSpec((None, EXPERT_FF, D_MODEL), lambda e, gs, cn: (e, 0, 0)),
        ],
        out_specs=pl.BlockSpec(memory_space=pl.ANY),
        scratch_shapes=[
            pltpu.VMEM((D_MODEL, EXPERT_FF), BF16),
            pltpu.VMEM((D_MODEL, EXPERT_FF), BF16),
            pltpu.VMEM((EXPERT_FF, D_MODEL), BF16),
            pltpu.VMEM((EXP_SLOTS, bm, half), U32),
            pltpu.VMEM((EXP_SLOTS, bm, half), U32),
            pltpu.SemaphoreType.DMA((EXP_SLOTS,)),
            pltpu.SemaphoreType.DMA((EXP_SLOTS,)),
        ],
    )
    return pl.pallas_call(
        _expert_kernel,
        grid_spec=grid_spec,
        out_shape=jax.ShapeDtypeStruct((nblk * bm, half), U32),
        compiler_params=pltpu.CompilerParams(
            dimension_semantics=("arbitrary",), vmem_limit_bytes=VMEM_LIMIT),
        name="experts",
    )(gstart, counts, xs, w_gate, w_up, w_down)


def _sc_gather_rows(table, idx):
    nrows = idx.shape[0]
    d = table.shape[1]
    per_w = nrows // SC_NW
    nwin = per_w // SC_WIN
    assert per_w * SC_NW == nrows and nwin * SC_WIN == per_w and nwin % 2 == 0
    @functools.partial(
        pl.kernel, mesh=_sc_mesh(),
        out_type=jax.ShapeDtypeStruct((nrows, d), table.dtype),
        scratch_types=[
            pltpu.VMEM((nwin, SC_WIN), I32),
            pltpu.VMEM((2, SC_WIN, d), table.dtype),
            pltpu.SemaphoreType.DMA((2,)),
            pltpu.SemaphoreType.DMA((2,)),
        ],
        compiler_params=pltpu.CompilerParams(use_tc_tiling_on_sc=True),
        name="sc_gather_rows",
    )
    def gather_kernel(table_hbm, idx_hbm, out_hbm, idx_v, rows_v, gsem, wsem):
        wid = lax.axis_index("s") * SC_NC + lax.axis_index("c")
        base = wid * per_w
        pltpu.sync_copy(idx_hbm.at[wid], idx_v)

        def gather(w, slot):
            return pltpu.make_async_copy(table_hbm.at[idx_v.at[w]], rows_v.at[slot], gsem.at[slot])

        def put(w, slot):
            return pltpu.make_async_copy(
                rows_v.at[slot], out_hbm.at[pl.ds(base + w * SC_WIN, SC_WIN)], wsem.at[slot])

        gather(0, 0).start()

        @pl.loop(0, nwin, step=2)
        def _(w0):
            for slot in range(2):
                w = w0 + slot
                gather(w, slot).wait()

                @pl.when(w + 1 < nwin)
                def _():
                    @pl.when(w >= 1)
                    def _():
                        put(w - 1, 1 - slot).wait()
                    gather(w + 1, 1 - slot).start()

                put(w, slot).start()

        put(nwin - 2, 0).wait()
        put(nwin - 1, 1).wait()

    return gather_kernel(table, idx.reshape(SC_NW, nwin, SC_WIN))


def _combine_kernel(y_ref, base_ref, gate_ref, g_ref, b_ref, out_ref):
    half = D_MODEL // 2
    gates = gate_ref[...]
    acc_lo = base_ref[:, :half]
    acc_hi = base_ref[:, half:]
    for k in range(TOP_K):
        y_lo, y_hi = _unpack_bf16_pair(y_ref[k])
        gk = gates[:, k:k + 1]
        acc_lo = acc_lo + gk * y_lo
        acc_hi = acc_hi + gk * y_hi
    mu = (jnp.sum(acc_lo, axis=-1, keepdims=True) + jnp.sum(acc_hi, axis=-1, keepdims=True)) / D_MODEL
    c_lo = acc_lo - mu
    c_hi = acc_hi - mu
    var = (jnp.sum(c_lo * c_lo, axis=-1, keepdims=True)
           + jnp.sum(c_hi * c_hi, axis=-1, keepdims=True)) / D_MODEL
    inv = lax.rsqrt(var + LN_EPS)
    out_ref[:, :half] = c_lo * inv * g_ref[:, :half] + b_ref[:, :half]
    out_ref[:, half:] = c_hi * inv * g_ref[:, half:] + b_ref[:, half:]


def _combine(ybuf, base, gate_tok, ln_g, ln_b):
    n = base.shape[0]
    tt = TT_COMB
    half = D_MODEL // 2
    return pl.pallas_call(
        _combine_kernel,
        grid=(n // tt,),
        in_specs=[
            pl.BlockSpec((TOP_K, tt, half), lambda i: (0, i, 0)),
            pl.BlockSpec((tt, D_MODEL), lambda i: (i, 0)),
            pl.BlockSpec((tt, TOP_K), lambda i: (i, 0)),
            pl.BlockSpec((1, D_MODEL), lambda i: (0, 0)),
            pl.BlockSpec((1, D_MODEL), lambda i: (0, 0)),
        ],
        out_specs=pl.BlockSpec((tt, D_MODEL), lambda i: (i, 0)),
        out_shape=jax.ShapeDtypeStruct((n, D_MODEL), F32),
        compiler_params=pltpu.CompilerParams(
            dimension_semantics=("parallel",), vmem_limit_bytes=VMEM_LIMIT),
        name="combine",
    )(ybuf, base, gate_tok, ln_g, ln_b)


def _regroup_w_in(w_in):
    o = 0
    cols = {}
    for name, width in (("dnq", DN_WIDTH), ("dnk", DN_WIDTH), ("dnv", DN_WIDTH), ("sq", SWA_WIDTH),
                        ("sk", SWA_KV_WIDTH), ("sv", SWA_KV_WIDTH), ("z", DN_WIDTH),
                        ("b", DN_HEADS), ("a", DN_HEADS)):
        cols[name] = w_in[:, o:o + width]
        o += width
    w_main = jnp.concatenate([cols[k] for k in ("dnq", "dnk", "dnv", "z", "sq", "sk", "sv")], axis=1)
    w_gates = jnp.concatenate(
        [cols["b"], cols["a"], jnp.zeros((D_MODEL, GATE_COLS - 2 * DN_HEADS), w_in.dtype)], axis=1)
    return w_main.astype(BF16), w_gates.astype(BF16)


def _layer(x, w_in, conv_w, a_log, dt_bias, dn_norm_g, sinks, w_out, ln1_g, ln1_b,
           router_w, router_bias, w_gate, w_up, w_down, sh_gate, sh_up, sh_down, ln2_g, ln2_b):
    b, t, d = x.shape
    n = b * t
    x2d = x.reshape(n, d)

    w_main, w_gates = _regroup_w_in(w_in)
    main, gates = _in_proj(x2d, w_main, w_gates)
    main3d = main.reshape(b, t, MAIN_COLS)

    pad = jnp.zeros((GATE_COLS - 2 * DN_HEADS,), F32)
    gpar = jnp.stack([jnp.concatenate([jnp.zeros((DN_HEADS,), F32), a_log.astype(F32), pad]),
                      jnp.concatenate([jnp.zeros((DN_HEADS,), F32), dt_bias.astype(F32), pad])])
    dn_out = _deltanet(main3d, gates.reshape(b, t, GATE_COLS), conv_w.astype(F32), gpar,
                       dn_norm_g.astype(F32).reshape(1, DN_HEAD_DIM))
    swa_out = _swa(main3d, sinks.astype(F32))

    base, xpk, logits_t = _post_mix(
        x2d, dn_out.reshape(n, DN_WIDTH), swa_out.reshape(n, SWA_WIDTH),
        w_out[:DN_WIDTH].astype(BF16), w_out[DN_WIDTH:].astype(BF16),
        ln1_g.reshape(1, d).astype(F32), ln1_b.reshape(1, d).astype(F32),
        router_w.T.astype(BF16),
        sh_gate.astype(BF16), sh_up.astype(BF16), sh_down.astype(BF16))

    eidx, gate, rank, cnt = _route(logits_t, router_bias.astype(F32).reshape(N_EXPERTS, 1))

    bm = BM_EXP
    counts = cnt[:, 0]
    padded = (counts + bm - 1) // bm * bm
    pend = jnp.cumsum(padded)
    pstart = pend - padded
    nblk = -(-(n * TOP_K) // bm) + N_EXPERTS
    gstart = (jnp.concatenate([pstart, pend[-1:]]) // bm).astype(I32)

    dest = _place(eidx, rank, pstart.astype(F32).reshape(N_EXPERTS, 1))
    xs = _sc_scatter_rows(xpk, dest, nblk * bm)
    ypk = _experts(gstart, counts, xs, w_gate, w_up, w_down)
    ybuf = _sc_gather_rows(ypk, dest.reshape(-1)).reshape(TOP_K, n, d // 2)
    out = _combine(ybuf, base, gate.T, ln2_g.reshape(1, d).astype(F32), ln2_b.reshape(1, d).astype(F32))
    return out.reshape(b, t, d)


def kernel(x, w_in, conv_w, a_log, dt_bias, dn_norm_g, sinks, w_out, ln1_g, ln1_b, router_w, router_bias,
           w_gate, w_up, w_down, shared_w_gate, shared_w_up, shared_w_down, ln2_g, ln2_b):
    depth = w_in.shape[0]
    for l in range(depth):
        x = _layer(x, w_in[l], conv_w[l], a_log[l], dt_bias[l], dn_norm_g[l], sinks[l], w_out[l],
                   ln1_g[l], ln1_b[l], router_w[l], router_bias[l], w_gate[l], w_up[l], w_down[l],
                   shared_w_gate[l], shared_w_up[l], shared_w_down[l], ln2_g[l], ln2_b[l])
    return x
```

```python
import functools

import jax
import jax.numpy as jnp
from jax import lax
from jax.experimental import pallas as pl
from jax.experimental.pallas import tpu as pltpu
from jax.experimental.pallas import tpu_sc as plsc

F32 = jnp.float32
BF16 = jnp.bfloat16
I32 = jnp.int32
U32 = jnp.uint32

D_MODEL = 1024
DN_HEADS = 4
DN_HEAD_DIM = 128
DN_WIDTH = DN_HEADS * DN_HEAD_DIM
CONV_WIDTH = 4
DN_CHUNK = 64
SWA_Q_HEADS = 8
SWA_KV_HEADS = 2
SWA_HEAD_DIM = 64
SWA_WIDTH = SWA_Q_HEADS * SWA_HEAD_DIM
SWA_KV_WIDTH = SWA_KV_HEADS * SWA_HEAD_DIM
SWA_WINDOW = 128
SWA_BLOCK = 128
N_EXPERTS = 256
N_GROUPS = 8
GROUP_SIZE = N_EXPERTS // N_GROUPS
TOPK_GROUPS = 4
TOP_K = 8
EXPERT_FF = 256
SHARED_FF = 256
ROUTED_SCALE = 2.5
DEEPNORM_ALPHA = 2.0 ** 0.25
LN_EPS = 1e-5
RMS_EPS = 1e-6
L2_EPS = 1e-6

COL_DNQ = 0
COL_DNK = DN_WIDTH
COL_DNV = 2 * DN_WIDTH
COL_Z = 3 * DN_WIDTH
COL_SQ = 4 * DN_WIDTH
COL_SK = COL_SQ + SWA_WIDTH
COL_SV = COL_SK + SWA_KV_WIDTH
MAIN_COLS = COL_SV + SWA_KV_WIDTH
GATE_COLS = 128

TM_PROJ = 1024
TS_DN = 256
DN_SEQS = 4
TM_POST = 1024
POST_SPLIT = 2
TT_ROUTE = 1024
TT_PLACE = 2048
BM_EXP = 256
EXP_GROUP = 2
EXP_SLOTS = 6
TT_COMB = 512
SC_NC = 2
SC_NS = 16
SC_NW = SC_NC * SC_NS
SC_WIN = 64
VMEM_LIMIT = 56 * 1024 * 1024
NEG_INF = float("-inf")


def _dot(a, b):
    return jnp.dot(a, b, preferred_element_type=F32)


def _mm(a, b):
    return _dot(a.astype(BF16), b.astype(BF16))


def _mm_nt(a, b):
    return lax.dot_general(a.astype(BF16), b.astype(BF16), (((1,), (1,)), ((), ())),
                           preferred_element_type=F32)


def _sigmoid(x):
    return 1.0 / (1.0 + jnp.exp(-x))


def _silu(x):
    return x * _sigmoid(x)


def _in_proj_kernel(x_ref, w_ref, wg_ref, main_ref, gates_ref):
    xb = x_ref[...].astype(BF16)
    main_ref[...] = _dot(xb, w_ref[...]).astype(BF16)
    gates_ref[...] = _dot(xb, wg_ref[...])


def _in_proj(x2d, w_main, w_gates):
    n = x2d.shape[0]
    return pl.pallas_call(
        _in_proj_kernel,
        grid=(n // TM_PROJ,),
        in_specs=[
            pl.BlockSpec((TM_PROJ, D_MODEL), lambda i: (i, 0)),
            pl.BlockSpec((D_MODEL, MAIN_COLS), lambda i: (0, 0)),
            pl.BlockSpec((D_MODEL, GATE_COLS), lambda i: (0, 0)),
        ],
        out_specs=[
            pl.BlockSpec((TM_PROJ, MAIN_COLS), lambda i: (i, 0)),
            pl.BlockSpec((TM_PROJ, GATE_COLS), lambda i: (i, 0)),
        ],
        out_shape=[
            jax.ShapeDtypeStruct((n, MAIN_COLS), BF16),
            jax.ShapeDtypeStruct((n, GATE_COLS), F32),
        ],
        compiler_params=pltpu.CompilerParams(
            dimension_semantics=("parallel",), vmem_limit_bytes=VMEM_LIMIT),
        name="in_proj",
    )(x2d, w_main, w_gates)


def _dn_kernel(x_ref, gates_ref, convw_ref, gpar_ref, normg_ref, out_ref,
               xc_ref, gl_ref, gc_ref, wq_ref, u_ref, kt_ref, attn_ref, egl_ref, s_ref, hist_ref):
    nseq = x_ref.shape[0]
    ts = x_ref.shape[1]
    c = DN_CHUNK
    hd = DN_HEAD_DIM
    qkv_w = 3 * DN_WIDTH
    nch = ts // c

    @pl.when(pl.program_id(1) == 0)
    def _():
        s_ref[...] = jnp.zeros_like(s_ref)
        hist_ref[...] = jnp.zeros_like(hist_ref)

    def stage_inputs(bi, carry):
        xc_ref[bi, 0:8, :] = hist_ref[bi]
        xc_ref[bi, 8:ts + 8, :] = x_ref[bi, :, 0:qkv_w].astype(F32)
        hist_ref[bi] = xc_ref[bi, ts:ts + 8, :]
        gsl = gates_ref[bi]
        sp_in = gsl + gpar_ref[1:2, :]
        softplus = jnp.maximum(sp_in, 0.0) + jnp.log(1.0 + jnp.exp(-jnp.abs(sp_in)))
        lane = lax.broadcasted_iota(I32, gsl.shape, 1)
        gl = jnp.where(lane < DN_HEADS, _sigmoid(gsl), -jnp.exp(gpar_ref[0:1, :]) * softplus)
        gl_ref[bi] = gl
        row_in_chunk = lax.broadcasted_iota(I32, gsl.shape, 0) % c
        gc = gl
        shift = 1
        while shift < c:
            gc = gc + jnp.where(row_in_chunk >= shift, pltpu.roll(gc, shift, 0), 0.0)
            shift *= 2
        gc_ref[bi] = gc
        return carry

    lax.fori_loop(0, nseq, stage_inputs, 0)

    ii = lax.broadcasted_iota(I32, (c, c), 0)
    jj = lax.broadcasted_iota(I32, (c, c), 1)
    tri_incl = ii >= jj
    tri_strict = ii > jj
    eye = jnp.where(ii == jj, 1.0, 0.0).astype(F32)
    heads = range(DN_HEADS)

    def conv_silu(bi, r0, col):
        w = convw_ref[:, col:col + hd]
        xt = xc_ref[bi, pl.ds(r0, c + 8), col:col + hd]
        y = w[CONV_WIDTH - 1:CONV_WIDTH, :] * xt[8:8 + c, :]
        for delay in range(1, CONV_WIDTH):
            tap = CONV_WIDTH - 1 - delay
            y = y + w[tap:tap + 1, :] * pltpu.roll(xt, delay, 0)[8:8 + c, :]
        return _silu(y)

    def l2n(t, scale):
        return t * (lax.rsqrt(jnp.sum(t * t, axis=-1, keepdims=True) + L2_EPS) * scale)

    def phase_a(ci):
        r0 = ci * c
        chains = []
        for bi in range(nseq):
            glc = gl_ref[bi, pl.ds(r0, c), :]
            gcc = gc_ref[bi, pl.ds(r0, c), :]
            gct = jnp.concatenate([gcc, gcc], axis=0).T
            egl_ref[bi, ci] = jnp.exp(gcc[c - 8:c, :])
            for h in heads:
                chains.append((bi, h, glc, gcc, gct))
        nchain = len(chains)
        q = [l2n(conv_silu(bi, r0, COL_DNQ + h * hd), hd ** -0.5) for (bi, h, _, _, _) in chains]
        k = [l2n(conv_silu(bi, r0, COL_DNK + h * hd), 1.0) for (bi, h, _, _, _) in chains]
        v = [conv_silu(bi, r0, COL_DNV + h * hd) for (bi, h, _, _, _) in chains]
        kb, vb, decay, egc = [], [], [], []
        for n_, (bi, h, glc, gcc, gct) in enumerate(chains):
            beta = glc[:, h:h + 1]
            gc_col = gcc[:, DN_HEADS + h:DN_HEADS + h + 1]
            gc_row = gct[DN_HEADS + h:DN_HEADS + h + 1, 0:c]
            decay.append(jnp.where(tri_incl, jnp.exp(jnp.minimum(gc_col - gc_row, 0.0)), 0.0))
            egc.append(jnp.exp(gc_col))
            e_tail = jnp.exp(gcc[c - 1:c, DN_HEADS + h:DN_HEADS + h + 1] - gc_col)
            kb.append(k[n_] * beta)
            vb.append(v[n_] * beta)
            kt_ref[bi, ci, h] = (k[n_] * e_tail).astype(BF16)
        kq = [_mm_nt(jnp.concatenate([kb[n_], q[n_]], axis=0), k[n_]) for n_ in range(nchain)]
        a_mat = [jnp.where(tri_strict, kq[n_][0:c] * decay[n_], 0.0) for n_ in range(nchain)]
        for n_, (bi, h, _, _, _) in enumerate(chains):
            attn_ref[bi, ci, h] = (kq[n_][c:2 * c] * decay[n_]).astype(BF16)
        t_inv = [eye - a for a in a_mat]
        p = a_mat
        for _ in range(5):
            p = [_mm(x, x) for x in p]
            t_inv = [t + _mm(t, x) for t, x in zip(t_inv, p)]
        for n_, (bi, h, _, _, _) in enumerate(chains):
            uw = _mm(t_inv[n_], jnp.concatenate([vb[n_], kb[n_] * egc[n_]], axis=1))
            u_ref[bi, ci, h] = uw[:, 0:hd]
            wq_ref[bi, ci, h, 0:c, :] = uw[:, hd:2 * hd].astype(BF16)
            wq_ref[bi, ci, h, c:2 * c, :] = (q[n_] * egc[n_]).astype(BF16)

    normg = normg_ref[...]

    def phase_b(ci):
        rows = pl.ds(ci * c, c)
        chains = [(bi, h) for bi in range(nseq) for h in heads]
        egl = [egl_ref[bi, ci] for bi in range(nseq)]
        s_old = [s_ref[bi, h] for bi, h in chains]
        ws = [_dot(wq_ref[bi, ci, h], s.astype(BF16)) for (bi, h), s in zip(chains, s_old)]
        v_new = [(u_ref[bi, ci, h] - w[0:c]).astype(BF16) for (bi, h), w in zip(chains, ws)]
        for n_, (bi, h) in enumerate(chains):
            s_ref[bi, h] = (s_old[n_] * egl[bi][7:8, DN_HEADS + h:DN_HEADS + h + 1]
                            + lax.dot_general(kt_ref[bi, ci, h], v_new[n_], (((0,), (0,)), ((), ())),
                                              preferred_element_type=F32))
        for n_, (bi, h) in enumerate(chains):
            o = ws[n_][c:2 * c] + _dot(attn_ref[bi, ci, h], v_new[n_])
            o = o * lax.rsqrt(jnp.mean(o * o, axis=-1, keepdims=True) + RMS_EPS) * normg
            z = x_ref[bi, rows, COL_Z + h * hd:COL_Z + (h + 1) * hd].astype(F32)
            out_ref[bi, rows, h * hd:(h + 1) * hd] = (o * _silu(z)).astype(out_ref.dtype)

    phase_a(0)
    for ci in range(nch):
        if ci + 1 < nch:
            phase_a(ci + 1)
        phase_b(ci)


def _deltanet(main3d, gates3d, conv_w, gpar, normg):
    b, t, _ = main3d.shape
    ts = TS_DN
    nseq = DN_SEQS
    nch = ts // DN_CHUNK
    dn_in = COL_Z + DN_WIDTH
    return pl.pallas_call(
        _dn_kernel,
        grid=(b // nseq, t // ts),
        in_specs=[
            pl.BlockSpec((nseq, ts, dn_in), lambda bi, si: (bi, si, 0)),
            pl.BlockSpec((nseq, ts, GATE_COLS), lambda bi, si: (bi, si, 0)),
            pl.BlockSpec((CONV_WIDTH, 3 * DN_WIDTH), lambda bi, si: (0, 0)),
            pl.BlockSpec((2, GATE_COLS), lambda bi, si: (0, 0)),
            pl.BlockSpec((1, DN_HEAD_DIM), lambda bi, si: (0, 0)),
        ],
        out_specs=pl.BlockSpec((nseq, ts, DN_WIDTH), lambda bi, si: (bi, si, 0)),
        out_shape=jax.ShapeDtypeStruct((b, t, DN_WIDTH), BF16),
        scratch_shapes=[
            pltpu.VMEM((nseq, ts + 8, 3 * DN_WIDTH), F32),
            pltpu.VMEM((nseq, ts, GATE_COLS), F32),
            pltpu.VMEM((nseq, ts, GATE_COLS), F32),
            pltpu.VMEM((nseq, nch, DN_HEADS, 2 * DN_CHUNK, DN_HEAD_DIM), BF16),
            pltpu.VMEM((nseq, nch, DN_HEADS, DN_CHUNK, DN_HEAD_DIM), F32),
            pltpu.VMEM((nseq, nch, DN_HEADS, DN_CHUNK, DN_HEAD_DIM), BF16),
            pltpu.VMEM((nseq, nch, DN_HEADS, DN_CHUNK, DN_CHUNK), BF16),
            pltpu.VMEM((nseq, nch, 8, GATE_COLS), F32),
            pltpu.VMEM((nseq, DN_HEADS, DN_HEAD_DIM, DN_HEAD_DIM), F32),
            pltpu.VMEM((nseq, 8, 3 * DN_WIDTH), F32),
        ],
        compiler_params=pltpu.CompilerParams(
            dimension_semantics=("parallel", "arbitrary"), vmem_limit_bytes=VMEM_LIMIT),
        name="deltanet",
    )(main3d, gates3d, conv_w, gpar, normg)


def _swa_kernel(sinks_ref, q_ref, kp_ref, kc_ref, vp_ref, vc_ref, out_ref, bias_ref):
    n = pl.program_id(1)
    blk = SWA_BLOCK
    d = SWA_HEAD_DIM
    grp = SWA_Q_HEADS // SWA_KV_HEADS

    @pl.when((pl.program_id(0) == 0) & (n == 0))
    def _():
        qi = lax.broadcasted_iota(I32, (blk, 2 * blk), 0)
        kj = lax.broadcasted_iota(I32, (blk, 2 * blk), 1)
        dist = qi + blk - kj
        valid = (dist >= 0) & (dist < SWA_WINDOW)
        dist_f = dist.astype(F32)
        for hq in range(SWA_Q_HEADS):
            slope = 2.0 ** (-8.0 * (hq + 1.0) / SWA_Q_HEADS)
            bias_ref[hq] = jnp.where(valid, -slope * dist_f, NEG_INF)

    kcol = lax.broadcasted_iota(I32, (1, 2 * blk), 1)
    colmask = jnp.where((kcol >= blk) | (n > 0), 0.0, NEG_INF)
    q_all = q_ref[...] * (d ** -0.5)
    kband = [jnp.concatenate([kp_ref[:, hk * d:(hk + 1) * d], kc_ref[:, hk * d:(hk + 1) * d]], axis=0)
             for hk in range(SWA_KV_HEADS)]
    vband = [jnp.concatenate([vp_ref[:, hk * d:(hk + 1) * d], vc_ref[:, hk * d:(hk + 1) * d]], axis=0)
             for hk in range(SWA_KV_HEADS)]
    heads = range(SWA_Q_HEADS)
    scores = [_mm_nt(q_all[:, hq * d:(hq + 1) * d], kband[hq // grp]) for hq in heads]
    probs, denoms = [], []
    for hq in heads:
        s = scores[hq] + bias_ref[hq] + colmask
        sink = sinks_ref[hq]
        m = jnp.maximum(jnp.max(s, axis=-1, keepdims=True), sink)
        p = jnp.exp(s - m)
        denoms.append(jnp.sum(p, axis=-1, keepdims=True) + jnp.exp(sink - m))
        probs.append(p.astype(BF16))
    outs = [_dot(probs[hq], vband[hq // grp]) / denoms[hq] for hq in heads]
    out_ref[...] = jnp.concatenate(outs, axis=-1).astype(out_ref.dtype)


def _swa(main3d, sinks):
    b, t, _ = main3d.shape
    blk = SWA_BLOCK
    qb = COL_SQ // SWA_WIDTH
    kb = COL_SK // SWA_KV_WIDTH
    vb = COL_SV // SWA_KV_WIDTH
    grid_spec = pltpu.PrefetchScalarGridSpec(
        num_scalar_prefetch=1,
        grid=(b, t // blk),
        in_specs=[
            pl.BlockSpec((None, blk, SWA_WIDTH), lambda bi, ni, s: (bi, ni, qb)),
            pl.BlockSpec((None, blk, SWA_KV_WIDTH), lambda bi, ni, s: (bi, jnp.maximum(ni - 1, 0), kb)),
            pl.BlockSpec((None, blk, SWA_KV_WIDTH), lambda bi, ni, s: (bi, ni, kb)),
            pl.BlockSpec((None, blk, SWA_KV_WIDTH), lambda bi, ni, s: (bi, jnp.maximum(ni - 1, 0), vb)),
            pl.BlockSpec((None, blk, SWA_KV_WIDTH), lambda bi, ni, s: (bi, ni, vb)),
        ],
        out_specs=pl.BlockSpec((None, blk, SWA_WIDTH), lambda bi, ni, s: (bi, ni, 0)),
        scratch_shapes=[pltpu.VMEM((SWA_Q_HEADS, blk, 2 * blk), F32)],
    )
    return pl.pallas_call(
        _swa_kernel,
        grid_spec=grid_spec,
        out_shape=jax.ShapeDtypeStruct((b, t, SWA_WIDTH), BF16),
        compiler_params=pltpu.CompilerParams(
            dimension_semantics=("arbitrary", "arbitrary"), vmem_limit_bytes=VMEM_LIMIT),
        name="swa",
    )(sinks, main3d, main3d, main3d, main3d, main3d)


def _layer_norm(y, g, b):
    mu = jnp.mean(y, axis=-1, keepdims=True)
    yc = y - mu
    var = jnp.mean(yc * yc, axis=-1, keepdims=True)
    return yc * lax.rsqrt(var + LN_EPS) * g + b


def _pack_bf16_pair(lo, hi):
    lo_bits = lax.bitcast_convert_type(lo.astype(BF16).astype(F32), U32)
    hi_bits = lax.bitcast_convert_type(hi.astype(BF16).astype(F32), U32)
    return (hi_bits & jnp.uint32(0xFFFF0000)) | (lo_bits >> 16)


def _unpack_bf16_pair(packed):
    lo = lax.bitcast_convert_type(packed << 16, F32)
    hi = lax.bitcast_convert_type(packed & jnp.uint32(0xFFFF0000), F32)
    return lo, hi


def _post_mix_kernel(x_ref, dn_ref, swa_ref, wo_dn_ref, wo_swa_ref, g_ref, b_ref,
                     rw_ref, sg_ref, su_ref, sd_ref,
                     base_ref, xpk_ref, logit_ref):
    half = D_MODEL // 2
    tm = x_ref.shape[0]
    sub = tm // POST_SPLIT
    parts = [pl.ds(i * sub, sub) for i in range(POST_SPLIT)]
    mix = [_dot(dn_ref[r, :], wo_dn_ref[...]) + _dot(swa_ref[r, :], wo_swa_ref[...]) for r in parts]
    x1 = [_layer_norm(DEEPNORM_ALPHA * x_ref[r, :] + m, g_ref[...], b_ref[...]) for r, m in zip(parts, mix)]
    xh = [x.astype(BF16) for x in x1]
    for r, x in zip(parts, x1):
        xpk_ref[r, :] = _pack_bf16_pair(x[:, :half], x[:, half:])
    for r, x in zip(parts, xh):
        logit_ref[:, r] = lax.dot_general(rw_ref[...], x, (((1,), (1,)), ((), ())),
                                          preferred_element_type=F32)
    gate_up = [(_dot(x, sg_ref[...]), _dot(x, su_ref[...])) for x in xh]
    hmid = [(_silu(g) * u).astype(BF16) for g, u in gate_up]
    for r, x, h in zip(parts, x1, hmid):
        base_ref[r, :] = DEEPNORM_ALPHA * x + _dot(h, sd_ref[...])


def _post_mix(x2d, dn2d, swa2d, wo_dn, wo_swa, ln_g, ln_b, rw_t, sg, su, sd):
    n = x2d.shape[0]
    tm = TM_POST
    full = lambda shape: pl.BlockSpec(shape, lambda i: (0, 0))
    return pl.pallas_call(
        _post_mix_kernel,
        grid=(n // tm,),
        in_specs=[
            pl.BlockSpec((tm, D_MODEL), lambda i: (i, 0)),
            pl.BlockSpec((tm, DN_WIDTH), lambda i: (i, 0)),
            pl.BlockSpec((tm, SWA_WIDTH), lambda i: (i, 0)),
            full((DN_WIDTH, D_MODEL)), full((SWA_WIDTH, D_MODEL)),
            full((1, D_MODEL)), full((1, D_MODEL)),
            full((N_EXPERTS, D_MODEL)),
            full((D_MODEL, SHARED_FF)), full((D_MODEL, SHARED_FF)), full((SHARED_FF, D_MODEL)),
        ],
        out_specs=[
            pl.BlockSpec((tm, D_MODEL), lambda i: (i, 0)),
            pl.BlockSpec((tm, D_MODEL // 2), lambda i: (i, 0)),
            pl.BlockSpec((N_EXPERTS, tm), lambda i: (0, i)),
        ],
        out_shape=[
            jax.ShapeDtypeStruct((n, D_MODEL), F32),
            jax.ShapeDtypeStruct((n, D_MODEL // 2), U32),
            jax.ShapeDtypeStruct((N_EXPERTS, n), F32),
        ],
        compiler_params=pltpu.CompilerParams(
            dimension_semantics=("parallel",), vmem_limit_bytes=VMEM_LIMIT),
        name="post_mix",
    )(x2d, dn2d, swa2d, wo_dn, wo_swa, ln_g, ln_b, rw_t, sg, su, sd)


def _route_kernel(lg_ref, bias_ref, eidx_ref, gate_ref, rank_ref, cnt_ref, carry_ref, pick_ref):
    @pl.when(pl.program_id(0) == 0)
    def _():
        carry_ref[...] = jnp.zeros_like(carry_ref)

    tt = lg_ref.shape[1]
    scores = _sigmoid(lg_ref[...])
    sel = scores + bias_ref[...]

    iog = lax.broadcasted_iota(I32, (GROUP_SIZE, tt), 0)
    grp_rows = []
    for g in range(N_GROUPS):
        blk = sel[g * GROUP_SIZE:(g + 1) * GROUP_SIZE, :]
        m1 = jnp.max(blk, axis=0, keepdims=True)
        i1 = jnp.min(jnp.where(blk == m1, iog, GROUP_SIZE), axis=0, keepdims=True)
        m2 = jnp.max(jnp.where(iog == i1, NEG_INF, blk), axis=0, keepdims=True)
        grp_rows.append(m1 + m2)
    gs = jnp.concatenate(grp_rows, axis=0)

    io8 = lax.broadcasted_iota(I32, (N_GROUPS, tt), 0)
    gsel = jnp.zeros((N_GROUPS, tt), F32)
    for _ in range(TOPK_GROUPS):
        mg = jnp.max(gs, axis=0, keepdims=True)
        ig = jnp.min(jnp.where(gs == mg, io8, N_GROUPS), axis=0, keepdims=True)
        hit = io8 == ig
        gsel = jnp.where(hit, 1.0, gsel)
        gs = jnp.where(hit, NEG_INF, gs)

    val = jnp.concatenate(
        [jnp.where(gsel[g:g + 1, :] > 0.0, sel[g * GROUP_SIZE:(g + 1) * GROUP_SIZE, :], NEG_INF)
         for g in range(N_GROUPS)], axis=0)

    ioe = lax.broadcasted_iota(I32, (N_EXPERTS, tt), 0)
    v = val
    for _ in range(TOP_K):
        m = jnp.max(v, axis=0, keepdims=True)
        v = jnp.where(v >= m, NEG_INF, v)
    picked = jnp.where(val >= m, 1.0, 0.0)
    pick_ref[...] = picked
    n_off = jnp.max(jnp.abs(jnp.sum(picked, axis=0, keepdims=True) - TOP_K))

    @pl.when(n_off > 0.0)
    def _():
        v = val
        onehot = jnp.zeros((N_EXPERTS, tt), F32)
        for _ in range(TOP_K):
            m = jnp.max(v, axis=0, keepdims=True)
            ik = jnp.min(jnp.where(v == m, ioe, N_EXPERTS), axis=0, keepdims=True)
            hit = ioe == ik
            v = jnp.where(hit, NEG_INF, v)
            onehot = jnp.where(hit, 1.0, onehot)
        pick_ref[...] = onehot

    onehot = pick_ref[...]
    oh16 = onehot.astype(BF16)
    ei = lax.broadcasted_iota(I32, (N_EXPERTS, N_EXPERTS), 0)
    ej = lax.broadcasted_iota(I32, (N_EXPERTS, N_EXPERTS), 1)
    slot = _dot(jnp.where(ej < ei, 1.0, 0.0).astype(BF16), oh16)
    ti = lax.broadcasted_iota(I32, (tt, tt), 0)
    tj = lax.broadcasted_iota(I32, (tt, tt), 1)
    cum = _dot(oh16, jnp.where(ti < tj, 1.0, 0.0).astype(BF16)) + jnp.broadcast_to(
        carry_ref[:, 0:1], (N_EXPERTS, tt))
    key = jnp.where(onehot > 0.0, slot, float(TOP_K))
    packed = ioe.astype(F32) + float(N_EXPERTS) * cum
    packed_rows, gate_rows = [], []
    for k in range(TOP_K):
        at_k = key == float(k)
        packed_rows.append(jnp.sum(jnp.where(at_k, packed, 0.0), axis=0, keepdims=True))
        gate_rows.append(jnp.sum(jnp.where(at_k, scores, 0.0), axis=0, keepdims=True))
    gsum = gate_rows[0]
    for r in gate_rows[1:]:
        gsum = gsum + r
    gate_ref[...] = jnp.concatenate(gate_rows, axis=0) / gsum * ROUTED_SCALE
    packed_i = jnp.concatenate(packed_rows, axis=0).astype(I32)
    eidx_ref[...] = packed_i & (N_EXPERTS - 1)
    rank_ref[...] = packed_i >> (N_EXPERTS.bit_length() - 1)
    carry_ref[...] = carry_ref[...] + jnp.broadcast_to(
        jnp.sum(onehot, axis=1, keepdims=True), carry_ref.shape)
    cnt_ref[...] = carry_ref[...].astype(I32)


def _route(logits_t, bias_col):
    n = logits_t.shape[1]
    tt = TT_ROUTE
    row_spec = pl.BlockSpec((TOP_K, tt), lambda i: (0, i))
    return pl.pallas_call(
        _route_kernel,
        grid=(n // tt,),
        in_specs=[
            pl.BlockSpec((N_EXPERTS, tt), lambda i: (0, i)),
            pl.BlockSpec((N_EXPERTS, 1), lambda i: (0, 0)),
        ],
        out_specs=[row_spec, row_spec, row_spec,
                   pl.BlockSpec((N_EXPERTS, 128), lambda i: (0, 0))],
        out_shape=[
            jax.ShapeDtypeStruct((TOP_K, n), I32),
            jax.ShapeDtypeStruct((TOP_K, n), F32),
            jax.ShapeDtypeStruct((TOP_K, n), I32),
            jax.ShapeDtypeStruct((N_EXPERTS, 128), I32),
        ],
        scratch_shapes=[pltpu.VMEM((N_EXPERTS, 128), F32), pltpu.VMEM((N_EXPERTS, tt), F32)],
        compiler_params=pltpu.CompilerParams(
            dimension_semantics=("arbitrary",), vmem_limit_bytes=VMEM_LIMIT),
        name="route",
    )(logits_t, bias_col)


def _place_kernel(eidx_ref, rank_ref, pstart_ref, dest_ref):
    tt = eidx_ref.shape[1]
    ioe = lax.broadcasted_iota(I32, (N_EXPERTS, tt), 0)
    pstart = pstart_ref[...]
    rows = [jnp.sum(jnp.where(ioe == eidx_ref[k:k + 1, :], pstart, 0.0), axis=0, keepdims=True)
            for k in range(TOP_K)]
    dest_ref[...] = jnp.concatenate(rows, axis=0).astype(I32) + rank_ref[...]


def _place(eidx, rank, pstart_col):
    n = eidx.shape[1]
    tt = TT_PLACE
    row_spec = pl.BlockSpec((TOP_K, tt), lambda i: (0, i))
    return pl.pallas_call(
        _place_kernel,
        grid=(n // tt,),
        in_specs=[row_spec, row_spec, pl.BlockSpec((N_EXPERTS, 1), lambda i: (0, 0))],
        out_specs=row_spec,
        out_shape=jax.ShapeDtypeStruct((TOP_K, n), I32),
        compiler_params=pltpu.CompilerParams(
            dimension_semantics=("parallel",), vmem_limit_bytes=VMEM_LIMIT),
        name="place",
    )(eidx, rank, pstart_col)


def _sc_mesh():
    return plsc.VectorSubcoreMesh(core_axis_name="c", subcore_axis_name="s",
                                  num_cores=SC_NC, num_subcores=SC_NS)


def _sc_scatter_rows(rows, idx, nrows_out):
    n, d = rows.shape
    nk = idx.shape[0]
    per_w = n // SC_NW
    nwin = per_w // SC_WIN
    assert per_w * SC_NW == n and nwin * SC_WIN == per_w and nwin % 2 == 0

    @functools.partial(
        pl.kernel, mesh=_sc_mesh(),
        out_type=jax.ShapeDtypeStruct((nrows_out, d), rows.dtype),
        scratch_types=[
            pltpu.VMEM((nwin, nk, SC_WIN), I32),
            pltpu.VMEM((2, SC_WIN, d), rows.dtype),
            pltpu.SemaphoreType.DMA((2,)),
            pltpu.SemaphoreType.DMA((2,)),
        ],
        compiler_params=pltpu.CompilerParams(use_tc_tiling_on_sc=True),
        name="sc_scatter_rows",
    )
    def scatter_kernel(rows_hbm, idx_hbm, out_hbm, idx_v, rows_v, lsem, ssem):
        wid = lax.axis_index("s") * SC_NC + lax.axis_index("c")
        base = wid * per_w
        pltpu.sync_copy(idx_hbm.at[wid], idx_v)

        def load(w, slot):
            return pltpu.make_async_copy(
                rows_hbm.at[pl.ds(base + w * SC_WIN, SC_WIN)], rows_v.at[slot], lsem.at[slot])

        def scat(w, k, slot):
            return pltpu.make_async_copy(rows_v.at[slot], out_hbm.at[idx_v.at[w, k]], ssem.at[slot])

        load(0, 0).start()

        @pl.loop(0, nwin, step=2)
        def _(w0):
            for slot in range(2):
                w = w0 + slot
                load(w, slot).wait()

                @pl.when(w + 1 < nwin)
                def _():
                    @pl.when(w >= 1)
                    def _():
                        for k in range(nk):
                            scat(w - 1, k, 1 - slot).wait()
                    load(w + 1, 1 - slot).start()

                for k in range(nk):
                    scat(w, k, slot).start()

        for k in range(nk):
            scat(nwin - 2, k, 0).wait()
        for k in range(nk):
            scat(nwin - 1, k, 1).wait()

    idx4 = idx.reshape(nk, SC_NW, nwin, SC_WIN).transpose(1, 2, 0, 3)
    return scatter_kernel(rows, idx4)


def _expert_kernel(gstart_ref, cnt_ref, xs_hbm, wg_ref, wu_ref, wd_ref, y_hbm,
                   wgb_ref, wub_ref, wdb_ref, xbuf_ref, ybuf_ref, xsem, ysem):
    e = pl.program_id(0)
    ne = pl.num_programs(0)
    bm = xbuf_ref.shape[1]
    nblk = y_hbm.shape[0] // bm
    half = D_MODEL // 2
    g_lo = gstart_ref[e]
    g_hi = gstart_ref[e + 1]
    g_end = gstart_ref[ne]

    def x_copy(g, slot):
        return pltpu.make_async_copy(xs_hbm.at[pl.ds(g * bm, bm), :], xbuf_ref.at[slot], xsem.at[slot])

    def y_copy(g, slot):
        return pltpu.make_async_copy(ybuf_ref.at[slot], y_hbm.at[pl.ds(g * bm, bm), :], ysem.at[slot])

    nslot = xbuf_ref.shape[0]

    ahead = nslot - EXP_GROUP

    @pl.when(e == 0)
    def _():
        for g0 in range(ahead):
            @pl.when(g0 < g_end)
            def _():
                x_copy(g0, g0).start()

    @pl.when(g_hi > g_lo)
    def _():
        wgb_ref[...] = wg_ref[...].astype(BF16)
        wub_ref[...] = wu_ref[...].astype(BF16)
        wdb_ref[...] = wd_ref[...].astype(BF16)

    row = lax.broadcasted_iota(I32, (bm, half), 0)

    def acquire(g):
        x_copy(g, g % nslot).wait()

        @pl.when(g + ahead < g_end)
        def _():
            x_copy(g + ahead, (g + ahead) % nslot).start()

        @pl.when(g >= nslot)
        def _():
            y_copy(g - nslot, g % nslot).wait()

    def load(g):
        n_valid = cnt_ref[e] - (g - g_lo) * bm
        x_lo, x_hi = _unpack_bf16_pair(jnp.where(row < n_valid, xbuf_ref[g % nslot], jnp.uint32(0)))
        return x_lo.astype(BF16), x_hi.astype(BF16)

    def gate_up(x):
        x_lo, x_hi = x
        gate = _dot(x_lo, wgb_ref[:half, :]) + _dot(x_hi, wgb_ref[half:, :])
        up = _dot(x_lo, wub_ref[:half, :]) + _dot(x_hi, wub_ref[half:, :])
        return gate, up

    def down(gu):
        gate, up = gu
        return _dot((_silu(gate) * up).astype(BF16), wdb_ref[...])

    def store(g, y):
        ybuf_ref[g % nslot] = _pack_bf16_pair(y[:, :half], y[:, half:])
        y_copy(g, g % nslot).start()

    def run_blocks(g, count):
        for j in range(count):
            acquire(g + j)
        gus = [gate_up(load(g + j)) for j in range(count)]
        ys = [down(gu) for gu in gus]
        for j in range(count):
            store(g + j, ys[j])

    def full_group(p, carry):
        run_blocks(g_lo + EXP_GROUP * p, EXP_GROUP)
        return carry

    n_own = g_hi - g_lo
    lax.fori_loop(0, n_own // EXP_GROUP, full_group, 0)
    size = EXP_GROUP // 2
    while size >= 1:
        @pl.when(n_own % (2 * size) >= size)
        def _(size=size):
            run_blocks(g_lo + n_own // (2 * size) * (2 * size), size)
        size //= 2

    @pl.when(e == ne - 1)
    def _():
        for back in range(nslot, 0, -1):
            @pl.when(g_end >= back)
            def _():
                y_copy(g_end - back, (g_end - back) % nslot).wait()

        ybuf_ref[0] = jnp.zeros((bm, half), U32)

        def fill(g, carry):
            y_copy(g, 0).start()
            return carry

        def drain(g, carry):
            y_copy(g, 0).wait()
            return carry

        lax.fori_loop(g_end, nblk, fill, 0)
        lax.fori_loop(g_end, nblk, drain, 0)


def _experts(gstart, counts, xs, w_gate, w_up, w_down):
    bm = BM_EXP
    nblk = xs.shape[0] // bm
    half = D_MODEL // 2
    grid_spec = pltpu.PrefetchScalarGridSpec(
        num_scalar_prefetch=2,
        grid=(N_EXPERTS,),
        in_specs=[
            pl.BlockSpec(memory_space=pl.ANY),
            pl.BlockSpec((None, D_MODEL, EXPERT_FF), lambda e, gs, cn: (e, 0, 0)),
            pl.BlockSpec((None, D_MODEL, EXPERT_FF), lambda e, gs, cn: (e, 0, 0)),
            pl.BlockSpec((None, EXPERT_FF, D_MODEL), lambda e, gs, cn: (e, 0, 0)),
        ],
        out_specs=pl.BlockSpec(memory_space=pl.ANY),
        scratch_shapes=[
            pltpu.VMEM((D_MODEL, EXPERT_FF), BF16),
            pltpu.VMEM((D_MODEL, EXPERT_FF), BF16),
            pltpu.VMEM((EXPERT_FF, D_MODEL), BF16),
            pltpu.VMEM((EXP_SLOTS, bm, half), U32),
            pltpu.VMEM((EXP_SLOTS, bm, half), U32),
            pltpu.SemaphoreType.DMA((EXP_SLOTS,)),
            pltpu.SemaphoreType.DMA((EXP_SLOTS,)),
        ],
    )
    return pl.pallas_call(
        _expert_kernel,
        grid_spec=grid_spec,
        out_shape=jax.ShapeDtypeStruct((nblk * bm, half), U32),
        compiler_params=pltpu.CompilerParams(
            dimension_semantics=("arbitrary",), vmem_limit_bytes=VMEM_LIMIT),
        name="experts",
    )(gstart, counts, xs, w_gate, w_up, w_down)


def _sc_gather_rows(table, idx):
    nrows = idx.shape[0]
    d = table.shape[1]
    per_w = nrows // SC_NW
    nwin = per_w // SC_WIN
    assert per_w * SC_NW == nrows and nwin * SC_WIN == per_w and nwin % 2 == 0
    @functools.partial(
        pl.kernel, mesh=_sc_mesh(),
        out_type=jax.ShapeDtypeStruct((nrows, d), table.dtype),
        scratch_types=[
            pltpu.VMEM((nwin, SC_WIN), I32),
            pltpu.VMEM((2, SC_WIN, d), table.dtype),
            pltpu.SemaphoreType.DMA((2,)),
            pltpu.SemaphoreType.DMA((2,)),
        ],
        compiler_params=pltpu.CompilerParams(use_tc_tiling_on_sc=True),
        name="sc_gather_rows",
    )
    def gather_kernel(table_hbm, idx_hbm, out_hbm, idx_v, rows_v, gsem, wsem):
        wid = lax.axis_index("s") * SC_NC + lax.axis_index("c")
        base = wid * per_w
        pltpu.sync_copy(idx_hbm.at[wid], idx_v)

        def gather(w, slot):
            return pltpu.make_async_copy(table_hbm.at[idx_v.at[w]], rows_v.at[slot], gsem.at[slot])

        def put(w, slot):
            return pltpu.make_async_copy(
                rows_v.at[slot], out_hbm.at[pl.ds(base + w * SC_WIN, SC_WIN)], wsem.at[slot])

        gather(0, 0).start()

        @pl.loop(0, nwin, step=2)
        def _(w0):
            for slot in range(2):
                w = w0 + slot
                gather(w, slot).wait()

                @pl.when(w + 1 < nwin)
                def _():
                    @pl.when(w >= 1)
                    def _():
                        put(w - 1, 1 - slot).wait()
                    gather(w + 1, 1 - slot).start()

                put(w, slot).start()

        put(nwin - 2, 0).wait()
        put(nwin - 1, 1).wait()

    return gather_kernel(table, idx.reshape(SC_NW, nwin, SC_WIN))


def _combine_kernel(y_ref, base_ref, gate_ref, g_ref, b_ref, out_ref):
    half = D_MODEL // 2
    gates = gate_ref[...].T
    acc_lo = base_ref[:, :half]
    acc_hi = base_ref[:, half:]
    for k in range(TOP_K):
        y_lo, y_hi = _unpack_bf16_pair(y_ref[k])
        gk = gates[:, k:k + 1]
        acc_lo = acc_lo + gk * y_lo
        acc_hi = acc_hi + gk * y_hi
    mu = (jnp.sum(acc_lo, axis=-1, keepdims=True) + jnp.sum(acc_hi, axis=-1, keepdims=True)) / D_MODEL
    c_lo = acc_lo - mu
    c_hi = acc_hi - mu
    var = (jnp.sum(c_lo * c_lo, axis=-1, keepdims=True)
           + jnp.sum(c_hi * c_hi, axis=-1, keepdims=True)) / D_MODEL
    inv = lax.rsqrt(var + LN_EPS)
    out_ref[:, :half] = c_lo * inv * g_ref[:, :half] + b_ref[:, :half]
    out_ref[:, half:] = c_hi * inv * g_ref[:, half:] + b_ref[:, half:]


def _combine(ybuf, base, gate_tok, ln_g, ln_b):
    n = base.shape[0]
    tt = TT_COMB
    half = D_MODEL // 2
    return pl.pallas_call(
        _combine_kernel,
        grid=(n // tt,),
        in_specs=[
            pl.BlockSpec((TOP_K, tt, half), lambda i: (0, i, 0)),
            pl.BlockSpec((tt, D_MODEL), lambda i: (i, 0)),
            pl.BlockSpec((TOP_K, tt), lambda i: (0, i)),
            pl.BlockSpec((1, D_MODEL), lambda i: (0, 0)),
            pl.BlockSpec((1, D_MODEL), lambda i: (0, 0)),
        ],
        out_specs=pl.BlockSpec((tt, D_MODEL), lambda i: (i, 0)),
        out_shape=jax.ShapeDtypeStruct((n, D_MODEL), F32),
        compiler_params=pltpu.CompilerParams(
            dimension_semantics=("parallel",), vmem_limit_bytes=VMEM_LIMIT),
        name="combine",
    )(ybuf, base, gate_tok, ln_g, ln_b)


def _regroup_w_in(w_in):
    o = 0
    cols = {}
    for name, width in (("dnq", DN_WIDTH), ("dnk", DN_WIDTH), ("dnv", DN_WIDTH), ("sq", SWA_WIDTH),
                        ("sk", SWA_KV_WIDTH), ("sv", SWA_KV_WIDTH), ("z", DN_WIDTH),
                        ("b", DN_HEADS), ("a", DN_HEADS)):
        cols[name] = w_in[:, o:o + width]
        o += width
    w_main = jnp.concatenate([cols[k] for k in ("dnq", "dnk", "dnv", "z", "sq", "sk", "sv")], axis=1)
    w_gates = jnp.concatenate(
        [cols["b"], cols["a"], jnp.zeros((D_MODEL, GATE_COLS - 2 * DN_HEADS), w_in.dtype)], axis=1)
    return w_main.astype(BF16), w_gates.astype(BF16)


def _layer(x, w_in, conv_w, a_log, dt_bias, dn_norm_g, sinks, w_out, ln1_g, ln1_b,
           router_w, router_bias, w_gate, w_up, w_down, sh_gate, sh_up, sh_down, ln2_g, ln2_b):
    b, t, d = x.shape
    n = b * t
    x2d = x.reshape(n, d)

    w_main, w_gates = _regroup_w_in(w_in)
    main, gates = _in_proj(x2d, w_main, w_gates)
    main3d = main.reshape(b, t, MAIN_COLS)

    pad = jnp.zeros((GATE_COLS - 2 * DN_HEADS,), F32)
    gpar = jnp.stack([jnp.concatenate([jnp.zeros((DN_HEADS,), F32), a_log.astype(F32), pad]),
                      jnp.concatenate([jnp.zeros((DN_HEADS,), F32), dt_bias.astype(F32), pad])])
    dn_out = _deltanet(main3d, gates.reshape(b, t, GATE_COLS), conv_w.astype(F32), gpar,
                       dn_norm_g.astype(F32).reshape(1, DN_HEAD_DIM))
    swa_out = _swa(main3d, sinks.astype(F32))

    base, xpk, logits_t = _post_mix(
        x2d, dn_out.reshape(n, DN_WIDTH), swa_out.reshape(n, SWA_WIDTH),
        w_out[:DN_WIDTH].astype(BF16), w_out[DN_WIDTH:].astype(BF16),
        ln1_g.reshape(1, d).astype(F32), ln1_b.reshape(1, d).astype(F32),
        router_w.T.astype(BF16),
        sh_gate.astype(BF16), sh_up.astype(BF16), sh_down.astype(BF16))

    eidx, gate, rank, cnt = _route(logits_t, router_bias.astype(F32).reshape(N_EXPERTS, 1))

    bm = BM_EXP
    counts = cnt[:, 0]
    padded = (counts + bm - 1) // bm * bm
    pend = jnp.cumsum(padded)
    pstart = pend - padded
    nblk = -(-(n * TOP_K) // bm) + N_EXPERTS
    gstart = (jnp.concatenate([pstart, pend[-1:]]) // bm).astype(I32)

    dest = _place(eidx, rank, pstart.astype(F32).reshape(N_EXPERTS, 1))
    xs = _sc_scatter_rows(xpk, dest, nblk * bm)
    ypk = _experts(gstart, counts, xs, w_gate, w_up, w_down)
    ybuf = _sc_gather_rows(ypk, dest.reshape(-1)).reshape(TOP_K, n, d // 2)
    out = _combine(ybuf, base, gate, ln2_g.reshape(1, d).astype(F32), ln2_b.reshape(1, d).astype(F32))
    return out.reshape(b, t, d)


def kernel(x, w_in, conv_w, a_log, dt_bias, dn_norm_g, sinks, w_out, ln1_g, ln1_b, router_w, router_bias,
           w_gate, w_up, w_down, shared_w_gate, shared_w_up, shared_w_down, ln2_g, ln2_b):
    depth = w_in.shape[0]
    for l in range(depth):
        x = _layer(x, w_in[l], conv_w[l], a_log[l], dt_bias[l], dn_norm_g[l], sinks[l], w_out[l],
                   ln1_g[l], ln1_b[l], router_w[l], router_bias[l], w_gate[l], w_up[l], w_down[l],
                   shared_w_gate[l], shared_w_up[l], shared_w_down[l], ln2_g[l], ln2_b[l])
    return x
```

```python
import functools

import jax
import jax.numpy as jnp
from jax import lax
from jax.experimental import pallas as pl
from jax.experimental.pallas import tpu as pltpu
from jax.experimental.pallas import tpu_sc as plsc

F32 = jnp.float32
BF16 = jnp.bfloat16
I32 = jnp.int32
U32 = jnp.uint32

D_MODEL = 1024
DN_HEADS = 4
DN_HEAD_DIM = 128
DN_WIDTH = DN_HEADS * DN_HEAD_DIM
CONV_WIDTH = 4
DN_CHUNK = 64
SWA_Q_HEADS = 8
SWA_KV_HEADS = 2
SWA_HEAD_DIM = 64
SWA_WIDTH = SWA_Q_HEADS * SWA_HEAD_DIM
SWA_KV_WIDTH = SWA_KV_HEADS * SWA_HEAD_DIM
SWA_WINDOW = 128
SWA_BLOCK = 128
N_EXPERTS = 256
N_GROUPS = 8
GROUP_SIZE = N_EXPERTS // N_GROUPS
TOPK_GROUPS = 4
TOP_K = 8
EXPERT_FF = 256
SHARED_FF = 256
ROUTED_SCALE = 2.5
DEEPNORM_ALPHA = 2.0 ** 0.25
LN_EPS = 1e-5
RMS_EPS = 1e-6
L2_EPS = 1e-6

COL_DNQ = 0
COL_DNK = DN_WIDTH
COL_DNV = 2 * DN_WIDTH
COL_Z = 3 * DN_WIDTH
COL_SQ = 4 * DN_WIDTH
COL_SK = COL_SQ + SWA_WIDTH
COL_SV = COL_SK + SWA_KV_WIDTH
MAIN_COLS = COL_SV + SWA_KV_WIDTH
GATE_COLS = 128

TM_PROJ = 1024
TS_DN = 256
DN_SEQS = 4
TM_POST = 1024
POST_SPLIT = 4
TT_ROUTE = 2048
TT_PLACE = 2048
BM_EXP = 256
EXP_GROUP = 2
EXP_SLOTS = 6
TT_COMB = 512
SC_NC = 2
SC_NS = 16
SC_NW = SC_NC * SC_NS
SC_WIN = 64
VMEM_LIMIT = 56 * 1024 * 1024
NEG_INF = float("-inf")


def _dot(a, b):
    return jnp.dot(a, b, preferred_element_type=F32)


def _mm(a, b):
    return _dot(a.astype(BF16), b.astype(BF16))


def _mm_nt(a, b):
    return lax.dot_general(a.astype(BF16), b.astype(BF16), (((1,), (1,)), ((), ())),
                           preferred_element_type=F32)


def _sigmoid(x):
    return 1.0 / (1.0 + jnp.exp(-x))


def _silu(x):
    return x * _sigmoid(x)


def _in_proj_kernel(x_ref, w_ref, wg_ref, main_ref, gates_ref):
    xb = x_ref[...].astype(BF16)
    main_ref[...] = _dot(xb, w_ref[...]).astype(BF16)
    gates_ref[...] = _dot(xb, wg_ref[...])


def _in_proj(x2d, w_main, w_gates):
    n = x2d.shape[0]
    return pl.pallas_call(
        _in_proj_kernel,
        grid=(n // TM_PROJ,),
        in_specs=[
            pl.BlockSpec((TM_PROJ, D_MODEL), lambda i: (i, 0)),
            pl.BlockSpec((D_MODEL, MAIN_COLS), lambda i: (0, 0)),
            pl.BlockSpec((D_MODEL, GATE_COLS), lambda i: (0, 0)),
        ],
        out_specs=[
            pl.BlockSpec((TM_PROJ, MAIN_COLS), lambda i: (i, 0)),
            pl.BlockSpec((TM_PROJ, GATE_COLS), lambda i: (i, 0)),
        ],
        out_shape=[
            jax.ShapeDtypeStruct((n, MAIN_COLS), BF16),
            jax.ShapeDtypeStruct((n, GATE_COLS), F32),
        ],
        compiler_params=pltpu.CompilerParams(
            dimension_semantics=("parallel",), vmem_limit_bytes=VMEM_LIMIT),
        name="in_proj",
    )(x2d, w_main, w_gates)


def _dn_kernel(x_ref, gates_ref, convw_ref, gpar_ref, normg_ref, out_ref,
               xc_ref, gl_ref, gc_ref, wq_ref, u_ref, kt_ref, attn_ref, egl_ref, s_ref, hist_ref):
    nseq = x_ref.shape[0]
    ts = x_ref.shape[1]
    c = DN_CHUNK
    hd = DN_HEAD_DIM
    qkv_w = 3 * DN_WIDTH
    nch = ts // c

    @pl.when(pl.program_id(1) == 0)
    def _():
        s_ref[...] = jnp.zeros_like(s_ref)
        hist_ref[...] = jnp.zeros_like(hist_ref)

    def stage_inputs(bi, carry):
        xc_ref[bi, 0:8, :] = hist_ref[bi]
        xc_ref[bi, 8:ts + 8, :] = x_ref[bi, :, 0:qkv_w].astype(F32)
        hist_ref[bi] = xc_ref[bi, ts:ts + 8, :]
        gsl = gates_ref[bi]
        sp_in = gsl + gpar_ref[1:2, :]
        softplus = jnp.maximum(sp_in, 0.0) + jnp.log(1.0 + jnp.exp(-jnp.abs(sp_in)))
        lane = lax.broadcasted_iota(I32, gsl.shape, 1)
        gl = jnp.where(lane < DN_HEADS, _sigmoid(gsl), -jnp.exp(gpar_ref[0:1, :]) * softplus)
        gl_ref[bi] = gl
        row_in_chunk = lax.broadcasted_iota(I32, gsl.shape, 0) % c
        gc = gl
        shift = 1
        while shift < c:
            gc = gc + jnp.where(row_in_chunk >= shift, pltpu.roll(gc, shift, 0), 0.0)
            shift *= 2
        gc_ref[bi] = gc
        return carry

    lax.fori_loop(0, nseq, stage_inputs, 0)

    ii = lax.broadcasted_iota(I32, (c, c), 0)
    jj = lax.broadcasted_iota(I32, (c, c), 1)
    tri_incl = ii >= jj
    tri_strict = ii > jj
    eye = jnp.where(ii == jj, 1.0, 0.0).astype(F32)
    heads = range(DN_HEADS)

    def conv_silu(bi, r0, col):
        w = convw_ref[:, col:col + hd]
        xt = xc_ref[bi, pl.ds(r0, c + 8), col:col + hd]
        y = w[CONV_WIDTH - 1:CONV_WIDTH, :] * xt[8:8 + c, :]
        for delay in range(1, CONV_WIDTH):
            tap = CONV_WIDTH - 1 - delay
            y = y + w[tap:tap + 1, :] * pltpu.roll(xt, delay, 0)[8:8 + c, :]
        return _silu(y)

    def l2n(t, scale):
        return t * (lax.rsqrt(jnp.sum(t * t, axis=-1, keepdims=True) + L2_EPS) * scale)

    def phase_a(ci):
        r0 = ci * c
        chains = []
        for bi in range(nseq):
            glc = gl_ref[bi, pl.ds(r0, c), :]
            gcc = gc_ref[bi, pl.ds(r0, c), :]
            gct = jnp.concatenate([gcc, gcc], axis=0).T
            egl_ref[bi, ci] = jnp.exp(gcc[c - 8:c, :])
            for h in heads:
                chains.append((bi, h, glc, gcc, gct))
        nchain = len(chains)
        q = [l2n(conv_silu(bi, r0, COL_DNQ + h * hd), hd ** -0.5) for (bi, h, _, _, _) in chains]
        k = [l2n(conv_silu(bi, r0, COL_DNK + h * hd), 1.0) for (bi, h, _, _, _) in chains]
        v = [conv_silu(bi, r0, COL_DNV + h * hd) for (bi, h, _, _, _) in chains]
        kb, vb, decay, egc = [], [], [], []
        for n_, (bi, h, glc, gcc, gct) in enumerate(chains):
            beta = glc[:, h:h + 1]
            gc_col = gcc[:, DN_HEADS + h:DN_HEADS + h + 1]
            gc_row = gct[DN_HEADS + h:DN_HEADS + h + 1, 0:c]
            decay.append(jnp.where(tri_incl, jnp.exp(jnp.minimum(gc_col - gc_row, 0.0)), 0.0))
            egc.append(jnp.exp(gc_col))
            e_tail = jnp.exp(gcc[c - 1:c, DN_HEADS + h:DN_HEADS + h + 1] - gc_col)
            kb.append(k[n_] * beta)
            vb.append(v[n_] * beta)
            kt_ref[bi, ci, h] = (k[n_] * e_tail).astype(BF16)
        kq = [_mm_nt(jnp.concatenate([kb[n_], q[n_]], axis=0), k[n_]) for n_ in range(nchain)]
        a_mat = [jnp.where(tri_strict, kq[n_][0:c] * decay[n_], 0.0) for n_ in range(nchain)]
        for n_, (bi, h, _, _, _) in enumerate(chains):
            attn_ref[bi, ci, h] = (kq[n_][c:2 * c] * decay[n_]).astype(BF16)
        t_inv = [eye - a for a in a_mat]
        p = a_mat
        for _ in range(5):
            p = [_mm(x, x) for x in p]
            t_inv = [t + _mm(t, x) for t, x in zip(t_inv, p)]
        for n_, (bi, h, _, _, _) in enumerate(chains):
            uw = _mm(t_inv[n_], jnp.concatenate([vb[n_], kb[n_] * egc[n_]], axis=1))
            u_ref[bi, ci, h] = uw[:, 0:hd]
            wq_ref[bi, ci, h, 0:c, :] = uw[:, hd:2 * hd].astype(BF16)
            wq_ref[bi, ci, h, c:2 * c, :] = (q[n_] * egc[n_]).astype(BF16)

    normg = normg_ref[...]

    def phase_b(ci):
        rows = pl.ds(ci * c, c)
        chains = [(bi, h) for bi in range(nseq) for h in heads]
        egl = [egl_ref[bi, ci] for bi in range(nseq)]
        s_old = [s_ref[bi, h] for bi, h in chains]
        ws = [_dot(wq_ref[bi, ci, h], s.astype(BF16)) for (bi, h), s in zip(chains, s_old)]
        v_new = [(u_ref[bi, ci, h] - w[0:c]).astype(BF16) for (bi, h), w in zip(chains, ws)]
        for n_, (bi, h) in enumerate(chains):
            s_ref[bi, h] = (s_old[n_] * egl[bi][7:8, DN_HEADS + h:DN_HEADS + h + 1]
                            + lax.dot_general(kt_ref[bi, ci, h], v_new[n_], (((0,), (0,)), ((), ())),
                                              preferred_element_type=F32))
        for n_, (bi, h) in enumerate(chains):
            o = ws[n_][c:2 * c] + _dot(attn_ref[bi, ci, h], v_new[n_])
            o = o * lax.rsqrt(jnp.mean(o * o, axis=-1, keepdims=True) + RMS_EPS) * normg
            z = x_ref[bi, rows, COL_Z + h * hd:COL_Z + (h + 1) * hd].astype(F32)
            out_ref[bi, rows, h * hd:(h + 1) * hd] = (o * _silu(z)).astype(out_ref.dtype)

    phase_a(0)
    for ci in range(nch):
        if ci + 1 < nch:
            phase_a(ci + 1)
        phase_b(ci)


def _deltanet(main3d, gates3d, conv_w, gpar, normg):
    b, t, _ = main3d.shape
    ts = TS_DN
    nseq = DN_SEQS
    nch = ts // DN_CHUNK
    dn_in = COL_Z + DN_WIDTH
    return pl.pallas_call(
        _dn_kernel,
        grid=(b // nseq, t // ts),
        in_specs=[
            pl.BlockSpec((nseq, ts, dn_in), lambda bi, si: (bi, si, 0)),
            pl.BlockSpec((nseq, ts, GATE_COLS), lambda bi, si: (bi, si, 0)),
            pl.BlockSpec((CONV_WIDTH, 3 * DN_WIDTH), lambda bi, si: (0, 0)),
            pl.BlockSpec((2, GATE_COLS), lambda bi, si: (0, 0)),
            pl.BlockSpec((1, DN_HEAD_DIM), lambda bi, si: (0, 0)),
        ],
        out_specs=pl.BlockSpec((nseq, ts, DN_WIDTH), lambda bi, si: (bi, si, 0)),
        out_shape=jax.ShapeDtypeStruct((b, t, DN_WIDTH), BF16),
        scratch_shapes=[
            pltpu.VMEM((nseq, ts + 8, 3 * DN_WIDTH), F32),
            pltpu.VMEM((nseq, ts, GATE_COLS), F32),
            pltpu.VMEM((nseq, ts, GATE_COLS), F32),
            pltpu.VMEM((nseq, nch, DN_HEADS, 2 * DN_CHUNK, DN_HEAD_DIM), BF16),
            pltpu.VMEM((nseq, nch, DN_HEADS, DN_CHUNK, DN_HEAD_DIM), F32),
            pltpu.VMEM((nseq, nch, DN_HEADS, DN_CHUNK, DN_HEAD_DIM), BF16),
            pltpu.VMEM((nseq, nch, DN_HEADS, DN_CHUNK, DN_CHUNK), BF16),
            pltpu.VMEM((nseq, nch, 8, GATE_COLS), F32),
            pltpu.VMEM((nseq, DN_HEADS, DN_HEAD_DIM, DN_HEAD_DIM), F32),
            pltpu.VMEM((nseq, 8, 3 * DN_WIDTH), F32),
        ],
        compiler_params=pltpu.CompilerParams(
            dimension_semantics=("parallel", "arbitrary"), vmem_limit_bytes=VMEM_LIMIT),
        name="deltanet",
    )(main3d, gates3d, conv_w, gpar, normg)


def _swa_kernel(sinks_ref, q_ref, kp_ref, kc_ref, vp_ref, vc_ref, out_ref, bias_ref):
    n = pl.program_id(1)
    blk = SWA_BLOCK
    d = SWA_HEAD_DIM
    grp = SWA_Q_HEADS // SWA_KV_HEADS

    @pl.when((pl.program_id(0) == 0) & (n == 0))
    def _():
        qi = lax.broadcasted_iota(I32, (blk, 2 * blk), 0)
        kj = lax.broadcasted_iota(I32, (blk, 2 * blk), 1)
        dist = qi + blk - kj
        valid = (dist >= 0) & (dist < SWA_WINDOW)
        dist_f = dist.astype(F32)
        for hq in range(SWA_Q_HEADS):
            slope = 2.0 ** (-8.0 * (hq + 1.0) / SWA_Q_HEADS)
            bias_ref[hq] = jnp.where(valid, -slope * dist_f, NEG_INF)

    kcol = lax.broadcasted_iota(I32, (1, 2 * blk), 1)
    colmask = jnp.where((kcol >= blk) | (n > 0), 0.0, NEG_INF)
    q_all = q_ref[...] * (d ** -0.5)
    kband = [jnp.concatenate([kp_ref[:, hk * d:(hk + 1) * d], kc_ref[:, hk * d:(hk + 1) * d]], axis=0)
             for hk in range(SWA_KV_HEADS)]
    vband = [jnp.concatenate([vp_ref[:, hk * d:(hk + 1) * d], vc_ref[:, hk * d:(hk + 1) * d]], axis=0)
             for hk in range(SWA_KV_HEADS)]
    heads = range(SWA_Q_HEADS)
    scores = [_mm_nt(q_all[:, hq * d:(hq + 1) * d], kband[hq // grp]) for hq in heads]
    probs, denoms = [], []
    for hq in heads:
        s = scores[hq] + bias_ref[hq] + colmask
        sink = sinks_ref[hq]
        m = jnp.maximum(jnp.max(s, axis=-1, keepdims=True), sink)
        p = jnp.exp(s - m)
        denoms.append(jnp.sum(p, axis=-1, keepdims=True) + jnp.exp(sink - m))
        probs.append(p.astype(BF16))
    outs = [_dot(probs[hq], vband[hq // grp]) / denoms[hq] for hq in heads]
    out_ref[...] = jnp.concatenate(outs, axis=-1).astype(out_ref.dtype)


def _swa(main3d, sinks):
    b, t, _ = main3d.shape
    blk = SWA_BLOCK
    qb = COL_SQ // SWA_WIDTH
    kb = COL_SK // SWA_KV_WIDTH
    vb = COL_SV // SWA_KV_WIDTH
    grid_spec = pltpu.PrefetchScalarGridSpec(
        num_scalar_prefetch=1,
        grid=(b, t // blk),
        in_specs=[
            pl.BlockSpec((None, blk, SWA_WIDTH), lambda bi, ni, s: (bi, ni, qb)),
            pl.BlockSpec((None, blk, SWA_KV_WIDTH), lambda bi, ni, s: (bi, jnp.maximum(ni - 1, 0), kb)),
            pl.BlockSpec((None, blk, SWA_KV_WIDTH), lambda bi, ni, s: (bi, ni, kb)),
            pl.BlockSpec((None, blk, SWA_KV_WIDTH), lambda bi, ni, s: (bi, jnp.maximum(ni - 1, 0), vb)),
            pl.BlockSpec((None, blk, SWA_KV_WIDTH), lambda bi, ni, s: (bi, ni, vb)),
        ],
        out_specs=pl.BlockSpec((None, blk, SWA_WIDTH), lambda bi, ni, s: (bi, ni, 0)),
        scratch_shapes=[pltpu.VMEM((SWA_Q_HEADS, blk, 2 * blk), F32)],
    )
    return pl.pallas_call(
        _swa_kernel,
        grid_spec=grid_spec,
        out_shape=jax.ShapeDtypeStruct((b, t, SWA_WIDTH), BF16),
        compiler_params=pltpu.CompilerParams(
            dimension_semantics=("arbitrary", "arbitrary"), vmem_limit_bytes=VMEM_LIMIT),
        name="swa",
    )(sinks, main3d, main3d, main3d, main3d, main3d)


def _layer_norm(y, g, b):
    mu = jnp.mean(y, axis=-1, keepdims=True)
    yc = y - mu
    var = jnp.mean(yc * yc, axis=-1, keepdims=True)
    return yc * lax.rsqrt(var + LN_EPS) * g + b


def _pack_bf16_pair(lo, hi):
    lo_bits = lax.bitcast_convert_type(lo.astype(BF16).astype(F32), U32)
    hi_bits = lax.bitcast_convert_type(hi.astype(BF16).astype(F32), U32)
    return (hi_bits & jnp.uint32(0xFFFF0000)) | (lo_bits >> 16)


def _unpack_bf16_pair(packed):
    lo = lax.bitcast_convert_type(packed << 16, F32)
    hi = lax.bitcast_convert_type(packed & jnp.uint32(0xFFFF0000), F32)
    return lo, hi


def _post_mix_kernel(x_ref, dn_ref, swa_ref, wo_dn_ref, wo_swa_ref, g_ref, b_ref,
                     rw_ref, sg_ref, su_ref, sd_ref,
                     base_ref, xpk_ref, logit_ref):
    half = D_MODEL // 2
    tm = x_ref.shape[0]
    sub = tm // POST_SPLIT
    parts = [pl.ds(i * sub, sub) for i in range(POST_SPLIT)]
    mix = [_dot(dn_ref[r, :], wo_dn_ref[...]) + _dot(swa_ref[r, :], wo_swa_ref[...]) for r in parts]
    x1 = [_layer_norm(DEEPNORM_ALPHA * x_ref[r, :] + m, g_ref[...], b_ref[...]) for r, m in zip(parts, mix)]
    xh = [x.astype(BF16) for x in x1]
    for r, x in zip(parts, x1):
        xpk_ref[r, :] = _pack_bf16_pair(x[:, :half], x[:, half:])
    for r, x in zip(parts, xh):
        logit_ref[:, r] = lax.dot_general(rw_ref[...], x, (((1,), (1,)), ((), ())),
                                          preferred_element_type=F32)
    gate_up = [(_dot(x, sg_ref[...]), _dot(x, su_ref[...])) for x in xh]
    hmid = [(_silu(g) * u).astype(BF16) for g, u in gate_up]
    for r, x, h in zip(parts, x1, hmid):
        base_ref[r, :] = DEEPNORM_ALPHA * x + _dot(h, sd_ref[...])


def _post_mix(x2d, dn2d, swa2d, wo_dn, wo_swa, ln_g, ln_b, rw_t, sg, su, sd):
    n = x2d.shape[0]
    tm = TM_POST
    full = lambda shape: pl.BlockSpec(shape, lambda i: (0, 0))
    return pl.pallas_call(
        _post_mix_kernel,
        grid=(n // tm,),
        in_specs=[
            pl.BlockSpec((tm, D_MODEL), lambda i: (i, 0)),
            pl.BlockSpec((tm, DN_WIDTH), lambda i: (i, 0)),
            pl.BlockSpec((tm, SWA_WIDTH), lambda i: (i, 0)),
            full((DN_WIDTH, D_MODEL)), full((SWA_WIDTH, D_MODEL)),
            full((1, D_MODEL)), full((1, D_MODEL)),
            full((N_EXPERTS, D_MODEL)),
            full((D_MODEL, SHARED_FF)), full((D_MODEL, SHARED_FF)), full((SHARED_FF, D_MODEL)),
        ],
        out_specs=[
            pl.BlockSpec((tm, D_MODEL), lambda i: (i, 0)),
            pl.BlockSpec((tm, D_MODEL // 2), lambda i: (i, 0)),
            pl.BlockSpec((N_EXPERTS, tm), lambda i: (0, i)),
        ],
        out_shape=[
            jax.ShapeDtypeStruct((n, D_MODEL), F32),
            jax.ShapeDtypeStruct((n, D_MODEL // 2), U32),
            jax.ShapeDtypeStruct((N_EXPERTS, n), F32),
        ],
        compiler_params=pltpu.CompilerParams(
            dimension_semantics=("parallel",), vmem_limit_bytes=VMEM_LIMIT),
        name="post_mix",
    )(x2d, dn2d, swa2d, wo_dn, wo_swa, ln_g, ln_b, rw_t, sg, su, sd)


def _route_kernel(lg_ref, bias_ref, eidx_ref, gate_ref, rank_ref, cnt_ref, carry_ref, pick_ref):
    @pl.when(pl.program_id(0) == 0)
    def _():
        carry_ref[...] = jnp.zeros_like(carry_ref)

    tt = lg_ref.shape[1]
    scores = _sigmoid(lg_ref[...])
    sel = scores + bias_ref[...]

    iog = lax.broadcasted_iota(I32, (GROUP_SIZE, tt), 0)
    grp_rows = []
    for g in range(N_GROUPS):
        blk = sel[g * GROUP_SIZE:(g + 1) * GROUP_SIZE, :]
        m1 = jnp.max(blk, axis=0, keepdims=True)
        i1 = jnp.min(jnp.where(blk == m1, iog, GROUP_SIZE), axis=0, keepdims=True)
        m2 = jnp.max(jnp.where(iog == i1, NEG_INF, blk), axis=0, keepdims=True)
        grp_rows.append(m1 + m2)
    gs = jnp.concatenate(grp_rows, axis=0)

    io8 = lax.broadcasted_iota(I32, (N_GROUPS, tt), 0)
    gsel = jnp.zeros((N_GROUPS, tt), F32)
    for _ in range(TOPK_GROUPS):
        mg = jnp.max(gs, axis=0, keepdims=True)
        ig = jnp.min(jnp.where(gs == mg, io8, N_GROUPS), axis=0, keepdims=True)
        hit = io8 == ig
        gsel = jnp.where(hit, 1.0, gsel)
        gs = jnp.where(hit, NEG_INF, gs)

    val = jnp.concatenate(
        [jnp.where(gsel[g:g + 1, :] > 0.0, sel[g * GROUP_SIZE:(g + 1) * GROUP_SIZE, :], NEG_INF)
         for g in range(N_GROUPS)], axis=0)

    ioe = lax.broadcasted_iota(I32, (N_EXPERTS, tt), 0)
    v = val
    for _ in range(TOP_K):
        m = jnp.max(v, axis=0, keepdims=True)
        v = jnp.where(v >= m, NEG_INF, v)
    picked = jnp.where(val >= m, 1.0, 0.0)
    pick_ref[...] = picked
    n_off = jnp.max(jnp.abs(jnp.sum(picked, axis=0, keepdims=True) - TOP_K))

    @pl.when(n_off > 0.0)
    def _():
        v = val
        onehot = jnp.zeros((N_EXPERTS, tt), F32)
        for _ in range(TOP_K):
            m = jnp.max(v, axis=0, keepdims=True)
            ik = jnp.min(jnp.where(v == m, ioe, N_EXPERTS), axis=0, keepdims=True)
            hit = ioe == ik
            v = jnp.where(hit, NEG_INF, v)
            onehot = jnp.where(hit, 1.0, onehot)
        pick_ref[...] = onehot

    onehot = pick_ref[...]
    oh16 = onehot.astype(BF16)
    ei = lax.broadcasted_iota(I32, (N_EXPERTS, N_EXPERTS), 0)
    ej = lax.broadcasted_iota(I32, (N_EXPERTS, N_EXPERTS), 1)
    slot = _dot(jnp.where(ej < ei, 1.0, 0.0).astype(BF16), oh16)
    ti = lax.broadcasted_iota(I32, (tt, tt), 0)
    tj = lax.broadcasted_iota(I32, (tt, tt), 1)
    cum = _dot(oh16, jnp.where(ti < tj, 1.0, 0.0).astype(BF16)) + jnp.broadcast_to(
        carry_ref[:, 0:1], (N_EXPERTS, tt))
    key = jnp.where(onehot > 0.0, slot, float(TOP_K))
    packed = ioe.astype(F32) + float(N_EXPERTS) * cum
    packed_rows, gate_rows = [], []
    for k in range(TOP_K):
        at_k = key == float(k)
        packed_rows.append(jnp.sum(jnp.where(at_k, packed, 0.0), axis=0, keepdims=True))
        gate_rows.append(jnp.sum(jnp.where(at_k, scores, 0.0), axis=0, keepdims=True))
    gsum = gate_rows[0]
    for r in gate_rows[1:]:
        gsum = gsum + r
    gate_ref[...] = jnp.concatenate(gate_rows, axis=0) / gsum * ROUTED_SCALE
    packed_i = jnp.concatenate(packed_rows, axis=0).astype(I32)
    eidx_ref[...] = packed_i & (N_EXPERTS - 1)
    rank_ref[...] = packed_i >> (N_EXPERTS.bit_length() - 1)
    carry_ref[...] = carry_ref[...] + jnp.broadcast_to(
        jnp.sum(onehot, axis=1, keepdims=True), carry_ref.shape)
    cnt_ref[...] = carry_ref[...].astype(I32)


def _route(logits_t, bias_col):
    n = logits_t.shape[1]
    tt = TT_ROUTE
    row_spec = pl.BlockSpec((TOP_K, tt), lambda i: (0, i))
    return pl.pallas_call(
        _route_kernel,
        grid=(n // tt,),
        in_specs=[
            pl.BlockSpec((N_EXPERTS, tt), lambda i: (0, i)),
            pl.BlockSpec((N_EXPERTS, 1), lambda i: (0, 0)),
        ],
        out_specs=[row_spec, row_spec, row_spec,
                   pl.BlockSpec((N_EXPERTS, 128), lambda i: (0, 0))],
        out_shape=[
            jax.ShapeDtypeStruct((TOP_K, n), I32),
            jax.ShapeDtypeStruct((TOP_K, n), F32),
            jax.ShapeDtypeStruct((TOP_K, n), I32),
            jax.ShapeDtypeStruct((N_EXPERTS, 128), I32),
        ],
        scratch_shapes=[pltpu.VMEM((N_EXPERTS, 128), F32), pltpu.VMEM((N_EXPERTS, tt), F32)],
        compiler_params=pltpu.CompilerParams(
            dimension_semantics=("arbitrary",), vmem_limit_bytes=VMEM_LIMIT),
        name="route",
    )(logits_t, bias_col)


def _place_kernel(eidx_ref, rank_ref, pstart_ref, dest_ref):
    tt = eidx_ref.shape[1]
    ioe = lax.broadcasted_iota(I32, (N_EXPERTS, tt), 0)
    pstart = pstart_ref[...]
    rows = [jnp.sum(jnp.where(ioe == eidx_ref[k:k + 1, :], pstart, 0.0), axis=0, keepdims=True)
            for k in range(TOP_K)]
    dest_ref[...] = jnp.concatenate(rows, axis=0).astype(I32) + rank_ref[...]


def _place(eidx, rank, pstart_col):
    n = eidx.shape[1]
    tt = TT_PLACE
    row_spec = pl.BlockSpec((TOP_K, tt), lambda i: (0, i))
    return pl.pallas_call(
        _place_kernel,
        grid=(n // tt,),
        in_specs=[row_spec, row_spec, pl.BlockSpec((N_EXPERTS, 1), lambda i: (0, 0))],
        out_specs=row_spec,
        out_shape=jax.ShapeDtypeStruct((TOP_K, n), I32),
        compiler_params=pltpu.CompilerParams(
            dimension_semantics=("parallel",), vmem_limit_bytes=VMEM_LIMIT),
        name="place",
    )(eidx, rank, pstart_col)


def _sc_mesh():
    return plsc.VectorSubcoreMesh(core_axis_name="c", subcore_axis_name="s",
                                  num_cores=SC_NC, num_subcores=SC_NS)


def _sc_scatter_rows(rows, idx, nrows_out):
    n, d = rows.shape
    nk = idx.shape[0]
    per_w = n // SC_NW
    nwin = per_w // SC_WIN
    assert per_w * SC_NW == n and nwin * SC_WIN == per_w and nwin % 2 == 0

    @functools.partial(
        pl.kernel, mesh=_sc_mesh(),
        out_type=jax.ShapeDtypeStruct((nrows_out, d), rows.dtype),
        scratch_types=[
            pltpu.VMEM((nwin, nk, SC_WIN), I32),
            pltpu.VMEM((2, SC_WIN, d), rows.dtype),
            pltpu.SemaphoreType.DMA((2,)),
            pltpu.SemaphoreType.DMA((2,)),
        ],
        compiler_params=pltpu.CompilerParams(use_tc_tiling_on_sc=True),
        name="sc_scatter_rows",
    )
    def scatter_kernel(rows_hbm, idx_hbm, out_hbm, idx_v, rows_v, lsem, ssem):
        wid = lax.axis_index("s") * SC_NC + lax.axis_index("c")
        base = wid * per_w
        pltpu.sync_copy(idx_hbm.at[wid], idx_v)

        def load(w, slot):
            return pltpu.make_async_copy(
                rows_hbm.at[pl.ds(base + w * SC_WIN, SC_WIN)], rows_v.at[slot], lsem.at[slot])

        def scat(w, k, slot):
            return pltpu.make_async_copy(rows_v.at[slot], out_hbm.at[idx_v.at[w, k]], ssem.at[slot])

        load(0, 0).start()

        @pl.loop(0, nwin, step=2)
        def _(w0):
            for slot in range(2):
                w = w0 + slot
                load(w, slot).wait()

                @pl.when(w + 1 < nwin)
                def _():
                    @pl.when(w >= 1)
                    def _():
                        for k in range(nk):
                            scat(w - 1, k, 1 - slot).wait()
                    load(w + 1, 1 - slot).start()

                for k in range(nk):
                    scat(w, k, slot).start()

        for k in range(nk):
            scat(nwin - 2, k, 0).wait()
        for k in range(nk):
            scat(nwin - 1, k, 1).wait()

    idx4 = idx.reshape(nk, SC_NW, nwin, SC_WIN).transpose(1, 2, 0, 3)
    return scatter_kernel(rows, idx4)


def _expert_kernel(gstart_ref, cnt_ref, xs_hbm, wg_ref, wu_ref, wd_ref, y_hbm,
                   wgb_ref, wub_ref, wdb_ref, xbuf_ref, ybuf_ref, xsem, ysem):
    e = pl.program_id(0)
    ne = pl.num_programs(0)
    bm = xbuf_ref.shape[1]
    nblk = y_hbm.shape[0] // bm
    half = D_MODEL // 2
    g_lo = gstart_ref[e]
    g_hi = gstart_ref[e + 1]
    g_end = gstart_ref[ne]

    def x_copy(g, slot):
        return pltpu.make_async_copy(xs_hbm.at[pl.ds(g * bm, bm), :], xbuf_ref.at[slot], xsem.at[slot])

    def y_copy(g, slot):
        return pltpu.make_async_copy(ybuf_ref.at[slot], y_hbm.at[pl.ds(g * bm, bm), :], ysem.at[slot])

    nslot = xbuf_ref.shape[0]

    ahead = nslot - EXP_GROUP

    @pl.when(e == 0)
    def _():
        for g0 in range(ahead):
            @pl.when(g0 < g_end)
            def _():
                x_copy(g0, g0).start()

    @pl.when(g_hi > g_lo)
    def _():
        wgb_ref[...] = wg_ref[...].astype(BF16)
        wub_ref[...] = wu_ref[...].astype(BF16)
        wdb_ref[...] = wd_ref[...].astype(BF16)

    row = lax.broadcasted_iota(I32, (bm, half), 0)

    def acquire(g):
        x_copy(g, g % nslot).wait()

        @pl.when(g + ahead < g_end)
        def _():
            x_copy(g + ahead, (g + ahead) % nslot).start()

        @pl.when(g >= nslot)
        def _():
            y_copy(g - nslot, g % nslot).wait()

    def load(g):
        n_valid = cnt_ref[e] - (g - g_lo) * bm
        x_lo, x_hi = _unpack_bf16_pair(jnp.where(row < n_valid, xbuf_ref[g % nslot], jnp.uint32(0)))
        return x_lo.astype(BF16), x_hi.astype(BF16)

    def gate_up(x):
        x_lo, x_hi = x
        gate = _dot(x_lo, wgb_ref[:half, :]) + _dot(x_hi, wgb_ref[half:, :])
        up = _dot(x_lo, wub_ref[:half, :]) + _dot(x_hi, wub_ref[half:, :])
        return gate, up

    def down(gu):
        gate, up = gu
        return _dot((_silu(gate) * up).astype(BF16), wdb_ref[...])

    def store(g, y):
        ybuf_ref[g % nslot] = _pack_bf16_pair(y[:, :half], y[:, half:])
        y_copy(g, g % nslot).start()

    def run_blocks(g, count):
        for j in range(count):
            acquire(g + j)
        gus = [gate_up(load(g + j)) for j in range(count)]
        ys = [down(gu) for gu in gus]
        for j in range(count):
            store(g + j, ys[j])

    def full_group(p, carry):
        run_blocks(g_lo + EXP_GROUP * p, EXP_GROUP)
        return carry

    n_own = g_hi - g_lo
    lax.fori_loop(0, n_own // EXP_GROUP, full_group, 0)
    size = EXP_GROUP // 2
    while size >= 1:
        @pl.when(n_own % (2 * size) >= size)
        def _(size=size):
            run_blocks(g_lo + n_own // (2 * size) * (2 * size), size)
        size //= 2

    @pl.when(e == ne - 1)
    def _():
        for back in range(nslot, 0, -1):
            @pl.when(g_end >= back)
            def _():
                y_copy(g_end - back, (g_end - back) % nslot).wait()

        ybuf_ref[0] = jnp.zeros((bm, half), U32)

        def fill(g, carry):
            y_copy(g, 0).start()
            return carry

        def drain(g, carry):
            y_copy(g, 0).wait()
            return carry

        lax.fori_loop(g_end, nblk, fill, 0)
        lax.fori_loop(g_end, nblk, drain, 0)


def _experts(gstart, counts, xs, w_gate, w_up, w_down):
    bm = BM_EXP
    nblk = xs.shape[0] // bm
    half = D_MODEL // 2
    grid_spec = pltpu.PrefetchScalarGridSpec(
        num_scalar_prefetch=2,
        grid=(N_EXPERTS,),
        in_specs=[
            pl.BlockSpec(memory_space=pl.ANY),
            pl.BlockSpec((None, D_MODEL, EXPERT_FF), lambda e, gs, cn: (e, 0, 0)),
            pl.BlockSpec((None, D_MODEL, EXPERT_FF), lambda e, gs, cn: (e, 0, 0)),
            pl.BlockSpec((None, EXPERT_FF, D_MODEL), lambda e, gs, cn: (e, 0, 0)),
        ],
        out_specs=pl.BlockSpec(memory_space=pl.ANY),
        scratch_shapes=[
            pltpu.VMEM((D_MODEL, EXPERT_FF), BF16),
            pltpu.VMEM((D_MODEL, EXPERT_FF), BF16),
            pltpu.VMEM((EXPERT_FF, D_MODEL), BF16),
            pltpu.VMEM((EXP_SLOTS, bm, half), U32),
            pltpu.VMEM((EXP_SLOTS, bm, half), U32),
            pltpu.SemaphoreType.DMA((EXP_SLOTS,)),
            pltpu.SemaphoreType.DMA((EXP_SLOTS,)),
        ],
    )
    return pl.pallas_call(
        _expert_kernel,
        grid_spec=grid_spec,
        out_shape=jax.ShapeDtypeStruct((nblk * bm, half), U32),
        compiler_params=pltpu.CompilerParams(
            dimension_semantics=("arbitrary",), vmem_limit_bytes=VMEM_LIMIT),
        name="experts",
    )(gstart, counts, xs, w_gate, w_up, w_down)


def _sc_gather_rows(table, idx):
    nrows = idx.shape[0]
    d = table.shape[1]
    per_w = nrows // SC_NW
    nwin = per_w // SC_WIN
    assert per_w * SC_NW == nrows and nwin * SC_WIN == per_w and nwin % 2 == 0
    @functools.partial(
        pl.kernel, mesh=_sc_mesh(),
        out_type=jax.ShapeDtypeStruct((nrows, d), table.dtype),
        scratch_types=[
            pltpu.VMEM((nwin, SC_WIN), I32),
            pltpu.VMEM((2, SC_WIN, d), table.dtype),
            pltpu.SemaphoreType.DMA((2,)),
            pltpu.SemaphoreType.DMA((2,)),
        ],
        compiler_params=pltpu.CompilerParams(use_tc_tiling_on_sc=True),
        name="sc_gather_rows",
    )
    def gather_kernel(table_hbm, idx_hbm, out_hbm, idx_v, rows_v, gsem, wsem):
        wid = lax.axis_index("s") * SC_NC + lax.axis_index("c")
        base = wid * per_w
        pltpu.sync_copy(idx_hbm.at[wid], idx_v)

        def gather(w, slot):
            return pltpu.make_async_copy(table_hbm.at[idx_v.at[w]], rows_v.at[slot], gsem.at[slot])

        def put(w, slot):
            return pltpu.make_async_copy(
                rows_v.at[slot], out_hbm.at[pl.ds(base + w * SC_WIN, SC_WIN)], wsem.at[slot])

        gather(0, 0).start()

        @pl.loop(0, nwin, step=2)
        def _(w0):
            for slot in range(2):
                w = w0 + slot
                gather(w, slot).wait()

                @pl.when(w + 1 < nwin)
                def _():
                    @pl.when(w >= 1)
                    def _():
                        put(w - 1, 1 - slot).wait()
                    gather(w + 1, 1 - slot).start()

                put(w, slot).start()

        put(nwin - 2, 0).wait()
        put(nwin - 1, 1).wait()

    return gather_kernel(table, idx.reshape(SC_NW, nwin, SC_WIN))


def _combine_kernel(y_ref, base_ref, gate_ref, g_ref, b_ref, out_ref):
    half = D_MODEL // 2
    gates = gate_ref[...].T
    acc_lo = base_ref[:, :half]
    acc_hi = base_ref[:, half:]
    for k in range(TOP_K):
        y_lo, y_hi = _unpack_bf16_pair(y_ref[k])
        gk = gates[:, k:k + 1]
        acc_lo = acc_lo + gk * y_lo
        acc_hi = acc_hi + gk * y_hi
    mu = (jnp.sum(acc_lo, axis=-1, keepdims=True) + jnp.sum(acc_hi, axis=-1, keepdims=True)) / D_MODEL
    c_lo = acc_lo - mu
    c_hi = acc_hi - mu
    var = (jnp.sum(c_lo * c_lo, axis=-1, keepdims=True)
           + jnp.sum(c_hi * c_hi, axis=-1, keepdims=True)) / D_MODEL
    inv = lax.rsqrt(var + LN_EPS)
    out_ref[:, :half] = c_lo * inv * g_ref[:, :half] + b_ref[:, :half]
    out_ref[:, half:] = c_hi * inv * g_ref[:, half:] + b_ref[:, half:]


def _combine(ybuf, base, gate_tok, ln_g, ln_b):
    n = base.shape[0]
    tt = TT_COMB
    half = D_MODEL // 2
    return pl.pallas_call(
        _combine_kernel,
        grid=(n // tt,),
        in_specs=[
            pl.BlockSpec((TOP_K, tt, half), lambda i: (0, i, 0)),
            pl.BlockSpec((tt, D_MODEL), lambda i: (i, 0)),
            pl.BlockSpec((TOP_K, tt), lambda i: (0, i)),
            pl.BlockSpec((1, D_MODEL), lambda i: (0, 0)),
            pl.BlockSpec((1, D_MODEL), lambda i: (0, 0)),
        ],
        out_specs=pl.BlockSpec((tt, D_MODEL), lambda i: (i, 0)),
        out_shape=jax.ShapeDtypeStruct((n, D_MODEL), F32),
        compiler_params=pltpu.CompilerParams(
            dimension_semantics=("parallel",), vmem_limit_bytes=VMEM_LIMIT),
        name="combine",
    )(ybuf, base, gate_tok, ln_g, ln_b)


def _regroup_w_in(w_in):
    o = 0
    cols = {}
    for name, width in (("dnq", DN_WIDTH), ("dnk", DN_WIDTH), ("dnv", DN_WIDTH), ("sq", SWA_WIDTH),
                        ("sk", SWA_KV_WIDTH), ("sv", SWA_KV_WIDTH), ("z", DN_WIDTH),
                        ("b", DN_HEADS), ("a", DN_HEADS)):
        cols[name] = w_in[:, o:o + width]
        o += width
    w_main = jnp.concatenate([cols[k] for k in ("dnq", "dnk", "dnv", "z", "sq", "sk", "sv")], axis=1)
    w_gates = jnp.concatenate(
        [cols["b"], cols["a"], jnp.zeros((D_MODEL, GATE_COLS - 2 * DN_HEADS), w_in.dtype)], axis=1)
    return w_main.astype(BF16), w_gates.astype(BF16)


def _layer(x, w_in, conv_w, a_log, dt_bias, dn_norm_g, sinks, w_out, ln1_g, ln1_b,
           router_w, router_bias, w_gate, w_up, w_down, sh_gate, sh_up, sh_down, ln2_g, ln2_b):
    b, t, d = x.shape
    n = b * t
    x2d = x.reshape(n, d)

    w_main, w_gates = _regroup_w_in(w_in)
    main, gates = _in_proj(x2d, w_main, w_gates)
    main3d = main.reshape(b, t, MAIN_COLS)

    pad = jnp.zeros((GATE_COLS - 2 * DN_HEADS,), F32)
    gpar = jnp.stack([jnp.concatenate([jnp.zeros((DN_HEADS,), F32), a_log.astype(F32), pad]),
                      jnp.concatenate([jnp.zeros((DN_HEADS,), F32), dt_bias.astype(F32), pad])])
    dn_out = _deltanet(main3d, gates.reshape(b, t, GATE_COLS), conv_w.astype(F32), gpar,
                       dn_norm_g.astype(F32).reshape(1, DN_HEAD_DIM))
    swa_out = _swa(main3d, sinks.astype(F32))

    base, xpk, logits_t = _post_mix(
        x2d, dn_out.reshape(n, DN_WIDTH), swa_out.reshape(n, SWA_WIDTH),
        w_out[:DN_WIDTH].astype(BF16), w_out[DN_WIDTH:].astype(BF16),
        ln1_g.reshape(1, d).astype(F32), ln1_b.reshape(1, d).astype(F32),
        router_w.T.astype(BF16),
        sh_gate.astype(BF16), sh_up.astype(BF16), sh_down.astype(BF16))

    eidx, gate, rank, cnt = _route(logits_t, router_bias.astype(F32).reshape(N_EXPERTS, 1))

    bm = BM_EXP
    counts = cnt[:, 0]
    padded = (counts + bm - 1) // bm * bm
    pend = jnp.cumsum(padded)
    pstart = pend - padded
    nblk = -(-(n * TOP_K) // bm) + N_EXPERTS
    gstart = (jnp.concatenate([pstart, pend[-1:]]) // bm).astype(I32)

    dest = _place(eidx, rank, pstart.astype(F32).reshape(N_EXPERTS, 1))
    xs = _sc_scatter_rows(xpk, dest, nblk * bm)
    ypk = _experts(gstart, counts, xs, w_gate, w_up, w_down)
    ybuf = _sc_gather_rows(ypk, dest.reshape(-1)).reshape(TOP_K, n, d // 2)
    out = _combine(ybuf, base, gate, ln2_g.reshape(1, d).astype(F32), ln2_b.reshape(1, d).astype(F32))
    return out.reshape(b, t, d)


def kernel(x, w_in, conv_w, a_log, dt_bias, dn_norm_g, sinks, w_out, ln1_g, ln1_b, router_w, router_bias,
           w_gate, w_up, w_down, shared_w_gate, shared_w_up, shared_w_down, ln2_g, ln2_b):
    depth = w_in.shape[0]
    for l in range(depth):
        x = _layer(x, w_in[l], conv_w[l], a_log[l], dt_bias[l], dn_norm_g[l], sinks[l], w_out[l],
                   ln1_g[l], ln1_b[l], router_w[l], router_bias[l], w_gate[l], w_up[l], w_down[l],
                   shared_w_gate[l], shared_w_up[l], shared_w_down[l], ln2_g[l], ln2_b[l])
    return x
```

```python
import functools

import jax
import jax.numpy as jnp
from jax import lax
from jax.experimental import pallas as pl
from jax.experimental.pallas import tpu as pltpu
from jax.experimental.pallas import tpu_sc as plsc

F32 = jnp.float32
BF16 = jnp.bfloat16
I32 = jnp.int32
U32 = jnp.uint32

D_MODEL = 1024
DN_HEADS = 4
DN_HEAD_DIM = 128
DN_WIDTH = DN_HEADS * DN_HEAD_DIM
CONV_WIDTH = 4
DN_CHUNK = 64
SWA_Q_HEADS = 8
SWA_KV_HEADS = 2
SWA_HEAD_DIM = 64
SWA_WIDTH = SWA_Q_HEADS * SWA_HEAD_DIM
SWA_KV_WIDTH = SWA_KV_HEADS * SWA_HEAD_DIM
SWA_WINDOW = 128
SWA_BLOCK = 128
N_EXPERTS = 256
N_GROUPS = 8
GROUP_SIZE = N_EXPERTS // N_GROUPS
TOPK_GROUPS = 4
TOP_K = 8
EXPERT_FF = 256
SHARED_FF = 256
ROUTED_SCALE = 2.5
DEEPNORM_ALPHA = 2.0 ** 0.25
LN_EPS = 1e-5
RMS_EPS = 1e-6
L2_EPS = 1e-6

COL_DNQ = 0
COL_DNK = DN_WIDTH
COL_DNV = 2 * DN_WIDTH
COL_Z = 3 * DN_WIDTH
COL_SQ = 4 * DN_WIDTH
COL_SK = COL_SQ + SWA_WIDTH
COL_SV = COL_SK + SWA_KV_WIDTH
MAIN_COLS = COL_SV + SWA_KV_WIDTH
GATE_COLS = 128

TM_PROJ = 1024
TS_DN = 256
DN_SEQS = 4
TM_POST = 1024
POST_SPLIT = 2
TT_ROUTE = 1024
TT_PLACE = 2048
BM_EXP = 256
EXP_GROUP = 2
EXP_SLOTS = 8
TT_COMB = 512
SC_NC = 2
SC_NS = 16
SC_NW = SC_NC * SC_NS
SC_WIN = 64
VMEM_LIMIT = 56 * 1024 * 1024
NEG_INF = float("-inf")


def _dot(a, b):
    return jnp.dot(a, b, preferred_element_type=F32)


def _mm(a, b):
    return _dot(a.astype(BF16), b.astype(BF16))


def _mm_nt(a, b):
    return lax.dot_general(a.astype(BF16), b.astype(BF16), (((1,), (1,)), ((), ())),
                           preferred_element_type=F32)


def _sigmoid(x):
    return 1.0 / (1.0 + jnp.exp(-x))


def _silu(x):
    return x * _sigmoid(x)


def _in_proj_kernel(x_ref, w_ref, wg_ref, main_ref, gates_ref):
    xb = x_ref[...].astype(BF16)
    main_ref[...] = _dot(xb, w_ref[...]).astype(BF16)
    gates_ref[...] = _dot(xb, wg_ref[...])


def _in_proj(x2d, w_main, w_gates):
    n = x2d.shape[0]
    return pl.pallas_call(
        _in_proj_kernel,
        grid=(n // TM_PROJ,),
        in_specs=[
            pl.BlockSpec((TM_PROJ, D_MODEL), lambda i: (i, 0)),
            pl.BlockSpec((D_MODEL, MAIN_COLS), lambda i: (0, 0)),
            pl.BlockSpec((D_MODEL, GATE_COLS), lambda i: (0, 0)),
        ],
        out_specs=[
            pl.BlockSpec((TM_PROJ, MAIN_COLS), lambda i: (i, 0)),
            pl.BlockSpec((TM_PROJ, GATE_COLS), lambda i: (i, 0)),
        ],
        out_shape=[
            jax.ShapeDtypeStruct((n, MAIN_COLS), BF16),
            jax.ShapeDtypeStruct((n, GATE_COLS), F32),
        ],
        compiler_params=pltpu.CompilerParams(
            dimension_semantics=("parallel",), vmem_limit_bytes=VMEM_LIMIT),
        name="in_proj",
    )(x2d, w_main, w_gates)


def _dn_kernel(x_ref, gates_ref, convw_ref, gpar_ref, normg_ref, out_ref,
               xc_ref, gl_ref, gc_ref, wq_ref, u_ref, kt_ref, attn_ref, egl_ref, s_ref, hist_ref):
    nseq = x_ref.shape[0]
    ts = x_ref.shape[1]
    c = DN_CHUNK
    hd = DN_HEAD_DIM
    qkv_w = 3 * DN_WIDTH
    nch = ts // c

    @pl.when(pl.program_id(1) == 0)
    def _():
        s_ref[...] = jnp.zeros_like(s_ref)
        hist_ref[...] = jnp.zeros_like(hist_ref)

    def stage_inputs(bi, carry):
        xc_ref[bi, 0:8, :] = hist_ref[bi]
        xc_ref[bi, 8:ts + 8, :] = x_ref[bi, :, 0:qkv_w].astype(F32)
        hist_ref[bi] = xc_ref[bi, ts:ts + 8, :]
        gsl = gates_ref[bi]
        sp_in = gsl + gpar_ref[1:2, :]
        softplus = jnp.maximum(sp_in, 0.0) + jnp.log(1.0 + jnp.exp(-jnp.abs(sp_in)))
        lane = lax.broadcasted_iota(I32, gsl.shape, 1)
        gl = jnp.where(lane < DN_HEADS, _sigmoid(gsl), -jnp.exp(gpar_ref[0:1, :]) * softplus)
        gl_ref[bi] = gl
        row_in_chunk = lax.broadcasted_iota(I32, gsl.shape, 0) % c
        gc = gl
        shift = 1
        while shift < c:
            gc = gc + jnp.where(row_in_chunk >= shift, pltpu.roll(gc, shift, 0), 0.0)
            shift *= 2
        gc_ref[bi] = gc
        return carry

    lax.fori_loop(0, nseq, stage_inputs, 0)

    ii = lax.broadcasted_iota(I32, (c, c), 0)
    jj = lax.broadcasted_iota(I32, (c, c), 1)
    tri_incl = ii >= jj
    tri_strict = ii > jj
    eye = jnp.where(ii == jj, 1.0, 0.0).astype(F32)
    heads = range(DN_HEADS)

    def conv_silu(bi, r0, col):
        w = convw_ref[:, col:col + hd]
        xt = xc_ref[bi, pl.ds(r0, c + 8), col:col + hd]
        y = w[CONV_WIDTH - 1:CONV_WIDTH, :] * xt[8:8 + c, :]
        for delay in range(1, CONV_WIDTH):
            tap = CONV_WIDTH - 1 - delay
            y = y + w[tap:tap + 1, :] * pltpu.roll(xt, delay, 0)[8:8 + c, :]
        return _silu(y)

    def l2n(t, scale):
        return t * (lax.rsqrt(jnp.sum(t * t, axis=-1, keepdims=True) + L2_EPS) * scale)

    def phase_a(ci):
        r0 = ci * c
        chains = []
        for bi in range(nseq):
            glc = gl_ref[bi, pl.ds(r0, c), :]
            gcc = gc_ref[bi, pl.ds(r0, c), :]
            gct = jnp.concatenate([gcc, gcc], axis=0).T
            egl_ref[bi, ci] = jnp.exp(gcc[c - 8:c, :])
            for h in heads:
                chains.append((bi, h, glc, gcc, gct))
        nchain = len(chains)
        q = [l2n(conv_silu(bi, r0, COL_DNQ + h * hd), hd ** -0.5) for (bi, h, _, _, _) in chains]
        k = [l2n(conv_silu(bi, r0, COL_DNK + h * hd), 1.0) for (bi, h, _, _, _) in chains]
        v = [conv_silu(bi, r0, COL_DNV + h * hd) for (bi, h, _, _, _) in chains]
        kb, vb, decay, egc = [], [], [], []
        for n_, (bi, h, glc, gcc, gct) in enumerate(chains):
            beta = glc[:, h:h + 1]
            gc_col = gcc[:, DN_HEADS + h:DN_HEADS + h + 1]
            gc_row = gct[DN_HEADS + h:DN_HEADS + h + 1, 0:c]
            decay.append(jnp.where(tri_incl, jnp.exp(jnp.minimum(gc_col - gc_row, 0.0)), 0.0))
            egc.append(jnp.exp(gc_col))
            e_tail = jnp.exp(gcc[c - 1:c, DN_HEADS + h:DN_HEADS + h + 1] - gc_col)
            kb.append(k[n_] * beta)
            vb.append(v[n_] * beta)
            kt_ref[bi, ci, h] = (k[n_] * e_tail).astype(BF16)
        kq = [_mm_nt(jnp.concatenate([kb[n_], q[n_]], axis=0), k[n_]) for n_ in range(nchain)]
        a_mat = [jnp.where(tri_strict, kq[n_][0:c] * decay[n_], 0.0) for n_ in range(nchain)]
        for n_, (bi, h, _, _, _) in enumerate(chains):
            attn_ref[bi, ci, h] = (kq[n_][c:2 * c] * decay[n_]).astype(BF16)
        t_inv = [eye - a for a in a_mat]
        p = a_mat
        for _ in range(5):
            p = [_mm(x, x) for x in p]
            t_inv = [t + _mm(t, x) for t, x in zip(t_inv, p)]
        for n_, (bi, h, _, _, _) in enumerate(chains):
            uw = _mm(t_inv[n_], jnp.concatenate([vb[n_], kb[n_] * egc[n_]], axis=1))
            u_ref[bi, ci, h] = uw[:, 0:hd]
            wq_ref[bi, ci, h, 0:c, :] = uw[:, hd:2 * hd].astype(BF16)
            wq_ref[bi, ci, h, c:2 * c, :] = (q[n_] * egc[n_]).astype(BF16)

    normg = normg_ref[...]

    def phase_b(ci):
        rows = pl.ds(ci * c, c)
        chains = [(bi, h) for bi in range(nseq) for h in heads]
        egl = [egl_ref[bi, ci] for bi in range(nseq)]
        s_old = [s_ref[bi, h] for bi, h in chains]
        ws = [_dot(wq_ref[bi, ci, h], s.astype(BF16)) for (bi, h), s in zip(chains, s_old)]
        v_new = [(u_ref[bi, ci, h] - w[0:c]).astype(BF16) for (bi, h), w in zip(chains, ws)]
        for n_, (bi, h) in enumerate(chains):
            s_ref[bi, h] = (s_old[n_] * egl[bi][7:8, DN_HEADS + h:DN_HEADS + h + 1]
                            + lax.dot_general(kt_ref[bi, ci, h], v_new[n_], (((0,), (0,)), ((), ())),
                                              preferred_element_type=F32))
        for n_, (bi, h) in enumerate(chains):
            o = ws[n_][c:2 * c] + _dot(attn_ref[bi, ci, h], v_new[n_])
            o = o * lax.rsqrt(jnp.mean(o * o, axis=-1, keepdims=True) + RMS_EPS) * normg
            z = x_ref[bi, rows, COL_Z + h * hd:COL_Z + (h + 1) * hd].astype(F32)
            out_ref[bi, rows, h * hd:(h + 1) * hd] = (o * _silu(z)).astype(out_ref.dtype)

    phase_a(0)
    for ci in range(nch):
        if ci + 1 < nch:
            phase_a(ci + 1)
        phase_b(ci)


def _deltanet(main3d, gates3d, conv_w, gpar, normg):
    b, t, _ = main3d.shape
    ts = TS_DN
    nseq = DN_SEQS
    nch = ts // DN_CHUNK
    dn_in = COL_Z + DN_WIDTH
    return pl.pallas_call(
        _dn_kernel,
        grid=(b // nseq, t // ts),
        in_specs=[
            pl.BlockSpec((nseq, ts, dn_in), lambda bi, si: (bi, si, 0)),
            pl.BlockSpec((nseq, ts, GATE_COLS), lambda bi, si: (bi, si, 0)),
            pl.BlockSpec((CONV_WIDTH, 3 * DN_WIDTH), lambda bi, si: (0, 0)),
            pl.BlockSpec((2, GATE_COLS), lambda bi, si: (0, 0)),
            pl.BlockSpec((1, DN_HEAD_DIM), lambda bi, si: (0, 0)),
        ],
        out_specs=pl.BlockSpec((nseq, ts, DN_WIDTH), lambda bi, si: (bi, si, 0)),
        out_shape=jax.ShapeDtypeStruct((b, t, DN_WIDTH), BF16),
        scratch_shapes=[
            pltpu.VMEM((nseq, ts + 8, 3 * DN_WIDTH), F32),
            pltpu.VMEM((nseq, ts, GATE_COLS), F32),
            pltpu.VMEM((nseq, ts, GATE_COLS), F32),
            pltpu.VMEM((nseq, nch, DN_HEADS, 2 * DN_CHUNK, DN_HEAD_DIM), BF16),
            pltpu.VMEM((nseq, nch, DN_HEADS, DN_CHUNK, DN_HEAD_DIM), F32),
            pltpu.VMEM((nseq, nch, DN_HEADS, DN_CHUNK, DN_HEAD_DIM), BF16),
            pltpu.VMEM((nseq, nch, DN_HEADS, DN_CHUNK, DN_CHUNK), BF16),
            pltpu.VMEM((nseq, nch, 8, GATE_COLS), F32),
            pltpu.VMEM((nseq, DN_HEADS, DN_HEAD_DIM, DN_HEAD_DIM), F32),
            pltpu.VMEM((nseq, 8, 3 * DN_WIDTH), F32),
        ],
        compiler_params=pltpu.CompilerParams(
            dimension_semantics=("parallel", "arbitrary"), vmem_limit_bytes=VMEM_LIMIT),
        name="deltanet",
    )(main3d, gates3d, conv_w, gpar, normg)


def _swa_kernel(sinks_ref, q_ref, kp_ref, kc_ref, vp_ref, vc_ref, out_ref, bias_ref):
    n = pl.program_id(1)
    blk = SWA_BLOCK
    d = SWA_HEAD_DIM
    grp = SWA_Q_HEADS // SWA_KV_HEADS

    @pl.when((pl.program_id(0) == 0) & (n == 0))
    def _():
        qi = lax.broadcasted_iota(I32, (blk, 2 * blk), 0)
        kj = lax.broadcasted_iota(I32, (blk, 2 * blk), 1)
        dist = qi + blk - kj
        valid = (dist >= 0) & (dist < SWA_WINDOW)
        dist_f = dist.astype(F32)
        for hq in range(SWA_Q_HEADS):
            slope = 2.0 ** (-8.0 * (hq + 1.0) / SWA_Q_HEADS)
            bias_ref[hq] = jnp.where(valid, -slope * dist_f, NEG_INF)

    kcol = lax.broadcasted_iota(I32, (1, 2 * blk), 1)
    colmask = jnp.where((kcol >= blk) | (n > 0), 0.0, NEG_INF)
    q_all = q_ref[...] * (d ** -0.5)
    kband = [jnp.concatenate([kp_ref[:, hk * d:(hk + 1) * d], kc_ref[:, hk * d:(hk + 1) * d]], axis=0)
             for hk in range(SWA_KV_HEADS)]
    vband = [jnp.concatenate([vp_ref[:, hk * d:(hk + 1) * d], vc_ref[:, hk * d:(hk + 1) * d]], axis=0)
             for hk in range(SWA_KV_HEADS)]
    heads = range(SWA_Q_HEADS)
    scores = [_mm_nt(q_all[:, hq * d:(hq + 1) * d], kband[hq // grp]) for hq in heads]
    probs, denoms = [], []
    for hq in heads:
        s = scores[hq] + bias_ref[hq] + colmask
        sink = sinks_ref[hq]
        m = jnp.maximum(jnp.max(s, axis=-1, keepdims=True), sink)
        p = jnp.exp(s - m)
        denoms.append(jnp.sum(p, axis=-1, keepdims=True) + jnp.exp(sink - m))
        probs.append(p.astype(BF16))
    outs = [_dot(probs[hq], vband[hq // grp]) / denoms[hq] for hq in heads]
    out_ref[...] = jnp.concatenate(outs, axis=-1).astype(out_ref.dtype)


def _swa(main3d, sinks):
    b, t, _ = main3d.shape
    blk = SWA_BLOCK
    qb = COL_SQ // SWA_WIDTH
    kb = COL_SK // SWA_KV_WIDTH
    vb = COL_SV // SWA_KV_WIDTH
    grid_spec = pltpu.PrefetchScalarGridSpec(
        num_scalar_prefetch=1,
        grid=(b, t // blk),
        in_specs=[
            pl.BlockSpec((None, blk, SWA_WIDTH), lambda bi, ni, s: (bi, ni, qb)),
            pl.BlockSpec((None, blk, SWA_KV_WIDTH), lambda bi, ni, s: (bi, jnp.maximum(ni - 1, 0), kb)),
            pl.BlockSpec((None, blk, SWA_KV_WIDTH), lambda bi, ni, s: (bi, ni, kb)),
            pl.BlockSpec((None, blk, SWA_KV_WIDTH), lambda bi, ni, s: (bi, jnp.maximum(ni - 1, 0), vb)),
            pl.BlockSpec((None, blk, SWA_KV_WIDTH), lambda bi, ni, s: (bi, ni, vb)),
        ],
        out_specs=pl.BlockSpec((None, blk, SWA_WIDTH), lambda bi, ni, s: (bi, ni, 0)),
        scratch_shapes=[pltpu.VMEM((SWA_Q_HEADS, blk, 2 * blk), F32)],
    )
    return pl.pallas_call(
        _swa_kernel,
        grid_spec=grid_spec,
        out_shape=jax.ShapeDtypeStruct((b, t, SWA_WIDTH), BF16),
        compiler_params=pltpu.CompilerParams(
            dimension_semantics=("arbitrary", "arbitrary"), vmem_limit_bytes=VMEM_LIMIT),
        name="swa",
    )(sinks, main3d, main3d, main3d, main3d, main3d)


def _layer_norm(y, g, b):
    mu = jnp.mean(y, axis=-1, keepdims=True)
    yc = y - mu
    var = jnp.mean(yc * yc, axis=-1, keepdims=True)
    return yc * lax.rsqrt(var + LN_EPS) * g + b


def _pack_bf16_pair(lo, hi):
    lo_bits = lax.bitcast_convert_type(lo.astype(BF16).astype(F32), U32)
    hi_bits = lax.bitcast_convert_type(hi.astype(BF16).astype(F32), U32)
    return (hi_bits & jnp.uint32(0xFFFF0000)) | (lo_bits >> 16)


def _unpack_bf16_pair(packed):
    lo = lax.bitcast_convert_type(packed << 16, F32)
    hi = lax.bitcast_convert_type(packed & jnp.uint32(0xFFFF0000), F32)
    return lo, hi


def _post_mix_kernel(x_ref, dn_ref, swa_ref, wo_dn_ref, wo_swa_ref, g_ref, b_ref,
                     rw_ref, sg_ref, su_ref, sd_ref,
                     base_ref, xpk_ref, logit_ref):
    half = D_MODEL // 2
    tm = x_ref.shape[0]
    sub = tm // POST_SPLIT
    parts = [pl.ds(i * sub, sub) for i in range(POST_SPLIT)]
    mix = [_dot(dn_ref[r, :], wo_dn_ref[...]) + _dot(swa_ref[r, :], wo_swa_ref[...]) for r in parts]
    x1 = [_layer_norm(DEEPNORM_ALPHA * x_ref[r, :] + m, g_ref[...], b_ref[...]) for r, m in zip(parts, mix)]
    xh = [x.astype(BF16) for x in x1]
    for r, x in zip(parts, x1):
        xpk_ref[r, :] = _pack_bf16_pair(x[:, :half], x[:, half:])
    for r, x in zip(parts, xh):
        logit_ref[:, r] = lax.dot_general(rw_ref[...], x, (((1,), (1,)), ((), ())),
                                          preferred_element_type=F32)
    gate_up = [(_dot(x, sg_ref[...]), _dot(x, su_ref[...])) for x in xh]
    hmid = [(_silu(g) * u).astype(BF16) for g, u in gate_up]
    for r, x, h in zip(parts, x1, hmid):
        base_ref[r, :] = DEEPNORM_ALPHA * x + _dot(h, sd_ref[...])


def _post_mix(x2d, dn2d, swa2d, wo_dn, wo_swa, ln_g, ln_b, rw_t, sg, su, sd):
    n = x2d.shape[0]
    tm = TM_POST
    full = lambda shape: pl.BlockSpec(shape, lambda i: (0, 0))
    return pl.pallas_call(
        _post_mix_kernel,
        grid=(n // tm,),
        in_specs=[
            pl.BlockSpec((tm, D_MODEL), lambda i: (i, 0)),
            pl.BlockSpec((tm, DN_WIDTH), lambda i: (i, 0)),
            pl.BlockSpec((tm, SWA_WIDTH), lambda i: (i, 0)),
            full((DN_WIDTH, D_MODEL)), full((SWA_WIDTH, D_MODEL)),
            full((1, D_MODEL)), full((1, D_MODEL)),
            full((N_EXPERTS, D_MODEL)),
            full((D_MODEL, SHARED_FF)), full((D_MODEL, SHARED_FF)), full((SHARED_FF, D_MODEL)),
        ],
        out_specs=[
            pl.BlockSpec((tm, D_MODEL), lambda i: (i, 0)),
            pl.BlockSpec((tm, D_MODEL // 2), lambda i: (i, 0)),
            pl.BlockSpec((N_EXPERTS, tm), lambda i: (0, i)),
        ],
        out_shape=[
            jax.ShapeDtypeStruct((n, D_MODEL), F32),
            jax.ShapeDtypeStruct((n, D_MODEL // 2), U32),
            jax.ShapeDtypeStruct((N_EXPERTS, n), F32),
        ],
        compiler_params=pltpu.CompilerParams(
            dimension_semantics=("parallel",), vmem_limit_bytes=VMEM_LIMIT),
        name="post_mix",
    )(x2d, dn2d, swa2d, wo_dn, wo_swa, ln_g, ln_b, rw_t, sg, su, sd)


def _route_kernel(lg_ref, bias_ref, eidx_ref, gate_ref, rank_ref, cnt_ref, carry_ref, pick_ref):
    @pl.when(pl.program_id(0) == 0)
    def _():
        carry_ref[...] = jnp.zeros_like(carry_ref)

    tt = lg_ref.shape[1]
    scores = _sigmoid(lg_ref[...])
    sel = scores + bias_ref[...]

    iog = lax.broadcasted_iota(I32, (GROUP_SIZE, tt), 0)
    grp_rows = []
    for g in range(N_GROUPS):
        blk = sel[g * GROUP_SIZE:(g + 1) * GROUP_SIZE, :]
        m1 = jnp.max(blk, axis=0, keepdims=True)
        i1 = jnp.min(jnp.where(blk == m1, iog, GROUP_SIZE), axis=0, keepdims=True)
        m2 = jnp.max(jnp.where(iog == i1, NEG_INF, blk), axis=0, keepdims=True)
        grp_rows.append(m1 + m2)
    gs = jnp.concatenate(grp_rows, axis=0)

    io8 = lax.broadcasted_iota(I32, (N_GROUPS, tt), 0)
    gsel = jnp.zeros((N_GROUPS, tt), F32)
    for _ in range(TOPK_GROUPS):
        mg = jnp.max(gs, axis=0, keepdims=True)
        ig = jnp.min(jnp.where(gs == mg, io8, N_GROUPS), axis=0, keepdims=True)
        hit = io8 == ig
        gsel = jnp.where(hit, 1.0, gsel)
        gs = jnp.where(hit, NEG_INF, gs)

    val = jnp.concatenate(
        [jnp.where(gsel[g:g + 1, :] > 0.0, sel[g * GROUP_SIZE:(g + 1) * GROUP_SIZE, :], NEG_INF)
         for g in range(N_GROUPS)], axis=0)

    ioe = lax.broadcasted_iota(I32, (N_EXPERTS, tt), 0)
    v = val
    for _ in range(TOP_K):
        m = jnp.max(v, axis=0, keepdims=True)
        v = jnp.where(v >= m, NEG_INF, v)
    picked = jnp.where(val >= m, 1.0, 0.0)
    pick_ref[...] = picked
    n_off = jnp.max(jnp.abs(jnp.sum(picked, axis=0, keepdims=True) - TOP_K))

    @pl.when(n_off > 0.0)
    def _():
        v = val
        onehot = jnp.zeros((N_EXPERTS, tt), F32)
        for _ in range(TOP_K):
            m = jnp.max(v, axis=0, keepdims=True)
            ik = jnp.min(jnp.where(v == m, ioe, N_EXPERTS), axis=0, keepdims=True)
            hit = ioe == ik
            v = jnp.where(hit, NEG_INF, v)
            onehot = jnp.where(hit, 1.0, onehot)
        pick_ref[...] = onehot

    onehot = pick_ref[...]
    oh16 = onehot.astype(BF16)
    ei = lax.broadcasted_iota(I32, (N_EXPERTS, N_EXPERTS), 0)
    ej = lax.broadcasted_iota(I32, (N_EXPERTS, N_EXPERTS), 1)
    slot = _dot(jnp.where(ej < ei, 1.0, 0.0).astype(BF16), oh16)
    ti = lax.broadcasted_iota(I32, (tt, tt), 0)
    tj = lax.broadcasted_iota(I32, (tt, tt), 1)
    cum = _dot(oh16, jnp.where(ti < tj, 1.0, 0.0).astype(BF16)) + jnp.broadcast_to(
        carry_ref[:, 0:1], (N_EXPERTS, tt))
    key = jnp.where(onehot > 0.0, slot, float(TOP_K))
    packed = ioe.astype(F32) + float(N_EXPERTS) * cum
    packed_rows, gate_rows = [], []
    for k in range(TOP_K):
        at_k = key == float(k)
        packed_rows.append(jnp.sum(jnp.where(at_k, packed, 0.0), axis=0, keepdims=True))
        gate_rows.append(jnp.sum(jnp.where(at_k, scores, 0.0), axis=0, keepdims=True))
    gsum = gate_rows[0]
    for r in gate_rows[1:]:
        gsum = gsum + r
    gate_ref[...] = jnp.concatenate(gate_rows, axis=0) / gsum * ROUTED_SCALE
    packed_i = jnp.concatenate(packed_rows, axis=0).astype(I32)
    eidx_ref[...] = packed_i & (N_EXPERTS - 1)
    rank_ref[...] = packed_i >> (N_EXPERTS.bit_length() - 1)
    carry_ref[...] = carry_ref[...] + jnp.broadcast_to(
        jnp.sum(onehot, axis=1, keepdims=True), carry_ref.shape)
    cnt_ref[...] = carry_ref[...].astype(I32)


def _route(logits_t, bias_col):
    n = logits_t.shape[1]
    tt = TT_ROUTE
    row_spec = pl.BlockSpec((TOP_K, tt), lambda i: (0, i))
    return pl.pallas_call(
        _route_kernel,
        grid=(n // tt,),
        in_specs=[
            pl.BlockSpec((N_EXPERTS, tt), lambda i: (0, i)),
            pl.BlockSpec((N_EXPERTS, 1), lambda i: (0, 0)),
        ],
        out_specs=[row_spec, row_spec, row_spec,
                   pl.BlockSpec((N_EXPERTS, 128), lambda i: (0, 0))],
        out_shape=[
            jax.ShapeDtypeStruct((TOP_K, n), I32),
            jax.ShapeDtypeStruct((TOP_K, n), F32),
            jax.ShapeDtypeStruct((TOP_K, n), I32),
            jax.ShapeDtypeStruct((N_EXPERTS, 128), I32),
        ],
        scratch_shapes=[pltpu.VMEM((N_EXPERTS, 128), F32), pltpu.VMEM((N_EXPERTS, tt), F32)],
        compiler_params=pltpu.CompilerParams(
            dimension_semantics=("arbitrary",), vmem_limit_bytes=VMEM_LIMIT),
        name="route",
    )(logits_t, bias_col)


def _place_kernel(eidx_ref, rank_ref, pstart_ref, dest_ref):
    tt = eidx_ref.shape[1]
    ioe = lax.broadcasted_iota(I32, (N_EXPERTS, tt), 0)
    pstart = pstart_ref[...]
    rows = [jnp.sum(jnp.where(ioe == eidx_ref[k:k + 1, :], pstart, 0.0), axis=0, keepdims=True)
            for k in range(TOP_K)]
    dest_ref[...] = jnp.concatenate(rows, axis=0).astype(I32) + rank_ref[...]


def _place(eidx, rank, pstart_col):
    n = eidx.shape[1]
    tt = TT_PLACE
    row_spec = pl.BlockSpec((TOP_K, tt), lambda i: (0, i))
    return pl.pallas_call(
        _place_kernel,
        grid=(n // tt,),
        in_specs=[row_spec, row_spec, pl.BlockSpec((N_EXPERTS, 1), lambda i: (0, 0))],
        out_specs=row_spec,
        out_shape=jax.ShapeDtypeStruct((TOP_K, n), I32),
        compiler_params=pltpu.CompilerParams(
            dimension_semantics=("parallel",), vmem_limit_bytes=VMEM_LIMIT),
        name="place",
    )(eidx, rank, pstart_col)


def _sc_mesh():
    return plsc.VectorSubcoreMesh(core_axis_name="c", subcore_axis_name="s",
                                  num_cores=SC_NC, num_subcores=SC_NS)


def _sc_scatter_rows(rows, idx, nrows_out):
    n, d = rows.shape
    nk = idx.shape[0]
    per_w = n // SC_NW
    nwin = per_w // SC_WIN
    assert per_w * SC_NW == n and nwin * SC_WIN == per_w and nwin % 2 == 0

    @functools.partial(
        pl.kernel, mesh=_sc_mesh(),
        out_type=jax.ShapeDtypeStruct((nrows_out, d), rows.dtype),
        scratch_types=[
            pltpu.VMEM((nwin, nk, SC_WIN), I32),
            pltpu.VMEM((2, SC_WIN, d), rows.dtype),
            pltpu.SemaphoreType.DMA((2,)),
            pltpu.SemaphoreType.DMA((2,)),
        ],
        compiler_params=pltpu.CompilerParams(use_tc_tiling_on_sc=True),
        name="sc_scatter_rows",
    )
    def scatter_kernel(rows_hbm, idx_hbm, out_hbm, idx_v, rows_v, lsem, ssem):
        wid = lax.axis_index("s") * SC_NC + lax.axis_index("c")
        base = wid * per_w
        pltpu.sync_copy(idx_hbm.at[wid], idx_v)

        def load(w, slot):
            return pltpu.make_async_copy(
                rows_hbm.at[pl.ds(base + w * SC_WIN, SC_WIN)], rows_v.at[slot], lsem.at[slot])

        def scat(w, k, slot):
            return pltpu.make_async_copy(rows_v.at[slot], out_hbm.at[idx_v.at[w, k]], ssem.at[slot])

        load(0, 0).start()

        @pl.loop(0, nwin, step=2)
        def _(w0):
            for slot in range(2):
                w = w0 + slot
                load(w, slot).wait()

                @pl.when(w + 1 < nwin)
                def _():
                    @pl.when(w >= 1)
                    def _():
                        for k in range(nk):
                            scat(w - 1, k, 1 - slot).wait()
                    load(w + 1, 1 - slot).start()

                for k in range(nk):
                    scat(w, k, slot).start()

        for k in range(nk):
            scat(nwin - 2, k, 0).wait()
        for k in range(nk):
            scat(nwin - 1, k, 1).wait()

    idx4 = idx.reshape(nk, SC_NW, nwin, SC_WIN).transpose(1, 2, 0, 3)
    return scatter_kernel(rows, idx4)


def _expert_kernel(gstart_ref, cnt_ref, xs_hbm, wg_ref, wu_ref, wd_ref, y_hbm,
                   wgb_ref, wub_ref, wdb_ref, xbuf_ref, ybuf_ref, xsem, ysem):
    e = pl.program_id(0)
    ne = pl.num_programs(0)
    bm = xbuf_ref.shape[1]
    nblk = y_hbm.shape[0] // bm
    half = D_MODEL // 2
    g_lo = gstart_ref[e]
    g_hi = gstart_ref[e + 1]
    g_end = gstart_ref[ne]

    def x_copy(g, slot):
        return pltpu.make_async_copy(xs_hbm.at[pl.ds(g * bm, bm), :], xbuf_ref.at[slot], xsem.at[slot])

    def y_copy(g, slot):
        return pltpu.make_async_copy(ybuf_ref.at[slot], y_hbm.at[pl.ds(g * bm, bm), :], ysem.at[slot])

    nslot = xbuf_ref.shape[0]

    ahead = nslot - EXP_GROUP

    @pl.when(e == 0)
    def _():
        for g0 in range(ahead):
            @pl.when(g0 < g_end)
            def _():
                x_copy(g0, g0).start()

    @pl.when(g_hi > g_lo)
    def _():
        wgb_ref[...] = wg_ref[...].astype(BF16)
        wub_ref[...] = wu_ref[...].astype(BF16)
        wdb_ref[...] = wd_ref[...].astype(BF16)

    row = lax.broadcasted_iota(I32, (bm, half), 0)

    def acquire(g):
        x_copy(g, g % nslot).wait()

        @pl.when(g + ahead < g_end)
        def _():
            x_copy(g + ahead, (g + ahead) % nslot).start()

        @pl.when(g >= nslot)
        def _():
            y_copy(g - nslot, g % nslot).wait()

    def load(g):
        n_valid = cnt_ref[e] - (g - g_lo) * bm
        x_lo, x_hi = _unpack_bf16_pair(jnp.where(row < n_valid, xbuf_ref[g % nslot], jnp.uint32(0)))
        return x_lo.astype(BF16), x_hi.astype(BF16)

    def gate_up(x):
        x_lo, x_hi = x
        gate = _dot(x_lo, wgb_ref[:half, :]) + _dot(x_hi, wgb_ref[half:, :])
        up = _dot(x_lo, wub_ref[:half, :]) + _dot(x_hi, wub_ref[half:, :])
        return gate, up

    def down(gu):
        gate, up = gu
        return _dot((_silu(gate) * up).astype(BF16), wdb_ref[...])

    def store(g, y):
        ybuf_ref[g % nslot] = _pack_bf16_pair(y[:, :half], y[:, half:])
        y_copy(g, g % nslot).start()

    def run_blocks(g, count):
        for j in range(count):
            acquire(g + j)
        gus = [gate_up(load(g + j)) for j in range(count)]
        ys = [down(gu) for gu in gus]
        for j in range(count):
            store(g + j, ys[j])

    def full_group(p, carry):
        run_blocks(g_lo + EXP_GROUP * p, EXP_GROUP)
        return carry

    n_own = g_hi - g_lo
    lax.fori_loop(0, n_own // EXP_GROUP, full_group, 0)
    size = EXP_GROUP // 2
    while size >= 1:
        @pl.when(n_own % (2 * size) >= size)
        def _(size=size):
            run_blocks(g_lo + n_own // (2 * size) * (2 * size), size)
        size //= 2

    @pl.when(e == ne - 1)
    def _():
        for back in range(nslot, 0, -1):
            @pl.when(g_end >= back)
            def _():
                y_copy(g_end - back, (g_end - back) % nslot).wait()

        ybuf_ref[0] = jnp.zeros((bm, half), U32)

        def fill(g, carry):
            y_copy(g, 0).start()
            return carry

        def drain(g, carry):
            y_copy(g, 0).wait()
            return carry

        lax.fori_loop(g_end, nblk, fill, 0)
        lax.fori_loop(g_end, nblk, drain, 0)


def _experts(gstart, counts, xs, w_gate, w_up, w_down):
    bm = BM_EXP
    nblk = xs.shape[0] // bm
    half = D_MODEL // 2
    grid_spec = pltpu.PrefetchScalarGridSpec(
        num_scalar_prefetch=2,
        grid=(N_EXPERTS,),
        in_specs=[
            pl.BlockSpec(memory_space=pl.ANY),
            pl.BlockSpec((None, D_MODEL, EXPERT_FF), lambda e, gs, cn: (e, 0, 0)),
            pl.BlockSpec((None, D_MODEL, EXPERT_FF), lambda e, gs, cn: (e, 0, 0)),
            pl.BlockSpec((None, EXPERT_FF, D_MODEL), lambda e, gs, cn: (e, 0, 0)),
        ],
        out_specs=pl.BlockSpec(memory_space=pl.ANY),
        scratch_shapes=[
            pltpu.VMEM((D_MODEL, EXPERT_FF), BF16),
            pltpu.VMEM((D_MODEL, EXPERT_FF), BF16),
            pltpu.VMEM((EXPERT_FF, D_MODEL), BF16),
            pltpu.VMEM((EXP_SLOTS, bm, half), U32),
            pltpu.VMEM((EXP_SLOTS, bm, half), U32),
            pltpu.SemaphoreType.DMA((EXP_SLOTS,)),
            pltpu.SemaphoreType.DMA((EXP_SLOTS,)),
        ],
    )
    return pl.pallas_call(
        _expert_kernel,
        grid_spec=grid_spec,
        out_shape=jax.ShapeDtypeStruct((nblk * bm, half), U32),
        compiler_params=pltpu.CompilerParams(
            dimension_semantics=("arbitrary",), vmem_limit_bytes=VMEM_LIMIT),
        name="experts",
    )(gstart, counts, xs, w_gate, w_up, w_down)


def _sc_gather_rows(table, idx):
    nrows = idx.shape[0]
    d = table.shape[1]
    per_w = nrows // SC_NW
    nwin = per_w // SC_WIN
    assert per_w * SC_NW == nrows and nwin * SC_WIN == per_w and nwin % 2 == 0
    @functools.partial(
        pl.kernel, mesh=_sc_mesh(),
        out_type=jax.ShapeDtypeStruct((nrows, d), table.dtype),
        scratch_types=[
            pltpu.VMEM((nwin, SC_WIN), I32),
            pltpu.VMEM((2, SC_WIN, d), table.dtype),
            pltpu.SemaphoreType.DMA((2,)),
            pltpu.SemaphoreType.DMA((2,)),
        ],
        compiler_params=pltpu.CompilerParams(use_tc_tiling_on_sc=True),
        name="sc_gather_rows",
    )
    def gather_kernel(table_hbm, idx_hbm, out_hbm, idx_v, rows_v, gsem, wsem):
        wid = lax.axis_index("s") * SC_NC + lax.axis_index("c")
        base = wid * per_w
        pltpu.sync_copy(idx_hbm.at[wid], idx_v)

        def gather(w, slot):
            return pltpu.make_async_copy(table_hbm.at[idx_v.at[w]], rows_v.at[slot], gsem.at[slot])

        def put(w, slot):
            return pltpu.make_async_copy(
                rows_v.at[slot], out_hbm.at[pl.ds(base + w * SC_WIN, SC_WIN)], wsem.at[slot])

        gather(0, 0).start()

        @pl.loop(0, nwin, step=2)
        def _(w0):
            for slot in range(2):
                w = w0 + slot
                gather(w, slot).wait()

                @pl.when(w + 1 < nwin)
                def _():
                    @pl.when(w >= 1)
                    def _():
                        put(w - 1, 1 - slot).wait()
                    gather(w + 1, 1 - slot).start()

                put(w, slot).start()

        put(nwin - 2, 0).wait()
        put(nwin - 1, 1).wait()

    return gather_kernel(table, idx.reshape(SC_NW, nwin, SC_WIN))


def _combine_kernel(y_ref, base_ref, gate_ref, g_ref, b_ref, out_ref):
    half = D_MODEL // 2
    gates = gate_ref[...].T
    acc_lo = base_ref[:, :half]
    acc_hi = base_ref[:, half:]
    for k in range(TOP_K):
        y_lo, y_hi = _unpack_bf16_pair(y_ref[k])
        gk = gates[:, k:k + 1]
        acc_lo = acc_lo + gk * y_lo
        acc_hi = acc_hi + gk * y_hi
    mu = (jnp.sum(acc_lo, axis=-1, keepdims=True) + jnp.sum(acc_hi, axis=-1, keepdims=True)) / D_MODEL
    c_lo = acc_lo - mu
    c_hi = acc_hi - mu
    var = (jnp.sum(c_lo * c_lo, axis=-1, keepdims=True)
           + jnp.sum(c_hi * c_hi, axis=-1, keepdims=True)) / D_MODEL
    inv = lax.rsqrt(var + LN_EPS)
    out_ref[:, :half] = c_lo * inv * g_ref[:, :half] + b_ref[:, :half]
    out_ref[:, half:] = c_hi * inv * g_ref[:, half:] + b_ref[:, half:]


def _combine(ybuf, base, gate_tok, ln_g, ln_b):
    n = base.shape[0]
    tt = TT_COMB
    half = D_MODEL // 2
    return pl.pallas_call(
        _combine_kernel,
        grid=(n // tt,),
        in_specs=[
            pl.BlockSpec((TOP_K, tt, half), lambda i: (0, i, 0)),
            pl.BlockSpec((tt, D_MODEL), lambda i: (i, 0)),
            pl.BlockSpec((TOP_K, tt), lambda i: (0, i)),
            pl.BlockSpec((1, D_MODEL), lambda i: (0, 0)),
            pl.BlockSpec((1, D_MODEL), lambda i: (0, 0)),
        ],
        out_specs=pl.BlockSpec((tt, D_MODEL), lambda i: (i, 0)),
        out_shape=jax.ShapeDtypeStruct((n, D_MODEL), F32),
        compiler_params=pltpu.CompilerParams(
            dimension_semantics=("parallel",), vmem_limit_bytes=VMEM_LIMIT),
        name="combine",
    )(ybuf, base, gate_tok, ln_g, ln_b)


def _regroup_w_in(w_in):
    o = 0
    cols = {}
    for name, width in (("dnq", DN_WIDTH), ("dnk", DN_WIDTH), ("dnv", DN_WIDTH), ("sq", SWA_WIDTH),
                        ("sk", SWA_KV_WIDTH), ("sv", SWA_KV_WIDTH), ("z", DN_WIDTH),
                        ("b", DN_HEADS), ("a", DN_HEADS)):
        cols[name] = w_in[:, o:o + width]
        o += width
    w_main = jnp.concatenate([cols[k] for k in ("dnq", "dnk", "dnv", "z", "sq", "sk", "sv")], axis=1)
    w_gates = jnp.concatenate(
        [cols["b"], cols["a"], jnp.zeros((D_MODEL, GATE_COLS - 2 * DN_HEADS), w_in.dtype)], axis=1)
    return w_main.astype(BF16), w_gates.astype(BF16)


def _layer(x, w_in, conv_w, a_log, dt_bias, dn_norm_g, sinks, w_out, ln1_g, ln1_b,
           router_w, router_bias, w_gate, w_up, w_down, sh_gate, sh_up, sh_down, ln2_g, ln2_b):
    b, t, d = x.shape
    n = b * t
    x2d = x.reshape(n, d)

    w_main, w_gates = _regroup_w_in(w_in)
    main, gates = _in_proj(x2d, w_main, w_gates)
    main3d = main.reshape(b, t, MAIN_COLS)

    pad = jnp.zeros((GATE_COLS - 2 * DN_HEADS,), F32)
    gpar = jnp.stack([jnp.concatenate([jnp.zeros((DN_HEADS,), F32), a_log.astype(F32), pad]),
                      jnp.concatenate([jnp.zeros((DN_HEADS,), F32), dt_bias.astype(F32), pad])])
    dn_out = _deltanet(main3d, gates.reshape(b, t, GATE_COLS), conv_w.astype(F32), gpar,
                       dn_norm_g.astype(F32).reshape(1, DN_HEAD_DIM))
    swa_out = _swa(main3d, sinks.astype(F32))

    base, xpk, logits_t = _post_mix(
        x2d, dn_out.reshape(n, DN_WIDTH), swa_out.reshape(n, SWA_WIDTH),
        w_out[:DN_WIDTH].astype(BF16), w_out[DN_WIDTH:].astype(BF16),
        ln1_g.reshape(1, d).astype(F32), ln1_b.reshape(1, d).astype(F32),
        router_w.T.astype(BF16),
        sh_gate.astype(BF16), sh_up.astype(BF16), sh_down.astype(BF16))

    eidx, gate, rank, cnt = _route(logits_t, router_bias.astype(F32).reshape(N_EXPERTS, 1))

    bm = BM_EXP
    counts = cnt[:, 0]
    padded = (counts + bm - 1) // bm * bm
    pend = jnp.cumsum(padded)
    pstart = pend - padded
    nblk = -(-(n * TOP_K) // bm) + N_EXPERTS
    gstart = (jnp.concatenate([pstart, pend[-1:]]) // bm).astype(I32)

    dest = _place(eidx, rank, pstart.astype(F32).reshape(N_EXPERTS, 1))
    xs = _sc_scatter_rows(xpk, dest, nblk * bm)
    ypk = _experts(gstart, counts, xs, w_gate, w_up, w_down)
    ybuf = _sc_gather_rows(ypk, dest.reshape(-1)).reshape(TOP_K, n, d // 2)
    out = _combine(ybuf, base, gate, ln2_g.reshape(1, d).astype(F32), ln2_b.reshape(1, d).astype(F32))
    return out.reshape(b, t, d)


def kernel(x, w_in, conv_w, a_log, dt_bias, dn_norm_g, sinks, w_out, ln1_g, ln1_b, router_w, router_bias,
           w_gate, w_up, w_down, shared_w_gate, shared_w_up, shared_w_down, ln2_g, ln2_b):
    depth = w_in.shape[0]
    for l in range(depth):
        x = _layer(x, w_in[l], conv_w[l], a_log[l], dt_bias[l], dn_norm_g[l], sinks[l], w_out[l],
                   ln1_g[l], ln1_b[l], router_w[l], router_bias[l], w_gate[l], w_up[l], w_down[l],
                   shared_w_gate[l], shared_w_up[l], shared_w_down[l], ln2_g[l], ln2_b[l])
    return x
```
